```python
import math
import jax, jax.numpy as jnp
from jax import lax
import numpy as np

D_MODEL = 1024
BATCH = 4
SEQ = 4096
DEPTH = 2
DEC_BATCH = 16
DEC_SEQ = 32
PAST_LEN = 2048

CHUNK = 64
Q_BLOCK = 128
HEAD_DIM = 64
N_DIFF_HEADS = D_MODEL // 256
N_SB_HEADS = D_MODEL // 128
N_FOX_HEADS = D_MODEL // HEAD_DIM
DIFF_W = N_DIFF_HEADS * 2 * HEAD_DIM
SB_W = N_SB_HEADS * HEAD_DIM
FOX_W = N_FOX_HEADS * HEAD_DIM
ROPE_DIM = HEAD_DIM // 4
ROPE_THETA = 500000.0
N_EXPERTS = 16
N_GROUPS = 4
EXPERTS_PER_GROUP = N_EXPERTS // N_GROUPS
TOP_K = 2
D_EXPERT = D_MODEL // 2
N_EVEN = (DEPTH + 1) // 2
N_ODD = DEPTH // 2
FORGET_BIAS_INIT = 3.0
EPS = 1e-6

kernel_name = 'hybrid_stream_diff_sb_fox_moe_step'


def _rmsnorm(x, g):
    xf = x.astype(jnp.float32)
    y = xf * lax.rsqrt(jnp.mean(xf * xf, axis=-1, keepdims=True) + EPS)
    return y.astype(x.dtype) * g


def _rope(x, pos):
    half = ROPE_DIM // 2
    inv = ROPE_THETA ** (-jnp.arange(half, dtype=jnp.float32) / half)
    ang = pos.astype(jnp.float32)[:, None] * inv[None, :]
    cos = jnp.cos(ang)[:, None, None, :]
    sin = jnp.sin(ang)[:, None, None, :]
    xr = x[..., :ROPE_DIM].astype(jnp.float32)
    x1, x2 = xr[..., :half], xr[..., half:]
    rot = jnp.concatenate([x1 * cos - x2 * sin, x2 * cos + x1 * sin], axis=-1)
    return jnp.concatenate([rot.astype(x.dtype), x[..., ROPE_DIM:]], axis=-1)


def _sweep(fn, q_arrays, q_pos):
    T = q_pos.shape[0]
    if T <= Q_BLOCK:
        return fn(*q_arrays, q_pos)
    nb = T // Q_BLOCK

    def split(a):
        return jnp.moveaxis(a.reshape(a.shape[0], nb, Q_BLOCK, *a.shape[2:]), 1, 0)

    xs = (*[split(a) for a in q_arrays], q_pos.reshape(nb, Q_BLOCK))
    out = lax.map(lambda args: fn(*args), xs)
    out = jnp.moveaxis(out, 0, 1)
    return out.reshape(out.shape[0], T, *out.shape[3:])


def _diff_block(q, qpos, k, v, kpos, lam):
    s = jnp.einsum('bqhcd,bkhcd->bchqk', q, k).astype(jnp.float32) * (HEAD_DIM ** -0.5)
    vis = (kpos[None, :] // CHUNK) <= (qpos[:, None] // CHUNK)
    p = jax.nn.softmax(jnp.where(vis, s, -jnp.inf), axis=-1)
    w = p[:, 0] - lam * p[:, 1]
    return jnp.einsum('bhqk,bkhe->bqhe', w.astype(v.dtype), v)


def _sb_block(q, qpos, k, v, kpos):
    z = jnp.einsum('bqhd,bkhd->bhqk', q, k).astype(jnp.float32) * (HEAD_DIM ** -0.5)
    vis = kpos[None, :] < qpos[:, None]
    log_keep = jnp.where(vis, jax.nn.log_sigmoid(-z), 0.0)
    later = lax.cumsum(log_keep, axis=3, reverse=True) - log_keep
    w = jnp.where(vis, jnp.exp(jax.nn.log_sigmoid(z) + later), 0.0)
    return jnp.einsum('bhqk,bkhd->bqhd', w.astype(v.dtype), v)


def _fox_block(q, dq, qpos, k, v, dk, kpos):
    s = jnp.einsum('bqhd,bkhd->bhqk', q, k).astype(jnp.float32) * (HEAD_DIM ** -0.5)
    bias = jnp.moveaxis(dq, 1, 2)[..., :, None] - jnp.moveaxis(dk, 1, 2)[..., None, :]
    vis = kpos[None, :] <= qpos[:, None]
    p = jax.nn.softmax(jnp.where(vis, s + bias, -jnp.inf), axis=-1)
    return jnp.einsum('bhqk,bkhd->bqhd', p.astype(v.dtype), v)


def _even_mixer(h, past, w_in, w_out, lq1, lk1, lq2, lk2, subln_g, layer):
    B, T, _ = h.shape
    P = 0 if past is None else past[0].shape[1]
    qpos = P + jnp.arange(T)
    kpos = jnp.arange(P + T)
    proj = h @ w_in
    aq, ak, av, bq, bk, bv = jnp.split(
        proj, [DIFF_W, 2 * DIFF_W, 3 * DIFF_W, 3 * DIFF_W + SB_W, 3 * DIFF_W + 2 * SB_W], axis=-1)
    aq = _rope(aq.reshape(B, T, N_DIFF_HEADS, 2, HEAD_DIM), qpos)
    ak = _rope(ak.reshape(B, T, N_DIFF_HEADS, 2, HEAD_DIM), qpos).reshape(B, T, N_DIFF_HEADS, 2 * HEAD_DIM)
    av = av.reshape(B, T, N_DIFF_HEADS, 2 * HEAD_DIM)
    bq = bq.reshape(B, T, N_SB_HEADS, HEAD_DIM)
    bk = bk.reshape(B, T, N_SB_HEADS, HEAD_DIM)
    bv = bv.reshape(B, T, N_SB_HEADS, HEAD_DIM)
    rows = (ak, av, bk, bv)
    if past is None:
        ak_all, av_all, bk_all, bv_all = ak, av, bk, bv
    else:
        ak_all = jnp.concatenate([past[0], ak], axis=1)
        av_all = jnp.concatenate([past[1], av], axis=1)
        bk_all = jnp.concatenate([past[2], bk], axis=1)
        bv_all = jnp.concatenate([past[3], bv], axis=1)
    lam_init = 0.8 - 0.6 * math.exp(-0.3 * layer)
    f32 = jnp.float32
    lam = (jnp.exp(jnp.sum(lq1.astype(f32) * lk1.astype(f32)))
           - jnp.exp(jnp.sum(lq2.astype(f32) * lk2.astype(f32))) + lam_init)
    kd = ak_all.reshape(B, P + T, N_DIFF_HEADS, 2, HEAD_DIM)
    a_out = _sweep(lambda qb, pb: _diff_block(qb, pb, kd, av_all, kpos, lam), (aq,), qpos)
    a_out = _rmsnorm(a_out, subln_g) * (1.0 - lam_init)
    b_out = _sweep(lambda qb, pb: _sb_block(qb, pb, bk_all, bv_all, kpos), (bq,), qpos)
    mixed = jnp.concatenate([a_out.reshape(B, T, DIFF_W), b_out.reshape(B, T, SB_W)], axis=-1)
    return mixed @ w_out, rows


def _odd_mixer(h, past, w_in, b_f, w_out):
    B, T, _ = h.shape
    P = 0 if past is None else past[0].shape[1]
    qpos = P + jnp.arange(T)
    kpos = jnp.arange(P + T)
    proj = h @ w_in
    q, k, v, fl = jnp.split(proj, [FOX_W, 2 * FOX_W, 3 * FOX_W], axis=-1)
    q = q.reshape(B, T, N_FOX_HEADS, HEAD_DIM)
    k = k.reshape(B, T, N_FOX_HEADS, HEAD_DIM)
    v = v.reshape(B, T, N_FOX_HEADS, HEAD_DIM)
    logf = jax.nn.log_sigmoid((fl + b_f).astype(jnp.float32))
    rows = (k, v, logf)
    if past is None:
        k_all, v_all, logf_all = k, v, logf
    else:
        k_all = jnp.concatenate([past[0], k], axis=1)
        v_all = jnp.concatenate([past[1], v], axis=1)
        logf_all = jnp.concatenate([past[2].astype(jnp.float32), logf], axis=1)
    dcum = jnp.cumsum(logf_all, axis=1)
    dq = dcum[:, P:]
    o = _sweep(lambda qb, db, pb: _fox_block(qb, db, pb, k_all, v_all, dcum, kpos), (q, dq), qpos)
    return o.reshape(B, T, FOX_W) @ w_out, rows


def _moe(h, w_router, b_router, w_gate, w_up, w_down):
    B, T, D = h.shape
    t = h.reshape(B * T, D)
    s = jax.nn.sigmoid((t @ w_router).astype(jnp.float32))
    sel = s + b_router.astype(jnp.float32)
    grp_score = lax.top_k(sel.reshape(-1, N_GROUPS, EXPERTS_PER_GROUP), TOP_K)[0].sum(-1)
    grp = jnp.argmax(grp_score, axis=-1)
    in_grp = (jnp.arange(N_EXPERTS) // EXPERTS_PER_GROUP)[None, :] == grp[:, None]
    _, idx = lax.top_k(jnp.where(in_grp, sel, -jnp.inf), TOP_K)
    w = jnp.take_along_axis(s, idx, axis=-1)
    w = w / jnp.sum(w, axis=-1, keepdims=True)
    gates = jnp.sum(jax.nn.one_hot(idx, N_EXPERTS, dtype=jnp.float32) * w[..., None], axis=1)
    a = jax.nn.silu(jnp.einsum('nd,edf->nef', t, w_gate)) * jnp.einsum('nd,edf->nef', t, w_up)
    a = a * gates[:, :, None].astype(a.dtype)
    return jnp.einsum('nef,efd->nd', a, w_down).reshape(B, T, D)


def _trunk(x, c, past, wts):
    new_even, new_odd = [], []
    for l in range(DEPTH):
        mod = jax.nn.silu(c) @ wts['w_ada'][l] + wts['b_ada'][l]
        sh1, sc1, g1, sh2, sc2, g2 = [m[:, None, :] for m in jnp.split(mod, 6, axis=-1)]
        h = _rmsnorm(x, wts['norm_mix'][l]) * (1.0 + sc1) + sh1
        i = l // 2
        if l % 2 == 0:
            p = None if past is None else (past['diff_k'][i], past['diff_v'][i], past['sb_k'][i], past['sb_v'][i])
            out, rows = _even_mixer(h, p, wts['w_in_even'][i], wts['w_out_even'][i], wts['diff_lq1'][i],
                                    wts['diff_lk1'][i], wts['diff_lq2'][i], wts['diff_lk2'][i],
                                    wts['diff_subln'][i], l)
            new_even.append(rows)
        else:
            p = None if past is None else (past['fox_k'][i], past['fox_v'][i], past['fox_logf'][i])
            out, rows = _odd_mixer(h, p, wts['w_in_odd'][i], wts['b_forget'][i], wts['w_out_odd'][i])
            new_odd.append(rows)
        x = x + g1 * out
        h = _rmsnorm(x, wts['norm_ffn'][l]) * (1.0 + sc2) + sh2
        x = x + g2 * _moe(h, wts['w_router'], wts['b_router'], wts['w_gate'][l], wts['w_up'][l], wts['w_down'][l])
    y = _rmsnorm(x, wts['norm_final'])
    even = [jnp.stack(r, axis=0) for r in zip(*new_even)]
    odd = [jnp.stack(r, axis=0) for r in zip(*new_odd)]
    return y, even, odd


def setup_inputs(seed: int = 0) -> dict:
    key = jax.random.key(seed)
    ks = iter(jax.random.split(key, 40))

    def nrm(shape, scale=1.0):
        return jax.random.normal(next(ks), shape, jnp.float32) * scale

    D = D_MODEL
    return {
        'x_prompt': nrm((BATCH, SEQ, D)),
        'x_sample': nrm((DEC_BATCH, DEC_SEQ, D)),
        'c_prompt': nrm((BATCH, D)),
        'c_sample': nrm((DEC_BATCH, D)),
        'cache_diff_k': nrm((N_EVEN, DEC_BATCH, PAST_LEN, N_DIFF_HEADS, 2 * HEAD_DIM)),
        'cache_diff_v': nrm((N_EVEN, DEC_BATCH, PAST_LEN, N_DIFF_HEADS, 2 * HEAD_DIM)),
        'cache_sb_k': nrm((N_EVEN, DEC_BATCH, PAST_LEN, N_SB_HEADS, HEAD_DIM)),
        'cache_sb_v': nrm((N_EVEN, DEC_BATCH, PAST_LEN, N_SB_HEADS, HEAD_DIM)),
        'cache_fox_k': nrm((N_ODD, DEC_BATCH, PAST_LEN, N_FOX_HEADS, HEAD_DIM)),
        'cache_fox_v': nrm((N_ODD, DEC_BATCH, PAST_LEN, N_FOX_HEADS, HEAD_DIM)),
        'cache_fox_logf': jax.nn.log_sigmoid(FORGET_BIAS_INIT + nrm((N_ODD, DEC_BATCH, PAST_LEN, N_FOX_HEADS))),
        'w_ada': nrm((DEPTH, D, 6 * D), 0.5 * D ** -0.5),
        'b_ada': nrm((DEPTH, 6 * D), 0.02),
        'norm_mix': 1.0 + nrm((DEPTH, D), 0.02),
        'norm_ffn': 1.0 + nrm((DEPTH, D), 0.02),
        'w_in_even': nrm((N_EVEN, D, 3 * DIFF_W + 3 * SB_W), D ** -0.5),
        'w_out_even': nrm((N_EVEN, DIFF_W + SB_W, D), (DIFF_W + SB_W) ** -0.5),
        'diff_lq1': nrm((N_EVEN, HEAD_DIM), 0.1),
        'diff_lk1': nrm((N_EVEN, HEAD_DIM), 0.1),
        'diff_lq2': nrm((N_EVEN, HEAD_DIM), 0.1),
        'diff_lk2': nrm((N_EVEN, HEAD_DIM), 0.1),
        'diff_subln': 1.0 + nrm((N_EVEN, 2 * HEAD_DIM), 0.02),
        'w_in_odd': nrm((N_ODD, D, 3 * FOX_W + N_FOX_HEADS), D ** -0.5),
        'b_forget': FORGET_BIAS_INIT + nrm((N_ODD, N_FOX_HEADS), 0.1),
        'w_out_odd': nrm((N_ODD, FOX_W, D), FOX_W ** -0.5),
        'w_router': nrm((D, N_EXPERTS), D ** -0.5),
        'b_router': nrm((N_EXPERTS,), 0.01),
        'w_gate': nrm((DEPTH, N_EXPERTS, D, D_EXPERT), D ** -0.5),
        'w_up': nrm((DEPTH, N_EXPERTS, D, D_EXPERT), D ** -0.5),
        'w_down': nrm((DEPTH, N_EXPERTS, D_EXPERT, D), D_EXPERT ** -0.5),
        'norm_final': 1.0 + nrm((D,), 0.02),
    }


def reference(x_prompt, x_sample, c_prompt, c_sample, cache_diff_k, cache_diff_v, cache_sb_k, cache_sb_v,
              cache_fox_k, cache_fox_v, cache_fox_logf, w_ada, b_ada, norm_mix, norm_ffn, w_in_even,
              w_out_even, diff_lq1, diff_lk1, diff_lq2, diff_lk2, diff_subln, w_in_odd, b_forget, w_out_odd,
              w_router, b_router, w_gate, w_up, w_down, norm_final):
    wts = {'w_ada': w_ada, 'b_ada': b_ada, 'norm_mix': norm_mix, 'norm_ffn': norm_ffn,
           'w_in_even': w_in_even, 'w_out_even': w_out_even, 'diff_lq1': diff_lq1, 'diff_lk1': diff_lk1,
           'diff_lq2': diff_lq2, 'diff_lk2': diff_lk2, 'diff_subln': diff_subln, 'w_in_odd': w_in_odd,
           'b_forget': b_forget, 'w_out_odd': w_out_odd, 'w_router': w_router, 'b_router': b_router,
           'w_gate': w_gate, 'w_up': w_up, 'w_down': w_down, 'norm_final': norm_final}
    past = {'diff_k': cache_diff_k, 'diff_v': cache_diff_v, 'sb_k': cache_sb_k, 'sb_v': cache_sb_v,
            'fox_k': cache_fox_k, 'fox_v': cache_fox_v, 'fox_logf': cache_fox_logf}
    y_prompt, (dk_p, dv_p, sk_p, sv_p), (fk_p, fv_p, fl_p) = _trunk(x_prompt, c_prompt, None, wts)
    y_sample, (dk_s, dv_s, sk_s, sv_s), (fk_s, fv_s, fl_s) = _trunk(x_sample, c_sample, past, wts)
    return (y_prompt, y_sample, dk_p, dv_p, sk_p, sv_p, fk_p, fv_p, fl_p,
            dk_s, dv_s, sk_s, sv_s, fk_s, fv_s, fl_s)
```

```python
import functools
import math

import jax
import jax.numpy as jnp
from jax import lax
from jax.experimental import pallas as pl
from jax.experimental.pallas import tpu as pltpu

D_MODEL = 1024
DEPTH = 2
CHUNK = 64
HEAD_DIM = 64
N_DIFF_HEADS = D_MODEL // 256
N_SB_HEADS = D_MODEL // 128
N_FOX_HEADS = D_MODEL // HEAD_DIM
DIFF_W = N_DIFF_HEADS * 2 * HEAD_DIM
SB_W = N_SB_HEADS * HEAD_DIM
FOX_W = N_FOX_HEADS * HEAD_DIM
ROPE_DIM = HEAD_DIM // 4
ROPE_THETA = 500000.0
N_EXPERTS = 16
N_GROUPS = 4
EXPERTS_PER_GROUP = N_EXPERTS // N_GROUPS
D_EXPERT = D_MODEL // 2
EPS = 1e-6

LANES = 128
SEG = 512
QK_SCALE = HEAD_DIM ** -0.5
MASKED = -1e30
VMEM_LIMIT = 48 * 1024 * 1024

F32 = jnp.float32
BF16 = jnp.bfloat16


def _params(*sem):
    return pltpu.CompilerParams(dimension_semantics=sem, vmem_limit_bytes=VMEM_LIMIT)


def _dot(a, b):
    return jnp.dot(a, b, preferred_element_type=F32)


def _dot_nt(a, b):
    return lax.dot_general(a, b, (((1,), (1,)), ((), ())), preferred_element_type=F32)


def _norm_mod(x, g, sc, sh):
    bb, tt, d = x.shape
    ms = jnp.mean(x * x, axis=-1, keepdims=True)
    h = x * lax.rsqrt(ms + EPS) * g
    h = h * (1.0 + sc) + sh
    return h.reshape(bb * tt, d)


def _log_sigmoid(x):
    return jnp.minimum(x, 0.0) - jnp.log(1.0 + jnp.exp(-jnp.abs(x)))


def _ada_kernel(c_ref, w_ref, b_ref, o_ref):
    c = c_ref[...]
    a = (c * jax.nn.sigmoid(c)).astype(BF16)
    o_ref[0] = _dot(a, w_ref[0].astype(BF16)) + b_ref[0]


def _ada_mod(c_all, w_ada, b_ada):
    rows = c_all.shape[0]
    tn = 1536
    width = 6 * D_MODEL
    return pl.pallas_call(
        _ada_kernel,
        grid=(DEPTH, width // tn),
        in_specs=[
            pl.BlockSpec((rows, D_MODEL), lambda l, j: (0, 0)),
            pl.BlockSpec((1, D_MODEL, tn), lambda l, j: (l, 0, j)),
            pl.BlockSpec((1, 1, tn), lambda l, j: (l, 0, j)),
        ],
        out_specs=pl.BlockSpec((1, rows, tn), lambda l, j: (l, 0, j)),
        out_shape=jax.ShapeDtypeStruct((DEPTH, rows, width), F32),
        compiler_params=_params("arbitrary", "arbitrary"),
        name="ada_mod",
    )(c_all, w_ada, b_ada.reshape(DEPTH, 1, width))


def _rope(p, cos, s1, s2):
    outs = []
    for c in range(SEG // LANES):
        pc = p[:, c * LANES:(c + 1) * LANES]
        up = pltpu.roll(pc, LANES - ROPE_DIM // 2, 1)
        down = pltpu.roll(pc, ROPE_DIM // 2, 1)
        outs.append(pc * cos + up * s1 + down * s2)
    return jnp.concatenate(outs, axis=1)


def _proj_even_kernel(x_ref, g_ref, sc_ref, sh_ref, w_ref, cos_ref, s1_ref, s2_ref,
                      qkv_ref, ak_ref, av_ref, bk_ref, bv_ref):
    bb, tt, _ = x_ref.shape
    h = _norm_mod(x_ref[...], g_ref[...], sc_ref[...], sh_ref[...]).astype(BF16)
    cos, s1, s2 = cos_ref[...], s1_ref[...], s2_ref[...]

    def seg(i):
        return _dot(h, w_ref[:, i * SEG:(i + 1) * SEG])

    def put(i, val, leaf_ref):
        val3 = val.reshape(bb, tt, SEG)
        qkv_ref[:, :, i * SEG:(i + 1) * SEG] = val3.astype(BF16)
        if leaf_ref is not None:
            leaf_ref[...] = val3

    put(0, _rope(seg(0), cos, s1, s2) * QK_SCALE, None)
    put(1, _rope(seg(1), cos, s1, s2), ak_ref)
    put(2, seg(2), av_ref)
    put(3, seg(3) * QK_SCALE, None)
    put(4, seg(4), bk_ref)
    put(5, seg(5), bv_ref)


def _proj_odd_kernel(x_ref, g_ref, sc_ref, sh_ref, w_ref, wf_ref, bf_ref,
                     qkv_ref, k_ref, v_ref, lf_ref):
    bb, tt, _ = x_ref.shape
    h = _norm_mod(x_ref[...], g_ref[...], sc_ref[...], sh_ref[...]).astype(BF16)

    def seg(i):
        return _dot(h, w_ref[:, i * SEG:(i + 1) * SEG]).reshape(bb, tt, SEG)

    for i in range(2):
        qkv_ref[:, :, i * SEG:(i + 1) * SEG] = (seg(i) * QK_SCALE).astype(BF16)
    for i in range(2, 6):
        val = seg(i)
        qkv_ref[:, :, i * SEG:(i + 1) * SEG] = val.astype(BF16)
        leaf_ref = k_ref if i < 4 else v_ref
        leaf_ref[:, :, (i % 2) * SEG:(i % 2 + 1) * SEG] = val
    fl = _dot(h, wf_ref[...]) + bf_ref[...]
    lf_ref[...] = _log_sigmoid(fl).reshape(bb, tt, N_FOX_HEADS)


def _row_specs(bb, tt):
    x_spec = pl.BlockSpec((bb, tt, D_MODEL), lambda i, j: (i, j, 0))
    g_spec = pl.BlockSpec((1, D_MODEL), lambda i, j: (0, 0))
    m_spec = pl.BlockSpec((bb, 1, D_MODEL), lambda i, j: (i, 0, 0))
    return x_spec, g_spec, m_spec


def _out_spec(bb, tt, width):
    return pl.BlockSpec((bb, tt, width), lambda i, j: (i, j, 0))


def _proj_even(x, g, sc, sh, w_bf, tables, bb, tt):
    B, T, _ = x.shape
    x_spec, g_spec, m_spec = _row_specs(bb, tt)
    tm = bb * tt
    t_spec = pl.BlockSpec((tm, LANES), (lambda i, j: (j, 0)) if bb == 1 else (lambda i, j: (0, 0)))
    leaf = jax.ShapeDtypeStruct((B, T, SEG), F32)
    return pl.pallas_call(
        _proj_even_kernel,
        grid=(B // bb, T // tt),
        in_specs=[x_spec, g_spec, m_spec, m_spec,
                  pl.BlockSpec(w_bf.shape, lambda i, j: (0, 0)), t_spec, t_spec, t_spec],
        out_specs=[_out_spec(bb, tt, 6 * SEG)] + [_out_spec(bb, tt, SEG)] * 4,
        out_shape=[jax.ShapeDtypeStruct((B, T, 6 * SEG), BF16), leaf, leaf, leaf, leaf],
        compiler_params=_params("arbitrary", "arbitrary"),
        name="proj_even",
    )(x, g, sc, sh, w_bf, *tables)


def _proj_odd(x, g, sc, sh, w_bf, wf_bf, b_f, bb, tt):
    B, T, _ = x.shape
    x_spec, g_spec, m_spec = _row_specs(bb, tt)
    leaf = jax.ShapeDtypeStruct((B, T, FOX_W), F32)
    return pl.pallas_call(
        _proj_odd_kernel,
        grid=(B // bb, T // tt),
        in_specs=[x_spec, g_spec, m_spec, m_spec,
                  pl.BlockSpec(w_bf.shape, lambda i, j: (0, 0)),
                  pl.BlockSpec(wf_bf.shape, lambda i, j: (0, 0)),
                  pl.BlockSpec((1, N_FOX_HEADS), lambda i, j: (0, 0))],
        out_specs=[_out_spec(bb, tt, 6 * SEG), _out_spec(bb, tt, FOX_W), _out_spec(bb, tt, FOX_W),
                   _out_spec(bb, tt, N_FOX_HEADS)],
        out_shape=[jax.ShapeDtypeStruct((B, T, 6 * SEG), BF16), leaf, leaf,
                   jax.ShapeDtypeStruct((B, T, N_FOX_HEADS), F32)],
        compiler_params=_params("arbitrary", "arbitrary"),
        name="proj_odd",
    )(x, g, sc, sh, w_bf, wf_bf, b_f.reshape(1, N_FOX_HEADS))


def _attn_kernel(*refs, kind, tq, tk, q_off, n_valid, lam_init):
    if kind == "diff":
        q_ref, k_ref, v_ref, lp_ref, sg_ref, o_ref, acc_ref, m_ref, l_ref = refs
    elif kind == "fox":
        q_ref, k_ref, v_ref, nd_ref, o_ref, acc_ref, m_ref, l_ref = refs
    else:
        q_ref, k_ref, v_ref, o_ref, acc_ref, m_ref = refs

    q0 = q_off + pl.program_id(2) * tq
    q = q_ref[0]
    low = lax.broadcasted_iota(jnp.int32, (tq, LANES), 1) < HEAD_DIM
    zero = jnp.zeros_like(q)
    q_half = (jnp.where(low, q, zero), jnp.where(low, zero, q))

    acc_ref[...] = jnp.zeros_like(acc_ref)
    if kind == "sb":
        m_ref[...] = jnp.zeros_like(m_ref)
        tri_r = lax.broadcasted_iota(jnp.int32, (tk, tk), 0)
        tri_c = lax.broadcasted_iota(jnp.int32, (tk, tk), 1)
        later_mat = jnp.where(tri_r > tri_c, 1.0, 0.0).astype(BF16)
    else:
        m_ref[...] = jnp.full_like(m_ref, MASKED)
        l_ref[...] = jnp.zeros_like(l_ref)

    qpos = q0 + lax.broadcasted_iota(jnp.int32, (tq, tk), 0)
    kidx = lax.broadcasted_iota(jnp.int32, (tq, tk), 1)

    def block(kb, masked):
        start = pl.multiple_of(kb * tk, tk)
        k = k_ref[0, pl.ds(start, tk), :]
        v = v_ref[0, pl.ds(start, tk), :]
        if masked:
            kpos = start + kidx
            if kind == "diff":
                vis = jnp.right_shift(kpos, 6) <= jnp.right_shift(qpos, 6)
            elif kind == "fox":
                vis = kpos <= qpos
            else:
                vis = kpos < qpos
            vis = jnp.logical_and(vis, kpos < n_valid)
        for h in range(2):
            s = _dot_nt(q_half[h], k)
            if kind == "sb":
                soft = jnp.log(1.0 + jnp.exp(-jnp.abs(s)))
                log_beta = jnp.minimum(s, 0.0) - soft
                log_keep = log_beta - s
                if masked:
                    log_keep = jnp.where(vis, log_keep, 0.0)
                hi = log_keep.astype(BF16)
                lo = (log_keep - hi.astype(F32)).astype(BF16)
                later = _dot(hi, later_mat) + _dot(lo, later_mat)
                w = jnp.exp(log_beta + later + m_ref[h])
                if masked:
                    w = jnp.where(vis, w, 0.0)
                acc_ref[h] += _dot(w.astype(BF16), v)
                m_ref[h] += jnp.sum(log_keep, axis=1, keepdims=True)
            else:
                if kind == "fox":
                    s = s + nd_ref[0, 0, h, pl.ds(kb, 1), :]
                if masked:
                    s = jnp.where(vis, s, MASKED)
                m_prev = m_ref[h]
                m_new = jnp.maximum(m_prev, jnp.max(s, axis=1, keepdims=True))
                alpha = jnp.exp(m_prev - m_new)
                p = jnp.exp(s - m_new)
                l_ref[h] = alpha * l_ref[h] + jnp.sum(p, axis=1, keepdims=True)
                acc_ref[h] = alpha * acc_ref[h] + _dot(p.astype(BF16), v)
                m_ref[h] = m_new

    n_full = q0 // tk
    block(n_full, True)

    def body(i, carry):
        block(n_full - 1 - i, False)
        return carry

    lax.fori_loop(0, n_full, body, 0)

    if kind == "sb":
        out = jnp.where(low, acc_ref[0], acc_ref[1])
    elif kind == "fox":
        out = jnp.where(low, acc_ref[0] * (1.0 / l_ref[0]), acc_ref[1] * (1.0 / l_ref[1]))
    else:
        lp = lp_ref[...]
        lam = (jnp.exp(jnp.sum(lp[0:1] * lp[1:2], axis=1, keepdims=True))
               - jnp.exp(jnp.sum(lp[2:3] * lp[3:4], axis=1, keepdims=True)) + lam_init)
        a = acc_ref[0] * (1.0 / l_ref[0]) - lam * (acc_ref[1] * (1.0 / l_ref[1]))
        ms = jnp.mean(a * a, axis=-1, keepdims=True)
        out = a * lax.rsqrt(ms + EPS) * sg_ref[...] * (1.0 - lam_init)
    o_ref[0] = out.astype(o_ref.dtype)


def _attention(kind, q_arr, q_col, k_arr, k_col, v_arr, v_col, n_col, *, tq, tk, q_off, n_valid,
               extras=(), lam_init=0.0):
    B, Tq, _ = q_arr.shape
    Tk = k_arr.shape[1]
    assert Tq % tq == 0 and Tk % tk == 0 and tk % tq == 0 and q_off % tk == 0
    assert q_off + Tq <= n_valid <= Tk
    in_specs = [
        pl.BlockSpec((1, tq, LANES), lambda b, c, i: (b, i, q_col + c)),
        pl.BlockSpec((1, Tk, LANES), lambda b, c, i: (b, 0, k_col + c)),
        pl.BlockSpec((1, Tk, LANES), lambda b, c, i: (b, 0, v_col + c)),
    ]
    scratch = [pltpu.VMEM((2, tq, LANES), F32), pltpu.VMEM((2, tq, 1), F32)]
    if kind == "diff":
        lp, sg = extras
        in_specs += [pl.BlockSpec(lp.shape, lambda b, c, i: (0, 0)),
                     pl.BlockSpec(sg.shape, lambda b, c, i: (0, 0))]
        scratch.append(pltpu.VMEM((2, tq, 1), F32))
    elif kind == "fox":
        (nd,) = extras
        in_specs.append(pl.BlockSpec((1, 1, 2, Tk // tk, tk), lambda b, c, i: (b, c, 0, 0, 0)))
        scratch.append(pltpu.VMEM((2, tq, 1), F32))
    kernel = functools.partial(_attn_kernel, kind=kind, tq=tq, tk=tk, q_off=q_off, n_valid=n_valid,
                               lam_init=lam_init)
    return pl.pallas_call(
        kernel,
        grid=(B, n_col, Tq // tq),
        in_specs=in_specs,
        out_specs=pl.BlockSpec((1, tq, LANES), lambda b, c, i: (b, i, c)),
        out_shape=jax.ShapeDtypeStruct((B, Tq, n_col * LANES), BF16),
        scratch_shapes=scratch,
        compiler_params=_params("arbitrary", "arbitrary", "arbitrary"),
        name="attn_" + kind,
    )(q_arr, k_arr, v_arr, *extras)


def _cumsum_kernel(x_ref, o_ref, carry_ref):
    blk = x_ref.shape[2]

    @pl.when(pl.program_id(1) == 0)
    def _():
        carry_ref[...] = jnp.zeros_like(carry_ref)

    x = x_ref[0]
    r = lax.broadcasted_iota(jnp.int32, (blk, blk), 0)
    c = lax.broadcasted_iota(jnp.int32, (blk, blk), 1)
    upto = jnp.where(r <= c, 1.0, 0.0).astype(BF16)
    hi = x.astype(BF16)
    r1 = x - hi.astype(F32)
    mid = r1.astype(BF16)
    lo = (r1 - mid.astype(F32)).astype(BF16)
    total = _dot(hi, upto) + _dot(mid, upto) + _dot(lo, upto) + carry_ref[...]
    o_ref[0] = -total
    carry_ref[...] = total[:, blk - 1:blk]


def _neg_cumsum(lf_t, blk):
    B, H, T = lf_t.shape
    return pl.pallas_call(
        _cumsum_kernel,
        grid=(B, T // blk),
        in_specs=[pl.BlockSpec((1, H, blk), lambda b, j: (b, 0, j))],
        out_specs=pl.BlockSpec((1, H, blk), lambda b, j: (b, 0, j)),
        out_shape=jax.ShapeDtypeStruct((B, H, T), F32),
        scratch_shapes=[pltpu.VMEM((H, 1), F32)],
        compiler_params=_params("arbitrary", "arbitrary"),
        name="neg_cumsum",
    )(lf_t)


def _outproj_kernel(*refs, nseg):
    a_refs = refs[:nseg]
    w_ref, x_ref, g_ref, o_ref = refs[nseg:]
    bb, tt, d = x_ref.shape
    acc = None
    off = 0
    for a_ref in a_refs:
        width = a_ref.shape[2]
        part = _dot(a_ref[...].reshape(bb * tt, width), w_ref[off:off + width, :])
        acc = part if acc is None else acc + part
        off += width
    o_ref[...] = x_ref[...] + g_ref[...] * acc.reshape(bb, tt, d)


def _outproj(mixed, w_bf, x, gate, bb, tt):
    B, T, _ = x.shape
    x_spec, _, m_spec = _row_specs(bb, tt)
    return pl.pallas_call(
        functools.partial(_outproj_kernel, nseg=len(mixed)),
        grid=(B // bb, T // tt),
        in_specs=[_out_spec(bb, tt, a.shape[2]) for a in mixed]
        + [pl.BlockSpec(w_bf.shape, lambda i, j: (0, 0)), x_spec, m_spec],
        out_specs=x_spec,
        out_shape=jax.ShapeDtypeStruct(x.shape, F32),
        compiler_params=_params("arbitrary", "arbitrary"),
        name="outproj",
    )(*mixed, w_bf, x, gate)


def _route(h, wr, br):
    tm = h.shape[0]
    logits = jnp.dot(h, wr, preferred_element_type=F32, precision=lax.Precision.HIGHEST)
    s = jax.nn.sigmoid(logits)
    sel = s + br
    lane = lax.broadcasted_iota(jnp.int32, (tm, N_EXPERTS), 1).astype(F32)
    neg = -jnp.inf

    def first_argmax(vals):
        top = jnp.max(vals, axis=1, keepdims=True)
        idx = jnp.min(jnp.where(vals == top, lane, float(N_EXPERTS)), axis=1, keepdims=True)
        return top, idx

    best = None
    for g in range(N_GROUPS):
        in_g = (lane >= g * EXPERTS_PER_GROUP) & (lane < (g + 1) * EXPERTS_PER_GROUP)
        top1, i1 = first_argmax(jnp.where(in_g, sel, neg))
        top2, i2 = first_argmax(jnp.where(in_g & (lane != i1), sel, neg))
        score = top1 + top2
        if best is None:
            best = (score, i1, i2)
        else:
            better = score > best[0]
            best = tuple(jnp.where(better, new, old) for new, old in zip((score, i1, i2), best))
    _, i1, i2 = best
    picked = jnp.where((lane == i1) | (lane == i2), s, 0.0)
    return picked / jnp.sum(picked, axis=1, keepdims=True)


def _moe_kernel(*refs, final):
    if final:
        (x_ref, g_ref, sc_ref, sh_ref, g2_ref, wr_ref, br_ref, wg_ref, wu_ref, wd_ref, gf_ref,
         o_ref, h_ref, gate_ref, acc_ref) = refs
    else:
        (x_ref, g_ref, sc_ref, sh_ref, g2_ref, wr_ref, br_ref, wg_ref, wu_ref, wd_ref,
         o_ref, h_ref, gate_ref, acc_ref) = refs
    bb, tt, d = x_ref.shape
    e = pl.program_id(2)

    @pl.when(e == 0)
    def _():
        h = _norm_mod(x_ref[...], g_ref[...], sc_ref[...], sh_ref[...])
        h_ref[...] = h.astype(BF16)
        gate_ref[...] = _route(h, wr_ref[...], br_ref[...])
        acc_ref[...] = jnp.zeros_like(acc_ref)

    h = h_ref[...]
    gt = _dot(h, wg_ref[0])
    up = _dot(h, wu_ref[0])
    gates = gate_ref[...]
    lane = lax.broadcasted_iota(jnp.int32, gates.shape, 1)
    ge = jnp.sum(jnp.where(lane == e, gates, 0.0), axis=1, keepdims=True)
    a = (gt * jax.nn.sigmoid(gt)) * up * ge
    acc_ref[...] += _dot(a.astype(BF16), wd_ref[0])

    @pl.when(e == N_EXPERTS - 1)
    def _():
        y = x_ref[...] + g2_ref[...] * acc_ref[...].reshape(bb, tt, d)
        if final:
            ms = jnp.mean(y * y, axis=-1, keepdims=True)
            y = y * lax.rsqrt(ms + EPS) * gf_ref[...]
        o_ref[...] = y


def _moe(x, g, sc, sh, g2, w_router, b_router, wg_bf, wu_bf, wd_bf, g_final, bb, tt):
    B, T, _ = x.shape
    final = g_final is not None
    x_spec = pl.BlockSpec((bb, tt, D_MODEL), lambda i, j, e: (i, j, 0))
    g_spec = pl.BlockSpec((1, D_MODEL), lambda i, j, e: (0, 0))
    m_spec = pl.BlockSpec((bb, 1, D_MODEL), lambda i, j, e: (i, 0, 0))
    in_specs = [x_spec, g_spec, m_spec, m_spec, m_spec,
                pl.BlockSpec((D_MODEL, N_EXPERTS), lambda i, j, e: (0, 0)),
                pl.BlockSpec((1, N_EXPERTS), lambda i, j, e: (0, 0)),
                pl.BlockSpec((1, D_MODEL, D_EXPERT), lambda i, j, e: (e, 0, 0)),
                pl.BlockSpec((1, D_MODEL, D_EXPERT), lambda i, j, e: (e, 0, 0)),
                pl.BlockSpec((1, D_EXPERT, D_MODEL), lambda i, j, e: (e, 0, 0))]
    args = [x, g, sc, sh, g2, w_router, b_router.reshape(1, N_EXPERTS), wg_bf, wu_bf, wd_bf]
    if final:
        in_specs.append(g_spec)
        args.append(g_final)
    tm = bb * tt
    return pl.pallas_call(
        functools.partial(_moe_kernel, final=final),
        grid=(B // bb, T // tt, N_EXPERTS),
        in_specs=in_specs,
        out_specs=x_spec,
        out_shape=jax.ShapeDtypeStruct(x.shape, F32),
        scratch_shapes=[pltpu.VMEM((tm, D_MODEL), BF16), pltpu.VMEM((tm, N_EXPERTS), F32),
                        pltpu.VMEM((tm, D_MODEL), F32)],
        compiler_params=_params("arbitrary", "arbitrary", "arbitrary"),
        name="moe",
    )(*args)


def _rope_tables(pos):
    half = ROPE_DIM // 2
    inv = ROPE_THETA ** (-jnp.arange(half, dtype=F32) / half)
    ang = pos.astype(F32)[:, None] * inv[None, :]
    cos, sin = jnp.cos(ang), jnp.sin(ang)
    ones = jnp.ones((pos.shape[0], HEAD_DIM - ROPE_DIM), F32)
    zeros = jnp.zeros_like(ones)
    zh = jnp.zeros_like(sin)
    c64 = jnp.concatenate([cos, cos, ones], axis=1)
    s1_64 = jnp.concatenate([-sin, zh, zeros], axis=1)
    s2_64 = jnp.concatenate([zh, sin, zeros], axis=1)
    return tuple(jnp.concatenate([t, t], axis=1) for t in (c64, s1_64, s2_64))


def _pad_time(a, total):
    return jnp.pad(a, ((0, 0), (0, total - a.shape[1])) + ((0, 0),) * (a.ndim - 2))


def _trunk(x, mod, past, wts, cfg):
    B, T, _ = x.shape
    P = 0 if past is None else past["diff_k"].shape[2]
    bb, tt_proj, tt_out, tt_moe, tq, tk = cfg
    n_valid = P + T
    Tk = -(-n_valid // tk) * tk
    tables = _rope_tables(P + jnp.arange(T))
    if bb > 1:
        tables = tuple(jnp.tile(t, (bb, 1)) for t in tables)
    leaves_even, leaves_odd = [], []
    for l in range(DEPTH):
        sh1, sc1, g1, sh2, sc2, g2 = [mod[l, :, k * D_MODEL:(k + 1) * D_MODEL][:, None, :] for k in range(6)]
        i = l // 2
        g_mix = wts["norm_mix"][l][None, :]
        if l % 2 == 0:
            qkv, ak, av, bk, bv = _proj_even(x, g_mix, sc1, sh1, wts["w_in_even"][i], tables, bb, tt_proj)
            leaves_even.append((ak, av, bk, bv))
            if past is None:
                srcs = [(qkv, (c * SEG) // LANES) for c in (1, 2, 4, 5)]
            else:
                caches = [past[n][i].reshape(B, P, SEG).astype(BF16) for n in ("diff_k", "diff_v", "sb_k", "sb_v")]
                srcs = [(_pad_time(jnp.concatenate([cache, qkv[:, :, c * SEG:(c + 1) * SEG]], axis=1), Tk), 0)
                        for cache, c in zip(caches, (1, 2, 4, 5))]
            lam_init = 0.8 - 0.6 * math.exp(-0.3 * l)
            lp = jnp.stack([wts["diff_lq1"][i], wts["diff_lk1"][i], wts["diff_lq2"][i], wts["diff_lk2"][i]])
            a_out = _attention("diff", qkv, 0, *srcs[0], *srcs[1], N_DIFF_HEADS, tq=tq, tk=tk, q_off=P,
                               n_valid=n_valid, extras=(lp, wts["diff_subln"][i][None, :]), lam_init=lam_init)
            b_out = _attention("sb", qkv, (3 * SEG) // LANES, *srcs[2], *srcs[3], N_SB_HEADS // 2, tq=tq, tk=tk,
                               q_off=P, n_valid=n_valid)
            x = _outproj([a_out, b_out], wts["w_out_even"][i], x, g1, bb, tt_out)
        else:
            qkv, k, v, lf = _proj_odd(x, g_mix, sc1, sh1, wts["w_in_odd"][i], wts["w_f_odd"][i],
                                      wts["b_forget"][i], bb, tt_proj)
            leaves_odd.append((k, v, lf))
            if past is None:
                srcs = [(qkv, (c * SEG) // LANES) for c in (2, 4)]
                lf_all = lf
            else:
                caches = [past[n][i].reshape(B, P, FOX_W).astype(BF16) for n in ("fox_k", "fox_v")]
                srcs = [(_pad_time(jnp.concatenate([cache, qkv[:, :, c * SEG:(c + 2) * SEG]], axis=1), Tk), 0)
                        for cache, c in zip(caches, (2, 4))]
                lf_all = jnp.concatenate([past["fox_logf"][i], lf], axis=1)
            lf_t = jnp.swapaxes(_pad_time(lf_all, Tk), 1, 2)
            nd = _neg_cumsum(lf_t, tk).reshape(B, N_FOX_HEADS // 2, 2, Tk // tk, tk)
            o = _attention("fox", qkv, 0, *srcs[0], *srcs[1], N_FOX_HEADS // 2, tq=tq, tk=tk, q_off=P,
                           n_valid=n_valid, extras=(nd,))
            x = _outproj([o], wts["w_out_odd"][i], x, g1, bb, tt_out)
        g_final = wts["norm_final"][None, :] if l == DEPTH - 1 else None
        x = _moe(x, wts["norm_ffn"][l][None, :], sc2, sh2, g2, wts["w_router"], wts["b_router"],
                 wts["w_gate"][l], wts["w_up"][l], wts["w_down"][l], g_final, bb, tt_moe)
    return x, leaves_even, leaves_odd


def kernel(x_prompt, x_sample, c_prompt, c_sample, cache_diff_k, cache_diff_v, cache_sb_k, cache_sb_v, cache_fox_k, cache_fox_v, cache_fox_logf, w_ada, b_ada, norm_mix, norm_ffn, w_in_even, w_out_even, diff_lq1, diff_lk1, diff_lq2, diff_lk2, diff_subln, w_in_odd, b_forget, w_out_odd, w_router, b_router, w_gate, w_up, w_down, norm_final):
    Bp, Tp, _ = x_prompt.shape
    Bs, Ts, _ = x_sample.shape
    wts = {
        "norm_mix": norm_mix, "norm_ffn": norm_ffn, "norm_final": norm_final,
        "w_in_even": w_in_even.astype(BF16), "w_out_even": w_out_even.astype(BF16),
        "w_in_odd": w_in_odd[:, :, :3 * FOX_W].astype(BF16), "w_f_odd": w_in_odd[:, :, 3 * FOX_W:].astype(BF16),
        "w_out_odd": w_out_odd.astype(BF16), "b_forget": b_forget,
        "diff_lq1": diff_lq1, "diff_lk1": diff_lk1, "diff_lq2": diff_lq2, "diff_lk2": diff_lk2,
        "diff_subln": diff_subln, "w_router": w_router, "b_router": b_router,
        "w_gate": w_gate.astype(BF16), "w_up": w_up.astype(BF16), "w_down": w_down.astype(BF16),
    }
    past = {"diff_k": cache_diff_k, "diff_v": cache_diff_v, "sb_k": cache_sb_k, "sb_v": cache_sb_v,
            "fox_k": cache_fox_k, "fox_v": cache_fox_v, "fox_logf": cache_fox_logf}

    rows = Bp + Bs
    rows_pad = -(-rows // 16) * 16
    c_all = jnp.pad(jnp.concatenate([c_prompt, c_sample], axis=0), ((0, rows_pad - rows), (0, 0)))
    mod = _ada_mod(c_all, w_ada, b_ada)

    cfg_prompt = (1, 256, 512, 1024, 256, 256)
    cfg_sample = (Bs, Ts, Ts, Ts, Ts, 256)
    y_p, even_p, odd_p = _trunk(x_prompt, mod[:, :Bp], None, wts, cfg_prompt)
    y_s, even_s, odd_s = _trunk(x_sample, mod[:, Bp:rows], past, wts, cfg_sample)

    def pack(B, T, even, odd):
        dk, dv, sk, sv = [jnp.stack(r, axis=0) for r in zip(*even)]
        fk, fv, fl = [jnp.stack(r, axis=0) for r in zip(*odd)]
        n_even, n_odd = dk.shape[0], fk.shape[0]
        return (dk.reshape(n_even, B, T, N_DIFF_HEADS, 2 * HEAD_DIM), dv.reshape(n_even, B, T, N_DIFF_HEADS, 2 * HEAD_DIM),
                sk.reshape(n_even, B, T, N_SB_HEADS, HEAD_DIM), sv.reshape(n_even, B, T, N_SB_HEADS, HEAD_DIM),
                fk.reshape(n_odd, B, T, N_FOX_HEADS, HEAD_DIM), fv.reshape(n_odd, B, T, N_FOX_HEADS, HEAD_DIM), fl)

    return (y_p, y_s) + pack(Bp, Tp, even_p, odd_p) + pack(Bs, Ts, even_s, odd_s)
```

```python
import functools
import math

import jax
import jax.numpy as jnp
from jax import lax
from jax.experimental import pallas as pl
from jax.experimental.pallas import tpu as pltpu

D_MODEL = 1024
DEPTH = 2
CHUNK = 64
HEAD_DIM = 64
N_DIFF_HEADS = D_MODEL // 256
N_SB_HEADS = D_MODEL // 128
N_FOX_HEADS = D_MODEL // HEAD_DIM
DIFF_W = N_DIFF_HEADS * 2 * HEAD_DIM
SB_W = N_SB_HEADS * HEAD_DIM
FOX_W = N_FOX_HEADS * HEAD_DIM
ROPE_DIM = HEAD_DIM // 4
ROPE_THETA = 500000.0
N_EXPERTS = 16
N_GROUPS = 4
EXPERTS_PER_GROUP = N_EXPERTS // N_GROUPS
D_EXPERT = D_MODEL // 2
EPS = 1e-6

LANES = 128
BF16_ROWS = 16
SEG = 512
QK_SCALE = HEAD_DIM ** -0.5
MASKED = -1e30
N_CHAIN = 4
VMEM_LIMIT = 48 * 1024 * 1024

F32 = jnp.float32
BF16 = jnp.bfloat16


def _params(*sem):
    return pltpu.CompilerParams(dimension_semantics=sem, vmem_limit_bytes=VMEM_LIMIT)


def _dot(a, b):
    return jnp.dot(a, b, preferred_element_type=F32)


def _dot_nt(a, b):
    return lax.dot_general(a, b, (((1,), (1,)), ((), ())), preferred_element_type=F32)


def _norm_mod(x, g, sc, sh):
    bb, tt, d = x.shape
    ms = jnp.mean(x * x, axis=-1, keepdims=True)
    h = x * lax.rsqrt(ms + EPS) * g
    h = h * (1.0 + sc) + sh
    return h.reshape(bb * tt, d)


def _log_sigmoid(x):
    return jnp.minimum(x, 0.0) - jnp.log(1.0 + jnp.exp(-jnp.abs(x)))


def _split3(x):
    hi = x.astype(BF16)
    r1 = x - hi.astype(F32)
    mid = r1.astype(BF16)
    lo = (r1 - mid.astype(F32)).astype(BF16)
    return hi, mid, lo


def _ada_kernel(c_ref, w_ref, b_ref, o_ref):
    c = c_ref[...]
    a = (c * jax.nn.sigmoid(c)).astype(BF16)
    o_ref[0] = _dot(a, w_ref[0].astype(BF16)) + b_ref[0]


def _ada_mod(c_all, w_ada, b_ada):
    rows = c_all.shape[0]
    tn = 1536
    width = 6 * D_MODEL
    return pl.pallas_call(
        _ada_kernel,
        grid=(DEPTH, width // tn),
        in_specs=[
            pl.BlockSpec((rows, D_MODEL), lambda l, j: (0, 0)),
            pl.BlockSpec((1, D_MODEL, tn), lambda l, j: (l, 0, j)),
            pl.BlockSpec((1, 1, tn), lambda l, j: (l, 0, j)),
        ],
        out_specs=pl.BlockSpec((1, rows, tn), lambda l, j: (l, 0, j)),
        out_shape=jax.ShapeDtypeStruct((DEPTH, rows, width), F32),
        compiler_params=_params("arbitrary", "arbitrary"),
        name="ada_mod",
    )(c_all, w_ada, b_ada.reshape(DEPTH, 1, width))


def _rope(p, cos, s1, s2):
    outs = []
    for c in range(SEG // LANES):
        pc = p[:, c * LANES:(c + 1) * LANES]
        up = pltpu.roll(pc, LANES - ROPE_DIM // 2, 1)
        down = pltpu.roll(pc, ROPE_DIM // 2, 1)
        outs.append(pc * cos + up * s1 + down * s2)
    return jnp.concatenate(outs, axis=1)


def _proj_even_kernel(x_ref, g_ref, sc_ref, sh_ref, w_ref, cos_ref, s1_ref, s2_ref,
                      qkv_ref, ak_ref, av_ref, bk_ref, bv_ref):
    bb, tt, _ = x_ref.shape
    h = _norm_mod(x_ref[...], g_ref[...], sc_ref[...], sh_ref[...]).astype(BF16)
    cos, s1, s2 = cos_ref[...], s1_ref[...], s2_ref[...]

    def seg(i):
        return _dot(h, w_ref[:, i * SEG:(i + 1) * SEG])

    def put(i, val, leaf_ref):
        val3 = val.reshape(bb, tt, SEG)
        qkv_ref[:, :, i * SEG:(i + 1) * SEG] = val3.astype(BF16)
        if leaf_ref is not None:
            leaf_ref[...] = val3

    put(0, _rope(seg(0), cos, s1, s2) * QK_SCALE, None)
    put(1, _rope(seg(1), cos, s1, s2), ak_ref)
    put(2, seg(2), av_ref)
    put(3, seg(3) * QK_SCALE, None)
    put(4, seg(4), bk_ref)
    put(5, seg(5), bv_ref)


def _proj_odd_kernel(x_ref, g_ref, sc_ref, sh_ref, w_ref, wf_ref, bf_ref,
                     qkv_ref, k_ref, v_ref, lf_ref):
    bb, tt, _ = x_ref.shape
    h = _norm_mod(x_ref[...], g_ref[...], sc_ref[...], sh_ref[...]).astype(BF16)

    def seg(i):
        return _dot(h, w_ref[:, i * SEG:(i + 1) * SEG]).reshape(bb, tt, SEG)

    for i in range(2):
        qkv_ref[:, :, i * SEG:(i + 1) * SEG] = (seg(i) * QK_SCALE).astype(BF16)
    for i in range(2, 6):
        val = seg(i)
        qkv_ref[:, :, i * SEG:(i + 1) * SEG] = val.astype(BF16)
        leaf_ref = k_ref if i < 4 else v_ref
        leaf_ref[:, :, (i % 2) * SEG:(i % 2 + 1) * SEG] = val
    fl = _dot(h, wf_ref[...]) + bf_ref[...]
    lf_ref[...] = _log_sigmoid(fl).reshape(bb, tt, N_FOX_HEADS)


def _row_specs(bb, tt):
    x_spec = pl.BlockSpec((bb, tt, D_MODEL), lambda i, j: (i, j, 0))
    g_spec = pl.BlockSpec((1, D_MODEL), lambda i, j: (0, 0))
    m_spec = pl.BlockSpec((bb, 1, D_MODEL), lambda i, j: (i, 0, 0))
    return x_spec, g_spec, m_spec


def _out_spec(bb, tt, width):
    return pl.BlockSpec((bb, tt, width), lambda i, j: (i, j, 0))


def _proj_even(x, g, sc, sh, w_bf, tables, bb, tt):
    B, T, _ = x.shape
    x_spec, g_spec, m_spec = _row_specs(bb, tt)
    tm = bb * tt
    t_spec = pl.BlockSpec((tm, LANES), (lambda i, j: (j, 0)) if bb == 1 else (lambda i, j: (0, 0)))
    leaf = jax.ShapeDtypeStruct((B, T, SEG), F32)
    return pl.pallas_call(
        _proj_even_kernel,
        grid=(B // bb, T // tt),
        in_specs=[x_spec, g_spec, m_spec, m_spec,
                  pl.BlockSpec(w_bf.shape, lambda i, j: (0, 0)), t_spec, t_spec, t_spec],
        out_specs=[_out_spec(bb, tt, 6 * SEG)] + [_out_spec(bb, tt, SEG)] * 4,
        out_shape=[jax.ShapeDtypeStruct((B, T, 6 * SEG), BF16), leaf, leaf, leaf, leaf],
        compiler_params=_params("arbitrary", "arbitrary"),
        name="proj_even",
    )(x, g, sc, sh, w_bf, *tables)


def _proj_odd(x, g, sc, sh, w_bf, wf_bf, b_f, bb, tt):
    B, T, _ = x.shape
    x_spec, g_spec, m_spec = _row_specs(bb, tt)
    leaf = jax.ShapeDtypeStruct((B, T, FOX_W), F32)
    return pl.pallas_call(
        _proj_odd_kernel,
        grid=(B // bb, T // tt),
        in_specs=[x_spec, g_spec, m_spec, m_spec,
                  pl.BlockSpec(w_bf.shape, lambda i, j: (0, 0)),
                  pl.BlockSpec(wf_bf.shape, lambda i, j: (0, 0)),
                  pl.BlockSpec((1, N_FOX_HEADS), lambda i, j: (0, 0))],
        out_specs=[_out_spec(bb, tt, 6 * SEG), _out_spec(bb, tt, FOX_W), _out_spec(bb, tt, FOX_W),
                   _out_spec(bb, tt, N_FOX_HEADS)],
        out_shape=[jax.ShapeDtypeStruct((B, T, 6 * SEG), BF16), leaf, leaf,
                   jax.ShapeDtypeStruct((B, T, N_FOX_HEADS), F32)],
        compiler_params=_params("arbitrary", "arbitrary"),
        name="proj_odd",
    )(x, g, sc, sh, w_bf, wf_bf, b_f.reshape(1, N_FOX_HEADS))


def _attn_kernel(*refs, kind, n_chain, tq, tk, q_off, n_valid, lam_init):
    if kind == "diff":
        q_ref, k_ref, vt_ref, lp_ref, sg_ref, o_ref, acc_ref, m_ref = refs
    elif kind == "fox":
        q_ref, k_ref, vt_ref, nd_ref, o_ref, acc_ref, m_ref = refs
    else:
        q_ref, k_ref, vt_ref, o_ref, acc_ref, m_ref = refs
    tq2 = 2 * tq
    q0 = q_off + pl.program_id(2) * tq

    low = lax.broadcasted_iota(jnp.int32, (tq, LANES), 1) < HEAD_DIM
    if kind == "fox":
        row = lax.broadcasted_iota(jnp.int32, (tq2, LANES), 0)
        col = lax.broadcasted_iota(jnp.int32, (tq2, LANES), 1)
        pick = jnp.logical_and(jnp.right_shift(row, tq.bit_length() - 1) * 3 <= col,
                               col < jnp.right_shift(row, tq.bit_length() - 1) * 3 + 3)
        bias_lanes = jnp.where(pick, 1.0, 0.0).astype(BF16)
    qqs = []
    for c in range(n_chain):
        q = q_ref[0, :, c * LANES:(c + 1) * LANES]
        zero = jnp.zeros_like(q)
        qq = jnp.concatenate([jnp.where(low, q, zero), jnp.where(low, zero, q)], axis=0)
        if kind == "fox":
            qq = jnp.concatenate([qq, bias_lanes], axis=1)
        qqs.append(qq)

    acc_ref[...] = jnp.zeros_like(acc_ref)
    if kind == "sb":
        m_ref[...] = jnp.zeros_like(m_ref)
        tri_r = lax.broadcasted_iota(jnp.int32, (tk, tk), 0)
        tri_c = lax.broadcasted_iota(jnp.int32, (tk, tk), 1)
        after = jnp.where(tri_c > tri_r, 1.0, 0.0).astype(BF16)
        after2 = jnp.concatenate([after, after], axis=1)
    else:
        m_ref[...] = jnp.full_like(m_ref, MASKED)
        ones_rows = jnp.ones((BF16_ROWS, tk), BF16)

    qpos = q0 + jnp.bitwise_and(lax.broadcasted_iota(jnp.int32, (tk, tq2), 1), tq - 1)
    kidx = lax.broadcasted_iota(jnp.int32, (tk, tq2), 0)

    def block(kb, masked):
        start = pl.multiple_of(kb * tk, tk)
        if masked:
            kpos = start + kidx
            if kind == "diff":
                vis = jnp.right_shift(kpos, 6) <= jnp.right_shift(qpos, 6)
            elif kind == "fox":
                vis = kpos <= qpos
            else:
                vis = kpos < qpos
            vis = jnp.logical_and(vis, kpos < n_valid)
        chains = range(n_chain)
        scores = []
        for c in chains:
            k = k_ref[0, pl.ds(start, tk), c * LANES:(c + 1) * LANES]
            if kind == "fox":
                k = jnp.concatenate([k, nd_ref[0, c, pl.ds(start, tk), :]], axis=1)
            scores.append(_dot_nt(k, qqs[c]))
        vts = [vt_ref[0, c, kb] for c in chains]
        if kind == "sb":
            log_betas, laters = [], []
            for c in chains:
                s = scores[c]
                soft = jnp.log(1.0 + jnp.exp(-jnp.abs(s)))
                log_beta = jnp.minimum(s, 0.0) - soft
                log_keep = log_beta - s
                if masked:
                    log_keep = jnp.where(vis, log_keep, 0.0)
                hi = log_keep.astype(BF16)
                lo = (log_keep - hi.astype(F32)).astype(BF16)
                laters.append(_dot(after2, jnp.concatenate([hi, lo], axis=0)) + m_ref[c])
                m_ref[c] += jnp.sum(log_keep, axis=0, keepdims=True)
                log_betas.append(log_beta)
            weights = []
            for c in chains:
                w = jnp.exp(log_betas[c] + laters[c])
                if masked:
                    w = jnp.where(vis, w, 0.0)
                weights.append(w.astype(BF16))
            for c in chains:
                acc_ref[c] += _dot(vts[c], weights[c])
        else:
            probs, alphas = [], []
            for c in chains:
                s = scores[c]
                if masked:
                    s = jnp.where(vis, s, MASKED)
                m_prev = m_ref[c]
                m_new = jnp.maximum(m_prev, jnp.max(s, axis=0, keepdims=True))
                alphas.append(jnp.exp(m_prev - m_new))
                probs.append(jnp.exp(s - m_new).astype(BF16))
                m_ref[c] = m_new
            for c in chains:
                vt_aug = jnp.concatenate([vts[c], ones_rows], axis=0)
                acc_ref[c] = alphas[c] * acc_ref[c] + _dot(vt_aug, probs[c])

    n_full = q0 // tk
    block(n_full, True)

    def body(i, carry):
        block(n_full - 1 - i, False)
        return carry

    lax.fori_loop(0, n_full, body, 0)

    if kind == "diff":
        lp = lp_ref[...]
        lam = (jnp.exp(jnp.sum(lp[0:1] * lp[1:2], axis=1, keepdims=True))
               - jnp.exp(jnp.sum(lp[2:3] * lp[3:4], axis=1, keepdims=True)) + lam_init)
    for c in range(n_chain):
        acc = acc_ref[c]
        if kind == "sb":
            out_t = jnp.concatenate([acc[0:HEAD_DIM, 0:tq], acc[HEAD_DIM:LANES, tq:tq2]], axis=0)
        else:
            r = 1.0 / acc[LANES:LANES + 1, :]
            if kind == "fox":
                out_t = jnp.concatenate([acc[0:HEAD_DIM, 0:tq] * r[:, 0:tq],
                                         acc[HEAD_DIM:LANES, tq:tq2] * r[:, tq:tq2]], axis=0)
            else:
                a = acc[0:LANES, 0:tq] * r[:, 0:tq] - lam * (acc[0:LANES, tq:tq2] * r[:, tq:tq2])
                ms = jnp.mean(a * a, axis=0, keepdims=True)
                out_t = a * lax.rsqrt(ms + EPS) * sg_ref[...] * (1.0 - lam_init)
        if tq < LANES:
            out_t = jnp.concatenate([out_t, jnp.zeros((LANES, LANES - tq), F32)], axis=1)
        o_ref[0, :, c * LANES:(c + 1) * LANES] = out_t.T[0:tq].astype(o_ref.dtype)


def _attention(kind, q_arr, q_col, k_arr, k_col, vt_arr, n_col, *, n_chain, tq, tk, q_off, n_valid,
               extras=(), lam_init=0.0):
    B, Tq, _ = q_arr.shape
    Tk = k_arr.shape[1]
    n_kb = Tk // tk
    assert Tq % tq == 0 and Tk % tk == 0 and tk % tq == 0 and q_off % tk == 0 and n_valid <= Tk
    assert tq & (tq - 1) == 0 and n_col % n_chain == 0 and q_col % n_chain == 0 and k_col % n_chain == 0
    wide = n_chain * LANES
    in_specs = [
        pl.BlockSpec((1, tq, wide), lambda b, c, i: (b, i, q_col // n_chain + c)),
        pl.BlockSpec((1, Tk, wide), lambda b, c, i: (b, 0, k_col // n_chain + c)),
        pl.BlockSpec((1, n_chain, n_kb, LANES, tk), lambda b, c, i: (b, c, 0, 0, 0)),
    ]
    acc_rows = LANES if kind == "sb" else LANES + BF16_ROWS
    scratch = [pltpu.VMEM((n_chain, acc_rows, 2 * tq), F32), pltpu.VMEM((n_chain, 1, 2 * tq), F32)]
    if kind == "diff":
        lp, sg = extras
        in_specs += [pl.BlockSpec(lp.shape, lambda b, c, i: (0, 0)),
                     pl.BlockSpec(sg.shape, lambda b, c, i: (0, 0))]
    elif kind == "fox":
        (nd,) = extras
        in_specs.append(pl.BlockSpec((1, n_chain, Tk, LANES), lambda b, c, i: (b, c, 0, 0)))
    kernel = functools.partial(_attn_kernel, kind=kind, n_chain=n_chain, tq=tq, tk=tk, q_off=q_off,
                               n_valid=n_valid, lam_init=lam_init)
    return pl.pallas_call(
        kernel,
        grid=(B, n_col // n_chain, Tq // tq),
        in_specs=in_specs,
        out_specs=pl.BlockSpec((1, tq, wide), lambda b, c, i: (b, i, c)),
        out_shape=jax.ShapeDtypeStruct((B, Tq, n_col * LANES), BF16),
        scratch_shapes=scratch,
        compiler_params=_params("arbitrary", "arbitrary", "arbitrary"),
        name="attn_" + kind,
    )(q_arr, k_arr, vt_arr, *extras)


def _key_blocks_t(v, tk):
    B, Tk, W = v.shape
    return v.reshape(B, Tk // tk, tk, W // LANES, LANES).transpose(0, 3, 1, 4, 2)


def _cumsum_kernel(x_ref, hi_ref, mid_ref, lo_ref, carry_ref):
    blk = x_ref.shape[1]

    @pl.when(pl.program_id(1) == 0)
    def _():
        carry_ref[...] = jnp.zeros_like(carry_ref)

    r = lax.broadcasted_iota(jnp.int32, (blk, blk), 0)
    c = lax.broadcasted_iota(jnp.int32, (blk, blk), 1)
    upto = jnp.where(c <= r, 1.0, 0.0).astype(BF16)
    hi, mid, lo = _split3(x_ref[0])
    total = _dot(upto, hi) + _dot(upto, mid) + _dot(upto, lo) + carry_ref[...]
    carry_ref[...] = total[blk - 1:blk, :]
    hi_ref[0], mid_ref[0], lo_ref[0] = _split3(-total)


def _neg_cumsum(lf, blk):
    B, T, H = lf.shape
    spec = pl.BlockSpec((1, blk, H), lambda b, j: (b, j, 0))
    piece = jax.ShapeDtypeStruct((B, T, H), BF16)
    return pl.pallas_call(
        _cumsum_kernel,
        grid=(B, T // blk),
        in_specs=[spec],
        out_specs=[spec, spec, spec],
        out_shape=[piece, piece, piece],
        scratch_shapes=[pltpu.VMEM((1, H), F32)],
        compiler_params=_params("arbitrary", "arbitrary"),
        name="neg_cumsum",
    )(lf)


def _outproj_kernel(*refs, nseg):
    a_refs = refs[:nseg]
    w_ref, x_ref, g_ref, o_ref = refs[nseg:]
    bb, tt, d = x_ref.shape
    acc = None
    off = 0
    for a_ref in a_refs:
        width = a_ref.shape[2]
        part = _dot(a_ref[...].reshape(bb * tt, width), w_ref[off:off + width, :])
        acc = part if acc is None else acc + part
        off += width
    o_ref[...] = x_ref[...] + g_ref[...] * acc.reshape(bb, tt, d)


def _outproj(mixed, w_bf, x, gate, bb, tt):
    B, T, _ = x.shape
    x_spec, _, m_spec = _row_specs(bb, tt)
    return pl.pallas_call(
        functools.partial(_outproj_kernel, nseg=len(mixed)),
        grid=(B // bb, T // tt),
        in_specs=[_out_spec(bb, tt, a.shape[2]) for a in mixed]
        + [pl.BlockSpec(w_bf.shape, lambda i, j: (0, 0)), x_spec, m_spec],
        out_specs=x_spec,
        out_shape=jax.ShapeDtypeStruct(x.shape, F32),
        compiler_params=_params("arbitrary", "arbitrary"),
        name="outproj",
    )(*mixed, w_bf, x, gate)


def _route(h, wr, br):
    tm = h.shape[0]
    logits = jnp.dot(h, wr, preferred_element_type=F32, precision=lax.Precision.HIGHEST)
    s = jax.nn.sigmoid(logits)
    sel = s + br
    lane = lax.broadcasted_iota(jnp.int32, (tm, N_EXPERTS), 1).astype(F32)
    neg = -jnp.inf

    def first_argmax(vals):
        top = jnp.max(vals, axis=1, keepdims=True)
        idx = jnp.min(jnp.where(vals == top, lane, float(N_EXPERTS)), axis=1, keepdims=True)
        return top, idx

    best = None
    for g in range(N_GROUPS):
        in_g = (lane >= g * EXPERTS_PER_GROUP) & (lane < (g + 1) * EXPERTS_PER_GROUP)
        top1, i1 = first_argmax(jnp.where(in_g, sel, neg))
        top2, i2 = first_argmax(jnp.where(in_g & (lane != i1), sel, neg))
        score = top1 + top2
        if best is None:
            best = (score, i1, i2)
        else:
            better = score > best[0]
            best = tuple(jnp.where(better, new, old) for new, old in zip((score, i1, i2), best))
    _, i1, i2 = best
    picked = jnp.where((lane == i1) | (lane == i2), s, 0.0)
    return picked / jnp.sum(picked, axis=1, keepdims=True)


def _moe_kernel(*refs, final):
    if final:
        (x_ref, g_ref, sc_ref, sh_ref, g2_ref, wr_ref, br_ref, wg_ref, wu_ref, wd_ref, gf_ref,
         o_ref, h_ref, gate_ref, acc_ref) = refs
    else:
        (x_ref, g_ref, sc_ref, sh_ref, g2_ref, wr_ref, br_ref, wg_ref, wu_ref, wd_ref,
         o_ref, h_ref, gate_ref, acc_ref) = refs
    bb, tt, d = x_ref.shape
    e = pl.program_id(2)

    @pl.when(e == 0)
    def _():
        h = _norm_mod(x_ref[...], g_ref[...], sc_ref[...], sh_ref[...])
        h_ref[...] = h.astype(BF16)
        gate_ref[...] = _route(h, wr_ref[...], br_ref[...])
        acc_ref[...] = jnp.zeros_like(acc_ref)

    h = h_ref[...]
    gt = _dot(h, wg_ref[0])
    up = _dot(h, wu_ref[0])
    gates = gate_ref[...]
    lane = lax.broadcasted_iota(jnp.int32, gates.shape, 1)
    ge = jnp.sum(jnp.where(lane == e, gates, 0.0), axis=1, keepdims=True)
    a = (gt * jax.nn.sigmoid(gt)) * up * ge
    acc_ref[...] += _dot(a.astype(BF16), wd_ref[0])

    @pl.when(e == N_EXPERTS - 1)
    def _():
        y = x_ref[...] + g2_ref[...] * acc_ref[...].reshape(bb, tt, d)
        if final:
            ms = jnp.mean(y * y, axis=-1, keepdims=True)
            y = y * lax.rsqrt(ms + EPS) * gf_ref[...]
        o_ref[...] = y


def _moe(x, g, sc, sh, g2, w_router, b_router, wg_bf, wu_bf, wd_bf, g_final, bb, tt):
    B, T, _ = x.shape
    final = g_final is not None
    x_spec = pl.BlockSpec((bb, tt, D_MODEL), lambda i, j, e: (i, j, 0))
    g_spec = pl.BlockSpec((1, D_MODEL), lambda i, j, e: (0, 0))
    m_spec = pl.BlockSpec((bb, 1, D_MODEL), lambda i, j, e: (i, 0, 0))
    in_specs = [x_spec, g_spec, m_spec, m_spec, m_spec,
                pl.BlockSpec((D_MODEL, N_EXPERTS), lambda i, j, e: (0, 0)),
                pl.BlockSpec((1, N_EXPERTS), lambda i, j, e: (0, 0)),
                pl.BlockSpec((1, D_MODEL, D_EXPERT), lambda i, j, e: (e, 0, 0)),
                pl.BlockSpec((1, D_MODEL, D_EXPERT), lambda i, j, e: (e, 0, 0)),
                pl.BlockSpec((1, D_EXPERT, D_MODEL), lambda i, j, e: (e, 0, 0))]
    args = [x, g, sc, sh, g2, w_router, b_router.reshape(1, N_EXPERTS), wg_bf, wu_bf, wd_bf]
    if final:
        in_specs.append(g_spec)
        args.append(g_final)
    tm = bb * tt
    return pl.pallas_call(
        functools.partial(_moe_kernel, final=final),
        grid=(B // bb, T // tt, N_EXPERTS),
        in_specs=in_specs,
        out_specs=x_spec,
        out_shape=jax.ShapeDtypeStruct(x.shape, F32),
        scratch_shapes=[pltpu.VMEM((tm, D_MODEL), BF16), pltpu.VMEM((tm, N_EXPERTS), F32),
                        pltpu.VMEM((tm, D_MODEL), F32)],
        compiler_params=_params("arbitrary", "arbitrary", "arbitrary"),
        name="moe",
    )(*args)


def _rope_tables(pos):
    half = ROPE_DIM // 2
    inv = ROPE_THETA ** (-jnp.arange(half, dtype=F32) / half)
    ang = pos.astype(F32)[:, None] * inv[None, :]
    cos, sin = jnp.cos(ang), jnp.sin(ang)
    ones = jnp.ones((pos.shape[0], HEAD_DIM - ROPE_DIM), F32)
    zeros = jnp.zeros_like(ones)
    zh = jnp.zeros_like(sin)
    c64 = jnp.concatenate([cos, cos, ones], axis=1)
    s1_64 = jnp.concatenate([-sin, zh, zeros], axis=1)
    s2_64 = jnp.concatenate([zh, sin, zeros], axis=1)
    return tuple(jnp.concatenate([t, t], axis=1) for t in (c64, s1_64, s2_64))


def _pad_time(a, total):
    return jnp.pad(a, ((0, 0), (0, total - a.shape[1])) + ((0, 0),) * (a.ndim - 2))


def _trunk(x, mod, past, wts, cfg):
    B, T, _ = x.shape
    P = 0 if past is None else past["diff_k"].shape[2]
    bb, tt_proj, tt_out, tt_moe, tq, tk = cfg
    n_valid = P + T
    Tk = -(-n_valid // tk) * tk
    Tq = -(-T // tq) * tq
    tables = _rope_tables(P + jnp.arange(T))
    if bb > 1:
        tables = tuple(jnp.tile(t, (bb, 1)) for t in tables)
    attn = functools.partial(_attention, n_chain=N_CHAIN, tq=tq, tk=tk, q_off=P, n_valid=n_valid)
    leaves_even, leaves_odd = [], []

    def keys_values(qkv, k_cols, v_cols, cache_k, cache_v):
        v_new = qkv[:, :, v_cols[0]:v_cols[1]]
        if past is None:
            return qkv, k_cols[0] // LANES, _key_blocks_t(v_new, tk)
        width = k_cols[1] - k_cols[0]
        k_all = jnp.concatenate([cache_k.reshape(B, P, width).astype(BF16), qkv[:, :, k_cols[0]:k_cols[1]]], axis=1)
        v_all = jnp.concatenate([cache_v.reshape(B, P, width).astype(BF16), v_new], axis=1)
        return _pad_time(k_all, Tk), 0, _key_blocks_t(_pad_time(v_all, Tk), tk)

    for l in range(DEPTH):
        sh1, sc1, g1, sh2, sc2, g2 = [mod[l, :, k * D_MODEL:(k + 1) * D_MODEL][:, None, :] for k in range(6)]
        i = l // 2
        g_mix = wts["norm_mix"][l][None, :]
        if l % 2 == 0:
            qkv, ak, av, bk, bv = _proj_even(x, g_mix, sc1, sh1, wts["w_in_even"][i], tables, bb, tt_proj)
            leaves_even.append((ak, av, bk, bv))
            q_arr = _pad_time(qkv, Tq)
            dk, dk_col, dvt = keys_values(qkv, (SEG, 2 * SEG), (2 * SEG, 3 * SEG),
                                          None if past is None else past["diff_k"][i],
                                          None if past is None else past["diff_v"][i])
            sk, sk_col, svt = keys_values(qkv, (4 * SEG, 5 * SEG), (5 * SEG, 6 * SEG),
                                          None if past is None else past["sb_k"][i],
                                          None if past is None else past["sb_v"][i])
            lam_init = 0.8 - 0.6 * math.exp(-0.3 * l)
            lp = jnp.stack([wts["diff_lq1"][i], wts["diff_lk1"][i], wts["diff_lq2"][i], wts["diff_lk2"][i]])
            a_out = attn("diff", q_arr, 0, dk, dk_col, dvt, N_DIFF_HEADS,
                         extras=(lp, wts["diff_subln"][i][:, None]), lam_init=lam_init)
            b_out = attn("sb", q_arr, (3 * SEG) // LANES, sk, sk_col, svt, N_SB_HEADS // 2)
            x = _outproj([a_out[:, :T], b_out[:, :T]], wts["w_out_even"][i], x, g1, bb, tt_out)
        else:
            qkv, k, v, lf = _proj_odd(x, g_mix, sc1, sh1, wts["w_in_odd"][i], wts["w_f_odd"][i],
                                      wts["b_forget"][i], bb, tt_proj)
            leaves_odd.append((k, v, lf))
            q_arr = _pad_time(qkv, Tq)
            fk, fk_col, fvt = keys_values(qkv, (2 * SEG, 4 * SEG), (4 * SEG, 6 * SEG),
                                          None if past is None else past["fox_k"][i],
                                          None if past is None else past["fox_v"][i])
            lf_all = lf if past is None else jnp.concatenate([past["fox_logf"][i], lf], axis=1)
            pieces = jnp.stack(_neg_cumsum(_pad_time(lf_all, Tk), tk), axis=-1)
            nd = pieces.reshape(B, Tk, N_FOX_HEADS // 2, 6).transpose(0, 2, 1, 3)
            nd = jnp.pad(nd, ((0, 0), (0, 0), (0, 0), (0, LANES - 6)))
            o = attn("fox", q_arr, 0, fk, fk_col, fvt, N_FOX_HEADS // 2, extras=(nd,))
            x = _outproj([o[:, :T]], wts["w_out_odd"][i], x, g1, bb, tt_out)
        g_final = wts["norm_final"][None, :] if l == DEPTH - 1 else None
        x = _moe(x, wts["norm_ffn"][l][None, :], sc2, sh2, g2, wts["w_router"], wts["b_router"],
                 wts["w_gate"][l], wts["w_up"][l], wts["w_down"][l], g_final, bb, tt_moe)
    return x, leaves_even, leaves_odd


def kernel(x_prompt, x_sample, c_prompt, c_sample, cache_diff_k, cache_diff_v, cache_sb_k, cache_sb_v, cache_fox_k, cache_fox_v, cache_fox_logf, w_ada, b_ada, norm_mix, norm_ffn, w_in_even, w_out_even, diff_lq1, diff_lk1, diff_lq2, diff_lk2, diff_subln, w_in_odd, b_forget, w_out_odd, w_router, b_router, w_gate, w_up, w_down, norm_final):
    Bp, Tp, _ = x_prompt.shape
    Bs, Ts, _ = x_sample.shape
    wts = {
        "norm_mix": norm_mix, "norm_ffn": norm_ffn, "norm_final": norm_final,
        "w_in_even": w_in_even.astype(BF16), "w_out_even": w_out_even.astype(BF16),
        "w_in_odd": w_in_odd[:, :, :3 * FOX_W].astype(BF16), "w_f_odd": w_in_odd[:, :, 3 * FOX_W:].astype(BF16),
        "w_out_odd": w_out_odd.astype(BF16), "b_forget": b_forget,
        "diff_lq1": diff_lq1, "diff_lk1": diff_lk1, "diff_lq2": diff_lq2, "diff_lk2": diff_lk2,
        "diff_subln": diff_subln, "w_router": w_router, "b_router": b_router,
        "w_gate": w_gate.astype(BF16), "w_up": w_up.astype(BF16), "w_down": w_down.astype(BF16),
    }
    past = {"diff_k": cache_diff_k, "diff_v": cache_diff_v, "sb_k": cache_sb_k, "sb_v": cache_sb_v,
            "fox_k": cache_fox_k, "fox_v": cache_fox_v, "fox_logf": cache_fox_logf}

    rows = Bp + Bs
    rows_pad = -(-rows // 16) * 16
    c_all = jnp.pad(jnp.concatenate([c_prompt, c_sample], axis=0), ((0, rows_pad - rows), (0, 0)))
    mod = _ada_mod(c_all, w_ada, b_ada)

    cfg_prompt = (1, 256, 512, 1024, 128, 256)
    cfg_sample = (Bs, Ts, Ts, Ts, 64, 256)
    y_p, even_p, odd_p = _trunk(x_prompt, mod[:, :Bp], None, wts, cfg_prompt)
    y_s, even_s, odd_s = _trunk(x_sample, mod[:, Bp:rows], past, wts, cfg_sample)

    def pack(B, T, even, odd):
        dk, dv, sk, sv = [jnp.stack(r, axis=0) for r in zip(*even)]
        fk, fv, fl = [jnp.stack(r, axis=0) for r in zip(*odd)]
        n_even, n_odd = dk.shape[0], fk.shape[0]
        return (dk.reshape(n_even, B, T, N_DIFF_HEADS, 2 * HEAD_DIM), dv.reshape(n_even, B, T, N_DIFF_HEADS, 2 * HEAD_DIM),
                sk.reshape(n_even, B, T, N_SB_HEADS, HEAD_DIM), sv.reshape(n_even, B, T, N_SB_HEADS, HEAD_DIM),
                fk.reshape(n_odd, B, T, N_FOX_HEADS, HEAD_DIM), fv.reshape(n_odd, B, T, N_FOX_HEADS, HEAD_DIM), fl)

    return (y_p, y_s) + pack(Bp, Tp, even_p, odd_p) + pack(Bs, Ts, even_s, odd_s)
```

```python
import functools
import math

import jax
import jax.numpy as jnp
from jax import lax
from jax.experimental import pallas as pl
from jax.experimental.pallas import tpu as pltpu

D_MODEL = 1024
DEPTH = 2
CHUNK = 64
HEAD_DIM = 64
N_DIFF_HEADS = D_MODEL // 256
N_SB_HEADS = D_MODEL // 128
N_FOX_HEADS = D_MODEL // HEAD_DIM
DIFF_W = N_DIFF_HEADS * 2 * HEAD_DIM
SB_W = N_SB_HEADS * HEAD_DIM
FOX_W = N_FOX_HEADS * HEAD_DIM
ROPE_DIM = HEAD_DIM // 4
ROPE_THETA = 500000.0
N_EXPERTS = 16
N_GROUPS = 4
EXPERTS_PER_GROUP = N_EXPERTS // N_GROUPS
D_EXPERT = D_MODEL // 2
EPS = 1e-6

LANES = 128
BF16_ROWS = 16
SEG = 512
QK_SCALE = HEAD_DIM ** -0.5
MASKED = -1e30
N_CHAIN = 4
SB_DEAD = -110.0
VMEM_LIMIT = 48 * 1024 * 1024

F32 = jnp.float32
BF16 = jnp.bfloat16


def _params(*sem):
    return pltpu.CompilerParams(dimension_semantics=sem, vmem_limit_bytes=VMEM_LIMIT)


def _dot(a, b):
    return jnp.dot(a, b, preferred_element_type=F32)


def _dot_nt(a, b):
    return lax.dot_general(a, b, (((1,), (1,)), ((), ())), preferred_element_type=F32)


def _norm_mod(x, g, sc, sh):
    bb, tt, d = x.shape
    ms = jnp.mean(x * x, axis=-1, keepdims=True)
    h = x * lax.rsqrt(ms + EPS) * g
    h = h * (1.0 + sc) + sh
    return h.reshape(bb * tt, d)


def _log_sigmoid(x):
    return jnp.minimum(x, 0.0) - jnp.log(1.0 + jnp.exp(-jnp.abs(x)))


def _split3(x):
    hi = x.astype(BF16)
    r1 = x - hi.astype(F32)
    mid = r1.astype(BF16)
    lo = (r1 - mid.astype(F32)).astype(BF16)
    return hi, mid, lo


def _ada_kernel(c_ref, w_ref, b_ref, o_ref):
    c = c_ref[...]
    a = (c * jax.nn.sigmoid(c)).astype(BF16)
    o_ref[0] = _dot(a, w_ref[0].astype(BF16)) + b_ref[0]


def _ada_mod(c_all, w_ada, b_ada):
    rows = c_all.shape[0]
    tn = 1536
    width = 6 * D_MODEL
    return pl.pallas_call(
        _ada_kernel,
        grid=(DEPTH, width // tn),
        in_specs=[
            pl.BlockSpec((rows, D_MODEL), lambda l, j: (0, 0)),
            pl.BlockSpec((1, D_MODEL, tn), lambda l, j: (l, 0, j)),
            pl.BlockSpec((1, 1, tn), lambda l, j: (l, 0, j)),
        ],
        out_specs=pl.BlockSpec((1, rows, tn), lambda l, j: (l, 0, j)),
        out_shape=jax.ShapeDtypeStruct((DEPTH, rows, width), F32),
        compiler_params=_params("arbitrary", "arbitrary"),
        name="ada_mod",
    )(c_all, w_ada, b_ada.reshape(DEPTH, 1, width))


def _rope(p, cos, s1, s2):
    outs = []
    for c in range(SEG // LANES):
        pc = p[:, c * LANES:(c + 1) * LANES]
        up = pltpu.roll(pc, LANES - ROPE_DIM // 2, 1)
        down = pltpu.roll(pc, ROPE_DIM // 2, 1)
        outs.append(pc * cos + up * s1 + down * s2)
    return jnp.concatenate(outs, axis=1)


def _proj_even_kernel(x_ref, g_ref, sc_ref, sh_ref, w_ref, cos_ref, s1_ref, s2_ref,
                      qkv_ref, ak_ref, av_ref, bk_ref, bv_ref):
    bb, tt, _ = x_ref.shape
    h = _norm_mod(x_ref[...], g_ref[...], sc_ref[...], sh_ref[...]).astype(BF16)
    cos, s1, s2 = cos_ref[...], s1_ref[...], s2_ref[...]

    def seg(i):
        return _dot(h, w_ref[:, i * SEG:(i + 1) * SEG])

    def put(i, val, leaf_ref):
        val3 = val.reshape(bb, tt, SEG)
        qkv_ref[:, :, i * SEG:(i + 1) * SEG] = val3.astype(BF16)
        if leaf_ref is not None:
            leaf_ref[...] = val3

    put(0, _rope(seg(0), cos, s1, s2) * QK_SCALE, None)
    put(1, _rope(seg(1), cos, s1, s2), ak_ref)
    put(2, seg(2), av_ref)
    put(3, seg(3) * QK_SCALE, None)
    put(4, seg(4), bk_ref)
    put(5, seg(5), bv_ref)


def _proj_odd_kernel(x_ref, g_ref, sc_ref, sh_ref, w_ref, wf_ref, bf_ref,
                     qkv_ref, k_ref, v_ref, lf_ref):
    bb, tt, _ = x_ref.shape
    h = _norm_mod(x_ref[...], g_ref[...], sc_ref[...], sh_ref[...]).astype(BF16)

    def seg(i):
        return _dot(h, w_ref[:, i * SEG:(i + 1) * SEG]).reshape(bb, tt, SEG)

    for i in range(2):
        qkv_ref[:, :, i * SEG:(i + 1) * SEG] = (seg(i) * QK_SCALE).astype(BF16)
    for i in range(2, 6):
        val = seg(i)
        qkv_ref[:, :, i * SEG:(i + 1) * SEG] = val.astype(BF16)
        leaf_ref = k_ref if i < 4 else v_ref
        leaf_ref[:, :, (i % 2) * SEG:(i % 2 + 1) * SEG] = val
    fl = _dot(h, wf_ref[...]) + bf_ref[...]
    lf_ref[...] = _log_sigmoid(fl).reshape(bb, tt, N_FOX_HEADS)


def _row_specs(bb, tt):
    x_spec = pl.BlockSpec((bb, tt, D_MODEL), lambda i, j: (i, j, 0))
    g_spec = pl.BlockSpec((1, D_MODEL), lambda i, j: (0, 0))
    m_spec = pl.BlockSpec((bb, 1, D_MODEL), lambda i, j: (i, 0, 0))
    return x_spec, g_spec, m_spec


def _out_spec(bb, tt, width):
    return pl.BlockSpec((bb, tt, width), lambda i, j: (i, j, 0))


def _proj_even(x, g, sc, sh, w_bf, tables, bb, tt):
    B, T, _ = x.shape
    x_spec, g_spec, m_spec = _row_specs(bb, tt)
    tm = bb * tt
    t_spec = pl.BlockSpec((tm, LANES), (lambda i, j: (j, 0)) if bb == 1 else (lambda i, j: (0, 0)))
    leaf = jax.ShapeDtypeStruct((B, T, SEG), F32)
    return pl.pallas_call(
        _proj_even_kernel,
        grid=(B // bb, T // tt),
        in_specs=[x_spec, g_spec, m_spec, m_spec,
                  pl.BlockSpec(w_bf.shape, lambda i, j: (0, 0)), t_spec, t_spec, t_spec],
        out_specs=[_out_spec(bb, tt, 6 * SEG)] + [_out_spec(bb, tt, SEG)] * 4,
        out_shape=[jax.ShapeDtypeStruct((B, T, 6 * SEG), BF16), leaf, leaf, leaf, leaf],
        compiler_params=_params("arbitrary", "arbitrary"),
        name="proj_even",
    )(x, g, sc, sh, w_bf, *tables)


def _proj_odd(x, g, sc, sh, w_bf, wf_bf, b_f, bb, tt):
    B, T, _ = x.shape
    x_spec, g_spec, m_spec = _row_specs(bb, tt)
    leaf = jax.ShapeDtypeStruct((B, T, FOX_W), F32)
    return pl.pallas_call(
        _proj_odd_kernel,
        grid=(B // bb, T // tt),
        in_specs=[x_spec, g_spec, m_spec, m_spec,
                  pl.BlockSpec(w_bf.shape, lambda i, j: (0, 0)),
                  pl.BlockSpec(wf_bf.shape, lambda i, j: (0, 0)),
                  pl.BlockSpec((1, N_FOX_HEADS), lambda i, j: (0, 0))],
        out_specs=[_out_spec(bb, tt, 6 * SEG), _out_spec(bb, tt, FOX_W), _out_spec(bb, tt, FOX_W),
                   _out_spec(bb, tt, N_FOX_HEADS)],
        out_shape=[jax.ShapeDtypeStruct((B, T, 6 * SEG), BF16), leaf, leaf,
                   jax.ShapeDtypeStruct((B, T, N_FOX_HEADS), F32)],
        compiler_params=_params("arbitrary", "arbitrary"),
        name="proj_odd",
    )(x, g, sc, sh, w_bf, wf_bf, b_f.reshape(1, N_FOX_HEADS))


def _attn_kernel(*refs, kind, n_chain, tq, tk, q_off, n_valid, lam_init):
    if kind == "diff":
        q_ref, k_ref, vt_ref, lp_ref, sg_ref, o_ref, acc_ref, m_ref, s_ref = refs
    elif kind == "fox":
        q_ref, k_ref, vt_ref, nd_ref, o_ref, acc_ref, m_ref, s_ref = refs
    else:
        q_ref, k_ref, vt_ref, o_ref, acc_ref, m_ref, s_ref = refs
    tq2 = 2 * tq
    q0 = q_off + pl.program_id(2) * tq

    low = lax.broadcasted_iota(jnp.int32, (tq, LANES), 1) < HEAD_DIM
    if kind == "fox":
        row = lax.broadcasted_iota(jnp.int32, (tq2, LANES), 0)
        col = lax.broadcasted_iota(jnp.int32, (tq2, LANES), 1)
        pick = jnp.logical_and(jnp.right_shift(row, tq.bit_length() - 1) * 3 <= col,
                               col < jnp.right_shift(row, tq.bit_length() - 1) * 3 + 3)
        bias_lanes = jnp.where(pick, 1.0, 0.0).astype(BF16)
    qqs = []
    for c in range(n_chain):
        q = q_ref[0, :, c * LANES:(c + 1) * LANES]
        zero = jnp.zeros_like(q)
        qq = jnp.concatenate([jnp.where(low, q, zero), jnp.where(low, zero, q)], axis=0)
        if kind == "fox":
            qq = jnp.concatenate([qq, bias_lanes], axis=1)
        qqs.append(qq)

    acc_ref[...] = jnp.zeros_like(acc_ref)
    if kind == "sb":
        m_ref[...] = jnp.zeros_like(m_ref)
        tri_r = lax.broadcasted_iota(jnp.int32, (tk, tk), 0)
        tri_c = lax.broadcasted_iota(jnp.int32, (tk, tk), 1)
        after = jnp.where(tri_c > tri_r, 1.0, 0.0).astype(BF16)
        after2 = jnp.concatenate([after, after], axis=1)
    else:
        m_ref[...] = jnp.full_like(m_ref, MASKED)
        ones_rows = jnp.ones((BF16_ROWS, tk), BF16)

    qpos = q0 + jnp.bitwise_and(lax.broadcasted_iota(jnp.int32, (tk, tq2), 1), tq - 1)
    kidx = lax.broadcasted_iota(jnp.int32, (tk, tq2), 0)

    chains = range(n_chain)

    def qk(kb, slot):
        start = pl.multiple_of(kb * tk, tk)
        for c in chains:
            k = k_ref[0, pl.ds(start, tk), c * LANES:(c + 1) * LANES]
            if kind == "fox":
                k = jnp.concatenate([k, nd_ref[0, c, pl.ds(start, tk), :]], axis=1)
            s_ref[slot, c] = _dot_nt(k, qqs[c])

    def consume(kb, slot, masked):
        scores = [s_ref[slot, c] for c in chains]
        if masked:
            kpos = kb * tk + kidx
            if kind == "diff":
                vis = jnp.right_shift(kpos, 6) <= jnp.right_shift(qpos, 6)
            elif kind == "fox":
                vis = kpos <= qpos
            else:
                vis = kpos < qpos
            vis = jnp.logical_and(vis, kpos < n_valid)
        vts = [vt_ref[0, c, kb] for c in chains]
        if kind == "sb":
            log_betas, laters = [], []
            for c in chains:
                s = scores[c]
                soft = jnp.log(1.0 + jnp.exp(-jnp.abs(s)))
                log_beta = jnp.minimum(s, 0.0) - soft
                log_keep = log_beta - s
                if masked:
                    log_keep = jnp.where(vis, log_keep, 0.0)
                hi = log_keep.astype(BF16)
                lo = (log_keep - hi.astype(F32)).astype(BF16)
                later = _dot(after2, jnp.concatenate([hi, lo], axis=0))
                laters.append(later + m_ref[c])
                m_ref[c] += later[0:1, :] + log_keep[0:1, :]
                log_betas.append(log_beta)
            weights = []
            for c in chains:
                w = jnp.exp(log_betas[c] + laters[c])
                if masked:
                    w = jnp.where(vis, w, 0.0)
                weights.append(w.astype(BF16))
            for c in chains:
                acc_ref[c] += _dot(vts[c], weights[c])
        else:
            probs, alphas = [], []
            for c in chains:
                s = scores[c]
                if masked:
                    s = jnp.where(vis, s, MASKED)
                m_prev = m_ref[c]
                m_new = jnp.maximum(m_prev, jnp.max(s, axis=0, keepdims=True))
                alphas.append(jnp.exp(m_prev - m_new))
                probs.append(jnp.exp(s - m_new).astype(BF16))
                m_ref[c] = m_new
            for c in chains:
                vt_aug = jnp.concatenate([vts[c], ones_rows], axis=0)
                acc_ref[c] = alphas[c] * acc_ref[c] + _dot(vt_aug, probs[c])

    n_full = q0 // tk
    qk(n_full, 0)
    qk(jnp.maximum(n_full - 1, 0), 1)
    consume(n_full, 0, True)

    def pair(j, carry):
        kb = n_full - 1 - 2 * j
        qk(jnp.maximum(kb - 1, 0), 0)
        consume(kb, 1, False)
        qk(jnp.maximum(kb - 2, 0), 1)
        consume(kb - 1, 0, False)
        return carry

    if kind == "sb":
        def top_sum():
            top = m_ref[0]
            for c in range(1, n_chain):
                top = jnp.maximum(top, m_ref[c])
            return jnp.max(top)

        def alive(carry):
            return jnp.logical_and(carry[0] < n_full // 2, carry[1] > SB_DEAD)

        def pair_and_check(carry):
            pair(carry[0], 0)
            return carry[0] + 1, top_sum()

        _, top = lax.while_loop(alive, pair_and_check, (jnp.int32(0), top_sum()))
        odd_block_left = jnp.logical_and(n_full % 2 == 1, top > SB_DEAD)
    else:
        lax.fori_loop(0, n_full // 2, pair, 0)
        odd_block_left = n_full % 2 == 1

    @pl.when(odd_block_left)
    def _():
        consume(0, 1, False)

    if kind == "diff":
        lp = lp_ref[...]
        lam = (jnp.exp(jnp.sum(lp[0:1] * lp[1:2], axis=1, keepdims=True))
               - jnp.exp(jnp.sum(lp[2:3] * lp[3:4], axis=1, keepdims=True)) + lam_init)
    for c in range(n_chain):
        acc = acc_ref[c]
        if kind == "sb":
            out_t = jnp.concatenate([acc[0:HEAD_DIM, 0:tq], acc[HEAD_DIM:LANES, tq:tq2]], axis=0)
        else:
            r = 1.0 / acc[LANES:LANES + 1, :]
            if kind == "fox":
                out_t = jnp.concatenate([acc[0:HEAD_DIM, 0:tq] * r[:, 0:tq],
                                         acc[HEAD_DIM:LANES, tq:tq2] * r[:, tq:tq2]], axis=0)
            else:
                a = acc[0:LANES, 0:tq] * r[:, 0:tq] - lam * (acc[0:LANES, tq:tq2] * r[:, tq:tq2])
                ms = jnp.mean(a * a, axis=0, keepdims=True)
                out_t = a * lax.rsqrt(ms + EPS) * sg_ref[...] * (1.0 - lam_init)
        if tq < LANES:
            out_t = jnp.concatenate([out_t, jnp.zeros((LANES, LANES - tq), F32)], axis=1)
        o_ref[0, :, c * LANES:(c + 1) * LANES] = out_t.T[0:tq].astype(o_ref.dtype)


def _attention(kind, q_arr, q_col, k_arr, k_col, vt_arr, n_col, *, n_chain, tq, tk, q_off, n_valid,
               extras=(), lam_init=0.0):
    B, Tq, _ = q_arr.shape
    Tk = k_arr.shape[1]
    n_kb = Tk // tk
    assert Tq % tq == 0 and Tk % tk == 0 and tk % tq == 0 and q_off % tk == 0 and n_valid <= Tk
    assert tq & (tq - 1) == 0 and n_col % n_chain == 0 and q_col % n_chain == 0 and k_col % n_chain == 0
    wide = n_chain * LANES
    in_specs = [
        pl.BlockSpec((1, tq, wide), lambda b, c, i: (b, i, q_col // n_chain + c)),
        pl.BlockSpec((1, Tk, wide), lambda b, c, i: (b, 0, k_col // n_chain + c)),
        pl.BlockSpec((1, n_chain, n_kb, LANES, tk), lambda b, c, i: (b, c, 0, 0, 0)),
    ]
    acc_rows = LANES if kind == "sb" else LANES + BF16_ROWS
    scratch = [pltpu.VMEM((n_chain, acc_rows, 2 * tq), F32), pltpu.VMEM((n_chain, 1, 2 * tq), F32),
               pltpu.VMEM((2, n_chain, tk, 2 * tq), F32)]
    if kind == "diff":
        lp, sg = extras
        in_specs += [pl.BlockSpec(lp.shape, lambda b, c, i: (0, 0)),
                     pl.BlockSpec(sg.shape, lambda b, c, i: (0, 0))]
    elif kind == "fox":
        (nd,) = extras
        in_specs.append(pl.BlockSpec((1, n_chain, Tk, LANES), lambda b, c, i: (b, c, 0, 0)))
    kernel = functools.partial(_attn_kernel, kind=kind, n_chain=n_chain, tq=tq, tk=tk, q_off=q_off,
                               n_valid=n_valid, lam_init=lam_init)
    return pl.pallas_call(
        kernel,
        grid=(B, n_col // n_chain, Tq // tq),
        in_specs=in_specs,
        out_specs=pl.BlockSpec((1, tq, wide), lambda b, c, i: (b, i, c)),
        out_shape=jax.ShapeDtypeStruct((B, Tq, n_col * LANES), BF16),
        scratch_shapes=scratch,
        compiler_params=_params("arbitrary", "arbitrary", "arbitrary"),
        name="attn_" + kind,
    )(q_arr, k_arr, vt_arr, *extras)


def _key_blocks_t(v, tk):
    B, Tk, W = v.shape
    return v.reshape(B, Tk // tk, tk, W // LANES, LANES).transpose(0, 3, 1, 4, 2)


def _cumsum_kernel(x_ref, hi_ref, mid_ref, lo_ref, *, blk):
    T, H = x_ref.shape[1:]
    r = lax.broadcasted_iota(jnp.int32, (blk, blk), 0)
    c = lax.broadcasted_iota(jnp.int32, (blk, blk), 1)
    upto = jnp.where(c <= r, 1.0, 0.0).astype(BF16)

    def chunk(j, carry):
        rows = pl.ds(pl.multiple_of(j * blk, blk), blk)
        hi, mid, lo = _split3(x_ref[0, rows, :])
        total = _dot(upto, hi) + _dot(upto, mid) + _dot(upto, lo) + carry
        hi_ref[0, rows, :], mid_ref[0, rows, :], lo_ref[0, rows, :] = _split3(-total)
        return total[blk - 1:blk, :]

    lax.fori_loop(0, T // blk, chunk, jnp.zeros((1, H), F32))


def _neg_cumsum(lf, blk):
    B, T, H = lf.shape
    spec = pl.BlockSpec((1, T, H), lambda b: (b, 0, 0))
    piece = jax.ShapeDtypeStruct((B, T, H), BF16)
    return pl.pallas_call(
        functools.partial(_cumsum_kernel, blk=blk),
        grid=(B,),
        in_specs=[spec],
        out_specs=[spec, spec, spec],
        out_shape=[piece, piece, piece],
        compiler_params=_params("arbitrary"),
        name="neg_cumsum",
    )(lf)


def _outproj_kernel(*refs, nseg):
    a_refs = refs[:nseg]
    w_ref, x_ref, g_ref, o_ref = refs[nseg:]
    bb, tt, d = x_ref.shape
    acc = None
    off = 0
    for a_ref in a_refs:
        width = a_ref.shape[2]
        part = _dot(a_ref[...].reshape(bb * tt, width), w_ref[off:off + width, :])
        acc = part if acc is None else acc + part
        off += width
    o_ref[...] = x_ref[...] + g_ref[...] * acc.reshape(bb, tt, d)


def _outproj(mixed, w_bf, x, gate, bb, tt):
    B, T, _ = x.shape
    x_spec, _, m_spec = _row_specs(bb, tt)
    return pl.pallas_call(
        functools.partial(_outproj_kernel, nseg=len(mixed)),
        grid=(B // bb, T // tt),
        in_specs=[_out_spec(bb, tt, a.shape[2]) for a in mixed]
        + [pl.BlockSpec(w_bf.shape, lambda i, j: (0, 0)), x_spec, m_spec],
        out_specs=x_spec,
        out_shape=jax.ShapeDtypeStruct(x.shape, F32),
        compiler_params=_params("arbitrary", "arbitrary"),
        name="outproj",
    )(*mixed, w_bf, x, gate)


def _route(h, wr, br):
    tm = h.shape[0]
    logits = jnp.dot(h, wr, preferred_element_type=F32, precision=lax.Precision.HIGHEST)
    s = jax.nn.sigmoid(logits)
    sel = s + br
    lane = lax.broadcasted_iota(jnp.int32, (tm, N_EXPERTS), 1).astype(F32)
    neg = -jnp.inf

    def first_argmax(vals):
        top = jnp.max(vals, axis=1, keepdims=True)
        idx = jnp.min(jnp.where(vals == top, lane, float(N_EXPERTS)), axis=1, keepdims=True)
        return top, idx

    best = None
    for g in range(N_GROUPS):
        in_g = (lane >= g * EXPERTS_PER_GROUP) & (lane < (g + 1) * EXPERTS_PER_GROUP)
        top1, i1 = first_argmax(jnp.where(in_g, sel, neg))
        top2, i2 = first_argmax(jnp.where(in_g & (lane != i1), sel, neg))
        score = top1 + top2
        if best is None:
            best = (score, i1, i2)
        else:
            better = score > best[0]
            best = tuple(jnp.where(better, new, old) for new, old in zip((score, i1, i2), best))
    _, i1, i2 = best
    picked = jnp.where((lane == i1) | (lane == i2), s, 0.0)
    return picked / jnp.sum(picked, axis=1, keepdims=True)


def _moe_kernel(*refs, final):
    if final:
        (x_ref, g_ref, sc_ref, sh_ref, g2_ref, wr_ref, br_ref, wg_ref, wu_ref, wd_ref, gf_ref,
         o_ref, h_ref, gate_ref) = refs
    else:
        (x_ref, g_ref, sc_ref, sh_ref, g2_ref, wr_ref, br_ref, wg_ref, wu_ref, wd_ref,
         o_ref, h_ref, gate_ref) = refs
    bb, tt, d = x_ref.shape
    e = pl.program_id(2)

    @pl.when(e == 0)
    def _():
        h = _norm_mod(x_ref[...], g_ref[...], sc_ref[...], sh_ref[...])
        h_ref[...] = h.astype(BF16)
        gate_ref[...] = _route(h, wr_ref[...], br_ref[...])
        o_ref[...] = jnp.zeros_like(o_ref)

    h = h_ref[...]
    gt = _dot(h, wg_ref[0].astype(BF16))
    up = _dot(h, wu_ref[0].astype(BF16))
    gates = gate_ref[...]
    lane = lax.broadcasted_iota(jnp.int32, gates.shape, 1)
    ge = jnp.sum(jnp.where(lane == e, gates, 0.0), axis=1, keepdims=True)
    a = (gt * jax.nn.sigmoid(gt)) * up * ge
    o_ref[...] += _dot(a.astype(BF16), wd_ref[0].astype(BF16)).reshape(bb, tt, d)

    @pl.when(e == N_EXPERTS - 1)
    def _():
        y = x_ref[...] + g2_ref[...] * o_ref[...]
        if final:
            ms = jnp.mean(y * y, axis=-1, keepdims=True)
            y = y * lax.rsqrt(ms + EPS) * gf_ref[...]
        o_ref[...] = y


def _moe(x, g, sc, sh, g2, w_router, b_router, wg_bf, wu_bf, wd_bf, g_final, bb, tt):
    B, T, _ = x.shape
    final = g_final is not None
    x_spec = pl.BlockSpec((bb, tt, D_MODEL), lambda i, j, e: (i, j, 0))
    g_spec = pl.BlockSpec((1, D_MODEL), lambda i, j, e: (0, 0))
    m_spec = pl.BlockSpec((bb, 1, D_MODEL), lambda i, j, e: (i, 0, 0))
    in_specs = [x_spec, g_spec, m_spec, m_spec, m_spec,
                pl.BlockSpec((D_MODEL, N_EXPERTS), lambda i, j, e: (0, 0)),
                pl.BlockSpec((1, N_EXPERTS), lambda i, j, e: (0, 0)),
                pl.BlockSpec((1, D_MODEL, D_EXPERT), lambda i, j, e: (e, 0, 0)),
                pl.BlockSpec((1, D_MODEL, D_EXPERT), lambda i, j, e: (e, 0, 0)),
                pl.BlockSpec((1, D_EXPERT, D_MODEL), lambda i, j, e: (e, 0, 0))]
    args = [x, g, sc, sh, g2, w_router, b_router.reshape(1, N_EXPERTS), wg_bf, wu_bf, wd_bf]
    if final:
        in_specs.append(g_spec)
        args.append(g_final)
    tm = bb * tt
    return pl.pallas_call(
        functools.partial(_moe_kernel, final=final),
        grid=(B // bb, T // tt, N_EXPERTS),
        in_specs=in_specs,
        out_specs=x_spec,
        out_shape=jax.ShapeDtypeStruct(x.shape, F32),
        scratch_shapes=[pltpu.VMEM((tm, D_MODEL), BF16), pltpu.VMEM((tm, N_EXPERTS), F32)],
        compiler_params=_params("arbitrary", "arbitrary", "arbitrary"),
        name="moe",
    )(*args)


def _rope_tables(pos):
    half = ROPE_DIM // 2
    inv = ROPE_THETA ** (-jnp.arange(half, dtype=F32) / half)
    ang = pos.astype(F32)[:, None] * inv[None, :]
    cos, sin = jnp.cos(ang), jnp.sin(ang)
    ones = jnp.ones((pos.shape[0], HEAD_DIM - ROPE_DIM), F32)
    zeros = jnp.zeros_like(ones)
    zh = jnp.zeros_like(sin)
    c64 = jnp.concatenate([cos, cos, ones], axis=1)
    s1_64 = jnp.concatenate([-sin, zh, zeros], axis=1)
    s2_64 = jnp.concatenate([zh, sin, zeros], axis=1)
    return tuple(jnp.concatenate([t, t], axis=1) for t in (c64, s1_64, s2_64))


def _pad_time(a, total):
    if total == a.shape[1]:
        return a
    return jnp.pad(a, ((0, 0), (0, total - a.shape[1])) + ((0, 0),) * (a.ndim - 2))


def _trunk(x, mod, past, wts, cfg):
    B, T, _ = x.shape
    P = 0 if past is None else past["diff_k"].shape[2]
    bb, tt_proj, tt_out, tt_moe, tq, tk = cfg
    n_valid = P + T
    Tk = -(-n_valid // tk) * tk
    Tq = -(-T // tq) * tq
    tables = _rope_tables(P + jnp.arange(T))
    if bb > 1:
        tables = tuple(jnp.tile(t, (bb, 1)) for t in tables)
    attn = functools.partial(_attention, n_chain=N_CHAIN, tq=tq, tk=tk, q_off=P, n_valid=n_valid)
    leaves_even, leaves_odd = [], []

    def keys_values(qkv, k_cols, v_cols, cache_k, cache_v):
        v_new = qkv[:, :, v_cols[0]:v_cols[1]]
        if past is None:
            return qkv, k_cols[0] // LANES, _key_blocks_t(v_new, tk)
        width = k_cols[1] - k_cols[0]
        k_all = jnp.concatenate([cache_k.reshape(B, P, width).astype(BF16), qkv[:, :, k_cols[0]:k_cols[1]]], axis=1)
        v_all = jnp.concatenate([cache_v.reshape(B, P, width).astype(BF16), v_new], axis=1)
        return _pad_time(k_all, Tk), 0, _key_blocks_t(_pad_time(v_all, Tk), tk)

    for l in range(DEPTH):
        sh1, sc1, g1, sh2, sc2, g2 = [mod[l, :, k * D_MODEL:(k + 1) * D_MODEL][:, None, :] for k in range(6)]
        i = l // 2
        g_mix = wts["norm_mix"][l][None, :]
        if l % 2 == 0:
            qkv, ak, av, bk, bv = _proj_even(x, g_mix, sc1, sh1, wts["w_in_even"][i], tables, bb, tt_proj)
            leaves_even.append((ak, av, bk, bv))
            q_arr = _pad_time(qkv, Tq)
            dk, dk_col, dvt = keys_values(qkv, (SEG, 2 * SEG), (2 * SEG, 3 * SEG),
                                          None if past is None else past["diff_k"][i],
                                          None if past is None else past["diff_v"][i])
            sk, sk_col, svt = keys_values(qkv, (4 * SEG, 5 * SEG), (5 * SEG, 6 * SEG),
                                          None if past is None else past["sb_k"][i],
                                          None if past is None else past["sb_v"][i])
            lam_init = 0.8 - 0.6 * math.exp(-0.3 * l)
            lp = jnp.stack([wts["diff_lq1"][i], wts["diff_lk1"][i], wts["diff_lq2"][i], wts["diff_lk2"][i]])
            a_out = attn("diff", q_arr, 0, dk, dk_col, dvt, N_DIFF_HEADS,
                         extras=(lp, wts["diff_subln"][i][:, None]), lam_init=lam_init)
            b_out = attn("sb", q_arr, (3 * SEG) // LANES, sk, sk_col, svt, N_SB_HEADS // 2)
            x = _outproj([a_out[:, :T], b_out[:, :T]], wts["w_out_even"][i], x, g1, bb, tt_out)
        else:
            qkv, k, v, lf = _proj_odd(x, g_mix, sc1, sh1, wts["w_in_odd"][i], wts["w_f_odd"][i],
                                      wts["b_forget"][i], bb, tt_proj)
            leaves_odd.append((k, v, lf))
            q_arr = _pad_time(qkv, Tq)
            fk, fk_col, fvt = keys_values(qkv, (2 * SEG, 4 * SEG), (4 * SEG, 6 * SEG),
                                          None if past is None else past["fox_k"][i],
                                          None if past is None else past["fox_v"][i])
            lf_all = lf if past is None else jnp.concatenate([past["fox_logf"][i], lf], axis=1)
            pieces = jnp.stack(_neg_cumsum(_pad_time(lf_all, Tk), tk), axis=-1)
            nd = pieces.reshape(B, Tk, N_FOX_HEADS // 2, 6).transpose(0, 2, 1, 3)
            nd = jnp.pad(nd, ((0, 0), (0, 0), (0, 0), (0, LANES - 6)))
            o = attn("fox", q_arr, 0, fk, fk_col, fvt, N_FOX_HEADS // 2, extras=(nd,))
            x = _outproj([o[:, :T]], wts["w_out_odd"][i], x, g1, bb, tt_out)
        g_final = wts["norm_final"][None, :] if l == DEPTH - 1 else None
        x = _moe(x, wts["norm_ffn"][l][None, :], sc2, sh2, g2, wts["w_router"], wts["b_router"],
                 wts["w_gate"][l], wts["w_up"][l], wts["w_down"][l], g_final, bb, tt_moe)
    return x, leaves_even, leaves_odd


def kernel(x_prompt, x_sample, c_prompt, c_sample, cache_diff_k, cache_diff_v, cache_sb_k, cache_sb_v, cache_fox_k, cache_fox_v, cache_fox_logf, w_ada, b_ada, norm_mix, norm_ffn, w_in_even, w_out_even, diff_lq1, diff_lk1, diff_lq2, diff_lk2, diff_subln, w_in_odd, b_forget, w_out_odd, w_router, b_router, w_gate, w_up, w_down, norm_final):
    Bp, Tp, _ = x_prompt.shape
    Bs, Ts, _ = x_sample.shape
    wts = {
        "norm_mix": norm_mix, "norm_ffn": norm_ffn, "norm_final": norm_final,
        "w_in_even": w_in_even.astype(BF16), "w_out_even": w_out_even.astype(BF16),
        "w_in_odd": w_in_odd[:, :, :3 * FOX_W].astype(BF16), "w_f_odd": w_in_odd[:, :, 3 * FOX_W:].astype(BF16),
        "w_out_odd": w_out_odd.astype(BF16), "b_forget": b_forget,
        "diff_lq1": diff_lq1, "diff_lk1": diff_lk1, "diff_lq2": diff_lq2, "diff_lk2": diff_lk2,
        "diff_subln": diff_subln, "w_router": w_router, "b_router": b_router,
        "w_gate": w_gate, "w_up": w_up, "w_down": w_down,
    }
    past = {"diff_k": cache_diff_k, "diff_v": cache_diff_v, "sb_k": cache_sb_k, "sb_v": cache_sb_v,
            "fox_k": cache_fox_k, "fox_v": cache_fox_v, "fox_logf": cache_fox_logf}

    rows = Bp + Bs
    rows_pad = -(-rows // 16) * 16
    c_all = jnp.pad(jnp.concatenate([c_prompt, c_sample], axis=0), ((0, rows_pad - rows), (0, 0)))
    mod = _ada_mod(c_all, w_ada, b_ada)

    cfg_prompt = (1, 256, 512, 1024, 128, 256)
    cfg_sample = (Bs, Ts, Ts, Ts, 64, 256)
    y_p, even_p, odd_p = _trunk(x_prompt, mod[:, :Bp], None, wts, cfg_prompt)
    y_s, even_s, odd_s = _trunk(x_sample, mod[:, Bp:rows], past, wts, cfg_sample)

    def layers(rows):
        return rows[0][None] if len(rows) == 1 else jnp.stack(rows, axis=0)

    def pack(B, T, even, odd):
        dk, dv, sk, sv = [layers(r) for r in zip(*even)]
        fk, fv, fl = [layers(r) for r in zip(*odd)]
        n_even, n_odd = dk.shape[0], fk.shape[0]
        return (dk.reshape(n_even, B, T, N_DIFF_HEADS, 2 * HEAD_DIM), dv.reshape(n_even, B, T, N_DIFF_HEADS, 2 * HEAD_DIM),
                sk.reshape(n_even, B, T, N_SB_HEADS, HEAD_DIM), sv.reshape(n_even, B, T, N_SB_HEADS, HEAD_DIM),
                fk.reshape(n_odd, B, T, N_FOX_HEADS, HEAD_DIM), fv.reshape(n_odd, B, T, N_FOX_HEADS, HEAD_DIM), fl)

    return (y_p, y_s) + pack(Bp, Tp, even_p, odd_p) + pack(Bs, Ts, even_s, odd_s)
```

```python
import functools
import math

import jax
import jax.numpy as jnp
from jax import lax
from jax.experimental import pallas as pl
from jax.experimental.pallas import tpu as pltpu

D_MODEL = 1024
DEPTH = 2
CHUNK = 64
HEAD_DIM = 64
N_DIFF_HEADS = D_MODEL // 256
N_SB_HEADS = D_MODEL // 128
N_FOX_HEADS = D_MODEL // HEAD_DIM
DIFF_W = N_DIFF_HEADS * 2 * HEAD_DIM
SB_W = N_SB_HEADS * HEAD_DIM
FOX_W = N_FOX_HEADS * HEAD_DIM
ROPE_DIM = HEAD_DIM // 4
ROPE_THETA = 500000.0
N_EXPERTS = 16
N_GROUPS = 4
EXPERTS_PER_GROUP = N_EXPERTS // N_GROUPS
D_EXPERT = D_MODEL // 2
EPS = 1e-6

LANES = 128
BF16_ROWS = 16
SEG = 512
QK_SCALE = HEAD_DIM ** -0.5
MASKED = -1e30
N_CHAIN = 4
SB_DEAD = -110.0
VMEM_LIMIT = 48 * 1024 * 1024

F32 = jnp.float32
BF16 = jnp.bfloat16


def _params(*sem):
    return pltpu.CompilerParams(dimension_semantics=sem, vmem_limit_bytes=VMEM_LIMIT)


def _dot(a, b):
    return jnp.dot(a, b, preferred_element_type=F32)


def _dot_nt(a, b):
    return lax.dot_general(a, b, (((1,), (1,)), ((), ())), preferred_element_type=F32)


def _norm_mod(x, g, sc, sh):
    bb, tt, d = x.shape
    ms = jnp.mean(x * x, axis=-1, keepdims=True)
    h = x * lax.rsqrt(ms + EPS) * g
    h = h * (1.0 + sc) + sh
    return h.reshape(bb * tt, d)


def _log_sigmoid(x):
    return jnp.minimum(x, 0.0) - jnp.log(1.0 + jnp.exp(-jnp.abs(x)))


def _split3(x):
    hi = x.astype(BF16)
    r1 = x - hi.astype(F32)
    mid = r1.astype(BF16)
    lo = (r1 - mid.astype(F32)).astype(BF16)
    return hi, mid, lo


def _ada_kernel(c_ref, w_ref, b_ref, o_ref):
    c = c_ref[...]
    a = (c * jax.nn.sigmoid(c)).astype(BF16)
    o_ref[0] = _dot(a, w_ref[0].astype(BF16)) + b_ref[0]


def _ada_mod(c_all, w_ada, b_ada):
    rows = c_all.shape[0]
    tn = 1536
    width = 6 * D_MODEL
    return pl.pallas_call(
        _ada_kernel,
        grid=(DEPTH, width // tn),
        in_specs=[
            pl.BlockSpec((rows, D_MODEL), lambda l, j: (0, 0)),
            pl.BlockSpec((1, D_MODEL, tn), lambda l, j: (l, 0, j)),
            pl.BlockSpec((1, 1, tn), lambda l, j: (l, 0, j)),
        ],
        out_specs=pl.BlockSpec((1, rows, tn), lambda l, j: (l, 0, j)),
        out_shape=jax.ShapeDtypeStruct((DEPTH, rows, width), F32),
        compiler_params=_params("arbitrary", "arbitrary"),
        name="ada_mod",
    )(c_all, w_ada, b_ada.reshape(DEPTH, 1, width))


def _rope(p, cos, s1, s2):
    outs = []
    for c in range(SEG // LANES):
        pc = p[:, c * LANES:(c + 1) * LANES]
        up = pltpu.roll(pc, LANES - ROPE_DIM // 2, 1)
        down = pltpu.roll(pc, ROPE_DIM // 2, 1)
        outs.append(pc * cos + up * s1 + down * s2)
    return jnp.concatenate(outs, axis=1)


def _proj_even_kernel(x_ref, g_ref, sc_ref, sh_ref, w_ref, cos_ref, s1_ref, s2_ref,
                      qkv_ref, ak_ref, av_ref, bk_ref, bv_ref):
    bb, tt, _ = x_ref.shape
    h = _norm_mod(x_ref[...], g_ref[...], sc_ref[...], sh_ref[...]).astype(BF16)
    cos, s1, s2 = cos_ref[...], s1_ref[...], s2_ref[...]

    def seg(i):
        return _dot(h, w_ref[:, i * SEG:(i + 1) * SEG])

    def put(i, val, leaf_ref):
        val3 = val.reshape(bb, tt, SEG)
        qkv_ref[:, :, i * SEG:(i + 1) * SEG] = val3.astype(BF16)
        if leaf_ref is not None:
            leaf_ref[...] = val3

    put(0, _rope(seg(0), cos, s1, s2) * QK_SCALE, None)
    put(1, _rope(seg(1), cos, s1, s2), ak_ref)
    put(2, seg(2), av_ref)
    put(3, seg(3) * QK_SCALE, None)
    put(4, seg(4), bk_ref)
    put(5, seg(5), bv_ref)


def _rows_by_head(ref, val, n_heads):
    tt = val.shape[0]
    for hd in range(n_heads):
        ref[0, pl.ds(hd, tt, stride=n_heads), :] = val[:, hd * LANES:(hd + 1) * LANES]


def _key_blocks(ref, first, val_t):
    n = val_t.shape[0] // LANES
    ref[0, first:first + n, 0] = val_t.reshape(n, LANES, val_t.shape[1]).astype(BF16)


def _proj_even_wide_kernel(x_ref, g_ref, sc_ref, sh_ref, w_ref, cos_ref, s1_ref, s2_ref,
                           qk_ref, ak_ref, av_ref, bkt_ref, bvt_ref, dvt_ref, svt_ref):
    h = _norm_mod(x_ref[...], g_ref[...], sc_ref[...], sh_ref[...]).astype(BF16)
    cos, s1, s2 = cos_ref[...], s1_ref[...], s2_ref[...]

    def seg(i):
        return _dot(h, w_ref[:, i * SEG:(i + 1) * SEG])

    qk_ref[0, :, 0:SEG] = (_rope(seg(0), cos, s1, s2) * QK_SCALE).astype(BF16)
    ak = _rope(seg(1), cos, s1, s2)
    qk_ref[0, :, SEG:2 * SEG] = ak.astype(BF16)
    _rows_by_head(ak_ref, ak, N_DIFF_HEADS)
    av = seg(2)
    _rows_by_head(av_ref, av, N_DIFF_HEADS)
    _key_blocks(dvt_ref, 0, av.T)
    qk_ref[0, :, 2 * SEG:3 * SEG] = (seg(3) * QK_SCALE).astype(BF16)
    bk = seg(4)
    qk_ref[0, :, 3 * SEG:4 * SEG] = bk.astype(BF16)
    bkt_ref[0] = bk.T
    bvt = seg(5).T
    bvt_ref[0] = bvt
    _key_blocks(svt_ref, 0, bvt)


def _proj_odd_wide_kernel(x_ref, g_ref, sc_ref, sh_ref, w_ref, wf_ref, bf_ref,
                          qk_ref, kt_ref, vt_ref, fvt_ref, lf_ref):
    h = _norm_mod(x_ref[...], g_ref[...], sc_ref[...], sh_ref[...]).astype(BF16)

    def seg(i):
        return _dot(h, w_ref[:, i * SEG:(i + 1) * SEG])

    for i in range(2):
        qk_ref[0, :, i * SEG:(i + 1) * SEG] = (seg(i) * QK_SCALE).astype(BF16)
    for i in range(2):
        k = seg(2 + i)
        qk_ref[0, :, (2 + i) * SEG:(3 + i) * SEG] = k.astype(BF16)
        kt_ref[0, i * SEG:(i + 1) * SEG, :] = k.T
        v_t = seg(4 + i).T
        vt_ref[0, i * SEG:(i + 1) * SEG, :] = v_t
        _key_blocks(fvt_ref, i * (SEG // LANES), v_t)
    fl = _dot(h, wf_ref[...]) + bf_ref[...]
    lf_ref[0] = _log_sigmoid(fl)


def _proj_odd_kernel(x_ref, g_ref, sc_ref, sh_ref, w_ref, wf_ref, bf_ref,
                     qkv_ref, k_ref, v_ref, lf_ref):
    bb, tt, _ = x_ref.shape
    h = _norm_mod(x_ref[...], g_ref[...], sc_ref[...], sh_ref[...]).astype(BF16)

    def seg(i):
        return _dot(h, w_ref[:, i * SEG:(i + 1) * SEG]).reshape(bb, tt, SEG)

    for i in range(2):
        qkv_ref[:, :, i * SEG:(i + 1) * SEG] = (seg(i) * QK_SCALE).astype(BF16)
    for i in range(2, 6):
        val = seg(i)
        qkv_ref[:, :, i * SEG:(i + 1) * SEG] = val.astype(BF16)
        leaf_ref = k_ref if i < 4 else v_ref
        leaf_ref[:, :, (i % 2) * SEG:(i % 2 + 1) * SEG] = val
    fl = _dot(h, wf_ref[...]) + bf_ref[...]
    lf_ref[...] = _log_sigmoid(fl).reshape(bb, tt, N_FOX_HEADS)


def _row_specs(bb, tt):
    x_spec = pl.BlockSpec((bb, tt, D_MODEL), lambda i, j: (i, j, 0))
    g_spec = pl.BlockSpec((1, D_MODEL), lambda i, j: (0, 0))
    m_spec = pl.BlockSpec((bb, 1, D_MODEL), lambda i, j: (i, 0, 0))
    return x_spec, g_spec, m_spec


def _out_spec(bb, tt, width):
    return pl.BlockSpec((bb, tt, width), lambda i, j: (i, j, 0))


def _proj_even(x, g, sc, sh, w_bf, tables, bb, tt):
    B, T, _ = x.shape
    x_spec, g_spec, m_spec = _row_specs(bb, tt)
    tm = bb * tt
    t_spec = pl.BlockSpec((tm, LANES), (lambda i, j: (j, 0)) if bb == 1 else (lambda i, j: (0, 0)))
    leaf = jax.ShapeDtypeStruct((B, T, SEG), F32)
    return pl.pallas_call(
        _proj_even_kernel,
        grid=(B // bb, T // tt),
        in_specs=[x_spec, g_spec, m_spec, m_spec,
                  pl.BlockSpec(w_bf.shape, lambda i, j: (0, 0)), t_spec, t_spec, t_spec],
        out_specs=[_out_spec(bb, tt, 6 * SEG)] + [_out_spec(bb, tt, SEG)] * 4,
        out_shape=[jax.ShapeDtypeStruct((B, T, 6 * SEG), BF16), leaf, leaf, leaf, leaf],
        compiler_params=_params("arbitrary", "arbitrary"),
        name="proj_even",
    )(x, g, sc, sh, w_bf, *tables)


def _wide_specs(B, T, tt):
    def rows(width, dtype):
        return pl.BlockSpec((1, tt, width), lambda i, j: (i, j, 0)), jax.ShapeDtypeStruct((B, T, width), dtype)

    def by_head(n_heads):
        return (pl.BlockSpec((1, tt * n_heads, LANES), lambda i, j: (i, j, 0)),
                jax.ShapeDtypeStruct((B, T * n_heads, LANES), F32))

    def transposed(width):
        return pl.BlockSpec((1, width, tt), lambda i, j: (i, 0, j)), jax.ShapeDtypeStruct((B, width, T), F32)

    def key_blocks(n_col):
        return (pl.BlockSpec((1, n_col, 1, LANES, tt), lambda i, j: (i, 0, j, 0, 0)),
                jax.ShapeDtypeStruct((B, n_col, T // tt, LANES, tt), BF16))

    return rows, by_head, transposed, key_blocks


def _proj_even_wide(x, g, sc, sh, w_bf, tables, tt):
    B, T, _ = x.shape
    x_spec, g_spec, m_spec = _row_specs(1, tt)
    t_spec = pl.BlockSpec((tt, LANES), lambda i, j: (j, 0))
    rows, by_head, transposed, key_blocks = _wide_specs(B, T, tt)
    outs = [rows(4 * SEG, BF16), by_head(N_DIFF_HEADS), by_head(N_DIFF_HEADS), transposed(SEG), transposed(SEG),
            key_blocks(SEG // LANES), key_blocks(SEG // LANES)]
    return pl.pallas_call(
        _proj_even_wide_kernel,
        grid=(B, T // tt),
        in_specs=[x_spec, g_spec, m_spec, m_spec,
                  pl.BlockSpec(w_bf.shape, lambda i, j: (0, 0)), t_spec, t_spec, t_spec],
        out_specs=[o[0] for o in outs],
        out_shape=[o[1] for o in outs],
        compiler_params=_params("arbitrary", "arbitrary"),
        name="proj_even_wide",
    )(x, g, sc, sh, w_bf, *tables)


def _proj_odd_wide(x, g, sc, sh, w_bf, wf_bf, b_f, tt):
    B, T, _ = x.shape
    x_spec, g_spec, m_spec = _row_specs(1, tt)
    rows, by_head, transposed, key_blocks = _wide_specs(B, T, tt)
    outs = [rows(4 * SEG, BF16), transposed(FOX_W), transposed(FOX_W), key_blocks(FOX_W // LANES),
            rows(N_FOX_HEADS, F32)]
    return pl.pallas_call(
        _proj_odd_wide_kernel,
        grid=(B, T // tt),
        in_specs=[x_spec, g_spec, m_spec, m_spec,
                  pl.BlockSpec(w_bf.shape, lambda i, j: (0, 0)),
                  pl.BlockSpec(wf_bf.shape, lambda i, j: (0, 0)),
                  pl.BlockSpec((1, N_FOX_HEADS), lambda i, j: (0, 0))],
        out_specs=[o[0] for o in outs],
        out_shape=[o[1] for o in outs],
        compiler_params=_params("arbitrary", "arbitrary"),
        name="proj_odd_wide",
    )(x, g, sc, sh, w_bf, wf_bf, b_f.reshape(1, N_FOX_HEADS))


def _proj_odd(x, g, sc, sh, w_bf, wf_bf, b_f, bb, tt):
    B, T, _ = x.shape
    x_spec, g_spec, m_spec = _row_specs(bb, tt)
    leaf = jax.ShapeDtypeStruct((B, T, FOX_W), F32)
    return pl.pallas_call(
        _proj_odd_kernel,
        grid=(B // bb, T // tt),
        in_specs=[x_spec, g_spec, m_spec, m_spec,
                  pl.BlockSpec(w_bf.shape, lambda i, j: (0, 0)),
                  pl.BlockSpec(wf_bf.shape, lambda i, j: (0, 0)),
                  pl.BlockSpec((1, N_FOX_HEADS), lambda i, j: (0, 0))],
        out_specs=[_out_spec(bb, tt, 6 * SEG), _out_spec(bb, tt, FOX_W), _out_spec(bb, tt, FOX_W),
                   _out_spec(bb, tt, N_FOX_HEADS)],
        out_shape=[jax.ShapeDtypeStruct((B, T, 6 * SEG), BF16), leaf, leaf,
                   jax.ShapeDtypeStruct((B, T, N_FOX_HEADS), F32)],
        compiler_params=_params("arbitrary", "arbitrary"),
        name="proj_odd",
    )(x, g, sc, sh, w_bf, wf_bf, b_f.reshape(1, N_FOX_HEADS))


def _attn_kernel(*refs, kind, n_chain, tq, tk, q_off, n_valid, lam_init):
    if kind == "diff":
        q_ref, k_ref, vt_ref, lp_ref, sg_ref, o_ref, acc_ref, m_ref, s_ref = refs
    elif kind == "fox":
        q_ref, k_ref, vt_ref, nd_ref, o_ref, acc_ref, m_ref, s_ref = refs
    else:
        q_ref, k_ref, vt_ref, o_ref, acc_ref, m_ref, s_ref = refs
    tq2 = 2 * tq
    q0 = q_off + pl.program_id(2) * tq

    low = lax.broadcasted_iota(jnp.int32, (tq, LANES), 1) < HEAD_DIM
    if kind == "fox":
        row = lax.broadcasted_iota(jnp.int32, (tq2, LANES), 0)
        col = lax.broadcasted_iota(jnp.int32, (tq2, LANES), 1)
        pick = jnp.logical_and(jnp.right_shift(row, tq.bit_length() - 1) * 3 <= col,
                               col < jnp.right_shift(row, tq.bit_length() - 1) * 3 + 3)
        bias_lanes = jnp.where(pick, 1.0, 0.0).astype(BF16)
    qqs = []
    for c in range(n_chain):
        q = q_ref[0, :, c * LANES:(c + 1) * LANES]
        zero = jnp.zeros_like(q)
        qq = jnp.concatenate([jnp.where(low, q, zero), jnp.where(low, zero, q)], axis=0)
        if kind == "fox":
            qq = jnp.concatenate([qq, bias_lanes], axis=1)
        qqs.append(qq)

    acc_ref[...] = jnp.zeros_like(acc_ref)
    if kind == "sb":
        m_ref[...] = jnp.zeros_like(m_ref)
        tri_r = lax.broadcasted_iota(jnp.int32, (tk, tk), 0)
        tri_c = lax.broadcasted_iota(jnp.int32, (tk, tk), 1)
        after = jnp.where(tri_c > tri_r, 1.0, 0.0).astype(BF16)
        after2 = jnp.concatenate([after, after], axis=1)
    else:
        m_ref[...] = jnp.full_like(m_ref, MASKED)
        ones_rows = jnp.ones((BF16_ROWS, tk), BF16)

    qpos = q0 + jnp.bitwise_and(lax.broadcasted_iota(jnp.int32, (tk, tq2), 1), tq - 1)
    kidx = lax.broadcasted_iota(jnp.int32, (tk, tq2), 0)

    chains = range(n_chain)

    def qk(kb, slot):
        start = pl.multiple_of(kb * tk, tk)
        for c in chains:
            k = k_ref[0, pl.ds(start, tk), c * LANES:(c + 1) * LANES]
            if kind == "fox":
                k = jnp.concatenate([k, nd_ref[0, c, pl.ds(start, tk), :]], axis=1)
            s_ref[slot, c] = _dot_nt(k, qqs[c])

    def consume(kb, slot, masked):
        scores = [s_ref[slot, c] for c in chains]
        if masked:
            kpos = kb * tk + kidx
            if kind == "diff":
                vis = jnp.right_shift(kpos, 6) <= jnp.right_shift(qpos, 6)
            elif kind == "fox":
                vis = kpos <= qpos
            else:
                vis = kpos < qpos
            vis = jnp.logical_and(vis, kpos < n_valid)
        vts = [vt_ref[0, c, kb] for c in chains]
        if kind == "sb":
            log_betas, laters = [], []
            for c in chains:
                s = scores[c]
                soft = jnp.log(1.0 + jnp.exp(-jnp.abs(s)))
                log_beta = jnp.minimum(s, 0.0) - soft
                log_keep = log_beta - s
                if masked:
                    log_keep = jnp.where(vis, log_keep, 0.0)
                hi = log_keep.astype(BF16)
                lo = (log_keep - hi.astype(F32)).astype(BF16)
                later = _dot(after2, jnp.concatenate([hi, lo], axis=0))
                laters.append(later + m_ref[c])
                m_ref[c] += later[0:1, :] + log_keep[0:1, :]
                log_betas.append(log_beta)
            weights = []
            for c in chains:
                w = jnp.exp(log_betas[c] + laters[c])
                if masked:
                    w = jnp.where(vis, w, 0.0)
                weights.append(w.astype(BF16))
            for c in chains:
                acc_ref[c] += _dot(vts[c], weights[c])
        else:
            probs, alphas = [], []
            for c in chains:
                s = scores[c]
                if masked:
                    s = jnp.where(vis, s, MASKED)
                m_prev = m_ref[c]
                m_new = jnp.maximum(m_prev, jnp.max(s, axis=0, keepdims=True))
                alphas.append(jnp.exp(m_prev - m_new))
                probs.append(jnp.exp(s - m_new).astype(BF16))
                m_ref[c] = m_new
            for c in chains:
                vt_aug = jnp.concatenate([vts[c], ones_rows], axis=0)
                acc_ref[c] = alphas[c] * acc_ref[c] + _dot(vt_aug, probs[c])

    n_full = q0 // tk
    qk(n_full, 0)
    qk(jnp.maximum(n_full - 1, 0), 1)
    consume(n_full, 0, True)

    def pair(j, carry):
        kb = n_full - 1 - 2 * j
        qk(jnp.maximum(kb - 1, 0), 0)
        consume(kb, 1, False)
        qk(jnp.maximum(kb - 2, 0), 1)
        consume(kb - 1, 0, False)
        return carry

    if kind == "sb":
        def top_sum():
            top = m_ref[0]
            for c in range(1, n_chain):
                top = jnp.maximum(top, m_ref[c])
            return jnp.max(top)

        def alive(carry):
            return jnp.logical_and(carry[0] < n_full // 2, carry[1] > SB_DEAD)

        def pair_and_check(carry):
            pair(carry[0], 0)
            return carry[0] + 1, top_sum()

        _, top = lax.while_loop(alive, pair_and_check, (jnp.int32(0), top_sum()))
        odd_block_left = jnp.logical_and(n_full % 2 == 1, top > SB_DEAD)
    else:
        lax.fori_loop(0, n_full // 2, pair, 0)
        odd_block_left = n_full % 2 == 1

    @pl.when(odd_block_left)
    def _():
        consume(0, 1, False)

    if kind == "diff":
        lp = lp_ref[...]
        lam = (jnp.exp(jnp.sum(lp[0:1] * lp[1:2], axis=1, keepdims=True))
               - jnp.exp(jnp.sum(lp[2:3] * lp[3:4], axis=1, keepdims=True)) + lam_init)
    for c in range(n_chain):
        acc = acc_ref[c]
        if kind == "sb":
            out_t = jnp.concatenate([acc[0:HEAD_DIM, 0:tq], acc[HEAD_DIM:LANES, tq:tq2]], axis=0)
        else:
            r = 1.0 / acc[LANES:LANES + 1, :]
            if kind == "fox":
                out_t = jnp.concatenate([acc[0:HEAD_DIM, 0:tq] * r[:, 0:tq],
                                         acc[HEAD_DIM:LANES, tq:tq2] * r[:, tq:tq2]], axis=0)
            else:
                a = acc[0:LANES, 0:tq] * r[:, 0:tq] - lam * (acc[0:LANES, tq:tq2] * r[:, tq:tq2])
                ms = jnp.mean(a * a, axis=0, keepdims=True)
                out_t = a * lax.rsqrt(ms + EPS) * sg_ref[...] * (1.0 - lam_init)
        if tq < LANES:
            out_t = jnp.concatenate([out_t, jnp.zeros((LANES, LANES - tq), F32)], axis=1)
        o_ref[0, :, c * LANES:(c + 1) * LANES] = out_t.T[0:tq].astype(o_ref.dtype)


def _attention(kind, q_arr, q_col, k_arr, k_col, vt_arr, n_col, *, n_chain, tq, tk, q_off, n_valid,
               extras=(), lam_init=0.0):
    B, Tq, _ = q_arr.shape
    Tk = k_arr.shape[1]
    n_kb = Tk // tk
    assert Tq % tq == 0 and Tk % tk == 0 and tk % tq == 0 and q_off % tk == 0 and n_valid <= Tk
    assert tq & (tq - 1) == 0 and n_col % n_chain == 0 and q_col % n_chain == 0 and k_col % n_chain == 0
    wide = n_chain * LANES
    in_specs = [
        pl.BlockSpec((1, tq, wide), lambda b, c, i: (b, i, q_col // n_chain + c)),
        pl.BlockSpec((1, Tk, wide), lambda b, c, i: (b, 0, k_col // n_chain + c)),
        pl.BlockSpec((1, n_chain, n_kb, LANES, tk), lambda b, c, i: (b, c, 0, 0, 0)),
    ]
    acc_rows = LANES if kind == "sb" else LANES + BF16_ROWS
    scratch = [pltpu.VMEM((n_chain, acc_rows, 2 * tq), F32), pltpu.VMEM((n_chain, 1, 2 * tq), F32),
               pltpu.VMEM((2, n_chain, tk, 2 * tq), F32)]
    if kind == "diff":
        lp, sg = extras
        in_specs += [pl.BlockSpec(lp.shape, lambda b, c, i: (0, 0)),
                     pl.BlockSpec(sg.shape, lambda b, c, i: (0, 0))]
    elif kind == "fox":
        (nd,) = extras
        in_specs.append(pl.BlockSpec((1, n_chain, Tk, LANES), lambda b, c, i: (b, c, 0, 0)))
    kernel = functools.partial(_attn_kernel, kind=kind, n_chain=n_chain, tq=tq, tk=tk, q_off=q_off,
                               n_valid=n_valid, lam_init=lam_init)
    return pl.pallas_call(
        kernel,
        grid=(B, n_col // n_chain, Tq // tq),
        in_specs=in_specs,
        out_specs=pl.BlockSpec((1, tq, wide), lambda b, c, i: (b, i, c)),
        out_shape=jax.ShapeDtypeStruct((B, Tq, n_col * LANES), BF16),
        scratch_shapes=scratch,
        compiler_params=_params("arbitrary", "arbitrary", "arbitrary"),
        name="attn_" + kind,
    )(q_arr, k_arr, vt_arr, *extras)


def _key_blocks_t(v, tk):
    B, Tk, W = v.shape
    return v.reshape(B, Tk // tk, tk, W // LANES, LANES).transpose(0, 3, 1, 4, 2)


def _cumsum_kernel(x_ref, o_ref, *, blk):
    T, H = x_ref.shape[1:]
    n_pair = H // 2
    r = lax.broadcasted_iota(jnp.int32, (blk, blk), 0)
    c = lax.broadcasted_iota(jnp.int32, (blk, blk), 1)
    upto = jnp.where(c <= r, 1.0, 0.0).astype(BF16)
    head = lax.broadcasted_iota(jnp.int32, (H, n_pair * LANES), 0)
    col = lax.broadcasted_iota(jnp.int32, (H, n_pair * LANES), 1)
    here = jnp.right_shift(col, 7) == jnp.right_shift(head, 1)
    lane = jnp.bitwise_and(col, LANES - 1) - 3 * jnp.bitwise_and(head, 1)
    place = [jnp.where(jnp.logical_and(here, lane == p), 1.0, 0.0).astype(BF16) for p in range(3)]

    def chunk(j, carry):
        rows = pl.ds(pl.multiple_of(j * blk, blk), blk)
        hi, mid, lo = _split3(x_ref[0, rows, :])
        total = _dot(upto, hi) + _dot(upto, mid) + _dot(upto, lo) + carry
        pieces = _split3(-total)
        lanes = sum(_dot(piece, sel) for piece, sel in zip(pieces, place))
        for pair in range(n_pair):
            o_ref[0, pair, rows, :] = lanes[:, pair * LANES:(pair + 1) * LANES].astype(BF16)
        return total[blk - 1:blk, :]

    lax.fori_loop(0, T // blk, chunk, jnp.zeros((1, H), F32))


def _neg_cumsum(lf, blk):
    B, T, H = lf.shape
    return pl.pallas_call(
        functools.partial(_cumsum_kernel, blk=blk),
        grid=(B,),
        in_specs=[pl.BlockSpec((1, T, H), lambda b: (b, 0, 0))],
        out_specs=pl.BlockSpec((1, H // 2, T, LANES), lambda b: (b, 0, 0, 0)),
        out_shape=jax.ShapeDtypeStruct((B, H // 2, T, LANES), BF16),
        compiler_params=_params("arbitrary"),
        name="neg_cumsum",
    )(lf)


def _outproj_kernel(*refs, nseg):
    a_refs = refs[:nseg]
    w_ref, x_ref, g_ref, o_ref = refs[nseg:]
    bb, tt, d = x_ref.shape
    acc = None
    off = 0
    for a_ref in a_refs:
        width = a_ref.shape[2]
        part = _dot(a_ref[...].reshape(bb * tt, width), w_ref[off:off + width, :])
        acc = part if acc is None else acc + part
        off += width
    o_ref[...] = x_ref[...] + g_ref[...] * acc.reshape(bb, tt, d)


def _outproj(mixed, w_bf, x, gate, bb, tt):
    B, T, _ = x.shape
    x_spec, _, m_spec = _row_specs(bb, tt)
    return pl.pallas_call(
        functools.partial(_outproj_kernel, nseg=len(mixed)),
        grid=(B // bb, T // tt),
        in_specs=[_out_spec(bb, tt, a.shape[2]) for a in mixed]
        + [pl.BlockSpec(w_bf.shape, lambda i, j: (0, 0)), x_spec, m_spec],
        out_specs=x_spec,
        out_shape=jax.ShapeDtypeStruct(x.shape, F32),
        compiler_params=_params("arbitrary", "arbitrary"),
        name="outproj",
    )(*mixed, w_bf, x, gate)


def _route(h, wr, br):
    tm = h.shape[0]
    logits = jnp.dot(h, wr, preferred_element_type=F32, precision=lax.Precision.HIGHEST)
    s = jax.nn.sigmoid(logits)
    sel = s + br
    lane = lax.broadcasted_iota(jnp.int32, (tm, N_EXPERTS), 1).astype(F32)
    neg = -jnp.inf

    def first_argmax(vals):
        top = jnp.max(vals, axis=1, keepdims=True)
        idx = jnp.min(jnp.where(vals == top, lane, float(N_EXPERTS)), axis=1, keepdims=True)
        return top, idx

    best = None
    for g in range(N_GROUPS):
        in_g = (lane >= g * EXPERTS_PER_GROUP) & (lane < (g + 1) * EXPERTS_PER_GROUP)
        top1, i1 = first_argmax(jnp.where(in_g, sel, neg))
        top2, i2 = first_argmax(jnp.where(in_g & (lane != i1), sel, neg))
        score = top1 + top2
        if best is None:
            best = (score, i1, i2)
        else:
            better = score > best[0]
            best = tuple(jnp.where(better, new, old) for new, old in zip((score, i1, i2), best))
    _, i1, i2 = best
    picked = jnp.where((lane == i1) | (lane == i2), s, 0.0)
    return picked / jnp.sum(picked, axis=1, keepdims=True)


def _moe_kernel(*refs, final):
    if final:
        (x_ref, g_ref, sc_ref, sh_ref, g2_ref, wr_ref, br_ref, wg_ref, wu_ref, wd_ref, gf_ref,
         o_ref, h_ref, gate_ref) = refs
    else:
        (x_ref, g_ref, sc_ref, sh_ref, g2_ref, wr_ref, br_ref, wg_ref, wu_ref, wd_ref,
         o_ref, h_ref, gate_ref) = refs
    bb, tt, d = x_ref.shape
    e = pl.program_id(2)

    @pl.when(e == 0)
    def _():
        h = _norm_mod(x_ref[...], g_ref[...], sc_ref[...], sh_ref[...])
        h_ref[...] = h.astype(BF16)
        gate_ref[...] = _route(h, wr_ref[...], br_ref[...])
        o_ref[...] = jnp.zeros_like(o_ref)

    h = h_ref[...]
    gt = _dot(h, wg_ref[0, 0].astype(BF16))
    up = _dot(h, wu_ref[0, 0].astype(BF16))
    gates = gate_ref[...]
    lane = lax.broadcasted_iota(jnp.int32, gates.shape, 1)
    ge = jnp.sum(jnp.where(lane == e, gates, 0.0), axis=1, keepdims=True)
    a = (gt * jax.nn.sigmoid(gt)) * up * ge
    o_ref[...] += _dot(a.astype(BF16), wd_ref[0, 0].astype(BF16)).reshape(bb, tt, d)

    @pl.when(e == N_EXPERTS - 1)
    def _():
        y = x_ref[...] + g2_ref[...] * o_ref[...]
        if final:
            ms = jnp.mean(y * y, axis=-1, keepdims=True)
            y = y * lax.rsqrt(ms + EPS) * gf_ref[...]
        o_ref[...] = y


def _moe(x, g, sc, sh, g2, w_router, b_router, w_gate, w_up, w_down, layer, g_final, bb, tt):
    B, T, _ = x.shape
    final = g_final is not None
    x_spec = pl.BlockSpec((bb, tt, D_MODEL), lambda i, j, e: (i, j, 0))
    g_spec = pl.BlockSpec((1, D_MODEL), lambda i, j, e: (0, 0))
    m_spec = pl.BlockSpec((bb, 1, D_MODEL), lambda i, j, e: (i, 0, 0))
    in_specs = [x_spec, g_spec, m_spec, m_spec, m_spec,
                pl.BlockSpec((D_MODEL, N_EXPERTS), lambda i, j, e: (0, 0)),
                pl.BlockSpec((1, N_EXPERTS), lambda i, j, e: (0, 0)),
                pl.BlockSpec((1, 1, D_MODEL, D_EXPERT), lambda i, j, e: (layer, e, 0, 0)),
                pl.BlockSpec((1, 1, D_MODEL, D_EXPERT), lambda i, j, e: (layer, e, 0, 0)),
                pl.BlockSpec((1, 1, D_EXPERT, D_MODEL), lambda i, j, e: (layer, e, 0, 0))]
    args = [x, g, sc, sh, g2, w_router, b_router.reshape(1, N_EXPERTS), w_gate, w_up, w_down]
    if final:
        in_specs.append(g_spec)
        args.append(g_final)
    tm = bb * tt
    return pl.pallas_call(
        functools.partial(_moe_kernel, final=final),
        grid=(B // bb, T // tt, N_EXPERTS),
        in_specs=in_specs,
        out_specs=x_spec,
        out_shape=jax.ShapeDtypeStruct(x.shape, F32),
        scratch_shapes=[pltpu.VMEM((tm, D_MODEL), BF16), pltpu.VMEM((tm, N_EXPERTS), F32)],
        compiler_params=_params("arbitrary", "arbitrary", "arbitrary"),
        name="moe",
    )(*args)


def _rope_tables(pos):
    half = ROPE_DIM // 2
    inv = ROPE_THETA ** (-jnp.arange(half, dtype=F32) / half)
    ang = pos.astype(F32)[:, None] * inv[None, :]
    cos, sin = jnp.cos(ang), jnp.sin(ang)
    ones = jnp.ones((pos.shape[0], HEAD_DIM - ROPE_DIM), F32)
    zeros = jnp.zeros_like(ones)
    zh = jnp.zeros_like(sin)
    c64 = jnp.concatenate([cos, cos, ones], axis=1)
    s1_64 = jnp.concatenate([-sin, zh, zeros], axis=1)
    s2_64 = jnp.concatenate([zh, sin, zeros], axis=1)
    return tuple(jnp.concatenate([t, t], axis=1) for t in (c64, s1_64, s2_64))


def _pad_time(a, total):
    if total == a.shape[1]:
        return a
    return jnp.pad(a, ((0, 0), (0, total - a.shape[1])) + ((0, 0),) * (a.ndim - 2))


def _trunk(x, mod, past, wts, cfg):
    B, T, _ = x.shape
    P = 0 if past is None else past["diff_k"].shape[2]
    bb, tt_proj, tt_out, tt_moe, tq, tk = cfg
    wide = past is None and bb == 1 and tt_proj == tk
    n_valid = P + T
    Tk = -(-n_valid // tk) * tk
    Tq = -(-T // tq) * tq
    tables = _rope_tables(P + jnp.arange(T))
    if bb > 1:
        tables = tuple(jnp.tile(t, (bb, 1)) for t in tables)
    attn = functools.partial(_attention, n_chain=N_CHAIN, tq=tq, tk=tk, q_off=P, n_valid=n_valid)
    leaves_even, leaves_odd = [], []

    def with_cache(qkv, k_cols, v_cols, cache_k, cache_v):
        width = k_cols[1] - k_cols[0]
        k_all = jnp.concatenate([cache_k.reshape(B, P, width).astype(BF16), qkv[:, :, k_cols[0]:k_cols[1]]], axis=1)
        v_all = jnp.concatenate([cache_v.reshape(B, P, width).astype(BF16), qkv[:, :, v_cols[0]:v_cols[1]]], axis=1)
        return _pad_time(k_all, Tk), _key_blocks_t(_pad_time(v_all, Tk), tk)

    def heads_last(leaf_t, n_heads):
        return leaf_t.reshape(B, n_heads, -1, T).transpose(0, 3, 1, 2)

    for l in range(DEPTH):
        sh1, sc1, g1, sh2, sc2, g2 = [mod[l, :, k * D_MODEL:(k + 1) * D_MODEL][:, None, :] for k in range(6)]
        i = l // 2
        g_mix = wts["norm_mix"][l][None, :]
        if l % 2 == 0:
            lam_init = 0.8 - 0.6 * math.exp(-0.3 * l)
            lp = jnp.stack([wts["diff_lq1"][i], wts["diff_lk1"][i], wts["diff_lq2"][i], wts["diff_lk2"][i]])
            diff_extras = (lp, wts["diff_subln"][i][:, None])
            if wide:
                qk, ak, av, bkt, bvt, dvt, svt = _proj_even_wide(x, g_mix, sc1, sh1, wts["w_in_even"][i], tables, tt_proj)
                leaves_even.append((ak.reshape(B, T, N_DIFF_HEADS, LANES), av.reshape(B, T, N_DIFF_HEADS, LANES),
                                    heads_last(bkt, N_SB_HEADS), heads_last(bvt, N_SB_HEADS)))
                q_arr, q_cols = qk, (0, 2 * SEG // LANES)
                dk, dk_col, sk, sk_col = qk, SEG // LANES, qk, 3 * SEG // LANES
            else:
                qkv, ak, av, bk, bv = _proj_even(x, g_mix, sc1, sh1, wts["w_in_even"][i], tables, bb, tt_proj)
                leaves_even.append((ak.reshape(B, T, N_DIFF_HEADS, LANES), av.reshape(B, T, N_DIFF_HEADS, LANES),
                                    bk.reshape(B, T, N_SB_HEADS, HEAD_DIM), bv.reshape(B, T, N_SB_HEADS, HEAD_DIM)))
                q_arr, q_cols = _pad_time(qkv, Tq), (0, 3 * SEG // LANES)
                dk, dvt = with_cache(qkv, (SEG, 2 * SEG), (2 * SEG, 3 * SEG), past["diff_k"][i], past["diff_v"][i])
                sk, svt = with_cache(qkv, (4 * SEG, 5 * SEG), (5 * SEG, 6 * SEG), past["sb_k"][i], past["sb_v"][i])
                dk_col = sk_col = 0
            a_out = attn("diff", q_arr, q_cols[0], dk, dk_col, dvt, N_DIFF_HEADS, extras=diff_extras, lam_init=lam_init)
            b_out = attn("sb", q_arr, q_cols[1], sk, sk_col, svt, N_SB_HEADS // 2)
            x = _outproj([a_out[:, :T], b_out[:, :T]], wts["w_out_even"][i], x, g1, bb, tt_out)
        else:
            if wide:
                qk, kt, vt, fvt, lf = _proj_odd_wide(x, g_mix, sc1, sh1, wts["w_in_odd"][i], wts["w_f_odd"][i],
                                                     wts["b_forget"][i], tt_proj)
                leaves_odd.append((heads_last(kt, N_FOX_HEADS), heads_last(vt, N_FOX_HEADS), lf))
                q_arr, fk, fk_col, lf_all = qk, qk, 2 * SEG // LANES, lf
            else:
                qkv, k, v, lf = _proj_odd(x, g_mix, sc1, sh1, wts["w_in_odd"][i], wts["w_f_odd"][i],
                                          wts["b_forget"][i], bb, tt_proj)
                leaves_odd.append((k.reshape(B, T, N_FOX_HEADS, HEAD_DIM), v.reshape(B, T, N_FOX_HEADS, HEAD_DIM), lf))
                q_arr = _pad_time(qkv, Tq)
                fk, fvt = with_cache(qkv, (2 * SEG, 4 * SEG), (4 * SEG, 6 * SEG), past["fox_k"][i], past["fox_v"][i])
                fk_col = 0
                lf_all = jnp.concatenate([past["fox_logf"][i], lf], axis=1)
            nd = _neg_cumsum(_pad_time(lf_all, Tk), tk)
            o = attn("fox", q_arr, 0, fk, fk_col, fvt, N_FOX_HEADS // 2, extras=(nd,))
            x = _outproj([o[:, :T]], wts["w_out_odd"][i], x, g1, bb, tt_out)
        g_final = wts["norm_final"][None, :] if l == DEPTH - 1 else None
        x = _moe(x, wts["norm_ffn"][l][None, :], sc2, sh2, g2, wts["w_router"], wts["b_router"],
                 wts["w_gate"], wts["w_up"], wts["w_down"], l, g_final, bb, tt_moe)
    return x, leaves_even, leaves_odd


def kernel(x_prompt, x_sample, c_prompt, c_sample, cache_diff_k, cache_diff_v, cache_sb_k, cache_sb_v, cache_fox_k, cache_fox_v, cache_fox_logf, w_ada, b_ada, norm_mix, norm_ffn, w_in_even, w_out_even, diff_lq1, diff_lk1, diff_lq2, diff_lk2, diff_subln, w_in_odd, b_forget, w_out_odd, w_router, b_router, w_gate, w_up, w_down, norm_final):
    Bp, Tp, _ = x_prompt.shape
    Bs, Ts, _ = x_sample.shape
    wts = {
        "norm_mix": norm_mix, "norm_ffn": norm_ffn, "norm_final": norm_final,
        "w_in_even": w_in_even.astype(BF16), "w_out_even": w_out_even.astype(BF16),
        "w_in_odd": w_in_odd[:, :, :3 * FOX_W].astype(BF16), "w_f_odd": w_in_odd[:, :, 3 * FOX_W:].astype(BF16),
        "w_out_odd": w_out_odd.astype(BF16), "b_forget": b_forget,
        "diff_lq1": diff_lq1, "diff_lk1": diff_lk1, "diff_lq2": diff_lq2, "diff_lk2": diff_lk2,
        "diff_subln": diff_subln, "w_router": w_router, "b_router": b_router,
        "w_gate": w_gate, "w_up": w_up, "w_down": w_down,
    }
    past = {"diff_k": cache_diff_k, "diff_v": cache_diff_v, "sb_k": cache_sb_k, "sb_v": cache_sb_v,
            "fox_k": cache_fox_k, "fox_v": cache_fox_v, "fox_logf": cache_fox_logf}

    rows = Bp + Bs
    rows_pad = -(-rows // 16) * 16
    c_all = jnp.pad(jnp.concatenate([c_prompt, c_sample], axis=0), ((0, rows_pad - rows), (0, 0)))
    mod = _ada_mod(c_all, w_ada, b_ada)

    cfg_prompt = (1, 256, 512, 1024, 128, 256)
    cfg_sample = (Bs, Ts, Ts, Ts, 64, 256)
    y_p, even_p, odd_p = _trunk(x_prompt, mod[:, :Bp], None, wts, cfg_prompt)
    y_s, even_s, odd_s = _trunk(x_sample, mod[:, Bp:rows], past, wts, cfg_sample)

    def layers(rows):
        return rows[0][None] if len(rows) == 1 else jnp.stack(rows, axis=0)

    def pack(even, odd):
        return tuple(layers(r) for r in zip(*even)) + tuple(layers(r) for r in zip(*odd))

    return (y_p, y_s) + pack(even_p, odd_p) + pack(even_s, odd_s)
```

```python
import functools
import math

import jax
import jax.numpy as jnp
from jax import lax
from jax.experimental import pallas as pl
from jax.experimental.pallas import tpu as pltpu

D_MODEL = 1024
DEPTH = 2
CHUNK = 64
HEAD_DIM = 64
N_DIFF_HEADS = D_MODEL // 256
N_SB_HEADS = D_MODEL // 128
N_FOX_HEADS = D_MODEL // HEAD_DIM
DIFF_W = N_DIFF_HEADS * 2 * HEAD_DIM
SB_W = N_SB_HEADS * HEAD_DIM
FOX_W = N_FOX_HEADS * HEAD_DIM
ROPE_DIM = HEAD_DIM // 4
ROPE_THETA = 500000.0
N_EXPERTS = 16
N_GROUPS = 4
EXPERTS_PER_GROUP = N_EXPERTS // N_GROUPS
D_EXPERT = D_MODEL // 2
EPS = 1e-6

LANES = 128
BF16_ROWS = 16
SEG = 512
QK_SCALE = HEAD_DIM ** -0.5
MASKED = -1e30
N_CHAIN = 4
SB_DEAD = -110.0
MOE_CHUNK = 128
VMEM_LIMIT = 48 * 1024 * 1024

F32 = jnp.float32
BF16 = jnp.bfloat16


def _params(*sem):
    return pltpu.CompilerParams(dimension_semantics=sem, vmem_limit_bytes=VMEM_LIMIT)


def _dot(a, b):
    return jnp.dot(a, b, preferred_element_type=F32)


def _dot_nt(a, b):
    return lax.dot_general(a, b, (((1,), (1,)), ((), ())), preferred_element_type=F32)


def _norm_mod(x, g, sc, sh):
    bb, tt, d = x.shape
    ms = jnp.mean(x * x, axis=-1, keepdims=True)
    h = x * lax.rsqrt(ms + EPS) * g
    h = h * (1.0 + sc) + sh
    return h.reshape(bb * tt, d)


def _log_sigmoid(x):
    return jnp.minimum(x, 0.0) - jnp.log(1.0 + jnp.exp(-jnp.abs(x)))


def _split3(x):
    hi = x.astype(BF16)
    r1 = x - hi.astype(F32)
    mid = r1.astype(BF16)
    lo = (r1 - mid.astype(F32)).astype(BF16)
    return hi, mid, lo


def _ada_kernel(c_ref, w_ref, b_ref, o_ref):
    c = c_ref[...]
    a = (c * jax.nn.sigmoid(c)).astype(BF16)
    o_ref[0] = _dot(a, w_ref[0].astype(BF16)) + b_ref[0]


def _ada_mod(c_all, w_ada, b_ada):
    rows = c_all.shape[0]
    tn = 1536
    width = 6 * D_MODEL
    return pl.pallas_call(
        _ada_kernel,
        grid=(DEPTH, width // tn),
        in_specs=[
            pl.BlockSpec((rows, D_MODEL), lambda l, j: (0, 0)),
            pl.BlockSpec((1, D_MODEL, tn), lambda l, j: (l, 0, j)),
            pl.BlockSpec((1, 1, tn), lambda l, j: (l, 0, j)),
        ],
        out_specs=pl.BlockSpec((1, rows, tn), lambda l, j: (l, 0, j)),
        out_shape=jax.ShapeDtypeStruct((DEPTH, rows, width), F32),
        compiler_params=_params("arbitrary", "arbitrary"),
        name="ada_mod",
    )(c_all, w_ada, b_ada.reshape(DEPTH, 1, width))


def _rope(p, cos, s1, s2):
    outs = []
    for c in range(SEG // LANES):
        pc = p[:, c * LANES:(c + 1) * LANES]
        up = pltpu.roll(pc, LANES - ROPE_DIM // 2, 1)
        down = pltpu.roll(pc, ROPE_DIM // 2, 1)
        outs.append(pc * cos + up * s1 + down * s2)
    return jnp.concatenate(outs, axis=1)


def _proj_even_kernel(x_ref, g_ref, sc_ref, sh_ref, w_ref, cos_ref, s1_ref, s2_ref,
                      qkv_ref, ak_ref, av_ref, bk_ref, bv_ref):
    bb, tt, _ = x_ref.shape
    h = _norm_mod(x_ref[...], g_ref[...], sc_ref[...], sh_ref[...]).astype(BF16)
    cos, s1, s2 = cos_ref[...], s1_ref[...], s2_ref[...]

    def seg(i):
        return _dot(h, w_ref[:, i * SEG:(i + 1) * SEG])

    def put(i, val, leaf_ref):
        val3 = val.reshape(bb, tt, SEG)
        qkv_ref[:, :, i * SEG:(i + 1) * SEG] = val3.astype(BF16)
        if leaf_ref is not None:
            leaf_ref[...] = val3

    put(0, _rope(seg(0), cos, s1, s2) * QK_SCALE, None)
    put(1, _rope(seg(1), cos, s1, s2), ak_ref)
    put(2, seg(2), av_ref)
    put(3, seg(3) * QK_SCALE, None)
    put(4, seg(4), bk_ref)
    put(5, seg(5), bv_ref)


def _rows_by_head(ref, val, n_heads):
    tt = val.shape[0]
    for hd in range(n_heads):
        ref[0, pl.ds(hd, tt, stride=n_heads), :] = val[:, hd * LANES:(hd + 1) * LANES]


def _key_blocks(ref, first, val_t):
    n = val_t.shape[0] // LANES
    ref[0, first:first + n, 0] = val_t.reshape(n, LANES, val_t.shape[1]).astype(BF16)


def _proj_even_wide_kernel(x_ref, g_ref, sc_ref, sh_ref, w_ref, cos_ref, s1_ref, s2_ref,
                           qk_ref, ak_ref, av_ref, bkt_ref, bvt_ref, dvt_ref, svt_ref):
    h = _norm_mod(x_ref[...], g_ref[...], sc_ref[...], sh_ref[...]).astype(BF16)
    cos, s1, s2 = cos_ref[...], s1_ref[...], s2_ref[...]

    def seg(i):
        return _dot(h, w_ref[:, i * SEG:(i + 1) * SEG])

    qk_ref[0, :, 0:SEG] = (_rope(seg(0), cos, s1, s2) * QK_SCALE).astype(BF16)
    ak = _rope(seg(1), cos, s1, s2)
    qk_ref[0, :, SEG:2 * SEG] = ak.astype(BF16)
    _rows_by_head(ak_ref, ak, N_DIFF_HEADS)
    av = seg(2)
    _rows_by_head(av_ref, av, N_DIFF_HEADS)
    _key_blocks(dvt_ref, 0, av.T)
    qk_ref[0, :, 2 * SEG:3 * SEG] = (seg(3) * QK_SCALE).astype(BF16)
    bk = seg(4)
    qk_ref[0, :, 3 * SEG:4 * SEG] = bk.astype(BF16)
    bkt_ref[0] = bk.T
    bvt = seg(5).T
    bvt_ref[0] = bvt
    _key_blocks(svt_ref, 0, bvt)


def _proj_odd_wide_kernel(x_ref, g_ref, sc_ref, sh_ref, w_ref, wf_ref, bf_ref,
                          qk_ref, kt_ref, vt_ref, fvt_ref, lf_ref):
    h = _norm_mod(x_ref[...], g_ref[...], sc_ref[...], sh_ref[...]).astype(BF16)

    def seg(i):
        return _dot(h, w_ref[:, i * SEG:(i + 1) * SEG])

    for i in range(2):
        qk_ref[0, :, i * SEG:(i + 1) * SEG] = (seg(i) * QK_SCALE).astype(BF16)
    for i in range(2):
        k = seg(2 + i)
        qk_ref[0, :, (2 + i) * SEG:(3 + i) * SEG] = k.astype(BF16)
        kt_ref[0, i * SEG:(i + 1) * SEG, :] = k.T
        v_t = seg(4 + i).T
        vt_ref[0, i * SEG:(i + 1) * SEG, :] = v_t
        _key_blocks(fvt_ref, i * (SEG // LANES), v_t)
    fl = _dot(h, wf_ref[...]) + bf_ref[...]
    lf_ref[0] = _log_sigmoid(fl)


def _proj_odd_kernel(x_ref, g_ref, sc_ref, sh_ref, w_ref, wf_ref, bf_ref,
                     qkv_ref, k_ref, v_ref, lf_ref):
    bb, tt, _ = x_ref.shape
    h = _norm_mod(x_ref[...], g_ref[...], sc_ref[...], sh_ref[...]).astype(BF16)

    def seg(i):
        return _dot(h, w_ref[:, i * SEG:(i + 1) * SEG]).reshape(bb, tt, SEG)

    for i in range(2):
        qkv_ref[:, :, i * SEG:(i + 1) * SEG] = (seg(i) * QK_SCALE).astype(BF16)
    for i in range(2, 6):
        val = seg(i)
        qkv_ref[:, :, i * SEG:(i + 1) * SEG] = val.astype(BF16)
        leaf_ref = k_ref if i < 4 else v_ref
        leaf_ref[:, :, (i % 2) * SEG:(i % 2 + 1) * SEG] = val
    fl = _dot(h, wf_ref[...]) + bf_ref[...]
    lf_ref[...] = _log_sigmoid(fl).reshape(bb, tt, N_FOX_HEADS)


def _row_specs(bb, tt):
    x_spec = pl.BlockSpec((bb, tt, D_MODEL), lambda i, j: (i, j, 0))
    g_spec = pl.BlockSpec((1, D_MODEL), lambda i, j: (0, 0))
    m_spec = pl.BlockSpec((bb, 1, D_MODEL), lambda i, j: (i, 0, 0))
    return x_spec, g_spec, m_spec


def _out_spec(bb, tt, width):
    return pl.BlockSpec((bb, tt, width), lambda i, j: (i, j, 0))


def _proj_even(x, g, sc, sh, w_bf, tables, bb, tt):
    B, T, _ = x.shape
    x_spec, g_spec, m_spec = _row_specs(bb, tt)
    tm = bb * tt
    t_spec = pl.BlockSpec((tm, LANES), (lambda i, j: (j, 0)) if bb == 1 else (lambda i, j: (0, 0)))
    leaf = jax.ShapeDtypeStruct((B, T, SEG), F32)
    return pl.pallas_call(
        _proj_even_kernel,
        grid=(B // bb, T // tt),
        in_specs=[x_spec, g_spec, m_spec, m_spec,
                  pl.BlockSpec(w_bf.shape, lambda i, j: (0, 0)), t_spec, t_spec, t_spec],
        out_specs=[_out_spec(bb, tt, 6 * SEG)] + [_out_spec(bb, tt, SEG)] * 4,
        out_shape=[jax.ShapeDtypeStruct((B, T, 6 * SEG), BF16), leaf, leaf, leaf, leaf],
        compiler_params=_params("arbitrary", "arbitrary"),
        name="proj_even",
    )(x, g, sc, sh, w_bf, *tables)


def _wide_specs(B, T, tt):
    def rows(width, dtype):
        return pl.BlockSpec((1, tt, width), lambda i, j: (i, j, 0)), jax.ShapeDtypeStruct((B, T, width), dtype)

    def by_head(n_heads):
        return (pl.BlockSpec((1, tt * n_heads, LANES), lambda i, j: (i, j, 0)),
                jax.ShapeDtypeStruct((B, T * n_heads, LANES), F32))

    def transposed(width):
        return pl.BlockSpec((1, width, tt), lambda i, j: (i, 0, j)), jax.ShapeDtypeStruct((B, width, T), F32)

    def key_blocks(n_col):
        return (pl.BlockSpec((1, n_col, 1, LANES, tt), lambda i, j: (i, 0, j, 0, 0)),
                jax.ShapeDtypeStruct((B, n_col, T // tt, LANES, tt), BF16))

    return rows, by_head, transposed, key_blocks


def _proj_even_wide(x, g, sc, sh, w_bf, tables, tt):
    B, T, _ = x.shape
    x_spec, g_spec, m_spec = _row_specs(1, tt)
    t_spec = pl.BlockSpec((tt, LANES), lambda i, j: (j, 0))
    rows, by_head, transposed, key_blocks = _wide_specs(B, T, tt)
    outs = [rows(4 * SEG, BF16), by_head(N_DIFF_HEADS), by_head(N_DIFF_HEADS), transposed(SEG), transposed(SEG),
            key_blocks(SEG // LANES), key_blocks(SEG // LANES)]
    return pl.pallas_call(
        _proj_even_wide_kernel,
        grid=(B, T // tt),
        in_specs=[x_spec, g_spec, m_spec, m_spec,
                  pl.BlockSpec(w_bf.shape, lambda i, j: (0, 0)), t_spec, t_spec, t_spec],
        out_specs=[o[0] for o in outs],
        out_shape=[o[1] for o in outs],
        compiler_params=_params("arbitrary", "arbitrary"),
        name="proj_even_wide",
    )(x, g, sc, sh, w_bf, *tables)


def _proj_odd_wide(x, g, sc, sh, w_bf, wf_bf, b_f, tt):
    B, T, _ = x.shape
    x_spec, g_spec, m_spec = _row_specs(1, tt)
    rows, by_head, transposed, key_blocks = _wide_specs(B, T, tt)
    outs = [rows(4 * SEG, BF16), transposed(FOX_W), transposed(FOX_W), key_blocks(FOX_W // LANES),
            rows(N_FOX_HEADS, F32)]
    return pl.pallas_call(
        _proj_odd_wide_kernel,
        grid=(B, T // tt),
        in_specs=[x_spec, g_spec, m_spec, m_spec,
                  pl.BlockSpec(w_bf.shape, lambda i, j: (0, 0)),
                  pl.BlockSpec(wf_bf.shape, lambda i, j: (0, 0)),
                  pl.BlockSpec((1, N_FOX_HEADS), lambda i, j: (0, 0))],
        out_specs=[o[0] for o in outs],
        out_shape=[o[1] for o in outs],
        compiler_params=_params("arbitrary", "arbitrary"),
        name="proj_odd_wide",
    )(x, g, sc, sh, w_bf, wf_bf, b_f.reshape(1, N_FOX_HEADS))


def _proj_odd(x, g, sc, sh, w_bf, wf_bf, b_f, bb, tt):
    B, T, _ = x.shape
    x_spec, g_spec, m_spec = _row_specs(bb, tt)
    leaf = jax.ShapeDtypeStruct((B, T, FOX_W), F32)
    return pl.pallas_call(
        _proj_odd_kernel,
        grid=(B // bb, T // tt),
        in_specs=[x_spec, g_spec, m_spec, m_spec,
                  pl.BlockSpec(w_bf.shape, lambda i, j: (0, 0)),
                  pl.BlockSpec(wf_bf.shape, lambda i, j: (0, 0)),
                  pl.BlockSpec((1, N_FOX_HEADS), lambda i, j: (0, 0))],
        out_specs=[_out_spec(bb, tt, 6 * SEG), _out_spec(bb, tt, FOX_W), _out_spec(bb, tt, FOX_W),
                   _out_spec(bb, tt, N_FOX_HEADS)],
        out_shape=[jax.ShapeDtypeStruct((B, T, 6 * SEG), BF16), leaf, leaf,
                   jax.ShapeDtypeStruct((B, T, N_FOX_HEADS), F32)],
        compiler_params=_params("arbitrary", "arbitrary"),
        name="proj_odd",
    )(x, g, sc, sh, w_bf, wf_bf, b_f.reshape(1, N_FOX_HEADS))


def _attn_kernel(*refs, kind, n_chain, tq, tk, q_off, n_valid, lam_init):
    if kind == "diff":
        q_ref, k_ref, vt_ref, lp_ref, sg_ref, o_ref, acc_ref, m_ref, s_ref = refs
    elif kind == "fox":
        q_ref, k_ref, vt_ref, nd_ref, o_ref, acc_ref, m_ref, s_ref = refs
    else:
        q_ref, k_ref, vt_ref, o_ref, acc_ref, m_ref, s_ref = refs
    tq2 = 2 * tq
    q0 = q_off + pl.program_id(2) * tq

    low = lax.broadcasted_iota(jnp.int32, (tq, LANES), 1) < HEAD_DIM
    if kind == "fox":
        row = lax.broadcasted_iota(jnp.int32, (tq2, LANES), 0)
        col = lax.broadcasted_iota(jnp.int32, (tq2, LANES), 1)
        pick = jnp.logical_and(jnp.right_shift(row, tq.bit_length() - 1) * 3 <= col,
                               col < jnp.right_shift(row, tq.bit_length() - 1) * 3 + 3)
        bias_lanes = jnp.where(pick, 1.0, 0.0).astype(BF16)
    qqs = []
    for c in range(n_chain):
        q = q_ref[0, :, c * LANES:(c + 1) * LANES]
        zero = jnp.zeros_like(q)
        qq = jnp.concatenate([jnp.where(low, q, zero), jnp.where(low, zero, q)], axis=0)
        if kind == "fox":
            qq = jnp.concatenate([qq, bias_lanes], axis=1)
        qqs.append(qq)

    acc_ref[...] = jnp.zeros_like(acc_ref)
    if kind == "sb":
        m_ref[...] = jnp.zeros_like(m_ref)
        tri_r = lax.broadcasted_iota(jnp.int32, (tk, tk), 0)
        tri_c = lax.broadcasted_iota(jnp.int32, (tk, tk), 1)
        after = jnp.where(tri_c > tri_r, 1.0, 0.0).astype(BF16)
        after2 = jnp.concatenate([after, after], axis=1)
    else:
        m_ref[...] = jnp.full_like(m_ref, MASKED)
        ones_rows = jnp.ones((BF16_ROWS, tk), BF16)

    qpos = q0 + jnp.bitwise_and(lax.broadcasted_iota(jnp.int32, (tk, tq2), 1), tq - 1)
    kidx = lax.broadcasted_iota(jnp.int32, (tk, tq2), 0)

    chains = range(n_chain)

    def qk(kb, slot):
        start = pl.multiple_of(kb * tk, tk)
        for c in chains:
            k = k_ref[0, pl.ds(start, tk), c * LANES:(c + 1) * LANES]
            if kind == "fox":
                k = jnp.concatenate([k, nd_ref[0, c, pl.ds(start, tk), :]], axis=1)
            s_ref[slot, c] = _dot_nt(k, qqs[c])

    def consume(kb, slot, masked):
        scores = [s_ref[slot, c] for c in chains]
        if masked:
            kpos = kb * tk + kidx
            if kind == "diff":
                vis = jnp.right_shift(kpos, 6) <= jnp.right_shift(qpos, 6)
            elif kind == "fox":
                vis = kpos <= qpos
            else:
                vis = kpos < qpos
            vis = jnp.logical_and(vis, kpos < n_valid)
        vts = [vt_ref[0, c, kb] for c in chains]
        if kind == "sb":
            log_betas, laters = [], []
            for c in chains:
                s = scores[c]
                soft = jnp.log(1.0 + jnp.exp(-jnp.abs(s)))
                log_beta = jnp.minimum(s, 0.0) - soft
                log_keep = log_beta - s
                if masked:
                    log_keep = jnp.where(vis, log_keep, 0.0)
                hi = log_keep.astype(BF16)
                lo = (log_keep - hi.astype(F32)).astype(BF16)
                later = _dot(after2, jnp.concatenate([hi, lo], axis=0))
                laters.append(later + m_ref[c])
                m_ref[c] += later[0:1, :] + log_keep[0:1, :]
                log_betas.append(log_beta)
            weights = []
            for c in chains:
                w = jnp.exp(log_betas[c] + laters[c])
                if masked:
                    w = jnp.where(vis, w, 0.0)
                weights.append(w.astype(BF16))
            for c in chains:
                acc_ref[c] += _dot(vts[c], weights[c])
        else:
            probs, alphas = [], []
            for c in chains:
                s = scores[c]
                if masked:
                    s = jnp.where(vis, s, MASKED)
                m_prev = m_ref[c]
                m_new = jnp.maximum(m_prev, jnp.max(s, axis=0, keepdims=True))
                alphas.append(jnp.exp(m_prev - m_new))
                probs.append(jnp.exp(s - m_new).astype(BF16))
                m_ref[c] = m_new
            for c in chains:
                vt_aug = jnp.concatenate([vts[c], ones_rows], axis=0)
                acc_ref[c] = alphas[c] * acc_ref[c] + _dot(vt_aug, probs[c])

    n_full = q0 // tk
    qk(n_full, 0)
    qk(jnp.maximum(n_full - 1, 0), 1)
    consume(n_full, 0, True)

    def pair(j, carry):
        kb = n_full - 1 - 2 * j
        qk(jnp.maximum(kb - 1, 0), 0)
        consume(kb, 1, False)
        qk(jnp.maximum(kb - 2, 0), 1)
        consume(kb - 1, 0, False)
        return carry

    if kind == "sb":
        def top_sum():
            top = m_ref[0]
            for c in range(1, n_chain):
                top = jnp.maximum(top, m_ref[c])
            return jnp.max(top)

        def alive(carry):
            return jnp.logical_and(carry[0] < n_full // 2, carry[1] > SB_DEAD)

        def pair_and_check(carry):
            pair(carry[0], 0)
            return carry[0] + 1, top_sum()

        _, top = lax.while_loop(alive, pair_and_check, (jnp.int32(0), top_sum()))
        odd_block_left = jnp.logical_and(n_full % 2 == 1, top > SB_DEAD)
    else:
        lax.fori_loop(0, n_full // 2, pair, 0)
        odd_block_left = n_full % 2 == 1

    @pl.when(odd_block_left)
    def _():
        consume(0, 1, False)

    if kind == "diff":
        lp = lp_ref[...]
        lam = (jnp.exp(jnp.sum(lp[0:1] * lp[1:2], axis=1, keepdims=True))
               - jnp.exp(jnp.sum(lp[2:3] * lp[3:4], axis=1, keepdims=True)) + lam_init)
    for c in range(n_chain):
        acc = acc_ref[c]
        if kind == "sb":
            out_t = jnp.concatenate([acc[0:HEAD_DIM, 0:tq], acc[HEAD_DIM:LANES, tq:tq2]], axis=0)
        else:
            r = 1.0 / acc[LANES:LANES + 1, :]
            if kind == "fox":
                out_t = jnp.concatenate([acc[0:HEAD_DIM, 0:tq] * r[:, 0:tq],
                                         acc[HEAD_DIM:LANES, tq:tq2] * r[:, tq:tq2]], axis=0)
            else:
                a = acc[0:LANES, 0:tq] * r[:, 0:tq] - lam * (acc[0:LANES, tq:tq2] * r[:, tq:tq2])
                ms = jnp.mean(a * a, axis=0, keepdims=True)
                out_t = a * lax.rsqrt(ms + EPS) * sg_ref[...] * (1.0 - lam_init)
        if tq < LANES:
            out_t = jnp.concatenate([out_t, jnp.zeros((LANES, LANES - tq), F32)], axis=1)
        o_ref[0, :, c * LANES:(c + 1) * LANES] = out_t.T[0:tq].astype(o_ref.dtype)


def _attention(kind, q_arr, q_col, k_arr, k_col, vt_arr, n_col, *, n_chain, tq, tk, q_off, n_valid,
               extras=(), lam_init=0.0):
    B, Tq, _ = q_arr.shape
    Tk = k_arr.shape[1]
    n_kb = Tk // tk
    assert Tq % tq == 0 and Tk % tk == 0 and tk % tq == 0 and q_off % tk == 0 and n_valid <= Tk
    assert tq & (tq - 1) == 0 and n_col % n_chain == 0 and q_col % n_chain == 0 and k_col % n_chain == 0
    wide = n_chain * LANES
    in_specs = [
        pl.BlockSpec((1, tq, wide), lambda b, c, i: (b, i, q_col // n_chain + c)),
        pl.BlockSpec((1, Tk, wide), lambda b, c, i: (b, 0, k_col // n_chain + c)),
        pl.BlockSpec((1, n_chain, n_kb, LANES, tk), lambda b, c, i: (b, c, 0, 0, 0)),
    ]
    acc_rows = LANES if kind == "sb" else LANES + BF16_ROWS
    scratch = [pltpu.VMEM((n_chain, acc_rows, 2 * tq), F32), pltpu.VMEM((n_chain, 1, 2 * tq), F32),
               pltpu.VMEM((2, n_chain, tk, 2 * tq), F32)]
    if kind == "diff":
        lp, sg = extras
        in_specs += [pl.BlockSpec(lp.shape, lambda b, c, i: (0, 0)),
                     pl.BlockSpec(sg.shape, lambda b, c, i: (0, 0))]
    elif kind == "fox":
        (nd,) = extras
        in_specs.append(pl.BlockSpec((1, n_chain, Tk, LANES), lambda b, c, i: (b, c, 0, 0)))
    kernel = functools.partial(_attn_kernel, kind=kind, n_chain=n_chain, tq=tq, tk=tk, q_off=q_off,
                               n_valid=n_valid, lam_init=lam_init)
    return pl.pallas_call(
        kernel,
        grid=(B, n_col // n_chain, Tq // tq),
        in_specs=in_specs,
        out_specs=pl.BlockSpec((1, tq, wide), lambda b, c, i: (b, i, c)),
        out_shape=jax.ShapeDtypeStruct((B, Tq, n_col * LANES), BF16),
        scratch_shapes=scratch,
        compiler_params=_params("arbitrary", "arbitrary", "arbitrary"),
        name="attn_" + kind,
    )(q_arr, k_arr, vt_arr, *extras)


def _key_blocks_t(v, tk):
    B, Tk, W = v.shape
    return v.reshape(B, Tk // tk, tk, W // LANES, LANES).transpose(0, 3, 1, 4, 2)


def _cumsum_kernel(x_ref, o_ref, *, blk):
    T, H = x_ref.shape[1:]
    n_pair = H // 2
    r = lax.broadcasted_iota(jnp.int32, (blk, blk), 0)
    c = lax.broadcasted_iota(jnp.int32, (blk, blk), 1)
    upto = jnp.where(c <= r, 1.0, 0.0).astype(BF16)
    head = lax.broadcasted_iota(jnp.int32, (H, n_pair * LANES), 0)
    col = lax.broadcasted_iota(jnp.int32, (H, n_pair * LANES), 1)
    here = jnp.right_shift(col, 7) == jnp.right_shift(head, 1)
    lane = jnp.bitwise_and(col, LANES - 1) - 3 * jnp.bitwise_and(head, 1)
    place = [jnp.where(jnp.logical_and(here, lane == p), 1.0, 0.0).astype(BF16) for p in range(3)]

    def chunk(j, carry):
        rows = pl.ds(pl.multiple_of(j * blk, blk), blk)
        hi, mid, lo = _split3(x_ref[0, rows, :])
        total = _dot(upto, hi) + _dot(upto, mid) + _dot(upto, lo) + carry
        pieces = _split3(-total)
        lanes = sum(_dot(piece, sel) for piece, sel in zip(pieces, place))
        for pair in range(n_pair):
            o_ref[0, pair, rows, :] = lanes[:, pair * LANES:(pair + 1) * LANES].astype(BF16)
        return total[blk - 1:blk, :]

    lax.fori_loop(0, T // blk, chunk, jnp.zeros((1, H), F32))


def _neg_cumsum(lf, blk):
    B, T, H = lf.shape
    return pl.pallas_call(
        functools.partial(_cumsum_kernel, blk=blk),
        grid=(B,),
        in_specs=[pl.BlockSpec((1, T, H), lambda b: (b, 0, 0))],
        out_specs=pl.BlockSpec((1, H // 2, T, LANES), lambda b: (b, 0, 0, 0)),
        out_shape=jax.ShapeDtypeStruct((B, H // 2, T, LANES), BF16),
        compiler_params=_params("arbitrary"),
        name="neg_cumsum",
    )(lf)


def _outproj_kernel(*refs, nseg):
    a_refs = refs[:nseg]
    w_ref, x_ref, g_ref, o_ref = refs[nseg:]
    bb, tt, d = x_ref.shape
    acc = None
    off = 0
    for a_ref in a_refs:
        width = a_ref.shape[2]
        part = _dot(a_ref[...].reshape(bb * tt, width), w_ref[off:off + width, :])
        acc = part if acc is None else acc + part
        off += width
    o_ref[...] = x_ref[...] + g_ref[...] * acc.reshape(bb, tt, d)


def _outproj(mixed, w_bf, x, gate, bb, tt):
    B, T, _ = x.shape
    x_spec, _, m_spec = _row_specs(bb, tt)
    return pl.pallas_call(
        functools.partial(_outproj_kernel, nseg=len(mixed)),
        grid=(B // bb, T // tt),
        in_specs=[_out_spec(bb, tt, a.shape[2]) for a in mixed]
        + [pl.BlockSpec(w_bf.shape, lambda i, j: (0, 0)), x_spec, m_spec],
        out_specs=x_spec,
        out_shape=jax.ShapeDtypeStruct(x.shape, F32),
        compiler_params=_params("arbitrary", "arbitrary"),
        name="outproj",
    )(*mixed, w_bf, x, gate)


def _route_t(h, wr_t, br):
    tm = h.shape[0]
    logits = lax.dot_general(wr_t, h, (((1,), (1,)), ((), ())), preferred_element_type=F32,
                             precision=lax.Precision.HIGHEST)
    s = jax.nn.sigmoid(logits)
    sel = s + br
    row = lax.broadcasted_iota(jnp.int32, (N_EXPERTS, tm), 0).astype(F32)
    neg = -jnp.inf

    def first_argmax(vals):
        top = jnp.max(vals, axis=0, keepdims=True)
        idx = jnp.min(jnp.where(vals == top, row, float(N_EXPERTS)), axis=0, keepdims=True)
        return top, idx

    best = None
    for g in range(N_GROUPS):
        in_g = (row >= g * EXPERTS_PER_GROUP) & (row < (g + 1) * EXPERTS_PER_GROUP)
        top1, i1 = first_argmax(jnp.where(in_g, sel, neg))
        top2, i2 = first_argmax(jnp.where(in_g & (row != i1), sel, neg))
        cand = (top1 + top2, i1, i2, jnp.full((1, tm), float(g), F32))
        if best is None:
            best = cand
        else:
            better = cand[0] > best[0]
            best = tuple(jnp.where(better, new, old) for new, old in zip(cand, best))
    _, i1, i2, group = best
    picked = jnp.where((row == i1) | (row == i2), s, 0.0)
    return picked / jnp.sum(picked, axis=0, keepdims=True), group


def _moe_kernel(*refs, final):
    if final:
        (x_ref, g_ref, sc_ref, sh_ref, g2_ref, wrt_ref, br_ref, wg_ref, wu_ref, wd_ref, gf_ref,
         o_ref, h_ref, xg_ref, yg_ref, gg_ref, gate_ref, pos_ref, before_ref, n_ref) = refs
    else:
        (x_ref, g_ref, sc_ref, sh_ref, g2_ref, wrt_ref, br_ref, wg_ref, wu_ref, wd_ref,
         o_ref, h_ref, xg_ref, yg_ref, gg_ref, gate_ref, pos_ref, before_ref, n_ref) = refs
    bb, tt, d = x_ref.shape
    tm = bb * tt
    e = pl.program_id(2)
    group = e // EXPERTS_PER_GROUP

    @pl.when((pl.program_id(0) == 0) & (pl.program_id(1) == 0) & (e == 0))
    def _():
        def fill(j, carry):
            rows = pl.ds(pl.multiple_of(j * MOE_CHUNK, MOE_CHUNK), MOE_CHUNK)
            r = lax.broadcasted_iota(jnp.int32, (MOE_CHUNK, tm), 0) + j * MOE_CHUNK
            c = lax.broadcasted_iota(jnp.int32, (MOE_CHUNK, tm), 1)
            before_ref[rows, :] = jnp.where(r < c, 1.0, 0.0).astype(BF16)
            return carry
        lax.fori_loop(0, tm // MOE_CHUNK, fill, 0)

    @pl.when(e == 0)
    def _():
        h = _norm_mod(x_ref[...], g_ref[...], sc_ref[...], sh_ref[...])
        h_ref[...] = h.astype(BF16)
        gates, grp = _route_t(h, wrt_ref[...], br_ref[...])
        gates = jnp.concatenate([gates, jnp.zeros((LANES - N_EXPERTS, tm), F32)], axis=0)
        hi = gates.astype(BF16)
        gate_ref[0] = hi
        gate_ref[1] = (gates - hi.astype(F32)).astype(BF16)
        grow = lax.broadcasted_iota(jnp.int32, pos_ref.shape, 0).astype(F32)
        member = jnp.where(grow == grp, 1.0, 0.0)
        rank = _dot(member.astype(BF16), before_ref[...])
        pos_ref[...] = jnp.where(member > 0.0, rank, -1.0)
        for g in range(N_GROUPS):
            count = jnp.sum(member[g:g + 1, :]).astype(jnp.int32)
            n_ref[g] = (count + MOE_CHUNK - 1) // MOE_CHUNK
        o_ref[...] = jnp.zeros_like(o_ref)

    n_chunks = n_ref[group]

    def chunk_rows(j):
        return pl.ds(pl.multiple_of(j * MOE_CHUNK, MOE_CHUNK), MOE_CHUNK)

    def one_hot(j):
        want = (lax.broadcasted_iota(jnp.int32, (MOE_CHUNK, tm), 0) + j * MOE_CHUNK).astype(F32)
        return jnp.where(pos_ref[pl.ds(group, 1), :] == want, 1.0, 0.0).astype(BF16)

    @pl.when(e % EXPERTS_PER_GROUP == 0)
    def _():
        def gather(j, carry):
            rows = chunk_rows(j)
            pick = one_hot(j)
            xg_ref[rows, :] = _dot(pick, h_ref[...]).astype(BF16)
            gg_ref[rows, :] = _dot_nt(pick, gate_ref[0]) + _dot_nt(pick, gate_ref[1])
            yg_ref[rows, :] = jnp.zeros((MOE_CHUNK, d), F32)
            return carry
        lax.fori_loop(0, n_chunks, gather, 0)

    wg, wu, wd = wg_ref[0, 0], wu_ref[0, 0], wd_ref[0, 0]
    lane = lax.broadcasted_iota(jnp.int32, (MOE_CHUNK, LANES), 1)

    def expert(j, carry):
        rows = chunk_rows(j)
        xr = xg_ref[rows, :]
        gt = _dot(xr, wg)
        up = _dot(xr, wu)
        ge = jnp.sum(jnp.where(lane == e, gg_ref[rows, :], 0.0), axis=1, keepdims=True)
        a = (gt * jax.nn.sigmoid(gt)) * up * ge
        yg_ref[rows, :] += _dot(a.astype(BF16), wd)
        return carry

    lax.fori_loop(0, n_chunks, expert, 0)

    @pl.when(e % EXPERTS_PER_GROUP == EXPERTS_PER_GROUP - 1)
    def _():
        def scatter(j, carry):
            back = lax.dot_general(one_hot(j), yg_ref[chunk_rows(j), :].astype(BF16),
                                   (((0,), (0,)), ((), ())), preferred_element_type=F32)
            o_ref[...] += back.reshape(bb, tt, d)
            return carry
        lax.fori_loop(0, n_chunks, scatter, 0)

    @pl.when(e == N_EXPERTS - 1)
    def _():
        y = x_ref[...] + g2_ref[...] * o_ref[...]
        if final:
            ms = jnp.mean(y * y, axis=-1, keepdims=True)
            y = y * lax.rsqrt(ms + EPS) * gf_ref[...]
        o_ref[...] = y


def _moe(x, g, sc, sh, g2, w_router, b_router, w_gate, w_up, w_down, layer, g_final, bb, tt):
    B, T, _ = x.shape
    final = g_final is not None
    x_spec = pl.BlockSpec((bb, tt, D_MODEL), lambda i, j, e: (i, j, 0))
    g_spec = pl.BlockSpec((1, D_MODEL), lambda i, j, e: (0, 0))
    m_spec = pl.BlockSpec((bb, 1, D_MODEL), lambda i, j, e: (i, 0, 0))
    in_specs = [x_spec, g_spec, m_spec, m_spec, m_spec,
                pl.BlockSpec((N_EXPERTS, D_MODEL), lambda i, j, e: (0, 0)),
                pl.BlockSpec((N_EXPERTS, 1), lambda i, j, e: (0, 0)),
                pl.BlockSpec((1, 1, D_MODEL, D_EXPERT), lambda i, j, e: (layer, e, 0, 0)),
                pl.BlockSpec((1, 1, D_MODEL, D_EXPERT), lambda i, j, e: (layer, e, 0, 0)),
                pl.BlockSpec((1, 1, D_EXPERT, D_MODEL), lambda i, j, e: (layer, e, 0, 0))]
    args = [x, g, sc, sh, g2, w_router.T, b_router.reshape(N_EXPERTS, 1), w_gate, w_up, w_down]
    if final:
        in_specs.append(g_spec)
        args.append(g_final)
    tm = bb * tt
    assert tm % MOE_CHUNK == 0
    packed = tm
    scratch = [pltpu.VMEM((tm, D_MODEL), BF16),
               pltpu.VMEM((packed, D_MODEL), BF16),
               pltpu.VMEM((packed, D_MODEL), F32),
               pltpu.VMEM((packed, LANES), F32),
               pltpu.VMEM((2, LANES, tm), BF16),
               pltpu.VMEM((2 * N_GROUPS, tm), F32),
               pltpu.VMEM((tm, tm), BF16),
               pltpu.SMEM((N_GROUPS,), jnp.int32)]
    return pl.pallas_call(
        functools.partial(_moe_kernel, final=final),
        grid=(B // bb, T // tt, N_EXPERTS),
        in_specs=in_specs,
        out_specs=x_spec,
        out_shape=jax.ShapeDtypeStruct(x.shape, F32),
        scratch_shapes=scratch,
        compiler_params=_params("arbitrary", "arbitrary", "arbitrary"),
        name="moe",
    )(*args)


def _rope_tables(pos):
    half = ROPE_DIM // 2
    inv = ROPE_THETA ** (-jnp.arange(half, dtype=F32) / half)
    ang = pos.astype(F32)[:, None] * inv[None, :]
    cos, sin = jnp.cos(ang), jnp.sin(ang)
    ones = jnp.ones((pos.shape[0], HEAD_DIM - ROPE_DIM), F32)
    zeros = jnp.zeros_like(ones)
    zh = jnp.zeros_like(sin)
    c64 = jnp.concatenate([cos, cos, ones], axis=1)
    s1_64 = jnp.concatenate([-sin, zh, zeros], axis=1)
    s2_64 = jnp.concatenate([zh, sin, zeros], axis=1)
    return tuple(jnp.concatenate([t, t], axis=1) for t in (c64, s1_64, s2_64))


def _pad_time(a, total):
    if total == a.shape[1]:
        return a
    return jnp.pad(a, ((0, 0), (0, total - a.shape[1])) + ((0, 0),) * (a.ndim - 2))


def _trunk(x, mod, past, wts, cfg):
    B, T, _ = x.shape
    P = 0 if past is None else past["diff_k"].shape[2]
    bb, tt_proj, tt_out, tt_moe, tq, tk = cfg
    wide = past is None and bb == 1 and tt_proj == tk
    n_valid = P + T
    Tk = -(-n_valid // tk) * tk
    Tq = -(-T // tq) * tq
    tables = _rope_tables(P + jnp.arange(T))
    if bb > 1:
        tables = tuple(jnp.tile(t, (bb, 1)) for t in tables)
    attn = functools.partial(_attention, n_chain=N_CHAIN, tq=tq, tk=tk, q_off=P, n_valid=n_valid)
    leaves_even, leaves_odd = [], []

    def with_cache(qkv, k_cols, v_cols, cache_k, cache_v):
        width = k_cols[1] - k_cols[0]
        k_all = jnp.concatenate([cache_k.reshape(B, P, width).astype(BF16), qkv[:, :, k_cols[0]:k_cols[1]]], axis=1)
        v_all = jnp.concatenate([cache_v.reshape(B, P, width).astype(BF16), qkv[:, :, v_cols[0]:v_cols[1]]], axis=1)
        return _pad_time(k_all, Tk), _key_blocks_t(_pad_time(v_all, Tk), tk)

    def heads_last(leaf_t, n_heads):
        return leaf_t.reshape(B, n_heads, -1, T).transpose(0, 3, 1, 2)

    for l in range(DEPTH):
        sh1, sc1, g1, sh2, sc2, g2 = [mod[l, :, k * D_MODEL:(k + 1) * D_MODEL][:, None, :] for k in range(6)]
        i = l // 2
        g_mix = wts["norm_mix"][l][None, :]
        if l % 2 == 0:
            lam_init = 0.8 - 0.6 * math.exp(-0.3 * l)
            lp = jnp.stack([wts["diff_lq1"][i], wts["diff_lk1"][i], wts["diff_lq2"][i], wts["diff_lk2"][i]])
            diff_extras = (lp, wts["diff_subln"][i][:, None])
            if wide:
                qk, ak, av, bkt, bvt, dvt, svt = _proj_even_wide(x, g_mix, sc1, sh1, wts["w_in_even"][i], tables, tt_proj)
                leaves_even.append((ak.reshape(B, T, N_DIFF_HEADS, LANES), av.reshape(B, T, N_DIFF_HEADS, LANES),
                                    heads_last(bkt, N_SB_HEADS), heads_last(bvt, N_SB_HEADS)))
                q_arr, q_cols = qk, (0, 2 * SEG // LANES)
                dk, dk_col, sk, sk_col = qk, SEG // LANES, qk, 3 * SEG // LANES
            else:
                qkv, ak, av, bk, bv = _proj_even(x, g_mix, sc1, sh1, wts["w_in_even"][i], tables, bb, tt_proj)
                leaves_even.append((ak.reshape(B, T, N_DIFF_HEADS, LANES), av.reshape(B, T, N_DIFF_HEADS, LANES),
                                    bk.reshape(B, T, N_SB_HEADS, HEAD_DIM), bv.reshape(B, T, N_SB_HEADS, HEAD_DIM)))
                q_arr, q_cols = _pad_time(qkv, Tq), (0, 3 * SEG // LANES)
                dk, dvt = with_cache(qkv, (SEG, 2 * SEG), (2 * SEG, 3 * SEG), past["diff_k"][i], past["diff_v"][i])
                sk, svt = with_cache(qkv, (4 * SEG, 5 * SEG), (5 * SEG, 6 * SEG), past["sb_k"][i], past["sb_v"][i])
                dk_col = sk_col = 0
            a_out = attn("diff", q_arr, q_cols[0], dk, dk_col, dvt, N_DIFF_HEADS, extras=diff_extras, lam_init=lam_init)
            b_out = attn("sb", q_arr, q_cols[1], sk, sk_col, svt, N_SB_HEADS // 2)
            x = _outproj([a_out[:, :T], b_out[:, :T]], wts["w_out_even"][i], x, g1, bb, tt_out)
        else:
            if wide:
                qk, kt, vt, fvt, lf = _proj_odd_wide(x, g_mix, sc1, sh1, wts["w_in_odd"][i], wts["w_f_odd"][i],
                                                     wts["b_forget"][i], tt_proj)
                leaves_odd.append((heads_last(kt, N_FOX_HEADS), heads_last(vt, N_FOX_HEADS), lf))
                q_arr, fk, fk_col, lf_all = qk, qk, 2 * SEG // LANES, lf
            else:
                qkv, k, v, lf = _proj_odd(x, g_mix, sc1, sh1, wts["w_in_odd"][i], wts["w_f_odd"][i],
                                          wts["b_forget"][i], bb, tt_proj)
                leaves_odd.append((k.reshape(B, T, N_FOX_HEADS, HEAD_DIM), v.reshape(B, T, N_FOX_HEADS, HEAD_DIM), lf))
                q_arr = _pad_time(qkv, Tq)
                fk, fvt = with_cache(qkv, (2 * SEG, 4 * SEG), (4 * SEG, 6 * SEG), past["fox_k"][i], past["fox_v"][i])
                fk_col = 0
                lf_all = jnp.concatenate([past["fox_logf"][i], lf], axis=1)
            nd = _neg_cumsum(_pad_time(lf_all, Tk), tk)
            o = attn("fox", q_arr, 0, fk, fk_col, fvt, N_FOX_HEADS // 2, extras=(nd,))
            x = _outproj([o[:, :T]], wts["w_out_odd"][i], x, g1, bb, tt_out)
        g_final = wts["norm_final"][None, :] if l == DEPTH - 1 else None
        x = _moe(x, wts["norm_ffn"][l][None, :], sc2, sh2, g2, wts["w_router"], wts["b_router"],
                 wts["w_gate"], wts["w_up"], wts["w_down"], l, g_final, bb, tt_moe)
    return x, leaves_even, leaves_odd


def kernel(x_prompt, x_sample, c_prompt, c_sample, cache_diff_k, cache_diff_v, cache_sb_k, cache_sb_v, cache_fox_k, cache_fox_v, cache_fox_logf, w_ada, b_ada, norm_mix, norm_ffn, w_in_even, w_out_even, diff_lq1, diff_lk1, diff_lq2, diff_lk2, diff_subln, w_in_odd, b_forget, w_out_odd, w_router, b_router, w_gate, w_up, w_down, norm_final):
    Bp, Tp, _ = x_prompt.shape
    Bs, Ts, _ = x_sample.shape
    wts = {
        "norm_mix": norm_mix, "norm_ffn": norm_ffn, "norm_final": norm_final,
        "w_in_even": w_in_even.astype(BF16), "w_out_even": w_out_even.astype(BF16),
        "w_in_odd": w_in_odd[:, :, :3 * FOX_W].astype(BF16), "w_f_odd": w_in_odd[:, :, 3 * FOX_W:].astype(BF16),
        "w_out_odd": w_out_odd.astype(BF16), "b_forget": b_forget,
        "diff_lq1": diff_lq1, "diff_lk1": diff_lk1, "diff_lq2": diff_lq2, "diff_lk2": diff_lk2,
        "diff_subln": diff_subln, "w_router": w_router, "b_router": b_router,
        "w_gate": w_gate.astype(BF16), "w_up": w_up.astype(BF16), "w_down": w_down.astype(BF16),
    }
    past = {"diff_k": cache_diff_k, "diff_v": cache_diff_v, "sb_k": cache_sb_k, "sb_v": cache_sb_v,
            "fox_k": cache_fox_k, "fox_v": cache_fox_v, "fox_logf": cache_fox_logf}

    rows = Bp + Bs
    rows_pad = -(-rows // 16) * 16
    c_all = jnp.pad(jnp.concatenate([c_prompt, c_sample], axis=0), ((0, rows_pad - rows), (0, 0)))
    mod = _ada_mod(c_all, w_ada, b_ada)

    cfg_prompt = (1, 256, 512, 1024, 128, 256)
    cfg_sample = (Bs, Ts, Ts, Ts, 64, 256)
    y_p, even_p, odd_p = _trunk(x_prompt, mod[:, :Bp], None, wts, cfg_prompt)
    y_s, even_s, odd_s = _trunk(x_sample, mod[:, Bp:rows], past, wts, cfg_sample)

    def layers(rows):
        return rows[0][None] if len(rows) == 1 else jnp.stack(rows, axis=0)

    def pack(even, odd):
        return tuple(layers(r) for r in zip(*even)) + tuple(layers(r) for r in zip(*odd))

    return (y_p, y_s) + pack(even_p, odd_p) + pack(even_s, odd_s)
```

```python
import functools
import math

import jax
import jax.numpy as jnp
from jax import lax
from jax.experimental import pallas as pl
from jax.experimental.pallas import tpu as pltpu

D_MODEL = 1024
DEPTH = 2
CHUNK = 64
HEAD_DIM = 64
N_DIFF_HEADS = D_MODEL // 256
N_SB_HEADS = D_MODEL // 128
N_FOX_HEADS = D_MODEL // HEAD_DIM
DIFF_W = N_DIFF_HEADS * 2 * HEAD_DIM
SB_W = N_SB_HEADS * HEAD_DIM
FOX_W = N_FOX_HEADS * HEAD_DIM
ROPE_DIM = HEAD_DIM // 4
ROPE_THETA = 500000.0
N_EXPERTS = 16
N_GROUPS = 4
EXPERTS_PER_GROUP = N_EXPERTS // N_GROUPS
D_EXPERT = D_MODEL // 2
EPS = 1e-6

LANES = 128
BF16_ROWS = 16
SEG = 512
QK_SCALE = HEAD_DIM ** -0.5
MASKED = -1e30
N_CHAIN = 4
SB_DEAD = -110.0
MOE_CHUNK = 128
VMEM_LIMIT = 48 * 1024 * 1024

F32 = jnp.float32
BF16 = jnp.bfloat16


def _params(*sem):
    return pltpu.CompilerParams(dimension_semantics=sem, vmem_limit_bytes=VMEM_LIMIT)


def _dot(a, b):
    return jnp.dot(a, b, preferred_element_type=F32)


def _dot_nt(a, b):
    return lax.dot_general(a, b, (((1,), (1,)), ((), ())), preferred_element_type=F32)


def _norm_mod(x, g, sc, sh):
    bb, tt, d = x.shape
    ms = jnp.mean(x * x, axis=-1, keepdims=True)
    h = x * lax.rsqrt(ms + EPS) * g
    h = h * (1.0 + sc) + sh
    return h.reshape(bb * tt, d)


def _log_sigmoid(x):
    return jnp.minimum(x, 0.0) - jnp.log(1.0 + jnp.exp(-jnp.abs(x)))


def _split3(x):
    hi = x.astype(BF16)
    r1 = x - hi.astype(F32)
    mid = r1.astype(BF16)
    lo = (r1 - mid.astype(F32)).astype(BF16)
    return hi, mid, lo


def _ada_kernel(c_ref, w_ref, b_ref, o_ref):
    c = c_ref[...]
    a = (c * jax.nn.sigmoid(c)).astype(BF16)
    o_ref[0] = _dot(a, w_ref[0].astype(BF16)) + b_ref[0]


def _ada_mod(c_all, w_ada, b_ada):
    rows = c_all.shape[0]
    tn = 1536
    width = 6 * D_MODEL
    return pl.pallas_call(
        _ada_kernel,
        grid=(DEPTH, width // tn),
        in_specs=[
            pl.BlockSpec((rows, D_MODEL), lambda l, j: (0, 0)),
            pl.BlockSpec((1, D_MODEL, tn), lambda l, j: (l, 0, j)),
            pl.BlockSpec((1, 1, tn), lambda l, j: (l, 0, j)),
        ],
        out_specs=pl.BlockSpec((1, rows, tn), lambda l, j: (l, 0, j)),
        out_shape=jax.ShapeDtypeStruct((DEPTH, rows, width), F32),
        compiler_params=_params("arbitrary", "arbitrary"),
        name="ada_mod",
    )(c_all, w_ada, b_ada.reshape(DEPTH, 1, width))


def _rope(p, cos, s1, s2):
    outs = []
    for c in range(SEG // LANES):
        pc = p[:, c * LANES:(c + 1) * LANES]
        up = pltpu.roll(pc, LANES - ROPE_DIM // 2, 1)
        down = pltpu.roll(pc, ROPE_DIM // 2, 1)
        outs.append(pc * cos + up * s1 + down * s2)
    return jnp.concatenate(outs, axis=1)


def _proj_even_kernel(x_ref, g_ref, sc_ref, sh_ref, w_ref, cos_ref, s1_ref, s2_ref,
                      qkv_ref, ak_ref, av_ref, bk_ref, bv_ref):
    bb, tt, _ = x_ref.shape
    h = _norm_mod(x_ref[...], g_ref[...], sc_ref[...], sh_ref[...]).astype(BF16)
    cos, s1, s2 = cos_ref[...], s1_ref[...], s2_ref[...]

    def seg(i):
        return _dot(h, w_ref[:, i * SEG:(i + 1) * SEG])

    def put(i, val, leaf_ref):
        val3 = val.reshape(bb, tt, SEG)
        qkv_ref[:, :, i * SEG:(i + 1) * SEG] = val3.astype(BF16)
        if leaf_ref is not None:
            leaf_ref[...] = val3

    put(0, _rope(seg(0), cos, s1, s2) * QK_SCALE, None)
    put(1, _rope(seg(1), cos, s1, s2), ak_ref)
    put(2, seg(2), av_ref)
    put(3, seg(3) * QK_SCALE, None)
    put(4, seg(4), bk_ref)
    put(5, seg(5), bv_ref)


def _rows_by_head(ref, val, n_heads):
    tt = val.shape[0]
    for hd in range(n_heads):
        ref[0, pl.ds(hd, tt, stride=n_heads), :] = val[:, hd * LANES:(hd + 1) * LANES]


def _key_blocks(ref, first, val_t):
    n = val_t.shape[0] // LANES
    ref[0, first:first + n, 0] = val_t.reshape(n, LANES, val_t.shape[1]).astype(BF16)


def _proj_even_wide_kernel(x_ref, g_ref, sc_ref, sh_ref, w_ref, cos_ref, s1_ref, s2_ref,
                           qk_ref, ak_ref, av_ref, bkt_ref, bvt_ref, dvt_ref, svt_ref):
    h = _norm_mod(x_ref[...], g_ref[...], sc_ref[...], sh_ref[...]).astype(BF16)
    cos, s1, s2 = cos_ref[...], s1_ref[...], s2_ref[...]

    def seg(i):
        return _dot(h, w_ref[:, i * SEG:(i + 1) * SEG])

    qk_ref[0, :, 0:SEG] = (_rope(seg(0), cos, s1, s2) * QK_SCALE).astype(BF16)
    ak = _rope(seg(1), cos, s1, s2)
    qk_ref[0, :, SEG:2 * SEG] = ak.astype(BF16)
    _rows_by_head(ak_ref, ak, N_DIFF_HEADS)
    av = seg(2)
    _rows_by_head(av_ref, av, N_DIFF_HEADS)
    _key_blocks(dvt_ref, 0, av.T)
    qk_ref[0, :, 2 * SEG:3 * SEG] = (seg(3) * QK_SCALE).astype(BF16)
    bk = seg(4)
    qk_ref[0, :, 3 * SEG:4 * SEG] = bk.astype(BF16)
    bkt_ref[0] = bk.T
    bvt = seg(5).T
    bvt_ref[0] = bvt
    _key_blocks(svt_ref, 0, bvt)


def _proj_odd_wide_kernel(x_ref, g_ref, sc_ref, sh_ref, w_ref, wf_ref, bf_ref,
                          qk_ref, kt_ref, vt_ref, fvt_ref, lf_ref):
    h = _norm_mod(x_ref[...], g_ref[...], sc_ref[...], sh_ref[...]).astype(BF16)

    def seg(i):
        return _dot(h, w_ref[:, i * SEG:(i + 1) * SEG])

    for i in range(2):
        qk_ref[0, :, i * SEG:(i + 1) * SEG] = (seg(i) * QK_SCALE).astype(BF16)
    for i in range(2):
        k = seg(2 + i)
        qk_ref[0, :, (2 + i) * SEG:(3 + i) * SEG] = k.astype(BF16)
        kt_ref[0, i * SEG:(i + 1) * SEG, :] = k.T
        v_t = seg(4 + i).T
        vt_ref[0, i * SEG:(i + 1) * SEG, :] = v_t
        _key_blocks(fvt_ref, i * (SEG // LANES), v_t)
    fl = _dot(h, wf_ref[...]) + bf_ref[...]
    lf_ref[0] = _log_sigmoid(fl)


def _proj_odd_kernel(x_ref, g_ref, sc_ref, sh_ref, w_ref, wf_ref, bf_ref,
                     qkv_ref, k_ref, v_ref, lf_ref):
    bb, tt, _ = x_ref.shape
    h = _norm_mod(x_ref[...], g_ref[...], sc_ref[...], sh_ref[...]).astype(BF16)

    def seg(i):
        return _dot(h, w_ref[:, i * SEG:(i + 1) * SEG]).reshape(bb, tt, SEG)

    for i in range(2):
        qkv_ref[:, :, i * SEG:(i + 1) * SEG] = (seg(i) * QK_SCALE).astype(BF16)
    for i in range(2, 6):
        val = seg(i)
        qkv_ref[:, :, i * SEG:(i + 1) * SEG] = val.astype(BF16)
        leaf_ref = k_ref if i < 4 else v_ref
        leaf_ref[:, :, (i % 2) * SEG:(i % 2 + 1) * SEG] = val
    fl = _dot(h, wf_ref[...]) + bf_ref[...]
    lf_ref[...] = _log_sigmoid(fl).reshape(bb, tt, N_FOX_HEADS)


def _row_specs(bb, tt):
    x_spec = pl.BlockSpec((bb, tt, D_MODEL), lambda i, j: (i, j, 0))
    g_spec = pl.BlockSpec((1, D_MODEL), lambda i, j: (0, 0))
    m_spec = pl.BlockSpec((bb, 1, D_MODEL), lambda i, j: (i, 0, 0))
    return x_spec, g_spec, m_spec


def _out_spec(bb, tt, width):
    return pl.BlockSpec((bb, tt, width), lambda i, j: (i, j, 0))


def _proj_even(x, g, sc, sh, w_bf, tables, bb, tt):
    B, T, _ = x.shape
    x_spec, g_spec, m_spec = _row_specs(bb, tt)
    tm = bb * tt
    t_spec = pl.BlockSpec((tm, LANES), (lambda i, j: (j, 0)) if bb == 1 else (lambda i, j: (0, 0)))
    leaf = jax.ShapeDtypeStruct((B, T, SEG), F32)
    return pl.pallas_call(
        _proj_even_kernel,
        grid=(B // bb, T // tt),
        in_specs=[x_spec, g_spec, m_spec, m_spec,
                  pl.BlockSpec(w_bf.shape, lambda i, j: (0, 0)), t_spec, t_spec, t_spec],
        out_specs=[_out_spec(bb, tt, 6 * SEG)] + [_out_spec(bb, tt, SEG)] * 4,
        out_shape=[jax.ShapeDtypeStruct((B, T, 6 * SEG), BF16), leaf, leaf, leaf, leaf],
        compiler_params=_params("arbitrary", "arbitrary"),
        name="proj_even",
    )(x, g, sc, sh, w_bf, *tables)


def _wide_specs(B, T, tt):
    def rows(width, dtype):
        return pl.BlockSpec((1, tt, width), lambda i, j: (i, j, 0)), jax.ShapeDtypeStruct((B, T, width), dtype)

    def by_head(n_heads):
        return (pl.BlockSpec((1, tt * n_heads, LANES), lambda i, j: (i, j, 0)),
                jax.ShapeDtypeStruct((B, T * n_heads, LANES), F32))

    def transposed(width):
        return pl.BlockSpec((1, width, tt), lambda i, j: (i, 0, j)), jax.ShapeDtypeStruct((B, width, T), F32)

    def key_blocks(n_col):
        return (pl.BlockSpec((1, n_col, 1, LANES, tt), lambda i, j: (i, 0, j, 0, 0)),
                jax.ShapeDtypeStruct((B, n_col, T // tt, LANES, tt), BF16))

    return rows, by_head, transposed, key_blocks


def _proj_even_wide(x, g, sc, sh, w_bf, tables, tt):
    B, T, _ = x.shape
    x_spec, g_spec, m_spec = _row_specs(1, tt)
    t_spec = pl.BlockSpec((tt, LANES), lambda i, j: (j, 0))
    rows, by_head, transposed, key_blocks = _wide_specs(B, T, tt)
    outs = [rows(4 * SEG, BF16), by_head(N_DIFF_HEADS), by_head(N_DIFF_HEADS), transposed(SEG), transposed(SEG),
            key_blocks(SEG // LANES), key_blocks(SEG // LANES)]
    return pl.pallas_call(
        _proj_even_wide_kernel,
        grid=(B, T // tt),
        in_specs=[x_spec, g_spec, m_spec, m_spec,
                  pl.BlockSpec(w_bf.shape, lambda i, j: (0, 0)), t_spec, t_spec, t_spec],
        out_specs=[o[0] for o in outs],
        out_shape=[o[1] for o in outs],
        compiler_params=_params("arbitrary", "arbitrary"),
        name="proj_even_wide",
    )(x, g, sc, sh, w_bf, *tables)


def _proj_odd_wide(x, g, sc, sh, w_bf, wf_bf, b_f, tt):
    B, T, _ = x.shape
    x_spec, g_spec, m_spec = _row_specs(1, tt)
    rows, by_head, transposed, key_blocks = _wide_specs(B, T, tt)
    outs = [rows(4 * SEG, BF16), transposed(FOX_W), transposed(FOX_W), key_blocks(FOX_W // LANES),
            rows(N_FOX_HEADS, F32)]
    return pl.pallas_call(
        _proj_odd_wide_kernel,
        grid=(B, T // tt),
        in_specs=[x_spec, g_spec, m_spec, m_spec,
                  pl.BlockSpec(w_bf.shape, lambda i, j: (0, 0)),
                  pl.BlockSpec(wf_bf.shape, lambda i, j: (0, 0)),
                  pl.BlockSpec((1, N_FOX_HEADS), lambda i, j: (0, 0))],
        out_specs=[o[0] for o in outs],
        out_shape=[o[1] for o in outs],
        compiler_params=_params("arbitrary", "arbitrary"),
        name="proj_odd_wide",
    )(x, g, sc, sh, w_bf, wf_bf, b_f.reshape(1, N_FOX_HEADS))


def _proj_odd(x, g, sc, sh, w_bf, wf_bf, b_f, bb, tt):
    B, T, _ = x.shape
    x_spec, g_spec, m_spec = _row_specs(bb, tt)
    leaf = jax.ShapeDtypeStruct((B, T, FOX_W), F32)
    return pl.pallas_call(
        _proj_odd_kernel,
        grid=(B // bb, T // tt),
        in_specs=[x_spec, g_spec, m_spec, m_spec,
                  pl.BlockSpec(w_bf.shape, lambda i, j: (0, 0)),
                  pl.BlockSpec(wf_bf.shape, lambda i, j: (0, 0)),
                  pl.BlockSpec((1, N_FOX_HEADS), lambda i, j: (0, 0))],
        out_specs=[_out_spec(bb, tt, 6 * SEG), _out_spec(bb, tt, FOX_W), _out_spec(bb, tt, FOX_W),
                   _out_spec(bb, tt, N_FOX_HEADS)],
        out_shape=[jax.ShapeDtypeStruct((B, T, 6 * SEG), BF16), leaf, leaf,
                   jax.ShapeDtypeStruct((B, T, N_FOX_HEADS), F32)],
        compiler_params=_params("arbitrary", "arbitrary"),
        name="proj_odd",
    )(x, g, sc, sh, w_bf, wf_bf, b_f.reshape(1, N_FOX_HEADS))


def _attn_kernel(*refs, kind, n_chain, tq, tk, q_off, n_valid, lam_init):
    if kind == "diff":
        q_ref, k_ref, vt_ref, lp_ref, sg_ref, o_ref, acc_ref, m_ref, s_ref = refs
    elif kind == "fox":
        q_ref, k_ref, vt_ref, nd_ref, o_ref, acc_ref, m_ref, s_ref = refs
    else:
        q_ref, k_ref, vt_ref, o_ref, acc_ref, m_ref, s_ref = refs
    tq2 = 2 * tq
    q0 = q_off + pl.program_id(2) * tq

    low = lax.broadcasted_iota(jnp.int32, (tq, LANES), 1) < HEAD_DIM
    if kind == "fox":
        row = lax.broadcasted_iota(jnp.int32, (tq2, LANES), 0)
        col = lax.broadcasted_iota(jnp.int32, (tq2, LANES), 1)
        pick = jnp.logical_and(jnp.right_shift(row, tq.bit_length() - 1) * 3 <= col,
                               col < jnp.right_shift(row, tq.bit_length() - 1) * 3 + 3)
        bias_lanes = jnp.where(pick, 1.0, 0.0).astype(BF16)
    qqs = []
    for c in range(n_chain):
        q = q_ref[0, :, c * LANES:(c + 1) * LANES]
        zero = jnp.zeros_like(q)
        qq = jnp.concatenate([jnp.where(low, q, zero), jnp.where(low, zero, q)], axis=0)
        if kind == "fox":
            qq = jnp.concatenate([qq, bias_lanes], axis=1)
        qqs.append(qq)

    acc_ref[...] = jnp.zeros_like(acc_ref)
    if kind == "sb":
        m_ref[...] = jnp.zeros_like(m_ref)
        tri_r = lax.broadcasted_iota(jnp.int32, (tk, tk), 0)
        tri_c = lax.broadcasted_iota(jnp.int32, (tk, tk), 1)
        after = jnp.where(tri_c > tri_r, 1.0, 0.0).astype(BF16)
        after2 = jnp.concatenate([after, after], axis=1)
    else:
        m_ref[...] = jnp.full_like(m_ref, MASKED)
        ones_rows = jnp.ones((BF16_ROWS, tk), BF16)

    qpos = q0 + jnp.bitwise_and(lax.broadcasted_iota(jnp.int32, (tk, tq2), 1), tq - 1)
    kidx = lax.broadcasted_iota(jnp.int32, (tk, tq2), 0)

    chains = range(n_chain)

    def qk(kb, slot):
        start = pl.multiple_of(kb * tk, tk)
        for c in chains:
            k = k_ref[0, pl.ds(start, tk), c * LANES:(c + 1) * LANES]
            if kind == "fox":
                k = jnp.concatenate([k, nd_ref[0, c, pl.ds(start, tk), :]], axis=1)
            s_ref[slot, c] = _dot_nt(k, qqs[c])

    def consume(kb, slot, masked):
        scores = [s_ref[slot, c] for c in chains]
        if masked:
            kpos = kb * tk + kidx
            if kind == "diff":
                vis = jnp.right_shift(kpos, 6) <= jnp.right_shift(qpos, 6)
            elif kind == "fox":
                vis = kpos <= qpos
            else:
                vis = kpos < qpos
            vis = jnp.logical_and(vis, kpos < n_valid)
        vts = [vt_ref[0, c, kb] for c in chains]
        if kind == "sb":
            log_betas, laters = [], []
            for c in chains:
                s = scores[c]
                soft = jnp.log(1.0 + jnp.exp(-jnp.abs(s)))
                log_beta = jnp.minimum(s, 0.0) - soft
                log_keep = log_beta - s
                if masked:
                    log_keep = jnp.where(vis, log_keep, 0.0)
                hi = log_keep.astype(BF16)
                lo = (log_keep - hi.astype(F32)).astype(BF16)
                later = _dot(after2, jnp.concatenate([hi, lo], axis=0))
                laters.append(later + m_ref[c])
                m_ref[c] += later[0:1, :] + log_keep[0:1, :]
                log_betas.append(log_beta)
            weights = []
            for c in chains:
                w = jnp.exp(log_betas[c] + laters[c])
                if masked:
                    w = jnp.where(vis, w, 0.0)
                weights.append(w.astype(BF16))
            for c in chains:
                acc_ref[c] += _dot(vts[c], weights[c])
        else:
            probs, alphas = [], []
            for c in chains:
                s = scores[c]
                if masked:
                    s = jnp.where(vis, s, MASKED)
                m_prev = m_ref[c]
                m_new = jnp.maximum(m_prev, jnp.max(s, axis=0, keepdims=True))
                alphas.append(jnp.exp(m_prev - m_new))
                probs.append(jnp.exp(s - m_new).astype(BF16))
                m_ref[c] = m_new
            for c in chains:
                vt_aug = jnp.concatenate([vts[c], ones_rows], axis=0)
                acc_ref[c] = alphas[c] * acc_ref[c] + _dot(vt_aug, probs[c])

    n_full = q0 // tk
    qk(n_full, 0)
    qk(jnp.maximum(n_full - 1, 0), 1)
    consume(n_full, 0, True)

    def pair(j, carry):
        kb = n_full - 1 - 2 * j
        qk(jnp.maximum(kb - 1, 0), 0)
        consume(kb, 1, False)
        qk(jnp.maximum(kb - 2, 0), 1)
        consume(kb - 1, 0, False)
        return carry

    if kind == "sb":
        def top_sum():
            top = m_ref[0]
            for c in range(1, n_chain):
                top = jnp.maximum(top, m_ref[c])
            return jnp.max(top)

        def alive(carry):
            return jnp.logical_and(carry[0] < n_full // 2, carry[1] > SB_DEAD)

        def pair_and_check(carry):
            pair(carry[0], 0)
            return carry[0] + 1, top_sum()

        _, top = lax.while_loop(alive, pair_and_check, (jnp.int32(0), top_sum()))
        odd_block_left = jnp.logical_and(n_full % 2 == 1, top > SB_DEAD)
    else:
        lax.fori_loop(0, n_full // 2, pair, 0)
        odd_block_left = n_full % 2 == 1

    @pl.when(odd_block_left)
    def _():
        consume(0, 1, False)

    if kind == "diff":
        lp = lp_ref[...]
        lam = (jnp.exp(jnp.sum(lp[0:1] * lp[1:2], axis=1, keepdims=True))
               - jnp.exp(jnp.sum(lp[2:3] * lp[3:4], axis=1, keepdims=True)) + lam_init)
    for c in range(n_chain):
        acc = acc_ref[c]
        if kind == "sb":
            out_t = jnp.concatenate([acc[0:HEAD_DIM, 0:tq], acc[HEAD_DIM:LANES, tq:tq2]], axis=0)
        else:
            r = 1.0 / acc[LANES:LANES + 1, :]
            if kind == "fox":
                out_t = jnp.concatenate([acc[0:HEAD_DIM, 0:tq] * r[:, 0:tq],
                                         acc[HEAD_DIM:LANES, tq:tq2] * r[:, tq:tq2]], axis=0)
            else:
                a = acc[0:LANES, 0:tq] * r[:, 0:tq] - lam * (acc[0:LANES, tq:tq2] * r[:, tq:tq2])
                ms = jnp.mean(a * a, axis=0, keepdims=True)
                out_t = a * lax.rsqrt(ms + EPS) * sg_ref[...] * (1.0 - lam_init)
        if tq < LANES:
            out_t = jnp.concatenate([out_t, jnp.zeros((LANES, LANES - tq), F32)], axis=1)
        o_ref[0, :, c * LANES:(c + 1) * LANES] = out_t.T[0:tq].astype(o_ref.dtype)


def _attention(kind, q_arr, q_col, k_arr, k_col, vt_arr, n_col, *, n_chain, tq, tk, q_off, n_valid,
               extras=(), lam_init=0.0):
    B, Tq, _ = q_arr.shape
    Tk = k_arr.shape[1]
    n_kb = Tk // tk
    assert Tq % tq == 0 and Tk % tk == 0 and tk % tq == 0 and q_off % tk == 0 and n_valid <= Tk
    assert tq & (tq - 1) == 0 and n_col % n_chain == 0 and q_col % n_chain == 0 and k_col % n_chain == 0
    wide = n_chain * LANES
    in_specs = [
        pl.BlockSpec((1, tq, wide), lambda b, c, i: (b, i, q_col // n_chain + c)),
        pl.BlockSpec((1, Tk, wide), lambda b, c, i: (b, 0, k_col // n_chain + c)),
        pl.BlockSpec((1, n_chain, n_kb, LANES, tk), lambda b, c, i: (b, c, 0, 0, 0)),
    ]
    acc_rows = LANES if kind == "sb" else LANES + BF16_ROWS
    scratch = [pltpu.VMEM((n_chain, acc_rows, 2 * tq), F32), pltpu.VMEM((n_chain, 1, 2 * tq), F32),
               pltpu.VMEM((2, n_chain, tk, 2 * tq), F32)]
    if kind == "diff":
        lp, sg = extras
        in_specs += [pl.BlockSpec(lp.shape, lambda b, c, i: (0, 0)),
                     pl.BlockSpec(sg.shape, lambda b, c, i: (0, 0))]
    elif kind == "fox":
        (nd,) = extras
        in_specs.append(pl.BlockSpec((1, n_chain, Tk, LANES), lambda b, c, i: (b, c, 0, 0)))
    kernel = functools.partial(_attn_kernel, kind=kind, n_chain=n_chain, tq=tq, tk=tk, q_off=q_off,
                               n_valid=n_valid, lam_init=lam_init)
    return pl.pallas_call(
        kernel,
        grid=(B, n_col // n_chain, Tq // tq),
        in_specs=in_specs,
        out_specs=pl.BlockSpec((1, tq, wide), lambda b, c, i: (b, i, c)),
        out_shape=jax.ShapeDtypeStruct((B, Tq, n_col * LANES), BF16),
        scratch_shapes=scratch,
        compiler_params=_params("arbitrary", "arbitrary", "arbitrary"),
        name="attn_" + kind,
    )(q_arr, k_arr, vt_arr, *extras)


def _key_blocks_t(v, tk):
    B, Tk, W = v.shape
    return v.reshape(B, Tk // tk, tk, W // LANES, LANES).transpose(0, 3, 1, 4, 2)


def _cache_prep_kernel(k_ref, v_ref, nk_ref, nvt_ref, ko_ref, vto_ref, *, n_heads, tk):
    kb = pl.program_id(1)
    n_past = pl.num_programs(1) - 1
    per_col = LANES // k_ref.shape[2]

    def column(ref, c):
        heads = [ref[0, pl.ds(c * per_col + i, tk, stride=n_heads), :] for i in range(per_col)]
        return heads[0] if per_col == 1 else jnp.concatenate(heads, axis=1)

    @pl.when(kb < n_past)
    def _():
        for c in range(n_heads // per_col):
            ko_ref[0, :, c * LANES:(c + 1) * LANES] = column(k_ref, c).astype(BF16)
            vto_ref[0, c, 0] = column(v_ref, c).T.astype(BF16)

    @pl.when(kb == n_past)
    def _():
        ko_ref[...] = nk_ref[...]
        vto_ref[...] = nvt_ref[...]


def _cache_prep(cache_k, cache_v, new_k, new_vt, tk):
    B, P, H, dh = cache_k.shape
    W = H * dh
    n_col, n_past = W // LANES, P // tk
    assert P % tk == 0 and LANES % dh == 0
    rows = pl.BlockSpec((1, tk * H, dh), lambda b, j: (b, jnp.minimum(j, n_past - 1), 0))
    return pl.pallas_call(
        functools.partial(_cache_prep_kernel, n_heads=H, tk=tk),
        grid=(B, n_past + 1),
        in_specs=[rows, rows,
                  pl.BlockSpec((1, tk, W), lambda b, j: (b, 0, 0)),
                  pl.BlockSpec((1, n_col, 1, LANES, tk), lambda b, j: (b, 0, 0, 0, 0))],
        out_specs=[pl.BlockSpec((1, tk, W), lambda b, j: (b, j, 0)),
                   pl.BlockSpec((1, n_col, 1, LANES, tk), lambda b, j: (b, 0, j, 0, 0))],
        out_shape=[jax.ShapeDtypeStruct((B, P + tk, W), BF16),
                   jax.ShapeDtypeStruct((B, n_col, n_past + 1, LANES, tk), BF16)],
        compiler_params=_params("arbitrary", "arbitrary"),
        name="cache_prep",
    )(cache_k.reshape(B, P * H, dh), cache_v.reshape(B, P * H, dh), new_k, new_vt)


def _cumsum_kernel(x_ref, o_ref, *, blk):
    T, H = x_ref.shape[1:]
    n_pair = H // 2
    r = lax.broadcasted_iota(jnp.int32, (blk, blk), 0)
    c = lax.broadcasted_iota(jnp.int32, (blk, blk), 1)
    upto = jnp.where(c <= r, 1.0, 0.0).astype(BF16)
    src = lax.broadcasted_iota(jnp.int32, (3 * H, n_pair * LANES), 0)
    col = lax.broadcasted_iota(jnp.int32, (3 * H, n_pair * LANES), 1)
    head, piece = jnp.bitwise_and(src, H - 1), jnp.right_shift(src, H.bit_length() - 1)
    here = jnp.right_shift(col, 7) == jnp.right_shift(head, 1)
    lane = jnp.bitwise_and(col, LANES - 1) - 3 * jnp.bitwise_and(head, 1)
    place = jnp.where(jnp.logical_and(here, lane == piece), 1.0, 0.0).astype(BF16)

    def chunk(j, carry):
        rows = pl.ds(pl.multiple_of(j * blk, blk), blk)
        hi, mid, lo = _split3(x_ref[0, rows, :])
        total = _dot(upto, hi) + _dot(upto, mid) + _dot(upto, lo) + carry
        pieces = jnp.concatenate([p.astype(F32) for p in _split3(-total)], axis=1).astype(BF16)
        lanes = _dot(pieces, place)
        for pair in range(n_pair):
            o_ref[0, pair, rows, :] = lanes[:, pair * LANES:(pair + 1) * LANES].astype(BF16)
        return total[blk - 1:blk, :]

    lax.fori_loop(0, T // blk, chunk, jnp.zeros((1, H), F32))


def _neg_cumsum(lf, blk):
    B, T, H = lf.shape
    return pl.pallas_call(
        functools.partial(_cumsum_kernel, blk=blk),
        grid=(B,),
        in_specs=[pl.BlockSpec((1, T, H), lambda b: (b, 0, 0))],
        out_specs=pl.BlockSpec((1, H // 2, T, LANES), lambda b: (b, 0, 0, 0)),
        out_shape=jax.ShapeDtypeStruct((B, H // 2, T, LANES), BF16),
        compiler_params=_params("arbitrary"),
        name="neg_cumsum",
    )(lf)


def _outproj_kernel(*refs, nseg):
    a_refs = refs[:nseg]
    w_ref, x_ref, g_ref, o_ref = refs[nseg:]
    bb, tt, d = x_ref.shape
    acc = None
    off = 0
    for a_ref in a_refs:
        width = a_ref.shape[2]
        part = _dot(a_ref[...].reshape(bb * tt, width), w_ref[off:off + width, :])
        acc = part if acc is None else acc + part
        off += width
    o_ref[...] = x_ref[...] + g_ref[...] * acc.reshape(bb, tt, d)


def _outproj(mixed, w_bf, x, gate, bb, tt):
    B, T, _ = x.shape
    x_spec, _, m_spec = _row_specs(bb, tt)
    return pl.pallas_call(
        functools.partial(_outproj_kernel, nseg=len(mixed)),
        grid=(B // bb, T // tt),
        in_specs=[_out_spec(bb, tt, a.shape[2]) for a in mixed]
        + [pl.BlockSpec(w_bf.shape, lambda i, j: (0, 0)), x_spec, m_spec],
        out_specs=x_spec,
        out_shape=jax.ShapeDtypeStruct(x.shape, F32),
        compiler_params=_params("arbitrary", "arbitrary"),
        name="outproj",
    )(*mixed, w_bf, x, gate)


def _route_t(h, wr_t, br):
    tm = h.shape[0]
    h_hi = h.astype(BF16)
    h_lo = (h - h_hi.astype(F32)).astype(BF16)
    w_pieces = jnp.concatenate(_split3(wr_t), axis=0)
    parts = _dot_nt(w_pieces, h_hi) + _dot_nt(w_pieces, h_lo)
    logits = parts[0:N_EXPERTS] + parts[N_EXPERTS:2 * N_EXPERTS] + parts[2 * N_EXPERTS:]
    s = jax.nn.sigmoid(logits)
    sel = s + br
    row = lax.broadcasted_iota(jnp.int32, (N_EXPERTS, tm), 0).astype(F32)
    neg = -jnp.inf

    def first_argmax(vals):
        top = jnp.max(vals, axis=0, keepdims=True)
        idx = jnp.min(jnp.where(vals == top, row, float(N_EXPERTS)), axis=0, keepdims=True)
        return top, idx

    best = None
    for g in range(N_GROUPS):
        in_g = (row >= g * EXPERTS_PER_GROUP) & (row < (g + 1) * EXPERTS_PER_GROUP)
        top1, i1 = first_argmax(jnp.where(in_g, sel, neg))
        top2, i2 = first_argmax(jnp.where(in_g & (row != i1), sel, neg))
        cand = (top1 + top2, i1, i2, jnp.full((1, tm), float(g), F32))
        if best is None:
            best = cand
        else:
            better = cand[0] > best[0]
            best = tuple(jnp.where(better, new, old) for new, old in zip(cand, best))
    _, i1, i2, group = best
    picked = jnp.where((row == i1) | (row == i2), s, 0.0)
    return picked / jnp.sum(picked, axis=0, keepdims=True), group


def _moe_kernel(*refs, final):
    if final:
        (x_ref, g_ref, sc_ref, sh_ref, g2_ref, wrt_ref, br_ref, wg_ref, wu_ref, wd_ref, gf_ref,
         o_ref, h_ref, xg_ref, yg_ref, gg_ref, gate_ref, pos_ref, before_ref, n_ref) = refs
    else:
        (x_ref, g_ref, sc_ref, sh_ref, g2_ref, wrt_ref, br_ref, wg_ref, wu_ref, wd_ref,
         o_ref, h_ref, xg_ref, yg_ref, gg_ref, gate_ref, pos_ref, before_ref, n_ref) = refs
    bb, tt, d = x_ref.shape
    tm = bb * tt
    e = pl.program_id(2)
    group = e // EXPERTS_PER_GROUP

    @pl.when((pl.program_id(0) == 0) & (pl.program_id(1) == 0) & (e == 0))
    def _():
        def fill(j, carry):
            rows = pl.ds(pl.multiple_of(j * MOE_CHUNK, MOE_CHUNK), MOE_CHUNK)
            r = lax.broadcasted_iota(jnp.int32, (MOE_CHUNK, tm), 0) + j * MOE_CHUNK
            c = lax.broadcasted_iota(jnp.int32, (MOE_CHUNK, tm), 1)
            before_ref[rows, :] = jnp.where(r < c, 1.0, 0.0).astype(BF16)
            return carry
        lax.fori_loop(0, tm // MOE_CHUNK, fill, 0)

    @pl.when(e == 0)
    def _():
        h = _norm_mod(x_ref[...], g_ref[...], sc_ref[...], sh_ref[...])
        h_ref[...] = h.astype(BF16)
        gates, grp = _route_t(h, wrt_ref[...], br_ref[...])
        gates = jnp.concatenate([gates, jnp.zeros((LANES - N_EXPERTS, tm), F32)], axis=0)
        hi = gates.astype(BF16)
        gate_ref[0] = hi
        gate_ref[1] = (gates - hi.astype(F32)).astype(BF16)
        grow = lax.broadcasted_iota(jnp.int32, pos_ref.shape, 0).astype(F32)
        member = jnp.where(grow == grp, 1.0, 0.0)
        rank = _dot(member.astype(BF16), before_ref[...])
        pos_ref[...] = jnp.where(member > 0.0, rank, -1.0)
        for g in range(N_GROUPS):
            count = jnp.sum(member[g:g + 1, :]).astype(jnp.int32)
            n_ref[g] = (count + MOE_CHUNK - 1) // MOE_CHUNK
        o_ref[...] = jnp.zeros_like(o_ref)

    n_chunks = n_ref[group]

    def chunk_rows(j):
        return pl.ds(pl.multiple_of(j * MOE_CHUNK, MOE_CHUNK), MOE_CHUNK)

    def one_hot(j, n_rows=MOE_CHUNK):
        want = (lax.broadcasted_iota(jnp.int32, (n_rows, tm), 0) + j * n_rows).astype(F32)
        return jnp.where(pos_ref[pl.ds(group, 1), :] == want, 1.0, 0.0).astype(BF16)

    @pl.when(e % EXPERTS_PER_GROUP == 0)
    def _():
        def gather(j, carry):
            rows = chunk_rows(j)
            pick = one_hot(j)
            xg_ref[rows, :] = _dot(pick, h_ref[...]).astype(BF16)
            gg_ref[rows, :] = _dot_nt(pick, gate_ref[0]) + _dot_nt(pick, gate_ref[1])
            yg_ref[rows, :] = jnp.zeros((MOE_CHUNK, d), F32)
            return carry
        lax.fori_loop(0, n_chunks, gather, 0)

        @pl.when(n_chunks % 2 == 1)
        def _():
            yg_ref[chunk_rows(n_chunks), :] = jnp.zeros((MOE_CHUNK, d), F32)

    wg, wu, wd = wg_ref[0, 0], wu_ref[0, 0], wd_ref[0, 0]
    lane = lax.broadcasted_iota(jnp.int32, (MOE_CHUNK, LANES), 1)

    def expert(j, carry):
        rows = chunk_rows(j)
        xr = xg_ref[rows, :]
        gt = _dot(xr, wg)
        up = _dot(xr, wu)
        ge = jnp.sum(jnp.where(lane == e, gg_ref[rows, :], 0.0), axis=1, keepdims=True)
        a = (gt * jax.nn.sigmoid(gt)) * up * ge
        yg_ref[rows, :] += _dot(a.astype(BF16), wd)
        return carry

    lax.fori_loop(0, n_chunks, expert, 0)

    @pl.when(e % EXPERTS_PER_GROUP == EXPERTS_PER_GROUP - 1)
    def _():
        def scatter(j, carry):
            rows = pl.ds(pl.multiple_of(j * 2 * MOE_CHUNK, 2 * MOE_CHUNK), 2 * MOE_CHUNK)
            back = lax.dot_general(one_hot(j, 2 * MOE_CHUNK), yg_ref[rows, :].astype(BF16),
                                   (((0,), (0,)), ((), ())), preferred_element_type=F32)
            o_ref[...] += back.reshape(bb, tt, d)
            return carry
        lax.fori_loop(0, (n_chunks + 1) // 2, scatter, 0)

    @pl.when(e == N_EXPERTS - 1)
    def _():
        y = x_ref[...] + g2_ref[...] * o_ref[...]
        if final:
            ms = jnp.mean(y * y, axis=-1, keepdims=True)
            y = y * lax.rsqrt(ms + EPS) * gf_ref[...]
        o_ref[...] = y


def _moe(x, g, sc, sh, g2, w_router, b_router, w_gate, w_up, w_down, layer, g_final, bb, tt):
    B, T, _ = x.shape
    final = g_final is not None
    x_spec = pl.BlockSpec((bb, tt, D_MODEL), lambda i, j, e: (i, j, 0))
    g_spec = pl.BlockSpec((1, D_MODEL), lambda i, j, e: (0, 0))
    m_spec = pl.BlockSpec((bb, 1, D_MODEL), lambda i, j, e: (i, 0, 0))
    in_specs = [x_spec, g_spec, m_spec, m_spec, m_spec,
                pl.BlockSpec((N_EXPERTS, D_MODEL), lambda i, j, e: (0, 0)),
                pl.BlockSpec((N_EXPERTS, 1), lambda i, j, e: (0, 0)),
                pl.BlockSpec((1, 1, D_MODEL, D_EXPERT), lambda i, j, e: (layer, e, 0, 0)),
                pl.BlockSpec((1, 1, D_MODEL, D_EXPERT), lambda i, j, e: (layer, e, 0, 0)),
                pl.BlockSpec((1, 1, D_EXPERT, D_MODEL), lambda i, j, e: (layer, e, 0, 0))]
    args = [x, g, sc, sh, g2, w_router.T, b_router.reshape(N_EXPERTS, 1), w_gate, w_up, w_down]
    if final:
        in_specs.append(g_spec)
        args.append(g_final)
    tm = bb * tt
    assert tm % (2 * MOE_CHUNK) == 0
    packed = tm
    scratch = [pltpu.VMEM((tm, D_MODEL), BF16),
               pltpu.VMEM((packed, D_MODEL), BF16),
               pltpu.VMEM((packed, D_MODEL), F32),
               pltpu.VMEM((packed, LANES), F32),
               pltpu.VMEM((2, LANES, tm), BF16),
               pltpu.VMEM((2 * N_GROUPS, tm), F32),
               pltpu.VMEM((tm, tm), BF16),
               pltpu.SMEM((N_GROUPS,), jnp.int32)]
    return pl.pallas_call(
        functools.partial(_moe_kernel, final=final),
        grid=(B // bb, T // tt, N_EXPERTS),
        in_specs=in_specs,
        out_specs=x_spec,
        out_shape=jax.ShapeDtypeStruct(x.shape, F32),
        scratch_shapes=scratch,
        compiler_params=_params("arbitrary", "arbitrary", "arbitrary"),
        name="moe",
    )(*args)


def _rope_tables(pos):
    half = ROPE_DIM // 2
    inv = ROPE_THETA ** (-jnp.arange(half, dtype=F32) / half)
    ang = pos.astype(F32)[:, None] * inv[None, :]
    cos, sin = jnp.cos(ang), jnp.sin(ang)
    ones = jnp.ones((pos.shape[0], HEAD_DIM - ROPE_DIM), F32)
    zeros = jnp.zeros_like(ones)
    zh = jnp.zeros_like(sin)
    c64 = jnp.concatenate([cos, cos, ones], axis=1)
    s1_64 = jnp.concatenate([-sin, zh, zeros], axis=1)
    s2_64 = jnp.concatenate([zh, sin, zeros], axis=1)
    return tuple(jnp.concatenate([t, t], axis=1) for t in (c64, s1_64, s2_64))


def _pad_time(a, total):
    if total == a.shape[1]:
        return a
    return jnp.pad(a, ((0, 0), (0, total - a.shape[1])) + ((0, 0),) * (a.ndim - 2))


def _trunk(x, mod, past, wts, cfg):
    B, T, _ = x.shape
    P = 0 if past is None else past["diff_k"].shape[2]
    bb, tt_proj, tt_out, tt_moe, tq, tk = cfg
    wide = past is None and bb == 1 and tt_proj == tk
    n_valid = P + T
    Tk = -(-n_valid // tk) * tk
    Tq = -(-T // tq) * tq
    tables = _rope_tables(P + jnp.arange(T))
    if bb > 1:
        tables = tuple(jnp.tile(t, (bb, 1)) for t in tables)
    attn = functools.partial(_attention, n_chain=N_CHAIN, tq=tq, tk=tk, q_off=P, n_valid=n_valid)
    leaves_even, leaves_odd = [], []

    def with_cache(qkv, k_cols, v_cols, cache_k, cache_v):
        assert Tk == P + tk
        new_k = _pad_time(qkv[:, :, k_cols[0]:k_cols[1]], tk)
        new_vt = _key_blocks_t(_pad_time(qkv[:, :, v_cols[0]:v_cols[1]], tk), tk)
        return _cache_prep(cache_k, cache_v, new_k, new_vt, tk)

    def heads_last(leaf_t, n_heads):
        return leaf_t.reshape(B, n_heads, -1, T).transpose(0, 3, 1, 2)

    for l in range(DEPTH):
        sh1, sc1, g1, sh2, sc2, g2 = [mod[l, :, k * D_MODEL:(k + 1) * D_MODEL][:, None, :] for k in range(6)]
        i = l // 2
        g_mix = wts["norm_mix"][l][None, :]
        if l % 2 == 0:
            lam_init = 0.8 - 0.6 * math.exp(-0.3 * l)
            lp = jnp.stack([wts["diff_lq1"][i], wts["diff_lk1"][i], wts["diff_lq2"][i], wts["diff_lk2"][i]])
            diff_extras = (lp, wts["diff_subln"][i][:, None])
            if wide:
                qk, ak, av, bkt, bvt, dvt, svt = _proj_even_wide(x, g_mix, sc1, sh1, wts["w_in_even"][i], tables, tt_proj)
                leaves_even.append((ak.reshape(B, T, N_DIFF_HEADS, LANES), av.reshape(B, T, N_DIFF_HEADS, LANES),
                                    heads_last(bkt, N_SB_HEADS), heads_last(bvt, N_SB_HEADS)))
                q_arr, q_cols = qk, (0, 2 * SEG // LANES)
                dk, dk_col, sk, sk_col = qk, SEG // LANES, qk, 3 * SEG // LANES
            else:
                qkv, ak, av, bk, bv = _proj_even(x, g_mix, sc1, sh1, wts["w_in_even"][i], tables, bb, tt_proj)
                leaves_even.append((ak.reshape(B, T, N_DIFF_HEADS, LANES), av.reshape(B, T, N_DIFF_HEADS, LANES),
                                    bk.reshape(B, T, N_SB_HEADS, HEAD_DIM), bv.reshape(B, T, N_SB_HEADS, HEAD_DIM)))
                q_arr, q_cols = _pad_time(qkv, Tq), (0, 3 * SEG // LANES)
                dk, dvt = with_cache(qkv, (SEG, 2 * SEG), (2 * SEG, 3 * SEG), past["diff_k"][i], past["diff_v"][i])
                sk, svt = with_cache(qkv, (4 * SEG, 5 * SEG), (5 * SEG, 6 * SEG), past["sb_k"][i], past["sb_v"][i])
                dk_col = sk_col = 0
            a_out = attn("diff", q_arr, q_cols[0], dk, dk_col, dvt, N_DIFF_HEADS, extras=diff_extras, lam_init=lam_init)
            b_out = attn("sb", q_arr, q_cols[1], sk, sk_col, svt, N_SB_HEADS // 2)
            x = _outproj([a_out[:, :T], b_out[:, :T]], wts["w_out_even"][i], x, g1, bb, tt_out)
        else:
            if wide:
                qk, kt, vt, fvt, lf = _proj_odd_wide(x, g_mix, sc1, sh1, wts["w_in_odd"][i], wts["w_f_odd"][i],
                                                     wts["b_forget"][i], tt_proj)
                leaves_odd.append((heads_last(kt, N_FOX_HEADS), heads_last(vt, N_FOX_HEADS), lf))
                q_arr, fk, fk_col, lf_all = qk, qk, 2 * SEG // LANES, lf
            else:
                qkv, k, v, lf = _proj_odd(x, g_mix, sc1, sh1, wts["w_in_odd"][i], wts["w_f_odd"][i],
                                          wts["b_forget"][i], bb, tt_proj)
                leaves_odd.append((k.reshape(B, T, N_FOX_HEADS, HEAD_DIM), v.reshape(B, T, N_FOX_HEADS, HEAD_DIM), lf))
                q_arr = _pad_time(qkv, Tq)
                fk, fvt = with_cache(qkv, (2 * SEG, 4 * SEG), (4 * SEG, 6 * SEG), past["fox_k"][i], past["fox_v"][i])
                fk_col = 0
                lf_all = jnp.concatenate([past["fox_logf"][i], lf], axis=1)
            nd = _neg_cumsum(_pad_time(lf_all, Tk), tk)
            o = attn("fox", q_arr, 0, fk, fk_col, fvt, N_FOX_HEADS // 2, extras=(nd,))
            x = _outproj([o[:, :T]], wts["w_out_odd"][i], x, g1, bb, tt_out)
        g_final = wts["norm_final"][None, :] if l == DEPTH - 1 else None
        x = _moe(x, wts["norm_ffn"][l][None, :], sc2, sh2, g2, wts["w_router"], wts["b_router"],
                 wts["w_gate"], wts["w_up"], wts["w_down"], l, g_final, bb, tt_moe)
    return x, leaves_even, leaves_odd


def kernel(x_prompt, x_sample, c_prompt, c_sample, cache_diff_k, cache_diff_v, cache_sb_k, cache_sb_v, cache_fox_k, cache_fox_v, cache_fox_logf, w_ada, b_ada, norm_mix, norm_ffn, w_in_even, w_out_even, diff_lq1, diff_lk1, diff_lq2, diff_lk2, diff_subln, w_in_odd, b_forget, w_out_odd, w_router, b_router, w_gate, w_up, w_down, norm_final):
    Bp, Tp, _ = x_prompt.shape
    Bs, Ts, _ = x_sample.shape
    wts = {
        "norm_mix": norm_mix, "norm_ffn": norm_ffn, "norm_final": norm_final,
        "w_in_even": w_in_even.astype(BF16), "w_out_even": w_out_even.astype(BF16),
        "w_in_odd": w_in_odd[:, :, :3 * FOX_W].astype(BF16), "w_f_odd": w_in_odd[:, :, 3 * FOX_W:].astype(BF16),
        "w_out_odd": w_out_odd.astype(BF16), "b_forget": b_forget,
        "diff_lq1": diff_lq1, "diff_lk1": diff_lk1, "diff_lq2": diff_lq2, "diff_lk2": diff_lk2,
        "diff_subln": diff_subln, "w_router": w_router, "b_router": b_router,
        "w_gate": w_gate.astype(BF16), "w_up": w_up.astype(BF16), "w_down": w_down.astype(BF16),
    }
    past = {"diff_k": cache_diff_k, "diff_v": cache_diff_v, "sb_k": cache_sb_k, "sb_v": cache_sb_v,
            "fox_k": cache_fox_k, "fox_v": cache_fox_v, "fox_logf": cache_fox_logf}

    rows = Bp + Bs
    rows_pad = -(-rows // 16) * 16
    c_all = jnp.pad(jnp.concatenate([c_prompt, c_sample], axis=0), ((0, rows_pad - rows), (0, 0)))
    mod = _ada_mod(c_all, w_ada, b_ada)

    cfg_prompt = (1, 256, 512, 1024, 128, 256)
    cfg_sample = (Bs, Ts, Ts, Ts, 64, 256)
    y_p, even_p, odd_p = _trunk(x_prompt, mod[:, :Bp], None, wts, cfg_prompt)
    y_s, even_s, odd_s = _trunk(x_sample, mod[:, Bp:rows], past, wts, cfg_sample)

    def layers(rows):
        return rows[0][None] if len(rows) == 1 else jnp.stack(rows, axis=0)

    def pack(even, odd):
        return tuple(layers(r) for r in zip(*even)) + tuple(layers(r) for r in zip(*odd))

    return (y_p, y_s) + pack(even_p, odd_p) + pack(even_s, odd_s)
```

```python
import functools
import math

import jax
import jax.numpy as jnp
from jax import lax
from jax.experimental import pallas as pl
from jax.experimental.pallas import tpu as pltpu

D_MODEL = 1024
DEPTH = 2
CHUNK = 64
HEAD_DIM = 64
N_DIFF_HEADS = D_MODEL // 256
N_SB_HEADS = D_MODEL // 128
N_FOX_HEADS = D_MODEL // HEAD_DIM
DIFF_W = N_DIFF_HEADS * 2 * HEAD_DIM
SB_W = N_SB_HEADS * HEAD_DIM
FOX_W = N_FOX_HEADS * HEAD_DIM
ROPE_DIM = HEAD_DIM // 4
ROPE_THETA = 500000.0
N_EXPERTS = 16
N_GROUPS = 4
EXPERTS_PER_GROUP = N_EXPERTS // N_GROUPS
D_EXPERT = D_MODEL // 2
EPS = 1e-6

LANES = 128
BF16_ROWS = 16
SEG = 512
QK_SCALE = HEAD_DIM ** -0.5
MASKED = -1e30
N_CHAIN = 4
SB_DEAD = -110.0
MOE_CHUNK = 128
VMEM_LIMIT = 48 * 1024 * 1024

F32 = jnp.float32
BF16 = jnp.bfloat16


def _params(*sem):
    return pltpu.CompilerParams(dimension_semantics=sem, vmem_limit_bytes=VMEM_LIMIT)


def _dot(a, b):
    return jnp.dot(a, b, preferred_element_type=F32)


def _dot_nt(a, b):
    return lax.dot_general(a, b, (((1,), (1,)), ((), ())), preferred_element_type=F32)


def _norm_mod(x, g, sc, sh):
    bb, tt, d = x.shape
    ms = jnp.mean(x * x, axis=-1, keepdims=True)
    h = x * lax.rsqrt(ms + EPS) * g
    h = h * (1.0 + sc) + sh
    return h.reshape(bb * tt, d)


def _log_sigmoid(x):
    return jnp.minimum(x, 0.0) - jnp.log(1.0 + jnp.exp(-jnp.abs(x)))


def _split3(x):
    hi = x.astype(BF16)
    r1 = x - hi.astype(F32)
    mid = r1.astype(BF16)
    lo = (r1 - mid.astype(F32)).astype(BF16)
    return hi, mid, lo


def _ada_kernel(c_ref, w_ref, b_ref, o_ref):
    c = c_ref[...]
    a = (c * jax.nn.sigmoid(c)).astype(BF16)
    o_ref[0] = _dot(a, w_ref[0].astype(BF16)) + b_ref[0]


def _ada_mod(c_all, w_ada, b_ada):
    rows = c_all.shape[0]
    tn = 1536
    width = 6 * D_MODEL
    return pl.pallas_call(
        _ada_kernel,
        grid=(DEPTH, width // tn),
        in_specs=[
            pl.BlockSpec((rows, D_MODEL), lambda l, j: (0, 0)),
            pl.BlockSpec((1, D_MODEL, tn), lambda l, j: (l, 0, j)),
            pl.BlockSpec((1, 1, tn), lambda l, j: (l, 0, j)),
        ],
        out_specs=pl.BlockSpec((1, rows, tn), lambda l, j: (l, 0, j)),
        out_shape=jax.ShapeDtypeStruct((DEPTH, rows, width), F32),
        compiler_params=_params("arbitrary", "arbitrary"),
        name="ada_mod",
    )(c_all, w_ada, b_ada.reshape(DEPTH, 1, width))


def _rope(p, cos, s1, s2):
    outs = []
    for c in range(SEG // LANES):
        pc = p[:, c * LANES:(c + 1) * LANES]
        up = pltpu.roll(pc, LANES - ROPE_DIM // 2, 1)
        down = pltpu.roll(pc, ROPE_DIM // 2, 1)
        outs.append(pc * cos + up * s1 + down * s2)
    return jnp.concatenate(outs, axis=1)


def _proj_even_kernel(x_ref, g_ref, sc_ref, sh_ref, w_ref, cos_ref, s1_ref, s2_ref,
                      qkv_ref, ak_ref, av_ref, bk_ref, bv_ref):
    bb, tt, _ = x_ref.shape
    h = _norm_mod(x_ref[...], g_ref[...], sc_ref[...], sh_ref[...]).astype(BF16)
    cos, s1, s2 = cos_ref[...], s1_ref[...], s2_ref[...]

    def seg(i):
        return _dot(h, w_ref[:, i * SEG:(i + 1) * SEG])

    def put(i, val, leaf_ref):
        val3 = val.reshape(bb, tt, SEG)
        qkv_ref[:, :, i * SEG:(i + 1) * SEG] = val3.astype(BF16)
        if leaf_ref is not None:
            leaf_ref[...] = val3

    put(0, _rope(seg(0), cos, s1, s2) * QK_SCALE, None)
    put(1, _rope(seg(1), cos, s1, s2), ak_ref)
    put(2, seg(2), av_ref)
    put(3, seg(3) * QK_SCALE, None)
    put(4, seg(4), bk_ref)
    put(5, seg(5), bv_ref)


def _rows_by_head(ref, val, n_heads):
    tt = val.shape[0]
    for hd in range(n_heads):
        ref[0, pl.ds(hd, tt, stride=n_heads), :] = val[:, hd * LANES:(hd + 1) * LANES]


def _key_blocks(ref, first, val_t):
    n = val_t.shape[0] // LANES
    ref[0, first:first + n, 0] = val_t.reshape(n, LANES, val_t.shape[1]).astype(BF16)


def _proj_even_wide_kernel(x_ref, g_ref, sc_ref, sh_ref, w_ref, cos_ref, s1_ref, s2_ref,
                           qk_ref, ak_ref, av_ref, bkt_ref, bvt_ref, dvt_ref, svt_ref):
    h = _norm_mod(x_ref[...], g_ref[...], sc_ref[...], sh_ref[...]).astype(BF16)
    cos, s1, s2 = cos_ref[...], s1_ref[...], s2_ref[...]

    def seg(i):
        return _dot(h, w_ref[:, i * SEG:(i + 1) * SEG])

    qk_ref[0, :, 0:SEG] = (_rope(seg(0), cos, s1, s2) * QK_SCALE).astype(BF16)
    ak = _rope(seg(1), cos, s1, s2)
    qk_ref[0, :, SEG:2 * SEG] = ak.astype(BF16)
    _rows_by_head(ak_ref, ak, N_DIFF_HEADS)
    av = seg(2)
    _rows_by_head(av_ref, av, N_DIFF_HEADS)
    _key_blocks(dvt_ref, 0, av.T)
    qk_ref[0, :, 2 * SEG:3 * SEG] = (seg(3) * QK_SCALE).astype(BF16)
    bk = seg(4)
    qk_ref[0, :, 3 * SEG:4 * SEG] = bk.astype(BF16)
    bkt_ref[0] = bk.T
    bvt = seg(5).T
    bvt_ref[0] = bvt
    _key_blocks(svt_ref, 0, bvt)


def _proj_odd_wide_kernel(x_ref, g_ref, sc_ref, sh_ref, w_ref, wf_ref, bf_ref,
                          qk_ref, kt_ref, vt_ref, fvt_ref, lf_ref):
    h = _norm_mod(x_ref[...], g_ref[...], sc_ref[...], sh_ref[...]).astype(BF16)

    def seg(i):
        return _dot(h, w_ref[:, i * SEG:(i + 1) * SEG])

    for i in range(2):
        qk_ref[0, :, i * SEG:(i + 1) * SEG] = (seg(i) * QK_SCALE).astype(BF16)
    for i in range(2):
        k = seg(2 + i)
        qk_ref[0, :, (2 + i) * SEG:(3 + i) * SEG] = k.astype(BF16)
        kt_ref[0, i * SEG:(i + 1) * SEG, :] = k.T
        v_t = seg(4 + i).T
        vt_ref[0, i * SEG:(i + 1) * SEG, :] = v_t
        _key_blocks(fvt_ref, i * (SEG // LANES), v_t)
    fl = _dot(h, wf_ref[...]) + bf_ref[...]
    lf_ref[0] = _log_sigmoid(fl)


def _proj_odd_kernel(x_ref, g_ref, sc_ref, sh_ref, w_ref, wf_ref, bf_ref,
                     qkv_ref, k_ref, v_ref, lf_ref):
    bb, tt, _ = x_ref.shape
    h = _norm_mod(x_ref[...], g_ref[...], sc_ref[...], sh_ref[...]).astype(BF16)

    def seg(i):
        return _dot(h, w_ref[:, i * SEG:(i + 1) * SEG]).reshape(bb, tt, SEG)

    for i in range(2):
        qkv_ref[:, :, i * SEG:(i + 1) * SEG] = (seg(i) * QK_SCALE).astype(BF16)
    for i in range(2, 6):
        val = seg(i)
        qkv_ref[:, :, i * SEG:(i + 1) * SEG] = val.astype(BF16)
        leaf_ref = k_ref if i < 4 else v_ref
        leaf_ref[:, :, (i % 2) * SEG:(i % 2 + 1) * SEG] = val
    fl = _dot(h, wf_ref[...]) + bf_ref[...]
    lf_ref[...] = _log_sigmoid(fl).reshape(bb, tt, N_FOX_HEADS)


def _row_specs(bb, tt):
    x_spec = pl.BlockSpec((bb, tt, D_MODEL), lambda i, j: (i, j, 0))
    g_spec = pl.BlockSpec((1, D_MODEL), lambda i, j: (0, 0))
    m_spec = pl.BlockSpec((bb, 1, D_MODEL), lambda i, j: (i, 0, 0))
    return x_spec, g_spec, m_spec


def _out_spec(bb, tt, width):
    return pl.BlockSpec((bb, tt, width), lambda i, j: (i, j, 0))


def _proj_even(x, g, sc, sh, w_bf, tables, bb, tt):
    B, T, _ = x.shape
    x_spec, g_spec, m_spec = _row_specs(bb, tt)
    tm = bb * tt
    t_spec = pl.BlockSpec((tm, LANES), (lambda i, j: (j, 0)) if bb == 1 else (lambda i, j: (0, 0)))
    leaf = jax.ShapeDtypeStruct((B, T, SEG), F32)
    return pl.pallas_call(
        _proj_even_kernel,
        grid=(B // bb, T // tt),
        in_specs=[x_spec, g_spec, m_spec, m_spec,
                  pl.BlockSpec(w_bf.shape, lambda i, j: (0, 0)), t_spec, t_spec, t_spec],
        out_specs=[_out_spec(bb, tt, 6 * SEG)] + [_out_spec(bb, tt, SEG)] * 4,
        out_shape=[jax.ShapeDtypeStruct((B, T, 6 * SEG), BF16), leaf, leaf, leaf, leaf],
        compiler_params=_params("arbitrary", "arbitrary"),
        name="proj_even",
    )(x, g, sc, sh, w_bf, *tables)


def _wide_specs(B, T, tt):
    def rows(width, dtype):
        return pl.BlockSpec((1, tt, width), lambda i, j: (i, j, 0)), jax.ShapeDtypeStruct((B, T, width), dtype)

    def by_head(n_heads):
        return (pl.BlockSpec((1, tt * n_heads, LANES), lambda i, j: (i, j, 0)),
                jax.ShapeDtypeStruct((B, T * n_heads, LANES), F32))

    def transposed(width):
        return pl.BlockSpec((1, width, tt), lambda i, j: (i, 0, j)), jax.ShapeDtypeStruct((B, width, T), F32)

    def key_blocks(n_col):
        return (pl.BlockSpec((1, n_col, 1, LANES, tt), lambda i, j: (i, 0, j, 0, 0)),
                jax.ShapeDtypeStruct((B, n_col, T // tt, LANES, tt), BF16))

    return rows, by_head, transposed, key_blocks


def _proj_even_wide(x, g, sc, sh, w_bf, tables, tt):
    B, T, _ = x.shape
    x_spec, g_spec, m_spec = _row_specs(1, tt)
    t_spec = pl.BlockSpec((tt, LANES), lambda i, j: (j, 0))
    rows, by_head, transposed, key_blocks = _wide_specs(B, T, tt)
    outs = [rows(4 * SEG, BF16), by_head(N_DIFF_HEADS), by_head(N_DIFF_HEADS), transposed(SEG), transposed(SEG),
            key_blocks(SEG // LANES), key_blocks(SEG // LANES)]
    return pl.pallas_call(
        _proj_even_wide_kernel,
        grid=(B, T // tt),
        in_specs=[x_spec, g_spec, m_spec, m_spec,
                  pl.BlockSpec(w_bf.shape, lambda i, j: (0, 0)), t_spec, t_spec, t_spec],
        out_specs=[o[0] for o in outs],
        out_shape=[o[1] for o in outs],
        compiler_params=_params("arbitrary", "arbitrary"),
        name="proj_even_wide",
    )(x, g, sc, sh, w_bf, *tables)


def _proj_odd_wide(x, g, sc, sh, w_bf, wf_bf, b_f, tt):
    B, T, _ = x.shape
    x_spec, g_spec, m_spec = _row_specs(1, tt)
    rows, by_head, transposed, key_blocks = _wide_specs(B, T, tt)
    outs = [rows(4 * SEG, BF16), transposed(FOX_W), transposed(FOX_W), key_blocks(FOX_W // LANES),
            rows(N_FOX_HEADS, F32)]
    return pl.pallas_call(
        _proj_odd_wide_kernel,
        grid=(B, T // tt),
        in_specs=[x_spec, g_spec, m_spec, m_spec,
                  pl.BlockSpec(w_bf.shape, lambda i, j: (0, 0)),
                  pl.BlockSpec(wf_bf.shape, lambda i, j: (0, 0)),
                  pl.BlockSpec((1, N_FOX_HEADS), lambda i, j: (0, 0))],
        out_specs=[o[0] for o in outs],
        out_shape=[o[1] for o in outs],
        compiler_params=_params("arbitrary", "arbitrary"),
        name="proj_odd_wide",
    )(x, g, sc, sh, w_bf, wf_bf, b_f.reshape(1, N_FOX_HEADS))


def _proj_odd(x, g, sc, sh, w_bf, wf_bf, b_f, bb, tt):
    B, T, _ = x.shape
    x_spec, g_spec, m_spec = _row_specs(bb, tt)
    leaf = jax.ShapeDtypeStruct((B, T, FOX_W), F32)
    return pl.pallas_call(
        _proj_odd_kernel,
        grid=(B // bb, T // tt),
        in_specs=[x_spec, g_spec, m_spec, m_spec,
                  pl.BlockSpec(w_bf.shape, lambda i, j: (0, 0)),
                  pl.BlockSpec(wf_bf.shape, lambda i, j: (0, 0)),
                  pl.BlockSpec((1, N_FOX_HEADS), lambda i, j: (0, 0))],
        out_specs=[_out_spec(bb, tt, 6 * SEG), _out_spec(bb, tt, FOX_W), _out_spec(bb, tt, FOX_W),
                   _out_spec(bb, tt, N_FOX_HEADS)],
        out_shape=[jax.ShapeDtypeStruct((B, T, 6 * SEG), BF16), leaf, leaf,
                   jax.ShapeDtypeStruct((B, T, N_FOX_HEADS), F32)],
        compiler_params=_params("arbitrary", "arbitrary"),
        name="proj_odd",
    )(x, g, sc, sh, w_bf, wf_bf, b_f.reshape(1, N_FOX_HEADS))


def _attn_kernel(*refs, kind, n_chain, tq, tk, q_off, n_valid, lam_init):
    if kind == "diff":
        q_ref, k_ref, vt_ref, lp_ref, sg_ref, o_ref, acc_ref, m_ref, s_ref = refs
    elif kind == "fox":
        q_ref, k_ref, vt_ref, nd_ref, o_ref, acc_ref, m_ref, s_ref = refs
    else:
        q_ref, k_ref, vt_ref, o_ref, acc_ref, m_ref, s_ref = refs
    tq2 = 2 * tq
    q0 = q_off + pl.program_id(2) * tq

    low = lax.broadcasted_iota(jnp.int32, (tq, LANES), 1) < HEAD_DIM
    if kind == "fox":
        row = lax.broadcasted_iota(jnp.int32, (tq2, LANES), 0)
        col = lax.broadcasted_iota(jnp.int32, (tq2, LANES), 1)
        pick = jnp.logical_and(jnp.right_shift(row, tq.bit_length() - 1) * 3 <= col,
                               col < jnp.right_shift(row, tq.bit_length() - 1) * 3 + 3)
        bias_lanes = jnp.where(pick, 1.0, 0.0).astype(BF16)
    qqs = []
    for c in range(n_chain):
        q = q_ref[0, :, c * LANES:(c + 1) * LANES]
        zero = jnp.zeros_like(q)
        qq = jnp.concatenate([jnp.where(low, q, zero), jnp.where(low, zero, q)], axis=0)
        if kind == "fox":
            qq = jnp.concatenate([qq, bias_lanes], axis=1)
        qqs.append(qq)

    acc_ref[...] = jnp.zeros_like(acc_ref)
    if kind == "sb":
        m_ref[...] = jnp.zeros_like(m_ref)
        tri_r = lax.broadcasted_iota(jnp.int32, (tk, tk), 0)
        tri_c = lax.broadcasted_iota(jnp.int32, (tk, tk), 1)
        after = jnp.where(tri_c > tri_r, 1.0, 0.0).astype(BF16)
        after2 = jnp.concatenate([after, after], axis=1)
    else:
        m_ref[...] = jnp.full_like(m_ref, MASKED)
        ones_rows = jnp.ones((BF16_ROWS, tk), BF16)

    qpos = q0 + jnp.bitwise_and(lax.broadcasted_iota(jnp.int32, (tk, tq2), 1), tq - 1)
    kidx = lax.broadcasted_iota(jnp.int32, (tk, tq2), 0)

    chains = range(n_chain)

    def qk(kb, slot):
        start = pl.multiple_of(kb * tk, tk)
        for c in chains:
            k = k_ref[0, pl.ds(start, tk), c * LANES:(c + 1) * LANES]
            if kind == "fox":
                k = jnp.concatenate([k, nd_ref[0, c, pl.ds(start, tk), :]], axis=1)
            s_ref[slot, c] = _dot_nt(k, qqs[c])

    def consume(kb, slot, masked):
        scores = [s_ref[slot, c] for c in chains]
        if masked:
            kpos = kb * tk + kidx
            if kind == "diff":
                vis = jnp.right_shift(kpos, 6) <= jnp.right_shift(qpos, 6)
            elif kind == "fox":
                vis = kpos <= qpos
            else:
                vis = kpos < qpos
            vis = jnp.logical_and(vis, kpos < n_valid)
        vts = [vt_ref[0, c, kb] for c in chains]
        if kind == "sb":
            log_betas, laters = [], []
            for c in chains:
                s = scores[c]
                soft = jnp.log(1.0 + jnp.exp(-jnp.abs(s)))
                log_beta = jnp.minimum(s, 0.0) - soft
                log_keep = log_beta - s
                if masked:
                    log_keep = jnp.where(vis, log_keep, 0.0)
                hi = log_keep.astype(BF16)
                lo = (log_keep - hi.astype(F32)).astype(BF16)
                later = _dot(after2, jnp.concatenate([hi, lo], axis=0))
                laters.append(later + m_ref[c])
                m_ref[c] += later[0:1, :] + log_keep[0:1, :]
                log_betas.append(log_beta)
            weights = []
            for c in chains:
                w = jnp.exp(log_betas[c] + laters[c])
                if masked:
                    w = jnp.where(vis, w, 0.0)
                weights.append(w.astype(BF16))
            for c in chains:
                acc_ref[c] += _dot(vts[c], weights[c])
        else:
            probs, alphas = [], []
            for c in chains:
                s = scores[c]
                if masked:
                    s = jnp.where(vis, s, MASKED)
                m_prev = m_ref[c]
                m_new = jnp.maximum(m_prev, jnp.max(s, axis=0, keepdims=True))
                alphas.append(jnp.exp(m_prev - m_new))
                probs.append(jnp.exp(s - m_new).astype(BF16))
                m_ref[c] = m_new
            for c in chains:
                vt_aug = jnp.concatenate([vts[c], ones_rows], axis=0)
                acc_ref[c] = alphas[c] * acc_ref[c] + _dot(vt_aug, probs[c])

    n_full = q0 // tk
    qk(n_full, 0)
    qk(jnp.maximum(n_full - 1, 0), 1)
    consume(n_full, 0, True)

    def pair(j, carry):
        kb = n_full - 1 - 2 * j
        qk(jnp.maximum(kb - 1, 0), 0)
        consume(kb, 1, False)
        qk(jnp.maximum(kb - 2, 0), 1)
        consume(kb - 1, 0, False)
        return carry

    if kind == "sb":
        def top_sum():
            top = m_ref[0]
            for c in range(1, n_chain):
                top = jnp.maximum(top, m_ref[c])
            return jnp.max(top)

        def alive(carry):
            return jnp.logical_and(carry[0] < n_full // 2, carry[1] > SB_DEAD)

        def pair_and_check(carry):
            pair(carry[0], 0)
            return carry[0] + 1, top_sum()

        _, top = lax.while_loop(alive, pair_and_check, (jnp.int32(0), top_sum()))
        odd_block_left = jnp.logical_and(n_full % 2 == 1, top > SB_DEAD)
    else:
        lax.fori_loop(0, n_full // 2, pair, 0)
        odd_block_left = n_full % 2 == 1

    @pl.when(odd_block_left)
    def _():
        consume(0, 1, False)

    if kind == "diff":
        lp = lp_ref[...]
        lam = (jnp.exp(jnp.sum(lp[0:1] * lp[1:2], axis=1, keepdims=True))
               - jnp.exp(jnp.sum(lp[2:3] * lp[3:4], axis=1, keepdims=True)) + lam_init)
    for c in range(n_chain):
        acc = acc_ref[c]
        if kind == "sb":
            out_t = jnp.concatenate([acc[0:HEAD_DIM, 0:tq], acc[HEAD_DIM:LANES, tq:tq2]], axis=0)
        else:
            r = 1.0 / acc[LANES:LANES + 1, :]
            if kind == "fox":
                out_t = jnp.concatenate([acc[0:HEAD_DIM, 0:tq] * r[:, 0:tq],
                                         acc[HEAD_DIM:LANES, tq:tq2] * r[:, tq:tq2]], axis=0)
            else:
                a = acc[0:LANES, 0:tq] * r[:, 0:tq] - lam * (acc[0:LANES, tq:tq2] * r[:, tq:tq2])
                ms = jnp.mean(a * a, axis=0, keepdims=True)
                out_t = a * lax.rsqrt(ms + EPS) * sg_ref[...] * (1.0 - lam_init)
        if tq < LANES:
            out_t = jnp.concatenate([out_t, jnp.zeros((LANES, LANES - tq), F32)], axis=1)
        o_ref[0, :, c * LANES:(c + 1) * LANES] = out_t.T[0:tq].astype(o_ref.dtype)


def _attention(kind, q_arr, q_col, k_arr, k_col, vt_arr, n_col, *, n_chain, tq, tk, q_off, n_valid,
               extras=(), lam_init=0.0):
    B, Tq, _ = q_arr.shape
    Tk = k_arr.shape[1]
    n_kb = Tk // tk
    assert Tq % tq == 0 and Tk % tk == 0 and tk % tq == 0 and q_off % tk == 0 and n_valid <= Tk
    assert tq & (tq - 1) == 0 and n_col % n_chain == 0 and q_col % n_chain == 0 and k_col % n_chain == 0
    wide = n_chain * LANES
    in_specs = [
        pl.BlockSpec((1, tq, wide), lambda b, c, i: (b, i, q_col // n_chain + c)),
        pl.BlockSpec((1, Tk, wide), lambda b, c, i: (b, 0, k_col // n_chain + c)),
        pl.BlockSpec((1, n_chain, n_kb, LANES, tk), lambda b, c, i: (b, c, 0, 0, 0)),
    ]
    acc_rows = LANES if kind == "sb" else LANES + BF16_ROWS
    scratch = [pltpu.VMEM((n_chain, acc_rows, 2 * tq), F32), pltpu.VMEM((n_chain, 1, 2 * tq), F32),
               pltpu.VMEM((2, n_chain, tk, 2 * tq), F32)]
    if kind == "diff":
        lp, sg = extras
        in_specs += [pl.BlockSpec(lp.shape, lambda b, c, i: (0, 0)),
                     pl.BlockSpec(sg.shape, lambda b, c, i: (0, 0))]
    elif kind == "fox":
        (nd,) = extras
        in_specs.append(pl.BlockSpec((1, n_chain, Tk, LANES), lambda b, c, i: (b, c, 0, 0)))
    kernel = functools.partial(_attn_kernel, kind=kind, n_chain=n_chain, tq=tq, tk=tk, q_off=q_off,
                               n_valid=n_valid, lam_init=lam_init)
    return pl.pallas_call(
        kernel,
        grid=(B, n_col // n_chain, Tq // tq),
        in_specs=in_specs,
        out_specs=pl.BlockSpec((1, tq, wide), lambda b, c, i: (b, i, c)),
        out_shape=jax.ShapeDtypeStruct((B, Tq, n_col * LANES), BF16),
        scratch_shapes=scratch,
        compiler_params=_params("arbitrary", "arbitrary", "arbitrary"),
        name="attn_" + kind,
    )(q_arr, k_arr, vt_arr, *extras)


def _key_blocks_t(v, tk):
    B, Tk, W = v.shape
    return v.reshape(B, Tk // tk, tk, W // LANES, LANES).transpose(0, 3, 1, 4, 2)


def _cache_prep_kernel(k_ref, v_ref, nk_ref, nvt_ref, ko_ref, vto_ref, *, n_heads, tk, time_minor):
    kb = pl.program_id(1)
    n_past = pl.num_programs(1) - 1

    @pl.when(kb < n_past)
    def _():
        for c in range(ko_ref.shape[2] // LANES):
            cols = slice(c * LANES, (c + 1) * LANES)
            if time_minor:
                ko_ref[0, :, cols] = k_ref[0, cols, :].T.astype(BF16)
                vto_ref[0, c, 0] = v_ref[0, cols, :].astype(BF16)
            else:
                head_rows = pl.ds(c, tk, stride=n_heads)
                ko_ref[0, :, cols] = k_ref[0, head_rows, :].astype(BF16)
                vto_ref[0, c, 0] = v_ref[0, head_rows, :].T.astype(BF16)

    @pl.when(kb == n_past)
    def _():
        ko_ref[...] = nk_ref[...]
        vto_ref[...] = nvt_ref[...]


def _cache_prep(cache_k, cache_v, new_k, new_vt, tk):
    B, P, H, dh = cache_k.shape
    W = H * dh
    n_col, n_past = W // LANES, P // tk
    assert P % tk == 0 and dh in (HEAD_DIM, LANES)
    time_minor = dh != LANES
    if time_minor:
        cache_k, cache_v = [c.transpose(0, 2, 3, 1).reshape(B, W, P) for c in (cache_k, cache_v)]
        rows = pl.BlockSpec((1, W, tk), lambda b, j: (b, 0, jnp.minimum(j, n_past - 1)))
    else:
        cache_k, cache_v = [c.reshape(B, P * H, dh) for c in (cache_k, cache_v)]
        rows = pl.BlockSpec((1, tk * H, dh), lambda b, j: (b, jnp.minimum(j, n_past - 1), 0))
    return pl.pallas_call(
        functools.partial(_cache_prep_kernel, n_heads=H, tk=tk, time_minor=time_minor),
        grid=(B, n_past + 1),
        in_specs=[rows, rows,
                  pl.BlockSpec((1, tk, W), lambda b, j: (b, 0, 0)),
                  pl.BlockSpec((1, n_col, 1, LANES, tk), lambda b, j: (b, 0, 0, 0, 0))],
        out_specs=[pl.BlockSpec((1, tk, W), lambda b, j: (b, j, 0)),
                   pl.BlockSpec((1, n_col, 1, LANES, tk), lambda b, j: (b, 0, j, 0, 0))],
        out_shape=[jax.ShapeDtypeStruct((B, P + tk, W), BF16),
                   jax.ShapeDtypeStruct((B, n_col, n_past + 1, LANES, tk), BF16)],
        compiler_params=_params("arbitrary", "arbitrary"),
        name="cache_prep",
    )(cache_k, cache_v, new_k, new_vt)


def _cumsum_kernel(x_ref, o_ref, *, blk):
    T, H = x_ref.shape[1:]
    n_pair = H // 2
    r = lax.broadcasted_iota(jnp.int32, (blk, blk), 0)
    c = lax.broadcasted_iota(jnp.int32, (blk, blk), 1)
    upto = jnp.where(c <= r, 1.0, 0.0).astype(BF16)
    src = lax.broadcasted_iota(jnp.int32, (3 * H, n_pair * LANES), 0)
    col = lax.broadcasted_iota(jnp.int32, (3 * H, n_pair * LANES), 1)
    head, piece = jnp.bitwise_and(src, H - 1), jnp.right_shift(src, H.bit_length() - 1)
    here = jnp.right_shift(col, 7) == jnp.right_shift(head, 1)
    lane = jnp.bitwise_and(col, LANES - 1) - 3 * jnp.bitwise_and(head, 1)
    place = jnp.where(jnp.logical_and(here, lane == piece), 1.0, 0.0).astype(BF16)

    def chunk(j, carry):
        rows = pl.ds(pl.multiple_of(j * blk, blk), blk)
        hi, mid, lo = _split3(x_ref[0, rows, :])
        total = _dot(upto, hi) + _dot(upto, mid) + _dot(upto, lo) + carry
        pieces = jnp.concatenate([p.astype(F32) for p in _split3(-total)], axis=1).astype(BF16)
        lanes = _dot(pieces, place)
        for pair in range(n_pair):
            o_ref[0, pair, rows, :] = lanes[:, pair * LANES:(pair + 1) * LANES].astype(BF16)
        return total[blk - 1:blk, :]

    lax.fori_loop(0, T // blk, chunk, jnp.zeros((1, H), F32))


def _neg_cumsum(lf, blk):
    B, T, H = lf.shape
    return pl.pallas_call(
        functools.partial(_cumsum_kernel, blk=blk),
        grid=(B,),
        in_specs=[pl.BlockSpec((1, T, H), lambda b: (b, 0, 0))],
        out_specs=pl.BlockSpec((1, H // 2, T, LANES), lambda b: (b, 0, 0, 0)),
        out_shape=jax.ShapeDtypeStruct((B, H // 2, T, LANES), BF16),
        compiler_params=_params("arbitrary"),
        name="neg_cumsum",
    )(lf)


def _outproj_kernel(*refs, nseg):
    a_refs = refs[:nseg]
    w_ref, x_ref, g_ref, o_ref = refs[nseg:]
    bb, tt, d = x_ref.shape
    acc = None
    off = 0
    for a_ref in a_refs:
        width = a_ref.shape[2]
        part = _dot(a_ref[...].reshape(bb * tt, width), w_ref[off:off + width, :])
        acc = part if acc is None else acc + part
        off += width
    o_ref[...] = x_ref[...] + g_ref[...] * acc.reshape(bb, tt, d)


def _outproj(mixed, w_bf, x, gate, bb, tt):
    B, T, _ = x.shape
    x_spec, _, m_spec = _row_specs(bb, tt)
    return pl.pallas_call(
        functools.partial(_outproj_kernel, nseg=len(mixed)),
        grid=(B // bb, T // tt),
        in_specs=[_out_spec(bb, tt, a.shape[2]) for a in mixed]
        + [pl.BlockSpec(w_bf.shape, lambda i, j: (0, 0)), x_spec, m_spec],
        out_specs=x_spec,
        out_shape=jax.ShapeDtypeStruct(x.shape, F32),
        compiler_params=_params("arbitrary", "arbitrary"),
        name="outproj",
    )(*mixed, w_bf, x, gate)


def _route_t(h, wr_t, br):
    tm = h.shape[0]
    h_hi = h.astype(BF16)
    h_lo = (h - h_hi.astype(F32)).astype(BF16)
    w_pieces = jnp.concatenate(_split3(wr_t), axis=0)
    parts = _dot_nt(w_pieces, h_hi) + _dot_nt(w_pieces, h_lo)
    logits = parts[0:N_EXPERTS] + parts[N_EXPERTS:2 * N_EXPERTS] + parts[2 * N_EXPERTS:]
    s = jax.nn.sigmoid(logits)
    sel = s + br
    row = lax.broadcasted_iota(jnp.int32, (N_EXPERTS, tm), 0).astype(F32)
    neg = -jnp.inf

    def first_argmax(vals):
        top = jnp.max(vals, axis=0, keepdims=True)
        idx = jnp.min(jnp.where(vals == top, row, float(N_EXPERTS)), axis=0, keepdims=True)
        return top, idx

    best = None
    for g in range(N_GROUPS):
        in_g = (row >= g * EXPERTS_PER_GROUP) & (row < (g + 1) * EXPERTS_PER_GROUP)
        top1, i1 = first_argmax(jnp.where(in_g, sel, neg))
        top2, i2 = first_argmax(jnp.where(in_g & (row != i1), sel, neg))
        cand = (top1 + top2, i1, i2, jnp.full((1, tm), float(g), F32))
        if best is None:
            best = cand
        else:
            better = cand[0] > best[0]
            best = tuple(jnp.where(better, new, old) for new, old in zip(cand, best))
    _, i1, i2, group = best
    picked = jnp.where((row == i1) | (row == i2), s, 0.0)
    return picked / jnp.sum(picked, axis=0, keepdims=True), group


def _moe_kernel(*refs, final):
    if final:
        (x_ref, g_ref, sc_ref, sh_ref, g2_ref, wrt_ref, br_ref, wg_ref, wu_ref, wd_ref, gf_ref,
         o_ref, h_ref, xg_ref, yg_ref, gg_ref, gate_ref, pos_ref, before_ref, n_ref) = refs
    else:
        (x_ref, g_ref, sc_ref, sh_ref, g2_ref, wrt_ref, br_ref, wg_ref, wu_ref, wd_ref,
         o_ref, h_ref, xg_ref, yg_ref, gg_ref, gate_ref, pos_ref, before_ref, n_ref) = refs
    bb, tt, d = x_ref.shape
    tm = bb * tt
    e = pl.program_id(2)
    group = e // EXPERTS_PER_GROUP

    @pl.when((pl.program_id(0) == 0) & (pl.program_id(1) == 0) & (e == 0))
    def _():
        def fill(j, carry):
            rows = pl.ds(pl.multiple_of(j * MOE_CHUNK, MOE_CHUNK), MOE_CHUNK)
            r = lax.broadcasted_iota(jnp.int32, (MOE_CHUNK, tm), 0) + j * MOE_CHUNK
            c = lax.broadcasted_iota(jnp.int32, (MOE_CHUNK, tm), 1)
            before_ref[rows, :] = jnp.where(r < c, 1.0, 0.0).astype(BF16)
            return carry
        lax.fori_loop(0, tm // MOE_CHUNK, fill, 0)

    @pl.when(e == 0)
    def _():
        h = _norm_mod(x_ref[...], g_ref[...], sc_ref[...], sh_ref[...])
        h_ref[...] = h.astype(BF16)
        gates, grp = _route_t(h, wrt_ref[...], br_ref[...])
        gates = jnp.concatenate([gates, jnp.zeros((LANES - N_EXPERTS, tm), F32)], axis=0)
        hi = gates.astype(BF16)
        gate_ref[0] = hi
        gate_ref[1] = (gates - hi.astype(F32)).astype(BF16)
        grow = lax.broadcasted_iota(jnp.int32, pos_ref.shape, 0).astype(F32)
        member = jnp.where(grow == grp, 1.0, 0.0)
        rank = _dot(member.astype(BF16), before_ref[...])
        pos_ref[...] = jnp.where(member > 0.0, rank, -1.0)
        for g in range(N_GROUPS):
            count = jnp.sum(member[g:g + 1, :]).astype(jnp.int32)
            n_ref[g] = (count + MOE_CHUNK - 1) // MOE_CHUNK
        o_ref[...] = jnp.zeros_like(o_ref)

    n_chunks = n_ref[group]

    def chunk_rows(j):
        return pl.ds(pl.multiple_of(j * MOE_CHUNK, MOE_CHUNK), MOE_CHUNK)

    def one_hot(j, n_rows=MOE_CHUNK):
        want = (lax.broadcasted_iota(jnp.int32, (n_rows, tm), 0) + j * n_rows).astype(F32)
        return jnp.where(pos_ref[pl.ds(group, 1), :] == want, 1.0, 0.0).astype(BF16)

    @pl.when(e % EXPERTS_PER_GROUP == 0)
    def _():
        def gather(j, carry):
            rows = chunk_rows(j)
            pick = one_hot(j)
            xg_ref[rows, :] = _dot(pick, h_ref[...]).astype(BF16)
            gg_ref[rows, :] = _dot_nt(pick, gate_ref[0]) + _dot_nt(pick, gate_ref[1])
            yg_ref[rows, :] = jnp.zeros((MOE_CHUNK, d), F32)
            return carry
        lax.fori_loop(0, n_chunks, gather, 0)

        @pl.when(n_chunks % 2 == 1)
        def _():
            yg_ref[chunk_rows(n_chunks), :] = jnp.zeros((MOE_CHUNK, d), F32)

    wg, wu, wd = wg_ref[0, 0], wu_ref[0, 0], wd_ref[0, 0]
    lane = lax.broadcasted_iota(jnp.int32, (MOE_CHUNK, LANES), 1)

    def expert(j, carry):
        rows = chunk_rows(j)
        xr = xg_ref[rows, :]
        gt = _dot(xr, wg)
        up = _dot(xr, wu)
        ge = jnp.sum(jnp.where(lane == e, gg_ref[rows, :], 0.0), axis=1, keepdims=True)
        a = (gt * jax.nn.sigmoid(gt)) * up * ge
        yg_ref[rows, :] += _dot(a.astype(BF16), wd)
        return carry

    lax.fori_loop(0, n_chunks, expert, 0)

    @pl.when(e % EXPERTS_PER_GROUP == EXPERTS_PER_GROUP - 1)
    def _():
        def scatter(j, carry):
            rows = pl.ds(pl.multiple_of(j * 2 * MOE_CHUNK, 2 * MOE_CHUNK), 2 * MOE_CHUNK)
            back = lax.dot_general(one_hot(j, 2 * MOE_CHUNK), yg_ref[rows, :].astype(BF16),
                                   (((0,), (0,)), ((), ())), preferred_element_type=F32)
            o_ref[...] += back.reshape(bb, tt, d)
            return carry
        lax.fori_loop(0, (n_chunks + 1) // 2, scatter, 0)

    @pl.when(e == N_EXPERTS - 1)
    def _():
        y = x_ref[...] + g2_ref[...] * o_ref[...]
        if final:
            ms = jnp.mean(y * y, axis=-1, keepdims=True)
            y = y * lax.rsqrt(ms + EPS) * gf_ref[...]
        o_ref[...] = y


def _moe(x, g, sc, sh, g2, w_router, b_router, w_gate, w_up, w_down, layer, g_final, bb, tt):
    B, T, _ = x.shape
    final = g_final is not None
    x_spec = pl.BlockSpec((bb, tt, D_MODEL), lambda i, j, e: (i, j, 0))
    g_spec = pl.BlockSpec((1, D_MODEL), lambda i, j, e: (0, 0))
    m_spec = pl.BlockSpec((bb, 1, D_MODEL), lambda i, j, e: (i, 0, 0))
    in_specs = [x_spec, g_spec, m_spec, m_spec, m_spec,
                pl.BlockSpec((N_EXPERTS, D_MODEL), lambda i, j, e: (0, 0)),
                pl.BlockSpec((N_EXPERTS, 1), lambda i, j, e: (0, 0)),
                pl.BlockSpec((1, 1, D_MODEL, D_EXPERT), lambda i, j, e: (layer, e, 0, 0)),
                pl.BlockSpec((1, 1, D_MODEL, D_EXPERT), lambda i, j, e: (layer, e, 0, 0)),
                pl.BlockSpec((1, 1, D_EXPERT, D_MODEL), lambda i, j, e: (layer, e, 0, 0))]
    args = [x, g, sc, sh, g2, w_router.T, b_router.reshape(N_EXPERTS, 1), w_gate, w_up, w_down]
    if final:
        in_specs.append(g_spec)
        args.append(g_final)
    tm = bb * tt
    assert tm % (2 * MOE_CHUNK) == 0
    packed = tm
    scratch = [pltpu.VMEM((tm, D_MODEL), BF16),
               pltpu.VMEM((packed, D_MODEL), BF16),
               pltpu.VMEM((packed, D_MODEL), F32),
               pltpu.VMEM((packed, LANES), F32),
               pltpu.VMEM((2, LANES, tm), BF16),
               pltpu.VMEM((2 * N_GROUPS, tm), F32),
               pltpu.VMEM((tm, tm), BF16),
               pltpu.SMEM((N_GROUPS,), jnp.int32)]
    return pl.pallas_call(
        functools.partial(_moe_kernel, final=final),
        grid=(B // bb, T // tt, N_EXPERTS),
        in_specs=in_specs,
        out_specs=x_spec,
        out_shape=jax.ShapeDtypeStruct(x.shape, F32),
        scratch_shapes=scratch,
        compiler_params=_params("arbitrary", "arbitrary", "arbitrary"),
        name="moe",
    )(*args)


def _rope_tables(pos):
    half = ROPE_DIM // 2
    inv = ROPE_THETA ** (-jnp.arange(half, dtype=F32) / half)
    ang = pos.astype(F32)[:, None] * inv[None, :]
    cos, sin = jnp.cos(ang), jnp.sin(ang)
    ones = jnp.ones((pos.shape[0], HEAD_DIM - ROPE_DIM), F32)
    zeros = jnp.zeros_like(ones)
    zh = jnp.zeros_like(sin)
    c64 = jnp.concatenate([cos, cos, ones], axis=1)
    s1_64 = jnp.concatenate([-sin, zh, zeros], axis=1)
    s2_64 = jnp.concatenate([zh, sin, zeros], axis=1)
    return tuple(jnp.concatenate([t, t], axis=1) for t in (c64, s1_64, s2_64))


def _pad_time(a, total):
    if total == a.shape[1]:
        return a
    return jnp.pad(a, ((0, 0), (0, total - a.shape[1])) + ((0, 0),) * (a.ndim - 2))


def _trunk(x, mod, past, wts, cfg):
    B, T, _ = x.shape
    P = 0 if past is None else past["diff_k"].shape[2]
    bb, tt_proj, tt_out, tt_moe, tq, tk = cfg
    wide = past is None and bb == 1 and tt_proj == tk
    n_valid = P + T
    Tk = -(-n_valid // tk) * tk
    Tq = -(-T // tq) * tq
    tables = _rope_tables(P + jnp.arange(T))
    if bb > 1:
        tables = tuple(jnp.tile(t, (bb, 1)) for t in tables)
    attn = functools.partial(_attention, n_chain=N_CHAIN, tq=tq, tk=tk, q_off=P, n_valid=n_valid)
    leaves_even, leaves_odd = [], []

    def with_cache(qkv, k_cols, v_cols, cache_k, cache_v):
        assert Tk == P + tk
        new_k = _pad_time(qkv[:, :, k_cols[0]:k_cols[1]], tk)
        new_vt = _key_blocks_t(_pad_time(qkv[:, :, v_cols[0]:v_cols[1]], tk), tk)
        return _cache_prep(cache_k, cache_v, new_k, new_vt, tk)

    def heads_last(leaf_t, n_heads):
        return leaf_t.reshape(B, n_heads, -1, T).transpose(0, 3, 1, 2)

    for l in range(DEPTH):
        sh1, sc1, g1, sh2, sc2, g2 = [mod[l, :, k * D_MODEL:(k + 1) * D_MODEL][:, None, :] for k in range(6)]
        i = l // 2
        g_mix = wts["norm_mix"][l][None, :]
        if l % 2 == 0:
            lam_init = 0.8 - 0.6 * math.exp(-0.3 * l)
            lp = jnp.stack([wts["diff_lq1"][i], wts["diff_lk1"][i], wts["diff_lq2"][i], wts["diff_lk2"][i]])
            diff_extras = (lp, wts["diff_subln"][i][:, None])
            if wide:
                qk, ak, av, bkt, bvt, dvt, svt = _proj_even_wide(x, g_mix, sc1, sh1, wts["w_in_even"][i], tables, tt_proj)
                leaves_even.append((ak.reshape(B, T, N_DIFF_HEADS, LANES), av.reshape(B, T, N_DIFF_HEADS, LANES),
                                    heads_last(bkt, N_SB_HEADS), heads_last(bvt, N_SB_HEADS)))
                q_arr, q_cols = qk, (0, 2 * SEG // LANES)
                dk, dk_col, sk, sk_col = qk, SEG // LANES, qk, 3 * SEG // LANES
            else:
                qkv, ak, av, bk, bv = _proj_even(x, g_mix, sc1, sh1, wts["w_in_even"][i], tables, bb, tt_proj)
                leaves_even.append((ak.reshape(B, T, N_DIFF_HEADS, LANES), av.reshape(B, T, N_DIFF_HEADS, LANES),
                                    bk.reshape(B, T, N_SB_HEADS, HEAD_DIM), bv.reshape(B, T, N_SB_HEADS, HEAD_DIM)))
                q_arr, q_cols = _pad_time(qkv, Tq), (0, 3 * SEG // LANES)
                dk, dvt = with_cache(qkv, (SEG, 2 * SEG), (2 * SEG, 3 * SEG), past["diff_k"][i], past["diff_v"][i])
                sk, svt = with_cache(qkv, (4 * SEG, 5 * SEG), (5 * SEG, 6 * SEG), past["sb_k"][i], past["sb_v"][i])
                dk_col = sk_col = 0
            a_out = attn("diff", q_arr, q_cols[0], dk, dk_col, dvt, N_DIFF_HEADS, extras=diff_extras, lam_init=lam_init)
            b_out = attn("sb", q_arr, q_cols[1], sk, sk_col, svt, N_SB_HEADS // 2)
            x = _outproj([a_out[:, :T], b_out[:, :T]], wts["w_out_even"][i], x, g1, bb, tt_out)
        else:
            if wide:
                qk, kt, vt, fvt, lf = _proj_odd_wide(x, g_mix, sc1, sh1, wts["w_in_odd"][i], wts["w_f_odd"][i],
                                                     wts["b_forget"][i], tt_proj)
                leaves_odd.append((heads_last(kt, N_FOX_HEADS), heads_last(vt, N_FOX_HEADS), lf))
                q_arr, fk, fk_col, lf_all = qk, qk, 2 * SEG // LANES, lf
            else:
                qkv, k, v, lf = _proj_odd(x, g_mix, sc1, sh1, wts["w_in_odd"][i], wts["w_f_odd"][i],
                                          wts["b_forget"][i], bb, tt_proj)
                leaves_odd.append((k.reshape(B, T, N_FOX_HEADS, HEAD_DIM), v.reshape(B, T, N_FOX_HEADS, HEAD_DIM), lf))
                q_arr = _pad_time(qkv, Tq)
                fk, fvt = with_cache(qkv, (2 * SEG, 4 * SEG), (4 * SEG, 6 * SEG), past["fox_k"][i], past["fox_v"][i])
                fk_col = 0
                lf_all = jnp.concatenate([past["fox_logf"][i], lf], axis=1)
            nd = _neg_cumsum(_pad_time(lf_all, Tk), tk)
            o = attn("fox", q_arr, 0, fk, fk_col, fvt, N_FOX_HEADS // 2, extras=(nd,))
            x = _outproj([o[:, :T]], wts["w_out_odd"][i], x, g1, bb, tt_out)
        g_final = wts["norm_final"][None, :] if l == DEPTH - 1 else None
        x = _moe(x, wts["norm_ffn"][l][None, :], sc2, sh2, g2, wts["w_router"], wts["b_router"],
                 wts["w_gate"], wts["w_up"], wts["w_down"], l, g_final, bb, tt_moe)
    return x, leaves_even, leaves_odd


def kernel(x_prompt, x_sample, c_prompt, c_sample, cache_diff_k, cache_diff_v, cache_sb_k, cache_sb_v, cache_fox_k, cache_fox_v, cache_fox_logf, w_ada, b_ada, norm_mix, norm_ffn, w_in_even, w_out_even, diff_lq1, diff_lk1, diff_lq2, diff_lk2, diff_subln, w_in_odd, b_forget, w_out_odd, w_router, b_router, w_gate, w_up, w_down, norm_final):
    Bp, Tp, _ = x_prompt.shape
    Bs, Ts, _ = x_sample.shape
    wts = {
        "norm_mix": norm_mix, "norm_ffn": norm_ffn, "norm_final": norm_final,
        "w_in_even": w_in_even.astype(BF16), "w_out_even": w_out_even.astype(BF16),
        "w_in_odd": w_in_odd[:, :, :3 * FOX_W].astype(BF16), "w_f_odd": w_in_odd[:, :, 3 * FOX_W:].astype(BF16),
        "w_out_odd": w_out_odd.astype(BF16), "b_forget": b_forget,
        "diff_lq1": diff_lq1, "diff_lk1": diff_lk1, "diff_lq2": diff_lq2, "diff_lk2": diff_lk2,
        "diff_subln": diff_subln, "w_router": w_router, "b_router": b_router,
        "w_gate": w_gate.astype(BF16), "w_up": w_up.astype(BF16), "w_down": w_down.astype(BF16),
    }
    past = {"diff_k": cache_diff_k, "diff_v": cache_diff_v, "sb_k": cache_sb_k, "sb_v": cache_sb_v,
            "fox_k": cache_fox_k, "fox_v": cache_fox_v, "fox_logf": cache_fox_logf}

    rows = Bp + Bs
    rows_pad = -(-rows // 16) * 16
    c_all = jnp.pad(jnp.concatenate([c_prompt, c_sample], axis=0), ((0, rows_pad - rows), (0, 0)))
    mod = _ada_mod(c_all, w_ada, b_ada)

    cfg_prompt = (1, 256, 512, 1024, 256, 256)
    cfg_sample = (Bs, Ts, Ts, Ts, 64, 256)
    y_p, even_p, odd_p = _trunk(x_prompt, mod[:, :Bp], None, wts, cfg_prompt)
    y_s, even_s, odd_s = _trunk(x_sample, mod[:, Bp:rows], past, wts, cfg_sample)

    def layers(rows):
        return rows[0][None] if len(rows) == 1 else jnp.stack(rows, axis=0)

    def pack(even, odd):
        return tuple(layers(r) for r in zip(*even)) + tuple(layers(r) for r in zip(*odd))

    return (y_p, y_s) + pack(even_p, odd_p) + pack(even_s, odd_s)
```

```python
import functools
import math

import jax
import jax.numpy as jnp
from jax import lax
from jax.experimental import pallas as pl
from jax.experimental.pallas import tpu as pltpu

D_MODEL = 1024
DEPTH = 2
CHUNK = 64
HEAD_DIM = 64
N_DIFF_HEADS = D_MODEL // 256
N_SB_HEADS = D_MODEL // 128
N_FOX_HEADS = D_MODEL // HEAD_DIM
DIFF_W = N_DIFF_HEADS * 2 * HEAD_DIM
SB_W = N_SB_HEADS * HEAD_DIM
FOX_W = N_FOX_HEADS * HEAD_DIM
ROPE_DIM = HEAD_DIM // 4
ROPE_THETA = 500000.0
N_EXPERTS = 16
N_GROUPS = 4
EXPERTS_PER_GROUP = N_EXPERTS // N_GROUPS
D_EXPERT = D_MODEL // 2
EPS = 1e-6

LANES = 128
BF16_ROWS = 16
SEG = 512
QK_SCALE = HEAD_DIM ** -0.5
MASKED = -1e30
N_CHAIN = 4
SB_DEAD = -110.0
MOE_CHUNK = 128
PREP_BLOCKS = 2
VMEM_LIMIT = 48 * 1024 * 1024

F32 = jnp.float32
BF16 = jnp.bfloat16


def _params(*sem):
    return pltpu.CompilerParams(dimension_semantics=sem, vmem_limit_bytes=VMEM_LIMIT)


def _dot(a, b):
    return jnp.dot(a, b, preferred_element_type=F32)


def _dot_nt(a, b):
    return lax.dot_general(a, b, (((1,), (1,)), ((), ())), preferred_element_type=F32)


def _norm_mod(x, g, sc, sh):
    bb, tt, d = x.shape
    ms = jnp.mean(x * x, axis=-1, keepdims=True)
    h = x * lax.rsqrt(ms + EPS) * g
    h = h * (1.0 + sc) + sh
    return h.reshape(bb * tt, d)


def _log_sigmoid(x):
    return jnp.minimum(x, 0.0) - jnp.log(1.0 + jnp.exp(-jnp.abs(x)))


def _split3(x):
    hi = x.astype(BF16)
    r1 = x - hi.astype(F32)
    mid = r1.astype(BF16)
    lo = (r1 - mid.astype(F32)).astype(BF16)
    return hi, mid, lo


def _ada_kernel(c_ref, w_ref, b_ref, o_ref):
    c = c_ref[...]
    a = (c * jax.nn.sigmoid(c)).astype(BF16)
    o_ref[0] = _dot(a, w_ref[0].astype(BF16)) + b_ref[0]


def _ada_mod(c_all, w_ada, b_ada):
    rows = c_all.shape[0]
    tn = 1536
    width = 6 * D_MODEL
    return pl.pallas_call(
        _ada_kernel,
        grid=(DEPTH, width // tn),
        in_specs=[
            pl.BlockSpec((rows, D_MODEL), lambda l, j: (0, 0)),
            pl.BlockSpec((1, D_MODEL, tn), lambda l, j: (l, 0, j)),
            pl.BlockSpec((1, 1, tn), lambda l, j: (l, 0, j)),
        ],
        out_specs=pl.BlockSpec((1, rows, tn), lambda l, j: (l, 0, j)),
        out_shape=jax.ShapeDtypeStruct((DEPTH, rows, width), F32),
        compiler_params=_params("arbitrary", "arbitrary"),
        name="ada_mod",
    )(c_all, w_ada, b_ada.reshape(DEPTH, 1, width))


def _rope(p, cos, s1, s2):
    outs = []
    for c in range(SEG // LANES):
        pc = p[:, c * LANES:(c + 1) * LANES]
        up = pltpu.roll(pc, LANES - ROPE_DIM // 2, 1)
        down = pltpu.roll(pc, ROPE_DIM // 2, 1)
        outs.append(pc * cos + up * s1 + down * s2)
    return jnp.concatenate(outs, axis=1)


def _proj_even_kernel(x_ref, g_ref, sc_ref, sh_ref, w_ref, cos_ref, s1_ref, s2_ref,
                      qkv_ref, ak_ref, av_ref, bk_ref, bv_ref):
    bb, tt, _ = x_ref.shape
    h = _norm_mod(x_ref[...], g_ref[...], sc_ref[...], sh_ref[...]).astype(BF16)
    cos, s1, s2 = cos_ref[...], s1_ref[...], s2_ref[...]

    def seg(i):
        return _dot(h, w_ref[:, i * SEG:(i + 1) * SEG])

    def put(i, val, leaf_ref):
        val3 = val.reshape(bb, tt, SEG)
        qkv_ref[:, :, i * SEG:(i + 1) * SEG] = val3.astype(BF16)
        if leaf_ref is not None:
            leaf_ref[...] = val3

    put(0, _rope(seg(0), cos, s1, s2) * QK_SCALE, None)
    put(1, _rope(seg(1), cos, s1, s2), ak_ref)
    put(2, seg(2), av_ref)
    put(3, seg(3) * QK_SCALE, None)
    put(4, seg(4), bk_ref)
    put(5, seg(5), bv_ref)


def _rows_by_head(ref, val, n_heads):
    tt = val.shape[0]
    for hd in range(n_heads):
        ref[0, pl.ds(hd, tt, stride=n_heads), :] = val[:, hd * LANES:(hd + 1) * LANES]


def _key_blocks(ref, first, val_t):
    n = val_t.shape[0] // LANES
    ref[0, first:first + n, 0] = val_t.reshape(n, LANES, val_t.shape[1]).astype(BF16)


def _proj_even_wide_kernel(x_ref, g_ref, sc_ref, sh_ref, w_ref, cos_ref, s1_ref, s2_ref,
                           qk_ref, ak_ref, av_ref, bkt_ref, bvt_ref, dvt_ref, svt_ref):
    h = _norm_mod(x_ref[...], g_ref[...], sc_ref[...], sh_ref[...]).astype(BF16)
    cos, s1, s2 = cos_ref[...], s1_ref[...], s2_ref[...]

    def seg(i):
        return _dot(h, w_ref[:, i * SEG:(i + 1) * SEG])

    qk_ref[0, :, 0:SEG] = (_rope(seg(0), cos, s1, s2) * QK_SCALE).astype(BF16)
    ak = _rope(seg(1), cos, s1, s2)
    qk_ref[0, :, SEG:2 * SEG] = ak.astype(BF16)
    _rows_by_head(ak_ref, ak, N_DIFF_HEADS)
    av = seg(2)
    _rows_by_head(av_ref, av, N_DIFF_HEADS)
    _key_blocks(dvt_ref, 0, av.T)
    qk_ref[0, :, 2 * SEG:3 * SEG] = (seg(3) * QK_SCALE).astype(BF16)
    bk = seg(4)
    qk_ref[0, :, 3 * SEG:4 * SEG] = bk.astype(BF16)
    bkt_ref[0] = bk.T
    bvt = seg(5).T
    bvt_ref[0] = bvt
    _key_blocks(svt_ref, 0, bvt)


def _proj_odd_wide_kernel(x_ref, g_ref, sc_ref, sh_ref, w_ref, wf_ref, bf_ref,
                          qk_ref, kt_ref, vt_ref, fvt_ref, lf_ref):
    h = _norm_mod(x_ref[...], g_ref[...], sc_ref[...], sh_ref[...]).astype(BF16)

    def seg(i):
        return _dot(h, w_ref[:, i * SEG:(i + 1) * SEG])

    for i in range(2):
        qk_ref[0, :, i * SEG:(i + 1) * SEG] = (seg(i) * QK_SCALE).astype(BF16)
    for i in range(2):
        k = seg(2 + i)
        qk_ref[0, :, (2 + i) * SEG:(3 + i) * SEG] = k.astype(BF16)
        kt_ref[0, i * SEG:(i + 1) * SEG, :] = k.T
        v_t = seg(4 + i).T
        vt_ref[0, i * SEG:(i + 1) * SEG, :] = v_t
        _key_blocks(fvt_ref, i * (SEG // LANES), v_t)
    fl = _dot(h, wf_ref[...]) + bf_ref[...]
    lf_ref[0] = _log_sigmoid(fl)


def _proj_odd_kernel(x_ref, g_ref, sc_ref, sh_ref, w_ref, wf_ref, bf_ref,
                     qkv_ref, k_ref, v_ref, lf_ref):
    bb, tt, _ = x_ref.shape
    h = _norm_mod(x_ref[...], g_ref[...], sc_ref[...], sh_ref[...]).astype(BF16)

    def seg(i):
        return _dot(h, w_ref[:, i * SEG:(i + 1) * SEG]).reshape(bb, tt, SEG)

    for i in range(2):
        qkv_ref[:, :, i * SEG:(i + 1) * SEG] = (seg(i) * QK_SCALE).astype(BF16)
    for i in range(2, 6):
        val = seg(i)
        qkv_ref[:, :, i * SEG:(i + 1) * SEG] = val.astype(BF16)
        leaf_ref = k_ref if i < 4 else v_ref
        leaf_ref[:, :, (i % 2) * SEG:(i % 2 + 1) * SEG] = val
    fl = _dot(h, wf_ref[...]) + bf_ref[...]
    lf_ref[...] = _log_sigmoid(fl).reshape(bb, tt, N_FOX_HEADS)


def _row_specs(bb, tt):
    x_spec = pl.BlockSpec((bb, tt, D_MODEL), lambda i, j: (i, j, 0))
    g_spec = pl.BlockSpec((1, D_MODEL), lambda i, j: (0, 0))
    m_spec = pl.BlockSpec((bb, 1, D_MODEL), lambda i, j: (i, 0, 0))
    return x_spec, g_spec, m_spec


def _out_spec(bb, tt, width):
    return pl.BlockSpec((bb, tt, width), lambda i, j: (i, j, 0))


def _proj_even(x, g, sc, sh, w_bf, tables, bb, tt):
    B, T, _ = x.shape
    x_spec, g_spec, m_spec = _row_specs(bb, tt)
    tm = bb * tt
    t_spec = pl.BlockSpec((tm, LANES), (lambda i, j: (j, 0)) if bb == 1 else (lambda i, j: (0, 0)))
    leaf = jax.ShapeDtypeStruct((B, T, SEG), F32)
    return pl.pallas_call(
        _proj_even_kernel,
        grid=(B // bb, T // tt),
        in_specs=[x_spec, g_spec, m_spec, m_spec,
                  pl.BlockSpec(w_bf.shape, lambda i, j: (0, 0)), t_spec, t_spec, t_spec],
        out_specs=[_out_spec(bb, tt, 6 * SEG)] + [_out_spec(bb, tt, SEG)] * 4,
        out_shape=[jax.ShapeDtypeStruct((B, T, 6 * SEG), BF16), leaf, leaf, leaf, leaf],
        compiler_params=_params("arbitrary", "arbitrary"),
        name="proj_even",
    )(x, g, sc, sh, w_bf, *tables)


def _wide_specs(B, T, tt):
    def rows(width, dtype):
        return pl.BlockSpec((1, tt, width), lambda i, j: (i, j, 0)), jax.ShapeDtypeStruct((B, T, width), dtype)

    def by_head(n_heads):
        return (pl.BlockSpec((1, tt * n_heads, LANES), lambda i, j: (i, j, 0)),
                jax.ShapeDtypeStruct((B, T * n_heads, LANES), F32))

    def transposed(width):
        return pl.BlockSpec((1, width, tt), lambda i, j: (i, 0, j)), jax.ShapeDtypeStruct((B, width, T), F32)

    def key_blocks(n_col):
        return (pl.BlockSpec((1, n_col, 1, LANES, tt), lambda i, j: (i, 0, j, 0, 0)),
                jax.ShapeDtypeStruct((B, n_col, T // tt, LANES, tt), BF16))

    return rows, by_head, transposed, key_blocks


def _proj_even_wide(x, g, sc, sh, w_bf, tables, tt):
    B, T, _ = x.shape
    x_spec, g_spec, m_spec = _row_specs(1, tt)
    t_spec = pl.BlockSpec((tt, LANES), lambda i, j: (j, 0))
    rows, by_head, transposed, key_blocks = _wide_specs(B, T, tt)
    outs = [rows(4 * SEG, BF16), by_head(N_DIFF_HEADS), by_head(N_DIFF_HEADS), transposed(SEG), transposed(SEG),
            key_blocks(SEG // LANES), key_blocks(SEG // LANES)]
    return pl.pallas_call(
        _proj_even_wide_kernel,
        grid=(B, T // tt),
        in_specs=[x_spec, g_spec, m_spec, m_spec,
                  pl.BlockSpec(w_bf.shape, lambda i, j: (0, 0)), t_spec, t_spec, t_spec],
        out_specs=[o[0] for o in outs],
        out_shape=[o[1] for o in outs],
        compiler_params=_params("arbitrary", "arbitrary"),
        name="proj_even_wide",
    )(x, g, sc, sh, w_bf, *tables)


def _proj_odd_wide(x, g, sc, sh, w_bf, wf_bf, b_f, tt):
    B, T, _ = x.shape
    x_spec, g_spec, m_spec = _row_specs(1, tt)
    rows, by_head, transposed, key_blocks = _wide_specs(B, T, tt)
    outs = [rows(4 * SEG, BF16), transposed(FOX_W), transposed(FOX_W), key_blocks(FOX_W // LANES),
            rows(N_FOX_HEADS, F32)]
    return pl.pallas_call(
        _proj_odd_wide_kernel,
        grid=(B, T // tt),
        in_specs=[x_spec, g_spec, m_spec, m_spec,
                  pl.BlockSpec(w_bf.shape, lambda i, j: (0, 0)),
                  pl.BlockSpec(wf_bf.shape, lambda i, j: (0, 0)),
                  pl.BlockSpec((1, N_FOX_HEADS), lambda i, j: (0, 0))],
        out_specs=[o[0] for o in outs],
        out_shape=[o[1] for o in outs],
        compiler_params=_params("arbitrary", "arbitrary"),
        name="proj_odd_wide",
    )(x, g, sc, sh, w_bf, wf_bf, b_f.reshape(1, N_FOX_HEADS))


def _proj_odd(x, g, sc, sh, w_bf, wf_bf, b_f, bb, tt):
    B, T, _ = x.shape
    x_spec, g_spec, m_spec = _row_specs(bb, tt)
    leaf = jax.ShapeDtypeStruct((B, T, FOX_W), F32)
    return pl.pallas_call(
        _proj_odd_kernel,
        grid=(B // bb, T // tt),
        in_specs=[x_spec, g_spec, m_spec, m_spec,
                  pl.BlockSpec(w_bf.shape, lambda i, j: (0, 0)),
                  pl.BlockSpec(wf_bf.shape, lambda i, j: (0, 0)),
                  pl.BlockSpec((1, N_FOX_HEADS), lambda i, j: (0, 0))],
        out_specs=[_out_spec(bb, tt, 6 * SEG), _out_spec(bb, tt, FOX_W), _out_spec(bb, tt, FOX_W),
                   _out_spec(bb, tt, N_FOX_HEADS)],
        out_shape=[jax.ShapeDtypeStruct((B, T, 6 * SEG), BF16), leaf, leaf,
                   jax.ShapeDtypeStruct((B, T, N_FOX_HEADS), F32)],
        compiler_params=_params("arbitrary", "arbitrary"),
        name="proj_odd",
    )(x, g, sc, sh, w_bf, wf_bf, b_f.reshape(1, N_FOX_HEADS))


def _attn_kernel(*refs, kind, n_chain, tq, tk, q_off, n_valid, lam_init):
    if kind == "diff":
        q_ref, k_ref, vt_ref, lp_ref, sg_ref, o_ref, acc_ref, m_ref, s_ref = refs
    elif kind == "fox":
        q_ref, k_ref, vt_ref, nd_ref, o_ref, acc_ref, m_ref, s_ref = refs
    else:
        q_ref, k_ref, vt_ref, o_ref, acc_ref, m_ref, s_ref = refs
    tq2 = 2 * tq
    q0 = q_off + pl.program_id(2) * tq

    low = lax.broadcasted_iota(jnp.int32, (tq, LANES), 1) < HEAD_DIM
    if kind == "fox":
        row = lax.broadcasted_iota(jnp.int32, (tq2, LANES), 0)
        col = lax.broadcasted_iota(jnp.int32, (tq2, LANES), 1)
        pick = jnp.logical_and(jnp.right_shift(row, tq.bit_length() - 1) * 3 <= col,
                               col < jnp.right_shift(row, tq.bit_length() - 1) * 3 + 3)
        bias_lanes = jnp.where(pick, 1.0, 0.0).astype(BF16)
    qqs = []
    for c in range(n_chain):
        q = q_ref[0, :, c * LANES:(c + 1) * LANES]
        zero = jnp.zeros_like(q)
        qq = jnp.concatenate([jnp.where(low, q, zero), jnp.where(low, zero, q)], axis=0)
        if kind == "fox":
            qq = jnp.concatenate([qq, bias_lanes], axis=1)
        qqs.append(qq)

    acc_ref[...] = jnp.zeros_like(acc_ref)
    if kind == "sb":
        m_ref[...] = jnp.zeros_like(m_ref)
        tri_r = lax.broadcasted_iota(jnp.int32, (tk, tk), 0)
        tri_c = lax.broadcasted_iota(jnp.int32, (tk, tk), 1)
        after = jnp.where(tri_c > tri_r, 1.0, 0.0).astype(BF16)
        after2 = jnp.concatenate([after, after], axis=1)
    else:
        m_ref[...] = jnp.full_like(m_ref, MASKED)
        ones_rows = jnp.ones((BF16_ROWS, tk), BF16)

    qpos = q0 + jnp.bitwise_and(lax.broadcasted_iota(jnp.int32, (tk, tq2), 1), tq - 1)
    kidx = lax.broadcasted_iota(jnp.int32, (tk, tq2), 0)

    chains = range(n_chain)

    def qk(kb, slot):
        start = pl.multiple_of(kb * tk, tk)
        for c in chains:
            k = k_ref[0, pl.ds(start, tk), c * LANES:(c + 1) * LANES]
            if kind == "fox":
                k = jnp.concatenate([k, nd_ref[0, c, pl.ds(start, tk), :]], axis=1)
            s_ref[slot, c] = _dot_nt(k, qqs[c])

    def consume(kb, slot, masked):
        scores = [s_ref[slot, c] for c in chains]
        if masked:
            kpos = kb * tk + kidx
            if kind == "diff":
                vis = jnp.right_shift(kpos, 6) <= jnp.right_shift(qpos, 6)
            elif kind == "fox":
                vis = kpos <= qpos
            else:
                vis = kpos < qpos
            vis = jnp.logical_and(vis, kpos < n_valid)
        vts = [vt_ref[0, c, kb] for c in chains]
        if kind == "sb":
            log_betas, laters = [], []
            for c in chains:
                s = scores[c]
                soft = jnp.log(1.0 + jnp.exp(-jnp.abs(s)))
                log_beta = jnp.minimum(s, 0.0) - soft
                log_keep = log_beta - s
                if masked:
                    log_keep = jnp.where(vis, log_keep, 0.0)
                hi = log_keep.astype(BF16)
                lo = (log_keep - hi.astype(F32)).astype(BF16)
                later = _dot(after2, jnp.concatenate([hi, lo], axis=0))
                laters.append(later + m_ref[c])
                m_ref[c] += later[0:1, :] + log_keep[0:1, :]
                log_betas.append(log_beta)
            weights = []
            for c in chains:
                w = jnp.exp(log_betas[c] + laters[c])
                if masked:
                    w = jnp.where(vis, w, 0.0)
                weights.append(w.astype(BF16))
            for c in chains:
                acc_ref[c] += _dot(vts[c], weights[c])
        else:
            probs, alphas = [], []
            for c in chains:
                s = scores[c]
                if masked:
                    s = jnp.where(vis, s, MASKED)
                m_prev = m_ref[c]
                m_new = jnp.maximum(m_prev, jnp.max(s, axis=0, keepdims=True))
                alphas.append(jnp.exp(m_prev - m_new))
                probs.append(jnp.exp(s - m_new).astype(BF16))
                m_ref[c] = m_new
            for c in chains:
                vt_aug = jnp.concatenate([vts[c], ones_rows], axis=0)
                acc_ref[c] = alphas[c] * acc_ref[c] + _dot(vt_aug, probs[c])

    n_full = q0 // tk
    qk(n_full, 0)
    qk(jnp.maximum(n_full - 1, 0), 1)
    consume(n_full, 0, True)

    def pair(j, carry):
        kb = n_full - 1 - 2 * j
        qk(jnp.maximum(kb - 1, 0), 0)
        consume(kb, 1, False)
        qk(jnp.maximum(kb - 2, 0), 1)
        consume(kb - 1, 0, False)
        return carry

    if kind == "sb":
        def top_sum():
            top = m_ref[0]
            for c in range(1, n_chain):
                top = jnp.maximum(top, m_ref[c])
            return jnp.max(top)

        def alive(carry):
            return jnp.logical_and(carry[0] < n_full // 2, carry[1] > SB_DEAD)

        def pair_and_check(carry):
            pair(carry[0], 0)
            return carry[0] + 1, top_sum()

        _, top = lax.while_loop(alive, pair_and_check, (jnp.int32(0), top_sum()))
        odd_block_left = jnp.logical_and(n_full % 2 == 1, top > SB_DEAD)
    else:
        lax.fori_loop(0, n_full // 2, pair, 0)
        odd_block_left = n_full % 2 == 1

    @pl.when(odd_block_left)
    def _():
        consume(0, 1, False)

    if kind == "diff":
        lp = lp_ref[...]
        lam = (jnp.exp(jnp.sum(lp[0:1] * lp[1:2], axis=1, keepdims=True))
               - jnp.exp(jnp.sum(lp[2:3] * lp[3:4], axis=1, keepdims=True)) + lam_init)
    for c in range(n_chain):
        acc = acc_ref[c]
        if kind == "sb":
            out_t = jnp.concatenate([acc[0:HEAD_DIM, 0:tq], acc[HEAD_DIM:LANES, tq:tq2]], axis=0)
        else:
            r = 1.0 / acc[LANES:LANES + 1, :]
            if kind == "fox":
                out_t = jnp.concatenate([acc[0:HEAD_DIM, 0:tq] * r[:, 0:tq],
                                         acc[HEAD_DIM:LANES, tq:tq2] * r[:, tq:tq2]], axis=0)
            else:
                a = acc[0:LANES, 0:tq] * r[:, 0:tq] - lam * (acc[0:LANES, tq:tq2] * r[:, tq:tq2])
                ms = jnp.mean(a * a, axis=0, keepdims=True)
                out_t = a * lax.rsqrt(ms + EPS) * sg_ref[...] * (1.0 - lam_init)
        if tq < LANES:
            out_t = jnp.concatenate([out_t, jnp.zeros((LANES, LANES - tq), F32)], axis=1)
        o_ref[0, :, c * LANES:(c + 1) * LANES] = out_t.T[0:tq].astype(o_ref.dtype)


def _attention(kind, q_arr, q_col, k_arr, k_col, vt_arr, n_col, *, n_chain, tq, tk, q_off, n_valid,
               extras=(), lam_init=0.0):
    B, Tq, _ = q_arr.shape
    Tk = k_arr.shape[1]
    n_kb = Tk // tk
    assert Tq % tq == 0 and Tk % tk == 0 and tk % tq == 0 and q_off % tk == 0 and n_valid <= Tk
    assert tq & (tq - 1) == 0 and n_col % n_chain == 0 and q_col % n_chain == 0 and k_col % n_chain == 0
    wide = n_chain * LANES
    in_specs = [
        pl.BlockSpec((1, tq, wide), lambda b, c, i: (b, i, q_col // n_chain + c)),
        pl.BlockSpec((1, Tk, wide), lambda b, c, i: (b, 0, k_col // n_chain + c)),
        pl.BlockSpec((1, n_chain, n_kb, LANES, tk), lambda b, c, i: (b, c, 0, 0, 0)),
    ]
    acc_rows = LANES if kind == "sb" else LANES + BF16_ROWS
    scratch = [pltpu.VMEM((n_chain, acc_rows, 2 * tq), F32), pltpu.VMEM((n_chain, 1, 2 * tq), F32),
               pltpu.VMEM((2, n_chain, tk, 2 * tq), F32)]
    if kind == "diff":
        lp, sg = extras
        in_specs += [pl.BlockSpec(lp.shape, lambda b, c, i: (0, 0)),
                     pl.BlockSpec(sg.shape, lambda b, c, i: (0, 0))]
    elif kind == "fox":
        (nd,) = extras
        in_specs.append(pl.BlockSpec((1, n_chain, Tk, LANES), lambda b, c, i: (b, c, 0, 0)))
    kernel = functools.partial(_attn_kernel, kind=kind, n_chain=n_chain, tq=tq, tk=tk, q_off=q_off,
                               n_valid=n_valid, lam_init=lam_init)
    return pl.pallas_call(
        kernel,
        grid=(B, n_col // n_chain, Tq // tq),
        in_specs=in_specs,
        out_specs=pl.BlockSpec((1, tq, wide), lambda b, c, i: (b, i, c)),
        out_shape=jax.ShapeDtypeStruct((B, Tq, n_col * LANES), BF16),
        scratch_shapes=scratch,
        compiler_params=_params("arbitrary", "arbitrary", "arbitrary"),
        name="attn_" + kind,
    )(q_arr, k_arr, vt_arr, *extras)


def _key_blocks_t(v, tk):
    B, Tk, W = v.shape
    return v.reshape(B, Tk // tk, tk, W // LANES, LANES).transpose(0, 3, 1, 4, 2)


def _cache_prep_kernel(k_ref, v_ref, nk_ref, nvt_ref, ko_ref, vto_ref, *, n_heads, tk, time_minor):
    step = pl.program_id(1)
    n_past = pl.num_programs(1) - 1

    @pl.when(step < n_past)
    def _():
        for u in range(vto_ref.shape[2]):
            for c in range(ko_ref.shape[2] // LANES):
                cols = slice(c * LANES, (c + 1) * LANES)
                if time_minor:
                    ko_ref[0, u * tk:(u + 1) * tk, cols] = k_ref[0, cols, u * tk:(u + 1) * tk].T.astype(BF16)
                    vto_ref[0, c, u] = v_ref[0, cols, u * tk:(u + 1) * tk].astype(BF16)
                else:
                    head_rows = pl.ds(u * tk * n_heads + c, tk, stride=n_heads)
                    ko_ref[0, u * tk:(u + 1) * tk, cols] = k_ref[0, head_rows, :].astype(BF16)
                    vto_ref[0, c, u] = v_ref[0, head_rows, :].T.astype(BF16)

    @pl.when(step == n_past)
    def _():
        ko_ref[...] = nk_ref[...]
        vto_ref[...] = nvt_ref[...]


def _cache_prep(cache_k, cache_v, new_k, new_vt, tk):
    B, P, H, dh = cache_k.shape
    W = H * dh
    g = PREP_BLOCKS
    n_col, n_past = W // LANES, P // (g * tk)
    assert P % (g * tk) == 0 and dh in (HEAD_DIM, LANES)
    time_minor = dh != LANES
    if time_minor:
        cache_k, cache_v = [c.transpose(0, 2, 3, 1).reshape(B, W, P) for c in (cache_k, cache_v)]
        rows = pl.BlockSpec((1, W, g * tk), lambda b, j: (b, 0, jnp.minimum(j, n_past - 1)))
    else:
        cache_k, cache_v = [c.reshape(B, P * H, dh) for c in (cache_k, cache_v)]
        rows = pl.BlockSpec((1, g * tk * H, dh), lambda b, j: (b, jnp.minimum(j, n_past - 1), 0))
    return pl.pallas_call(
        functools.partial(_cache_prep_kernel, n_heads=H, tk=tk, time_minor=time_minor),
        grid=(B, n_past + 1),
        in_specs=[rows, rows,
                  pl.BlockSpec((1, g * tk, W), lambda b, j: (b, 0, 0)),
                  pl.BlockSpec((1, n_col, g, LANES, tk), lambda b, j: (b, 0, 0, 0, 0))],
        out_specs=[pl.BlockSpec((1, g * tk, W), lambda b, j: (b, j, 0)),
                   pl.BlockSpec((1, n_col, g, LANES, tk), lambda b, j: (b, 0, j, 0, 0))],
        out_shape=[jax.ShapeDtypeStruct((B, P + g * tk, W), BF16),
                   jax.ShapeDtypeStruct((B, n_col, P // tk + g, LANES, tk), BF16)],
        compiler_params=_params("arbitrary", "arbitrary"),
        name="cache_prep",
    )(cache_k, cache_v, new_k, new_vt)


def _cumsum_kernel(x_ref, o_ref, *, blk):
    T, H = x_ref.shape[1:]
    n_pair = H // 2
    r = lax.broadcasted_iota(jnp.int32, (blk, blk), 0)
    c = lax.broadcasted_iota(jnp.int32, (blk, blk), 1)
    upto = jnp.where(c <= r, 1.0, 0.0).astype(BF16)
    src = lax.broadcasted_iota(jnp.int32, (3 * H, n_pair * LANES), 0)
    col = lax.broadcasted_iota(jnp.int32, (3 * H, n_pair * LANES), 1)
    head, piece = jnp.bitwise_and(src, H - 1), jnp.right_shift(src, H.bit_length() - 1)
    here = jnp.right_shift(col, 7) == jnp.right_shift(head, 1)
    lane = jnp.bitwise_and(col, LANES - 1) - 3 * jnp.bitwise_and(head, 1)
    place = jnp.where(jnp.logical_and(here, lane == piece), 1.0, 0.0).astype(BF16)

    def chunk(j, carry):
        rows = pl.ds(pl.multiple_of(j * blk, blk), blk)
        hi, mid, lo = _split3(x_ref[0, rows, :])
        total = _dot(upto, hi) + _dot(upto, mid) + _dot(upto, lo) + carry
        pieces = jnp.concatenate([p.astype(F32) for p in _split3(-total)], axis=1).astype(BF16)
        lanes = _dot(pieces, place)
        for pair in range(n_pair):
            o_ref[0, pair, rows, :] = lanes[:, pair * LANES:(pair + 1) * LANES].astype(BF16)
        return total[blk - 1:blk, :]

    lax.fori_loop(0, T // blk, chunk, jnp.zeros((1, H), F32))


def _neg_cumsum(lf, blk):
    B, T, H = lf.shape
    return pl.pallas_call(
        functools.partial(_cumsum_kernel, blk=blk),
        grid=(B,),
        in_specs=[pl.BlockSpec((1, T, H), lambda b: (b, 0, 0))],
        out_specs=pl.BlockSpec((1, H // 2, T, LANES), lambda b: (b, 0, 0, 0)),
        out_shape=jax.ShapeDtypeStruct((B, H // 2, T, LANES), BF16),
        compiler_params=_params("arbitrary"),
        name="neg_cumsum",
    )(lf)


def _outproj_kernel(*refs, nseg):
    a_refs = refs[:nseg]
    w_ref, x_ref, g_ref, o_ref = refs[nseg:]
    bb, tt, d = x_ref.shape
    acc = None
    off = 0
    for a_ref in a_refs:
        width = a_ref.shape[2]
        part = _dot(a_ref[...].reshape(bb * tt, width), w_ref[off:off + width, :])
        acc = part if acc is None else acc + part
        off += width
    o_ref[...] = x_ref[...] + g_ref[...] * acc.reshape(bb, tt, d)


def _outproj(mixed, w_bf, x, gate, bb, tt):
    B, T, _ = x.shape
    x_spec, _, m_spec = _row_specs(bb, tt)
    return pl.pallas_call(
        functools.partial(_outproj_kernel, nseg=len(mixed)),
        grid=(B // bb, T // tt),
        in_specs=[_out_spec(bb, tt, a.shape[2]) for a in mixed]
        + [pl.BlockSpec(w_bf.shape, lambda i, j: (0, 0)), x_spec, m_spec],
        out_specs=x_spec,
        out_shape=jax.ShapeDtypeStruct(x.shape, F32),
        compiler_params=_params("arbitrary", "arbitrary"),
        name="outproj",
    )(*mixed, w_bf, x, gate)


def _route_t(h, wr_t, br):
    tm = h.shape[0]
    h_hi = h.astype(BF16)
    h_lo = (h - h_hi.astype(F32)).astype(BF16)
    w_pieces = jnp.concatenate(_split3(wr_t), axis=0)
    parts = _dot_nt(w_pieces, h_hi) + _dot_nt(w_pieces, h_lo)
    logits = parts[0:N_EXPERTS] + parts[N_EXPERTS:2 * N_EXPERTS] + parts[2 * N_EXPERTS:]
    s = jax.nn.sigmoid(logits)
    sel = s + br
    row = lax.broadcasted_iota(jnp.int32, (N_EXPERTS, tm), 0).astype(F32)
    neg = -jnp.inf

    def first_argmax(vals):
        top = jnp.max(vals, axis=0, keepdims=True)
        idx = jnp.min(jnp.where(vals == top, row, float(N_EXPERTS)), axis=0, keepdims=True)
        return top, idx

    best = None
    for g in range(N_GROUPS):
        in_g = (row >= g * EXPERTS_PER_GROUP) & (row < (g + 1) * EXPERTS_PER_GROUP)
        top1, i1 = first_argmax(jnp.where(in_g, sel, neg))
        top2, i2 = first_argmax(jnp.where(in_g & (row != i1), sel, neg))
        cand = (top1 + top2, i1, i2, jnp.full((1, tm), float(g), F32))
        if best is None:
            best = cand
        else:
            better = cand[0] > best[0]
            best = tuple(jnp.where(better, new, old) for new, old in zip(cand, best))
    _, i1, i2, group = best
    picked = jnp.where((row == i1) | (row == i2), s, 0.0)
    return picked / jnp.sum(picked, axis=0, keepdims=True), group


def _moe_kernel(*refs, final):
    if final:
        (x_ref, g_ref, sc_ref, sh_ref, g2_ref, wrt_ref, br_ref, wg_ref, wu_ref, wd_ref, gf_ref,
         o_ref, h_ref, xg_ref, yg_ref, gg_ref, gate_ref, pos_ref, before_ref, n_ref) = refs
    else:
        (x_ref, g_ref, sc_ref, sh_ref, g2_ref, wrt_ref, br_ref, wg_ref, wu_ref, wd_ref,
         o_ref, h_ref, xg_ref, yg_ref, gg_ref, gate_ref, pos_ref, before_ref, n_ref) = refs
    bb, tt, d = x_ref.shape
    tm = bb * tt
    e = pl.program_id(2)
    group = e // EXPERTS_PER_GROUP

    @pl.when((pl.program_id(0) == 0) & (pl.program_id(1) == 0) & (e == 0))
    def _():
        def fill(j, carry):
            rows = pl.ds(pl.multiple_of(j * MOE_CHUNK, MOE_CHUNK), MOE_CHUNK)
            r = lax.broadcasted_iota(jnp.int32, (MOE_CHUNK, tm), 0) + j * MOE_CHUNK
            c = lax.broadcasted_iota(jnp.int32, (MOE_CHUNK, tm), 1)
            before_ref[rows, :] = jnp.where(r < c, 1.0, 0.0).astype(BF16)
            return carry
        lax.fori_loop(0, tm // MOE_CHUNK, fill, 0)

    @pl.when(e == 0)
    def _():
        h = _norm_mod(x_ref[...], g_ref[...], sc_ref[...], sh_ref[...])
        h_ref[...] = h.astype(BF16)
        gates, grp = _route_t(h, wrt_ref[...], br_ref[...])
        gates = jnp.concatenate([gates, jnp.zeros((LANES - N_EXPERTS, tm), F32)], axis=0)
        hi = gates.astype(BF16)
        gate_ref[0] = hi
        gate_ref[1] = (gates - hi.astype(F32)).astype(BF16)
        grow = lax.broadcasted_iota(jnp.int32, pos_ref.shape, 0).astype(F32)
        member = jnp.where(grow == grp, 1.0, 0.0)
        rank = _dot(member.astype(BF16), before_ref[...])
        pos_ref[...] = jnp.where(member > 0.0, rank, -1.0)
        for g in range(N_GROUPS):
            count = jnp.sum(member[g:g + 1, :]).astype(jnp.int32)
            n_ref[g] = (count + MOE_CHUNK - 1) // MOE_CHUNK
        o_ref[...] = jnp.zeros_like(o_ref)

    n_chunks = n_ref[group]

    def chunk_rows(j):
        return pl.ds(pl.multiple_of(j * MOE_CHUNK, MOE_CHUNK), MOE_CHUNK)

    def one_hot(j, n_rows=MOE_CHUNK):
        want = (lax.broadcasted_iota(jnp.int32, (n_rows, tm), 0) + j * n_rows).astype(F32)
        return jnp.where(pos_ref[pl.ds(group, 1), :] == want, 1.0, 0.0).astype(BF16)

    @pl.when(e % EXPERTS_PER_GROUP == 0)
    def _():
        def gather(j, carry):
            rows = pl.ds(pl.multiple_of(j * 2 * MOE_CHUNK, 2 * MOE_CHUNK), 2 * MOE_CHUNK)
            pick = one_hot(j, 2 * MOE_CHUNK)
            xg_ref[rows, :] = _dot(pick, h_ref[...]).astype(BF16)
            gg_ref[rows, :] = _dot_nt(pick, gate_ref[0]) + _dot_nt(pick, gate_ref[1])
            yg_ref[rows, :] = jnp.zeros((2 * MOE_CHUNK, d), F32)
            return carry
        lax.fori_loop(0, (n_chunks + 1) // 2, gather, 0)

    wg, wu, wd = wg_ref[0, 0], wu_ref[0, 0], wd_ref[0, 0]
    lane = lax.broadcasted_iota(jnp.int32, (MOE_CHUNK, LANES), 1)

    def expert(chunks):
        rows = [chunk_rows(j) for j in chunks]
        xs = [xg_ref[r, :] for r in rows]
        gts = [_dot(x, wg) for x in xs]
        ups = [_dot(x, wu) for x in xs]
        acts = []
        for r, gt, up in zip(rows, gts, ups):
            ge = jnp.sum(jnp.where(lane == e, gg_ref[r, :], 0.0), axis=1, keepdims=True)
            acts.append(((gt * jax.nn.sigmoid(gt)) * up * ge).astype(BF16))
        for r, a in zip(rows, acts):
            yg_ref[r, :] += _dot(a, wd)

    def expert_pair(j, carry):
        expert((2 * j, 2 * j + 1))
        return carry

    lax.fori_loop(0, n_chunks // 2, expert_pair, 0)

    @pl.when(n_chunks % 2 == 1)
    def _():
        expert((n_chunks - 1,))

    @pl.when(e % EXPERTS_PER_GROUP == EXPERTS_PER_GROUP - 1)
    def _():
        def scatter(j, carry):
            rows = pl.ds(pl.multiple_of(j * 2 * MOE_CHUNK, 2 * MOE_CHUNK), 2 * MOE_CHUNK)
            back = lax.dot_general(one_hot(j, 2 * MOE_CHUNK), yg_ref[rows, :].astype(BF16),
                                   (((0,), (0,)), ((), ())), preferred_element_type=F32)
            o_ref[...] += back.reshape(bb, tt, d)
            return carry
        lax.fori_loop(0, (n_chunks + 1) // 2, scatter, 0)

    @pl.when(e == N_EXPERTS - 1)
    def _():
        y = x_ref[...] + g2_ref[...] * o_ref[...]
        if final:
            ms = jnp.mean(y * y, axis=-1, keepdims=True)
            y = y * lax.rsqrt(ms + EPS) * gf_ref[...]
        o_ref[...] = y


def _moe(x, g, sc, sh, g2, w_router, b_router, w_gate, w_up, w_down, layer, g_final, bb, tt):
    B, T, _ = x.shape
    final = g_final is not None
    x_spec = pl.BlockSpec((bb, tt, D_MODEL), lambda i, j, e: (i, j, 0))
    g_spec = pl.BlockSpec((1, D_MODEL), lambda i, j, e: (0, 0))
    m_spec = pl.BlockSpec((bb, 1, D_MODEL), lambda i, j, e: (i, 0, 0))
    in_specs = [x_spec, g_spec, m_spec, m_spec, m_spec,
                pl.BlockSpec((N_EXPERTS, D_MODEL), lambda i, j, e: (0, 0)),
                pl.BlockSpec((N_EXPERTS, 1), lambda i, j, e: (0, 0)),
                pl.BlockSpec((1, 1, D_MODEL, D_EXPERT), lambda i, j, e: (layer, e, 0, 0)),
                pl.BlockSpec((1, 1, D_MODEL, D_EXPERT), lambda i, j, e: (layer, e, 0, 0)),
                pl.BlockSpec((1, 1, D_EXPERT, D_MODEL), lambda i, j, e: (layer, e, 0, 0))]
    args = [x, g, sc, sh, g2, w_router.T, b_router.reshape(N_EXPERTS, 1), w_gate, w_up, w_down]
    if final:
        in_specs.append(g_spec)
        args.append(g_final)
    tm = bb * tt
    assert tm % (2 * MOE_CHUNK) == 0
    packed = tm
    scratch = [pltpu.VMEM((tm, D_MODEL), BF16),
               pltpu.VMEM((packed, D_MODEL), BF16),
               pltpu.VMEM((packed, D_MODEL), F32),
               pltpu.VMEM((packed, LANES), F32),
               pltpu.VMEM((2, LANES, tm), BF16),
               pltpu.VMEM((2 * N_GROUPS, tm), F32),
               pltpu.VMEM((tm, tm), BF16),
               pltpu.SMEM((N_GROUPS,), jnp.int32)]
    return pl.pallas_call(
        functools.partial(_moe_kernel, final=final),
        grid=(B // bb, T // tt, N_EXPERTS),
        in_specs=in_specs,
        out_specs=x_spec,
        out_shape=jax.ShapeDtypeStruct(x.shape, F32),
        scratch_shapes=scratch,
        compiler_params=_params("arbitrary", "arbitrary", "arbitrary"),
        name="moe",
    )(*args)


def _rope_tables(pos):
    half = ROPE_DIM // 2
    inv = ROPE_THETA ** (-jnp.arange(half, dtype=F32) / half)
    ang = pos.astype(F32)[:, None] * inv[None, :]
    cos, sin = jnp.cos(ang), jnp.sin(ang)
    ones = jnp.ones((pos.shape[0], HEAD_DIM - ROPE_DIM), F32)
    zeros = jnp.zeros_like(ones)
    zh = jnp.zeros_like(sin)
    c64 = jnp.concatenate([cos, cos, ones], axis=1)
    s1_64 = jnp.concatenate([-sin, zh, zeros], axis=1)
    s2_64 = jnp.concatenate([zh, sin, zeros], axis=1)
    return tuple(jnp.concatenate([t, t], axis=1) for t in (c64, s1_64, s2_64))


def _pad_time(a, total):
    if total == a.shape[1]:
        return a
    return jnp.pad(a, ((0, 0), (0, total - a.shape[1])) + ((0, 0),) * (a.ndim - 2))


def _trunk(x, mod, past, wts, cfg):
    B, T, _ = x.shape
    P = 0 if past is None else past["diff_k"].shape[2]
    bb, tt_proj, tt_out, tt_moe, tq, tk = cfg
    wide = past is None and bb == 1 and tt_proj == tk
    n_valid = P + T
    if past is None:
        Tk = -(-n_valid // tk) * tk
    else:
        assert T <= tk
        Tk = P + PREP_BLOCKS * tk
    Tq = -(-T // tq) * tq
    tables = _rope_tables(P + jnp.arange(T))
    if bb > 1:
        tables = tuple(jnp.tile(t, (bb, 1)) for t in tables)
    attn = functools.partial(_attention, n_chain=N_CHAIN, tq=tq, tk=tk, q_off=P, n_valid=n_valid)
    leaves_even, leaves_odd = [], []

    def with_cache(qkv, k_cols, v_cols, cache_k, cache_v):
        new_k = _pad_time(qkv[:, :, k_cols[0]:k_cols[1]], Tk - P)
        new_vt = _key_blocks_t(_pad_time(qkv[:, :, v_cols[0]:v_cols[1]], Tk - P), tk)
        return _cache_prep(cache_k, cache_v, new_k, new_vt, tk)

    def heads_last(leaf_t, n_heads):
        return leaf_t.reshape(B, n_heads, -1, T).transpose(0, 3, 1, 2)

    for l in range(DEPTH):
        sh1, sc1, g1, sh2, sc2, g2 = [mod[l, :, k * D_MODEL:(k + 1) * D_MODEL][:, None, :] for k in range(6)]
        i = l // 2
        g_mix = wts["norm_mix"][l][None, :]
        if l % 2 == 0:
            lam_init = 0.8 - 0.6 * math.exp(-0.3 * l)
            lp = jnp.stack([wts["diff_lq1"][i], wts["diff_lk1"][i], wts["diff_lq2"][i], wts["diff_lk2"][i]])
            diff_extras = (lp, wts["diff_subln"][i][:, None])
            if wide:
                qk, ak, av, bkt, bvt, dvt, svt = _proj_even_wide(x, g_mix, sc1, sh1, wts["w_in_even"][i], tables, tt_proj)
                leaves_even.append((ak.reshape(B, T, N_DIFF_HEADS, LANES), av.reshape(B, T, N_DIFF_HEADS, LANES),
                                    heads_last(bkt, N_SB_HEADS), heads_last(bvt, N_SB_HEADS)))
                q_arr, q_cols = qk, (0, 2 * SEG // LANES)
                dk, dk_col, sk, sk_col = qk, SEG // LANES, qk, 3 * SEG // LANES
            else:
                qkv, ak, av, bk, bv = _proj_even(x, g_mix, sc1, sh1, wts["w_in_even"][i], tables, bb, tt_proj)
                leaves_even.append((ak.reshape(B, T, N_DIFF_HEADS, LANES), av.reshape(B, T, N_DIFF_HEADS, LANES),
                                    bk.reshape(B, T, N_SB_HEADS, HEAD_DIM), bv.reshape(B, T, N_SB_HEADS, HEAD_DIM)))
                q_arr, q_cols = _pad_time(qkv, Tq), (0, 3 * SEG // LANES)
                dk, dvt = with_cache(qkv, (SEG, 2 * SEG), (2 * SEG, 3 * SEG), past["diff_k"][i], past["diff_v"][i])
                sk, svt = with_cache(qkv, (4 * SEG, 5 * SEG), (5 * SEG, 6 * SEG), past["sb_k"][i], past["sb_v"][i])
                dk_col = sk_col = 0
            a_out = attn("diff", q_arr, q_cols[0], dk, dk_col, dvt, N_DIFF_HEADS, extras=diff_extras, lam_init=lam_init)
            b_out = attn("sb", q_arr, q_cols[1], sk, sk_col, svt, N_SB_HEADS // 2)
            x = _outproj([a_out[:, :T], b_out[:, :T]], wts["w_out_even"][i], x, g1, bb, tt_out)
        else:
            if wide:
                qk, kt, vt, fvt, lf = _proj_odd_wide(x, g_mix, sc1, sh1, wts["w_in_odd"][i], wts["w_f_odd"][i],
                                                     wts["b_forget"][i], tt_proj)
                leaves_odd.append((heads_last(kt, N_FOX_HEADS), heads_last(vt, N_FOX_HEADS), lf))
                q_arr, fk, fk_col, lf_all = qk, qk, 2 * SEG // LANES, lf
            else:
                qkv, k, v, lf = _proj_odd(x, g_mix, sc1, sh1, wts["w_in_odd"][i], wts["w_f_odd"][i],
                                          wts["b_forget"][i], bb, tt_proj)
                leaves_odd.append((k.reshape(B, T, N_FOX_HEADS, HEAD_DIM), v.reshape(B, T, N_FOX_HEADS, HEAD_DIM), lf))
                q_arr = _pad_time(qkv, Tq)
                fk, fvt = with_cache(qkv, (2 * SEG, 4 * SEG), (4 * SEG, 6 * SEG), past["fox_k"][i], past["fox_v"][i])
                fk_col = 0
                lf_all = jnp.concatenate([past["fox_logf"][i], lf], axis=1)
            nd = _neg_cumsum(_pad_time(lf_all, Tk), tk)
            o = attn("fox", q_arr, 0, fk, fk_col, fvt, N_FOX_HEADS // 2, extras=(nd,))
            x = _outproj([o[:, :T]], wts["w_out_odd"][i], x, g1, bb, tt_out)
        g_final = wts["norm_final"][None, :] if l == DEPTH - 1 else None
        x = _moe(x, wts["norm_ffn"][l][None, :], sc2, sh2, g2, wts["w_router"], wts["b_router"],
                 wts["w_gate"], wts["w_up"], wts["w_down"], l, g_final, bb, tt_moe)
    return x, leaves_even, leaves_odd


def kernel(x_prompt, x_sample, c_prompt, c_sample, cache_diff_k, cache_diff_v, cache_sb_k, cache_sb_v, cache_fox_k, cache_fox_v, cache_fox_logf, w_ada, b_ada, norm_mix, norm_ffn, w_in_even, w_out_even, diff_lq1, diff_lk1, diff_lq2, diff_lk2, diff_subln, w_in_odd, b_forget, w_out_odd, w_router, b_router, w_gate, w_up, w_down, norm_final):
    Bp, Tp, _ = x_prompt.shape
    Bs, Ts, _ = x_sample.shape
    wts = {
        "norm_mix": norm_mix, "norm_ffn": norm_ffn, "norm_final": norm_final,
        "w_in_even": w_in_even.astype(BF16), "w_out_even": w_out_even.astype(BF16),
        "w_in_odd": w_in_odd[:, :, :3 * FOX_W].astype(BF16), "w_f_odd": w_in_odd[:, :, 3 * FOX_W:].astype(BF16),
        "w_out_odd": w_out_odd.astype(BF16), "b_forget": b_forget,
        "diff_lq1": diff_lq1, "diff_lk1": diff_lk1, "diff_lq2": diff_lq2, "diff_lk2": diff_lk2,
        "diff_subln": diff_subln, "w_router": w_router, "b_router": b_router,
        "w_gate": w_gate.astype(BF16), "w_up": w_up.astype(BF16), "w_down": w_down.astype(BF16),
    }
    past = {"diff_k": cache_diff_k, "diff_v": cache_diff_v, "sb_k": cache_sb_k, "sb_v": cache_sb_v,
            "fox_k": cache_fox_k, "fox_v": cache_fox_v, "fox_logf": cache_fox_logf}

    rows = Bp + Bs
    rows_pad = -(-rows // 16) * 16
    c_all = jnp.pad(jnp.concatenate([c_prompt, c_sample], axis=0), ((0, rows_pad - rows), (0, 0)))
    mod = _ada_mod(c_all, w_ada, b_ada)

    cfg_prompt = (1, 256, 512, 1024, 256, 256)
    cfg_sample = (Bs, Ts, Ts, Ts, 64, 256)
    y_p, even_p, odd_p = _trunk(x_prompt, mod[:, :Bp], None, wts, cfg_prompt)
    y_s, even_s, odd_s = _trunk(x_sample, mod[:, Bp:rows], past, wts, cfg_sample)

    def layers(rows):
        return rows[0][None] if len(rows) == 1 else jnp.stack(rows, axis=0)

    def pack(even, odd):
        return tuple(layers(r) for r in zip(*even)) + tuple(layers(r) for r in zip(*odd))

    return (y_p, y_s) + pack(even_p, odd_p) + pack(even_s, odd_s)
```

```python
import functools
import math

import jax
import jax.numpy as jnp
from jax import lax
from jax.experimental import pallas as pl
from jax.experimental.pallas import tpu as pltpu

D_MODEL = 1024
DEPTH = 2
CHUNK = 64
HEAD_DIM = 64
N_DIFF_HEADS = D_MODEL // 256
N_SB_HEADS = D_MODEL // 128
N_FOX_HEADS = D_MODEL // HEAD_DIM
DIFF_W = N_DIFF_HEADS * 2 * HEAD_DIM
SB_W = N_SB_HEADS * HEAD_DIM
FOX_W = N_FOX_HEADS * HEAD_DIM
ROPE_DIM = HEAD_DIM // 4
ROPE_THETA = 500000.0
N_EXPERTS = 16
N_GROUPS = 4
EXPERTS_PER_GROUP = N_EXPERTS // N_GROUPS
D_EXPERT = D_MODEL // 2
EPS = 1e-6

LANES = 128
BF16_ROWS = 16
SEG = 512
QK_SCALE = HEAD_DIM ** -0.5
MASKED = -1e30
N_CHAIN = 4
SB_DEAD = -110.0
MOE_CHUNK = 128
EXPERTS_PER_STEP = 2
PREP_BLOCKS = 2
VMEM_LIMIT = 48 * 1024 * 1024

F32 = jnp.float32
BF16 = jnp.bfloat16


def _params(*sem):
    return pltpu.CompilerParams(dimension_semantics=sem, vmem_limit_bytes=VMEM_LIMIT)


def _dot(a, b):
    return jnp.dot(a, b, preferred_element_type=F32)


def _dot_nt(a, b):
    return lax.dot_general(a, b, (((1,), (1,)), ((), ())), preferred_element_type=F32)


def _norm_mod(x, g, sc, sh):
    bb, tt, d = x.shape
    ms = jnp.mean(x * x, axis=-1, keepdims=True)
    h = x * lax.rsqrt(ms + EPS) * g
    h = h * (1.0 + sc) + sh
    return h.reshape(bb * tt, d)


def _log_sigmoid(x):
    return jnp.minimum(x, 0.0) - jnp.log(1.0 + jnp.exp(-jnp.abs(x)))


def _split3(x):
    hi = x.astype(BF16)
    r1 = x - hi.astype(F32)
    mid = r1.astype(BF16)
    lo = (r1 - mid.astype(F32)).astype(BF16)
    return hi, mid, lo


def _ada_kernel(c_ref, w_ref, b_ref, o_ref):
    c = c_ref[...]
    a = (c * jax.nn.sigmoid(c)).astype(BF16)
    o_ref[0] = _dot(a, w_ref[0].astype(BF16)) + b_ref[0]


def _ada_mod(c_all, w_ada, b_ada):
    rows = c_all.shape[0]
    tn = 1536
    width = 6 * D_MODEL
    return pl.pallas_call(
        _ada_kernel,
        grid=(DEPTH, width // tn),
        in_specs=[
            pl.BlockSpec((rows, D_MODEL), lambda l, j: (0, 0)),
            pl.BlockSpec((1, D_MODEL, tn), lambda l, j: (l, 0, j)),
            pl.BlockSpec((1, 1, tn), lambda l, j: (l, 0, j)),
        ],
        out_specs=pl.BlockSpec((1, rows, tn), lambda l, j: (l, 0, j)),
        out_shape=jax.ShapeDtypeStruct((DEPTH, rows, width), F32),
        compiler_params=_params("arbitrary", "arbitrary"),
        name="ada_mod",
    )(c_all, w_ada, b_ada.reshape(DEPTH, 1, width))


def _rope(p, cos, s1, s2):
    outs = []
    for c in range(SEG // LANES):
        pc = p[:, c * LANES:(c + 1) * LANES]
        up = pltpu.roll(pc, LANES - ROPE_DIM // 2, 1)
        down = pltpu.roll(pc, ROPE_DIM // 2, 1)
        outs.append(pc * cos + up * s1 + down * s2)
    return jnp.concatenate(outs, axis=1)


def _proj_even_kernel(x_ref, g_ref, sc_ref, sh_ref, w_ref, cos_ref, s1_ref, s2_ref,
                      qkv_ref, ak_ref, av_ref, bk_ref, bv_ref):
    bb, tt, _ = x_ref.shape
    h = _norm_mod(x_ref[...], g_ref[...], sc_ref[...], sh_ref[...]).astype(BF16)
    cos, s1, s2 = cos_ref[...], s1_ref[...], s2_ref[...]

    def seg(i):
        return _dot(h, w_ref[:, i * SEG:(i + 1) * SEG])

    def put(i, val, leaf_ref):
        val3 = val.reshape(bb, tt, SEG)
        qkv_ref[:, :, i * SEG:(i + 1) * SEG] = val3.astype(BF16)
        if leaf_ref is not None:
            leaf_ref[...] = val3

    put(0, _rope(seg(0), cos, s1, s2) * QK_SCALE, None)
    put(1, _rope(seg(1), cos, s1, s2), ak_ref)
    put(2, seg(2), av_ref)
    put(3, seg(3) * QK_SCALE, None)
    put(4, seg(4), bk_ref)
    put(5, seg(5), bv_ref)


def _rows_by_head(ref, val, n_heads):
    tt = val.shape[0]
    for hd in range(n_heads):
        ref[0, pl.ds(hd, tt, stride=n_heads), :] = val[:, hd * LANES:(hd + 1) * LANES]


def _key_blocks(ref, first, val_t):
    n = val_t.shape[0] // LANES
    ref[0, first:first + n, 0] = val_t.reshape(n, LANES, val_t.shape[1]).astype(BF16)


def _proj_even_wide_kernel(x_ref, g_ref, sc_ref, sh_ref, w_ref, cos_ref, s1_ref, s2_ref,
                           qk_ref, ak_ref, av_ref, bkt_ref, bvt_ref, dvt_ref, svt_ref):
    h = _norm_mod(x_ref[...], g_ref[...], sc_ref[...], sh_ref[...]).astype(BF16)
    cos, s1, s2 = cos_ref[...], s1_ref[...], s2_ref[...]

    def seg(i):
        return _dot(h, w_ref[:, i * SEG:(i + 1) * SEG])

    qk_ref[0, :, 0:SEG] = (_rope(seg(0), cos, s1, s2) * QK_SCALE).astype(BF16)
    ak = _rope(seg(1), cos, s1, s2)
    qk_ref[0, :, SEG:2 * SEG] = ak.astype(BF16)
    _rows_by_head(ak_ref, ak, N_DIFF_HEADS)
    av = seg(2)
    _rows_by_head(av_ref, av, N_DIFF_HEADS)
    _key_blocks(dvt_ref, 0, av.T)
    qk_ref[0, :, 2 * SEG:3 * SEG] = (seg(3) * QK_SCALE).astype(BF16)
    bk = seg(4)
    qk_ref[0, :, 3 * SEG:4 * SEG] = bk.astype(BF16)
    bkt_ref[0] = bk.T
    bvt = seg(5).T
    bvt_ref[0] = bvt
    _key_blocks(svt_ref, 0, bvt)


def _proj_odd_wide_kernel(x_ref, g_ref, sc_ref, sh_ref, w_ref, wf_ref, bf_ref,
                          qk_ref, kt_ref, vt_ref, fvt_ref, lf_ref):
    h = _norm_mod(x_ref[...], g_ref[...], sc_ref[...], sh_ref[...]).astype(BF16)

    def seg(i):
        return _dot(h, w_ref[:, i * SEG:(i + 1) * SEG])

    for i in range(2):
        qk_ref[0, :, i * SEG:(i + 1) * SEG] = (seg(i) * QK_SCALE).astype(BF16)
    for i in range(2):
        k = seg(2 + i)
        qk_ref[0, :, (2 + i) * SEG:(3 + i) * SEG] = k.astype(BF16)
        kt_ref[0, i * SEG:(i + 1) * SEG, :] = k.T
        v_t = seg(4 + i).T
        vt_ref[0, i * SEG:(i + 1) * SEG, :] = v_t
        _key_blocks(fvt_ref, i * (SEG // LANES), v_t)
    fl = _dot(h, wf_ref[...]) + bf_ref[...]
    lf_ref[0] = _log_sigmoid(fl)


def _proj_odd_kernel(x_ref, g_ref, sc_ref, sh_ref, w_ref, wf_ref, bf_ref,
                     qkv_ref, k_ref, v_ref, lf_ref):
    bb, tt, _ = x_ref.shape
    h = _norm_mod(x_ref[...], g_ref[...], sc_ref[...], sh_ref[...]).astype(BF16)

    def seg(i):
        return _dot(h, w_ref[:, i * SEG:(i + 1) * SEG]).reshape(bb, tt, SEG)

    for i in range(2):
        qkv_ref[:, :, i * SEG:(i + 1) * SEG] = (seg(i) * QK_SCALE).astype(BF16)
    for i in range(2, 6):
        val = seg(i)
        qkv_ref[:, :, i * SEG:(i + 1) * SEG] = val.astype(BF16)
        leaf_ref = k_ref if i < 4 else v_ref
        leaf_ref[:, :, (i % 2) * SEG:(i % 2 + 1) * SEG] = val
    fl = _dot(h, wf_ref[...]) + bf_ref[...]
    lf_ref[...] = _log_sigmoid(fl).reshape(bb, tt, N_FOX_HEADS)


def _row_specs(bb, tt):
    x_spec = pl.BlockSpec((bb, tt, D_MODEL), lambda i, j: (i, j, 0))
    g_spec = pl.BlockSpec((1, D_MODEL), lambda i, j: (0, 0))
    m_spec = pl.BlockSpec((bb, 1, D_MODEL), lambda i, j: (i, 0, 0))
    return x_spec, g_spec, m_spec


def _out_spec(bb, tt, width):
    return pl.BlockSpec((bb, tt, width), lambda i, j: (i, j, 0))


def _proj_even(x, g, sc, sh, w_bf, tables, bb, tt):
    B, T, _ = x.shape
    x_spec, g_spec, m_spec = _row_specs(bb, tt)
    tm = bb * tt
    t_spec = pl.BlockSpec((tm, LANES), (lambda i, j: (j, 0)) if bb == 1 else (lambda i, j: (0, 0)))
    leaf = jax.ShapeDtypeStruct((B, T, SEG), F32)
    return pl.pallas_call(
        _proj_even_kernel,
        grid=(B // bb, T // tt),
        in_specs=[x_spec, g_spec, m_spec, m_spec,
                  pl.BlockSpec(w_bf.shape, lambda i, j: (0, 0)), t_spec, t_spec, t_spec],
        out_specs=[_out_spec(bb, tt, 6 * SEG)] + [_out_spec(bb, tt, SEG)] * 4,
        out_shape=[jax.ShapeDtypeStruct((B, T, 6 * SEG), BF16), leaf, leaf, leaf, leaf],
        compiler_params=_params("arbitrary", "arbitrary"),
        name="proj_even",
    )(x, g, sc, sh, w_bf, *tables)


def _wide_specs(B, T, tt):
    def rows(width, dtype):
        return pl.BlockSpec((1, tt, width), lambda i, j: (i, j, 0)), jax.ShapeDtypeStruct((B, T, width), dtype)

    def by_head(n_heads):
        return (pl.BlockSpec((1, tt * n_heads, LANES), lambda i, j: (i, j, 0)),
                jax.ShapeDtypeStruct((B, T * n_heads, LANES), F32))

    def transposed(width):
        return pl.BlockSpec((1, width, tt), lambda i, j: (i, 0, j)), jax.ShapeDtypeStruct((B, width, T), F32)

    def key_blocks(n_col):
        return (pl.BlockSpec((1, n_col, 1, LANES, tt), lambda i, j: (i, 0, j, 0, 0)),
                jax.ShapeDtypeStruct((B, n_col, T // tt, LANES, tt), BF16))

    return rows, by_head, transposed, key_blocks


def _proj_even_wide(x, g, sc, sh, w_bf, tables, tt):
    B, T, _ = x.shape
    x_spec, g_spec, m_spec = _row_specs(1, tt)
    t_spec = pl.BlockSpec((tt, LANES), lambda i, j: (j, 0))
    rows, by_head, transposed, key_blocks = _wide_specs(B, T, tt)
    outs = [rows(4 * SEG, BF16), by_head(N_DIFF_HEADS), by_head(N_DIFF_HEADS), transposed(SEG), transposed(SEG),
            key_blocks(SEG // LANES), key_blocks(SEG // LANES)]
    return pl.pallas_call(
        _proj_even_wide_kernel,
        grid=(B, T // tt),
        in_specs=[x_spec, g_spec, m_spec, m_spec,
                  pl.BlockSpec(w_bf.shape, lambda i, j: (0, 0)), t_spec, t_spec, t_spec],
        out_specs=[o[0] for o in outs],
        out_shape=[o[1] for o in outs],
        compiler_params=_params("arbitrary", "arbitrary"),
        name="proj_even_wide",
    )(x, g, sc, sh, w_bf, *tables)


def _proj_odd_wide(x, g, sc, sh, w_bf, wf_bf, b_f, tt):
    B, T, _ = x.shape
    x_spec, g_spec, m_spec = _row_specs(1, tt)
    rows, by_head, transposed, key_blocks = _wide_specs(B, T, tt)
    outs = [rows(4 * SEG, BF16), transposed(FOX_W), transposed(FOX_W), key_blocks(FOX_W // LANES),
            rows(N_FOX_HEADS, F32)]
    return pl.pallas_call(
        _proj_odd_wide_kernel,
        grid=(B, T // tt),
        in_specs=[x_spec, g_spec, m_spec, m_spec,
                  pl.BlockSpec(w_bf.shape, lambda i, j: (0, 0)),
                  pl.BlockSpec(wf_bf.shape, lambda i, j: (0, 0)),
                  pl.BlockSpec((1, N_FOX_HEADS), lambda i, j: (0, 0))],
        out_specs=[o[0] for o in outs],
        out_shape=[o[1] for o in outs],
        compiler_params=_params("arbitrary", "arbitrary"),
        name="proj_odd_wide",
    )(x, g, sc, sh, w_bf, wf_bf, b_f.reshape(1, N_FOX_HEADS))


def _proj_odd(x, g, sc, sh, w_bf, wf_bf, b_f, bb, tt):
    B, T, _ = x.shape
    x_spec, g_spec, m_spec = _row_specs(bb, tt)
    leaf = jax.ShapeDtypeStruct((B, T, FOX_W), F32)
    return pl.pallas_call(
        _proj_odd_kernel,
        grid=(B // bb, T // tt),
        in_specs=[x_spec, g_spec, m_spec, m_spec,
                  pl.BlockSpec(w_bf.shape, lambda i, j: (0, 0)),
                  pl.BlockSpec(wf_bf.shape, lambda i, j: (0, 0)),
                  pl.BlockSpec((1, N_FOX_HEADS), lambda i, j: (0, 0))],
        out_specs=[_out_spec(bb, tt, 6 * SEG), _out_spec(bb, tt, FOX_W), _out_spec(bb, tt, FOX_W),
                   _out_spec(bb, tt, N_FOX_HEADS)],
        out_shape=[jax.ShapeDtypeStruct((B, T, 6 * SEG), BF16), leaf, leaf,
                   jax.ShapeDtypeStruct((B, T, N_FOX_HEADS), F32)],
        compiler_params=_params("arbitrary", "arbitrary"),
        name="proj_odd",
    )(x, g, sc, sh, w_bf, wf_bf, b_f.reshape(1, N_FOX_HEADS))


def _attn_kernel(*refs, kind, n_chain, tq, tk, q_off, n_valid, lam_init):
    if kind == "diff":
        q_ref, k_ref, vt_ref, lp_ref, sg_ref, o_ref, acc_ref, m_ref, s_ref = refs
    elif kind == "fox":
        q_ref, k_ref, vt_ref, nd_ref, o_ref, acc_ref, m_ref, s_ref = refs
    else:
        q_ref, k_ref, vt_ref, o_ref, acc_ref, m_ref, s_ref = refs
    tq2 = 2 * tq
    q0 = q_off + pl.program_id(2) * tq

    low = lax.broadcasted_iota(jnp.int32, (tq, LANES), 1) < HEAD_DIM
    if kind == "fox":
        row = lax.broadcasted_iota(jnp.int32, (tq2, LANES), 0)
        col = lax.broadcasted_iota(jnp.int32, (tq2, LANES), 1)
        first = jnp.right_shift(row, tq.bit_length() - 1) * 3
    qqs = []
    for c in range(n_chain):
        q = q_ref[0, :, c * LANES:(c + 1) * LANES]
        zero = jnp.zeros_like(q)
        qq = jnp.concatenate([jnp.where(low, q, zero), jnp.where(low, zero, q)], axis=0)
        if kind == "fox":
            pick = jnp.logical_and(first + 6 * c <= col, col < first + 6 * c + 3)
            qq = jnp.concatenate([qq, jnp.where(pick, 1.0, 0.0).astype(BF16)], axis=1)
        qqs.append(qq)

    acc_ref[...] = jnp.zeros_like(acc_ref)
    if kind == "sb":
        m_ref[...] = jnp.zeros_like(m_ref)
        tri_r = lax.broadcasted_iota(jnp.int32, (tk, tk), 0)
        tri_c = lax.broadcasted_iota(jnp.int32, (tk, tk), 1)
        after = jnp.where(tri_c > tri_r, 1.0, 0.0).astype(BF16)
        after2 = jnp.concatenate([after, after], axis=1)
    else:
        m_ref[...] = jnp.full_like(m_ref, MASKED)
        ones_rows = jnp.ones((BF16_ROWS, tk), BF16)

    qpos = q0 + jnp.bitwise_and(lax.broadcasted_iota(jnp.int32, (tk, tq2), 1), tq - 1)
    kidx = lax.broadcasted_iota(jnp.int32, (tk, tq2), 0)

    chains = range(n_chain)

    def qk(kb, slot):
        start = pl.multiple_of(kb * tk, tk)
        for c in chains:
            k = k_ref[0, pl.ds(start, tk), c * LANES:(c + 1) * LANES]
            if kind == "fox":
                k = jnp.concatenate([k, nd_ref[0, 0, pl.ds(start, tk), :]], axis=1)
            s_ref[slot, c] = _dot_nt(k, qqs[c])

    def consume(kb, slot, masked):
        scores = [s_ref[slot, c] for c in chains]
        if masked:
            kpos = kb * tk + kidx
            if kind == "diff":
                vis = jnp.right_shift(kpos, 6) <= jnp.right_shift(qpos, 6)
            elif kind == "fox":
                vis = kpos <= qpos
            else:
                vis = kpos < qpos
            vis = jnp.logical_and(vis, kpos < n_valid)
        vts = [vt_ref[0, c, kb] for c in chains]
        if kind == "sb":
            log_betas, laters = [], []
            for c in chains:
                s = scores[c]
                soft = jnp.log(1.0 + jnp.exp(-jnp.abs(s)))
                log_beta = jnp.minimum(s, 0.0) - soft
                log_keep = log_beta - s
                if masked:
                    log_keep = jnp.where(vis, log_keep, 0.0)
                hi = log_keep.astype(BF16)
                lo = (log_keep - hi.astype(F32)).astype(BF16)
                later = _dot(after2, jnp.concatenate([hi, lo], axis=0))
                laters.append(later + m_ref[c])
                m_ref[c] += later[0:1, :] + log_keep[0:1, :]
                log_betas.append(log_beta)
            weights = []
            for c in chains:
                w = jnp.exp(log_betas[c] + laters[c])
                if masked:
                    w = jnp.where(vis, w, 0.0)
                weights.append(w.astype(BF16))
            for c in chains:
                acc_ref[c] += _dot(vts[c], weights[c])
        else:
            probs, alphas = [], []
            for c in chains:
                s = scores[c]
                if masked:
                    s = jnp.where(vis, s, MASKED)
                m_prev = m_ref[c]
                m_new = jnp.maximum(m_prev, jnp.max(s, axis=0, keepdims=True))
                alphas.append(jnp.exp(m_prev - m_new))
                probs.append(jnp.exp(s - m_new).astype(BF16))
                m_ref[c] = m_new
            for c in chains:
                vt_aug = jnp.concatenate([vts[c], ones_rows], axis=0)
                acc_ref[c] = alphas[c] * acc_ref[c] + _dot(vt_aug, probs[c])

    n_full = q0 // tk
    qk(n_full, 0)
    qk(jnp.maximum(n_full - 1, 0), 1)
    consume(n_full, 0, True)

    def pair(j, carry):
        kb = n_full - 1 - 2 * j
        qk(jnp.maximum(kb - 1, 0), 0)
        consume(kb, 1, False)
        qk(jnp.maximum(kb - 2, 0), 1)
        consume(kb - 1, 0, False)
        return carry

    if kind == "sb":
        def top_sum():
            top = m_ref[0]
            for c in range(1, n_chain):
                top = jnp.maximum(top, m_ref[c])
            return jnp.max(top)

        def alive(carry):
            return jnp.logical_and(carry[0] < n_full // 2, carry[1] > SB_DEAD)

        def pair_and_check(carry):
            pair(carry[0], 0)
            return carry[0] + 1, top_sum()

        _, top = lax.while_loop(alive, pair_and_check, (jnp.int32(0), top_sum()))
        odd_block_left = jnp.logical_and(n_full % 2 == 1, top > SB_DEAD)
    else:
        lax.fori_loop(0, n_full // 2, pair, 0)
        odd_block_left = n_full % 2 == 1

    @pl.when(odd_block_left)
    def _():
        consume(0, 1, False)

    if kind == "diff":
        lp = lp_ref[...]
        lam = (jnp.exp(jnp.sum(lp[0:1] * lp[1:2], axis=1, keepdims=True))
               - jnp.exp(jnp.sum(lp[2:3] * lp[3:4], axis=1, keepdims=True)) + lam_init)
    for c in range(n_chain):
        acc = acc_ref[c]
        if kind == "sb":
            out_t = jnp.concatenate([acc[0:HEAD_DIM, 0:tq], acc[HEAD_DIM:LANES, tq:tq2]], axis=0)
        else:
            r = 1.0 / acc[LANES:LANES + 1, :]
            if kind == "fox":
                out_t = jnp.concatenate([acc[0:HEAD_DIM, 0:tq] * r[:, 0:tq],
                                         acc[HEAD_DIM:LANES, tq:tq2] * r[:, tq:tq2]], axis=0)
            else:
                a = acc[0:LANES, 0:tq] * r[:, 0:tq] - lam * (acc[0:LANES, tq:tq2] * r[:, tq:tq2])
                ms = jnp.mean(a * a, axis=0, keepdims=True)
                out_t = a * lax.rsqrt(ms + EPS) * sg_ref[...] * (1.0 - lam_init)
        if tq < LANES:
            out_t = jnp.concatenate([out_t, jnp.zeros((LANES, LANES - tq), F32)], axis=1)
        o_ref[0, :, c * LANES:(c + 1) * LANES] = out_t.T[0:tq].astype(o_ref.dtype)


def _attention(kind, q_arr, q_col, k_arr, k_col, vt_arr, n_col, *, n_chain, tq, tk, q_off, n_valid,
               extras=(), lam_init=0.0):
    B, Tq, _ = q_arr.shape
    Tk = k_arr.shape[1]
    n_kb = Tk // tk
    assert Tq % tq == 0 and Tk % tk == 0 and tk % tq == 0 and q_off % tk == 0 and n_valid <= Tk
    assert tq & (tq - 1) == 0 and n_col % n_chain == 0 and q_col % n_chain == 0 and k_col % n_chain == 0
    wide = n_chain * LANES
    in_specs = [
        pl.BlockSpec((1, tq, wide), lambda b, c, i: (b, i, q_col // n_chain + c)),
        pl.BlockSpec((1, Tk, wide), lambda b, c, i: (b, 0, k_col // n_chain + c)),
        pl.BlockSpec((1, n_chain, n_kb, LANES, tk), lambda b, c, i: (b, c, 0, 0, 0)),
    ]
    acc_rows = LANES if kind == "sb" else LANES + BF16_ROWS
    scratch = [pltpu.VMEM((n_chain, acc_rows, 2 * tq), F32), pltpu.VMEM((n_chain, 1, 2 * tq), F32),
               pltpu.VMEM((2, n_chain, tk, 2 * tq), F32)]
    if kind == "diff":
        lp, sg = extras
        in_specs += [pl.BlockSpec(lp.shape, lambda b, c, i: (0, 0)),
                     pl.BlockSpec(sg.shape, lambda b, c, i: (0, 0))]
    elif kind == "fox":
        (nd,) = extras
        in_specs.append(pl.BlockSpec((1, 1, Tk, LANES), lambda b, c, i: (b, c, 0, 0)))
    kernel = functools.partial(_attn_kernel, kind=kind, n_chain=n_chain, tq=tq, tk=tk, q_off=q_off,
                               n_valid=n_valid, lam_init=lam_init)
    return pl.pallas_call(
        kernel,
        grid=(B, n_col // n_chain, Tq // tq),
        in_specs=in_specs,
        out_specs=pl.BlockSpec((1, tq, wide), lambda b, c, i: (b, i, c)),
        out_shape=jax.ShapeDtypeStruct((B, Tq, n_col * LANES), BF16),
        scratch_shapes=scratch,
        compiler_params=_params("arbitrary", "arbitrary", "arbitrary"),
        name="attn_" + kind,
    )(q_arr, k_arr, vt_arr, *extras)


def _key_blocks_t(v, tk):
    B, Tk, W = v.shape
    return v.reshape(B, Tk // tk, tk, W // LANES, LANES).transpose(0, 3, 1, 4, 2)


def _cache_prep_kernel(k_ref, v_ref, nk_ref, nvt_ref, ko_ref, vto_ref, *, n_heads, tk, time_minor):
    step = pl.program_id(1)
    n_past = pl.num_programs(1) - 1

    @pl.when(step < n_past)
    def _():
        for u in range(vto_ref.shape[2]):
            for c in range(ko_ref.shape[2] // LANES):
                cols = slice(c * LANES, (c + 1) * LANES)
                if time_minor:
                    ko_ref[0, u * tk:(u + 1) * tk, cols] = k_ref[0, cols, u * tk:(u + 1) * tk].T.astype(BF16)
                    vto_ref[0, c, u] = v_ref[0, cols, u * tk:(u + 1) * tk].astype(BF16)
                else:
                    head_rows = pl.ds(u * tk * n_heads + c, tk, stride=n_heads)
                    ko_ref[0, u * tk:(u + 1) * tk, cols] = k_ref[0, head_rows, :].astype(BF16)
                    vto_ref[0, c, u] = v_ref[0, head_rows, :].T.astype(BF16)

    @pl.when(step == n_past)
    def _():
        ko_ref[...] = nk_ref[...]
        vto_ref[...] = nvt_ref[...]


def _cache_prep(cache_k, cache_v, new_k, new_vt, tk):
    B, P, H, dh = cache_k.shape
    W = H * dh
    g = PREP_BLOCKS
    n_col, n_past = W // LANES, P // (g * tk)
    assert P % (g * tk) == 0 and dh in (HEAD_DIM, LANES)
    time_minor = dh != LANES
    if time_minor:
        cache_k, cache_v = [c.transpose(0, 2, 3, 1).reshape(B, W, P) for c in (cache_k, cache_v)]
        rows = pl.BlockSpec((1, W, g * tk), lambda b, j: (b, 0, jnp.minimum(j, n_past - 1)))
    else:
        cache_k, cache_v = [c.reshape(B, P * H, dh) for c in (cache_k, cache_v)]
        rows = pl.BlockSpec((1, g * tk * H, dh), lambda b, j: (b, jnp.minimum(j, n_past - 1), 0))
    return pl.pallas_call(
        functools.partial(_cache_prep_kernel, n_heads=H, tk=tk, time_minor=time_minor),
        grid=(B, n_past + 1),
        in_specs=[rows, rows,
                  pl.BlockSpec((1, g * tk, W), lambda b, j: (b, 0, 0)),
                  pl.BlockSpec((1, n_col, g, LANES, tk), lambda b, j: (b, 0, 0, 0, 0))],
        out_specs=[pl.BlockSpec((1, g * tk, W), lambda b, j: (b, j, 0)),
                   pl.BlockSpec((1, n_col, g, LANES, tk), lambda b, j: (b, 0, j, 0, 0))],
        out_shape=[jax.ShapeDtypeStruct((B, P + g * tk, W), BF16),
                   jax.ShapeDtypeStruct((B, n_col, P // tk + g, LANES, tk), BF16)],
        compiler_params=_params("arbitrary", "arbitrary"),
        name="cache_prep",
    )(cache_k, cache_v, new_k, new_vt)


def _cumsum_kernel(x_ref, o_ref, *, blk):
    T, H = x_ref.shape[1:]
    n_blocks = o_ref.shape[1]
    r = lax.broadcasted_iota(jnp.int32, (blk, blk), 0)
    c = lax.broadcasted_iota(jnp.int32, (blk, blk), 1)
    upto = jnp.where(c <= r, 1.0, 0.0).astype(BF16)
    per_block = 2 * N_CHAIN
    src = lax.broadcasted_iota(jnp.int32, (3 * H, n_blocks * LANES), 0)
    col = lax.broadcasted_iota(jnp.int32, (3 * H, n_blocks * LANES), 1)
    head, piece = jnp.bitwise_and(src, H - 1), jnp.right_shift(src, H.bit_length() - 1)
    here = jnp.right_shift(col, 7) == jnp.right_shift(head, per_block.bit_length() - 1)
    lane = jnp.bitwise_and(col, LANES - 1) - 3 * jnp.bitwise_and(head, per_block - 1)
    place = jnp.where(jnp.logical_and(here, lane == piece), 1.0, 0.0).astype(BF16)

    def chunk(j, carry):
        rows = pl.ds(pl.multiple_of(j * blk, blk), blk)
        hi, mid, lo = _split3(x_ref[0, rows, :])
        total = _dot(upto, hi) + _dot(upto, mid) + _dot(upto, lo) + carry
        pieces = jnp.concatenate([p.astype(F32) for p in _split3(-total)], axis=1).astype(BF16)
        lanes = _dot(pieces, place)
        for blkno in range(n_blocks):
            o_ref[0, blkno, rows, :] = lanes[:, blkno * LANES:(blkno + 1) * LANES].astype(BF16)
        return total[blk - 1:blk, :]

    lax.fori_loop(0, T // blk, chunk, jnp.zeros((1, H), F32))


def _neg_cumsum(lf, blk):
    B, T, H = lf.shape
    n_blocks = H // (2 * N_CHAIN)
    assert H % (2 * N_CHAIN) == 0 and H & (H - 1) == 0
    return pl.pallas_call(
        functools.partial(_cumsum_kernel, blk=blk),
        grid=(B,),
        in_specs=[pl.BlockSpec((1, T, H), lambda b: (b, 0, 0))],
        out_specs=pl.BlockSpec((1, n_blocks, T, LANES), lambda b: (b, 0, 0, 0)),
        out_shape=jax.ShapeDtypeStruct((B, n_blocks, T, LANES), BF16),
        compiler_params=_params("arbitrary"),
        name="neg_cumsum",
    )(lf)


def _outproj_kernel(*refs, nseg):
    a_refs = refs[:nseg]
    w_ref, x_ref, g_ref, o_ref = refs[nseg:]
    bb, tt, d = x_ref.shape
    acc = None
    off = 0
    for a_ref in a_refs:
        width = a_ref.shape[2]
        part = _dot(a_ref[...].reshape(bb * tt, width), w_ref[off:off + width, :])
        acc = part if acc is None else acc + part
        off += width
    o_ref[...] = x_ref[...] + g_ref[...] * acc.reshape(bb, tt, d)


def _outproj(mixed, w_bf, x, gate, bb, tt):
    B, T, _ = x.shape
    x_spec, _, m_spec = _row_specs(bb, tt)
    return pl.pallas_call(
        functools.partial(_outproj_kernel, nseg=len(mixed)),
        grid=(B // bb, T // tt),
        in_specs=[_out_spec(bb, tt, a.shape[2]) for a in mixed]
        + [pl.BlockSpec(w_bf.shape, lambda i, j: (0, 0)), x_spec, m_spec],
        out_specs=x_spec,
        out_shape=jax.ShapeDtypeStruct(x.shape, F32),
        compiler_params=_params("arbitrary", "arbitrary"),
        name="outproj",
    )(*mixed, w_bf, x, gate)


def _route_t(h, wr_t, br):
    tm = h.shape[0]
    h_hi = h.astype(BF16)
    h_lo = (h - h_hi.astype(F32)).astype(BF16)
    w_pieces = jnp.concatenate(_split3(wr_t), axis=0)
    parts = _dot_nt(w_pieces, h_hi) + _dot_nt(w_pieces, h_lo)
    logits = parts[0:N_EXPERTS] + parts[N_EXPERTS:2 * N_EXPERTS] + parts[2 * N_EXPERTS:]
    s = jax.nn.sigmoid(logits)
    sel = s + br
    row = lax.broadcasted_iota(jnp.int32, (N_EXPERTS, tm), 0).astype(F32)
    neg = -jnp.inf

    def first_argmax(vals):
        top = jnp.max(vals, axis=0, keepdims=True)
        idx = jnp.min(jnp.where(vals == top, row, float(N_EXPERTS)), axis=0, keepdims=True)
        return top, idx

    best = None
    for g in range(N_GROUPS):
        in_g = (row >= g * EXPERTS_PER_GROUP) & (row < (g + 1) * EXPERTS_PER_GROUP)
        top1, i1 = first_argmax(jnp.where(in_g, sel, neg))
        top2, i2 = first_argmax(jnp.where(in_g & (row != i1), sel, neg))
        cand = (top1 + top2, i1, i2, jnp.full((1, tm), float(g), F32))
        if best is None:
            best = cand
        else:
            better = cand[0] > best[0]
            best = tuple(jnp.where(better, new, old) for new, old in zip(cand, best))
    _, i1, i2, group = best
    picked = jnp.where((row == i1) | (row == i2), s, 0.0)
    return picked / jnp.sum(picked, axis=0, keepdims=True), group


def _moe_kernel(*refs, final):
    if final:
        (x_ref, g_ref, sc_ref, sh_ref, g2_ref, wrt_ref, br_ref, wg_ref, wu_ref, wd_ref, gf_ref,
         o_ref, h_ref, xg_ref, yg_ref, gg_ref, gate_ref, pos_ref, before_ref, n_ref) = refs
    else:
        (x_ref, g_ref, sc_ref, sh_ref, g2_ref, wrt_ref, br_ref, wg_ref, wu_ref, wd_ref,
         o_ref, h_ref, xg_ref, yg_ref, gg_ref, gate_ref, pos_ref, before_ref, n_ref) = refs
    bb, tt, d = x_ref.shape
    tm = bb * tt
    step = pl.program_id(2)
    first_expert = step * EXPERTS_PER_STEP
    group = first_expert // EXPERTS_PER_GROUP

    @pl.when((pl.program_id(0) == 0) & (pl.program_id(1) == 0) & (step == 0))
    def _():
        def fill(j, carry):
            rows = pl.ds(pl.multiple_of(j * MOE_CHUNK, MOE_CHUNK), MOE_CHUNK)
            r = lax.broadcasted_iota(jnp.int32, (MOE_CHUNK, tm), 0) + j * MOE_CHUNK
            c = lax.broadcasted_iota(jnp.int32, (MOE_CHUNK, tm), 1)
            before_ref[rows, :] = jnp.where(r < c, 1.0, 0.0).astype(BF16)
            return carry
        lax.fori_loop(0, tm // MOE_CHUNK, fill, 0)

    @pl.when(step == 0)
    def _():
        h = _norm_mod(x_ref[...], g_ref[...], sc_ref[...], sh_ref[...])
        h_ref[...] = h.astype(BF16)
        gates, grp = _route_t(h, wrt_ref[...], br_ref[...])
        gates = jnp.concatenate([gates, jnp.zeros((LANES - N_EXPERTS, tm), F32)], axis=0)
        hi = gates.astype(BF16)
        gate_ref[0] = hi
        gate_ref[1] = (gates - hi.astype(F32)).astype(BF16)
        grow = lax.broadcasted_iota(jnp.int32, pos_ref.shape, 0).astype(F32)
        member = jnp.where(grow == grp, 1.0, 0.0)
        rank = _dot(member.astype(BF16), before_ref[...])
        pos_ref[...] = jnp.where(member > 0.0, rank, -1.0)
        for g in range(N_GROUPS):
            count = jnp.sum(member[g:g + 1, :]).astype(jnp.int32)
            n_ref[g] = (count + MOE_CHUNK - 1) // MOE_CHUNK
        o_ref[...] = jnp.zeros_like(o_ref)

    n_chunks = n_ref[group]

    def chunk_rows(j):
        return pl.ds(pl.multiple_of(j * MOE_CHUNK, MOE_CHUNK), MOE_CHUNK)

    def one_hot(j, n_rows=MOE_CHUNK):
        want = (lax.broadcasted_iota(jnp.int32, (n_rows, tm), 0) + j * n_rows).astype(F32)
        return jnp.where(pos_ref[pl.ds(group, 1), :] == want, 1.0, 0.0).astype(BF16)

    @pl.when(first_expert % EXPERTS_PER_GROUP == 0)
    def _():
        def gather(j, carry):
            rows = pl.ds(pl.multiple_of(j * 2 * MOE_CHUNK, 2 * MOE_CHUNK), 2 * MOE_CHUNK)
            pick = one_hot(j, 2 * MOE_CHUNK)
            xg_ref[rows, :] = _dot(pick, h_ref[...]).astype(BF16)
            gg_ref[rows, :] = _dot_nt(pick, gate_ref[0]) + _dot_nt(pick, gate_ref[1])
            yg_ref[rows, :] = jnp.zeros((2 * MOE_CHUNK, d), F32)
            return carry
        lax.fori_loop(0, (n_chunks + 1) // 2, gather, 0)

    local = range(EXPERTS_PER_STEP)
    wd_all = wd_ref[0].reshape(EXPERTS_PER_STEP * D_EXPERT, d)
    lane = lax.broadcasted_iota(jnp.int32, (MOE_CHUNK, LANES), 1)

    def expert(chunks):
        rows = [chunk_rows(j) for j in chunks]
        xs = [xg_ref[r, :] for r in rows]
        gts = [[_dot(x, wg_ref[0, u]) for u in local] for x in xs]
        ups = [[_dot(x, wu_ref[0, u]) for u in local] for x in xs]
        acts = []
        for r, gt, up in zip(rows, gts, ups):
            gg = gg_ref[r, :]
            parts = []
            for u in local:
                ge = jnp.sum(jnp.where(lane == first_expert + u, gg, 0.0), axis=1, keepdims=True)
                parts.append(((gt[u] * jax.nn.sigmoid(gt[u])) * up[u] * ge).astype(BF16))
            acts.append(jnp.concatenate(parts, axis=1))
        for r, a in zip(rows, acts):
            yg_ref[r, :] += _dot(a, wd_all)

    def expert_pair(j, carry):
        expert((2 * j, 2 * j + 1))
        return carry

    lax.fori_loop(0, n_chunks // 2, expert_pair, 0)

    @pl.when(n_chunks % 2 == 1)
    def _():
        expert((n_chunks - 1,))

    @pl.when((first_expert + EXPERTS_PER_STEP) % EXPERTS_PER_GROUP == 0)
    def _():
        def scatter(j, carry):
            rows = pl.ds(pl.multiple_of(j * 2 * MOE_CHUNK, 2 * MOE_CHUNK), 2 * MOE_CHUNK)
            back = lax.dot_general(one_hot(j, 2 * MOE_CHUNK), yg_ref[rows, :].astype(BF16),
                                   (((0,), (0,)), ((), ())), preferred_element_type=F32)
            o_ref[...] += back.reshape(bb, tt, d)
            return carry
        lax.fori_loop(0, (n_chunks + 1) // 2, scatter, 0)

    @pl.when(step == N_EXPERTS // EXPERTS_PER_STEP - 1)
    def _():
        y = x_ref[...] + g2_ref[...] * o_ref[...]
        if final:
            ms = jnp.mean(y * y, axis=-1, keepdims=True)
            y = y * lax.rsqrt(ms + EPS) * gf_ref[...]
        o_ref[...] = y


def _moe(x, g, sc, sh, g2, w_router, b_router, w_gate, w_up, w_down, layer, g_final, bb, tt):
    B, T, _ = x.shape
    final = g_final is not None
    x_spec = pl.BlockSpec((bb, tt, D_MODEL), lambda i, j, e: (i, j, 0))
    g_spec = pl.BlockSpec((1, D_MODEL), lambda i, j, e: (0, 0))
    m_spec = pl.BlockSpec((bb, 1, D_MODEL), lambda i, j, e: (i, 0, 0))
    in_specs = [x_spec, g_spec, m_spec, m_spec, m_spec,
                pl.BlockSpec((N_EXPERTS, D_MODEL), lambda i, j, e: (0, 0)),
                pl.BlockSpec((N_EXPERTS, 1), lambda i, j, e: (0, 0)),
                pl.BlockSpec((1, EXPERTS_PER_STEP, D_MODEL, D_EXPERT), lambda i, j, e: (layer, e, 0, 0)),
                pl.BlockSpec((1, EXPERTS_PER_STEP, D_MODEL, D_EXPERT), lambda i, j, e: (layer, e, 0, 0)),
                pl.BlockSpec((1, EXPERTS_PER_STEP, D_EXPERT, D_MODEL), lambda i, j, e: (layer, e, 0, 0))]
    args = [x, g, sc, sh, g2, w_router.T, b_router.reshape(N_EXPERTS, 1), w_gate, w_up, w_down]
    if final:
        in_specs.append(g_spec)
        args.append(g_final)
    tm = bb * tt
    assert tm % (2 * MOE_CHUNK) == 0
    packed = tm
    scratch = [pltpu.VMEM((tm, D_MODEL), BF16),
               pltpu.VMEM((packed, D_MODEL), BF16),
               pltpu.VMEM((packed, D_MODEL), F32),
               pltpu.VMEM((packed, LANES), F32),
               pltpu.VMEM((2, LANES, tm), BF16),
               pltpu.VMEM((2 * N_GROUPS, tm), F32),
               pltpu.VMEM((tm, tm), BF16),
               pltpu.SMEM((N_GROUPS,), jnp.int32)]
    return pl.pallas_call(
        functools.partial(_moe_kernel, final=final),
        grid=(B // bb, T // tt, N_EXPERTS // EXPERTS_PER_STEP),
        in_specs=in_specs,
        out_specs=x_spec,
        out_shape=jax.ShapeDtypeStruct(x.shape, F32),
        scratch_shapes=scratch,
        compiler_params=_params("arbitrary", "arbitrary", "arbitrary"),
        name="moe",
    )(*args)


def _rope_tables(pos):
    half = ROPE_DIM // 2
    inv = ROPE_THETA ** (-jnp.arange(half, dtype=F32) / half)
    ang = pos.astype(F32)[:, None] * inv[None, :]
    cos, sin = jnp.cos(ang), jnp.sin(ang)
    ones = jnp.ones((pos.shape[0], HEAD_DIM - ROPE_DIM), F32)
    zeros = jnp.zeros_like(ones)
    zh = jnp.zeros_like(sin)
    c64 = jnp.concatenate([cos, cos, ones], axis=1)
    s1_64 = jnp.concatenate([-sin, zh, zeros], axis=1)
    s2_64 = jnp.concatenate([zh, sin, zeros], axis=1)
    return tuple(jnp.concatenate([t, t], axis=1) for t in (c64, s1_64, s2_64))


def _pad_time(a, total):
    if total == a.shape[1]:
        return a
    return jnp.pad(a, ((0, 0), (0, total - a.shape[1])) + ((0, 0),) * (a.ndim - 2))


def _trunk(x, mod, past, wts, cfg):
    B, T, _ = x.shape
    P = 0 if past is None else past["diff_k"].shape[2]
    bb, tt_proj, tt_out, tt_moe, tq, tk = cfg
    wide = past is None and bb == 1 and tt_proj == tk
    n_valid = P + T
    if past is None:
        Tk = -(-n_valid // tk) * tk
    else:
        assert T <= tk
        Tk = P + PREP_BLOCKS * tk
    Tq = -(-T // tq) * tq
    tables = _rope_tables(P + jnp.arange(T))
    if bb > 1:
        tables = tuple(jnp.tile(t, (bb, 1)) for t in tables)
    attn = functools.partial(_attention, n_chain=N_CHAIN, tq=tq, tk=tk, q_off=P, n_valid=n_valid)
    leaves_even, leaves_odd = [], []

    def with_cache(qkv, k_cols, v_cols, cache_k, cache_v):
        new_k = _pad_time(qkv[:, :, k_cols[0]:k_cols[1]], Tk - P)
        new_vt = _key_blocks_t(_pad_time(qkv[:, :, v_cols[0]:v_cols[1]], Tk - P), tk)
        return _cache_prep(cache_k, cache_v, new_k, new_vt, tk)

    def heads_last(leaf_t, n_heads):
        return leaf_t.reshape(B, n_heads, -1, T).transpose(0, 3, 1, 2)

    for l in range(DEPTH):
        sh1, sc1, g1, sh2, sc2, g2 = [mod[l, :, k * D_MODEL:(k + 1) * D_MODEL][:, None, :] for k in range(6)]
        i = l // 2
        g_mix = wts["norm_mix"][l][None, :]
        if l % 2 == 0:
            lam_init = 0.8 - 0.6 * math.exp(-0.3 * l)
            lp = jnp.stack([wts["diff_lq1"][i], wts["diff_lk1"][i], wts["diff_lq2"][i], wts["diff_lk2"][i]])
            diff_extras = (lp, wts["diff_subln"][i][:, None])
            if wide:
                qk, ak, av, bkt, bvt, dvt, svt = _proj_even_wide(x, g_mix, sc1, sh1, wts["w_in_even"][i], tables, tt_proj)
                leaves_even.append((ak.reshape(B, T, N_DIFF_HEADS, LANES), av.reshape(B, T, N_DIFF_HEADS, LANES),
                                    heads_last(bkt, N_SB_HEADS), heads_last(bvt, N_SB_HEADS)))
                q_arr, q_cols = qk, (0, 2 * SEG // LANES)
                dk, dk_col, sk, sk_col = qk, SEG // LANES, qk, 3 * SEG // LANES
            else:
                qkv, ak, av, bk, bv = _proj_even(x, g_mix, sc1, sh1, wts["w_in_even"][i], tables, bb, tt_proj)
                leaves_even.append((ak.reshape(B, T, N_DIFF_HEADS, LANES), av.reshape(B, T, N_DIFF_HEADS, LANES),
                                    bk.reshape(B, T, N_SB_HEADS, HEAD_DIM), bv.reshape(B, T, N_SB_HEADS, HEAD_DIM)))
                q_arr, q_cols = _pad_time(qkv, Tq), (0, 3 * SEG // LANES)
                dk, dvt = with_cache(qkv, (SEG, 2 * SEG), (2 * SEG, 3 * SEG), past["diff_k"][i], past["diff_v"][i])
                sk, svt = with_cache(qkv, (4 * SEG, 5 * SEG), (5 * SEG, 6 * SEG), past["sb_k"][i], past["sb_v"][i])
                dk_col = sk_col = 0
            a_out = attn("diff", q_arr, q_cols[0], dk, dk_col, dvt, N_DIFF_HEADS, extras=diff_extras, lam_init=lam_init)
            b_out = attn("sb", q_arr, q_cols[1], sk, sk_col, svt, N_SB_HEADS // 2)
            x = _outproj([a_out[:, :T], b_out[:, :T]], wts["w_out_even"][i], x, g1, bb, tt_out)
        else:
            if wide:
                qk, kt, vt, fvt, lf = _proj_odd_wide(x, g_mix, sc1, sh1, wts["w_in_odd"][i], wts["w_f_odd"][i],
                                                     wts["b_forget"][i], tt_proj)
                leaves_odd.append((heads_last(kt, N_FOX_HEADS), heads_last(vt, N_FOX_HEADS), lf))
                q_arr, fk, fk_col, lf_all = qk, qk, 2 * SEG // LANES, lf
            else:
                qkv, k, v, lf = _proj_odd(x, g_mix, sc1, sh1, wts["w_in_odd"][i], wts["w_f_odd"][i],
                                          wts["b_forget"][i], bb, tt_proj)
                leaves_odd.append((k.reshape(B, T, N_FOX_HEADS, HEAD_DIM), v.reshape(B, T, N_FOX_HEADS, HEAD_DIM), lf))
                q_arr = _pad_time(qkv, Tq)
                fk, fvt = with_cache(qkv, (2 * SEG, 4 * SEG), (4 * SEG, 6 * SEG), past["fox_k"][i], past["fox_v"][i])
                fk_col = 0
                lf_all = jnp.concatenate([past["fox_logf"][i], lf], axis=1)
            nd = _neg_cumsum(_pad_time(lf_all, Tk), tk)
            o = attn("fox", q_arr, 0, fk, fk_col, fvt, N_FOX_HEADS // 2, extras=(nd,))
            x = _outproj([o[:, :T]], wts["w_out_odd"][i], x, g1, bb, tt_out)
        g_final = wts["norm_final"][None, :] if l == DEPTH - 1 else None
        x = _moe(x, wts["norm_ffn"][l][None, :], sc2, sh2, g2, wts["w_router"], wts["b_router"],
                 wts["w_gate"], wts["w_up"], wts["w_down"], l, g_final, bb, tt_moe)
    return x, leaves_even, leaves_odd


def kernel(x_prompt, x_sample, c_prompt, c_sample, cache_diff_k, cache_diff_v, cache_sb_k, cache_sb_v, cache_fox_k, cache_fox_v, cache_fox_logf, w_ada, b_ada, norm_mix, norm_ffn, w_in_even, w_out_even, diff_lq1, diff_lk1, diff_lq2, diff_lk2, diff_subln, w_in_odd, b_forget, w_out_odd, w_router, b_router, w_gate, w_up, w_down, norm_final):
    Bp, Tp, _ = x_prompt.shape
    Bs, Ts, _ = x_sample.shape
    wts = {
        "norm_mix": norm_mix, "norm_ffn": norm_ffn, "norm_final": norm_final,
        "w_in_even": w_in_even.astype(BF16), "w_out_even": w_out_even.astype(BF16),
        "w_in_odd": w_in_odd[:, :, :3 * FOX_W].astype(BF16), "w_f_odd": w_in_odd[:, :, 3 * FOX_W:].astype(BF16),
        "w_out_odd": w_out_odd.astype(BF16), "b_forget": b_forget,
        "diff_lq1": diff_lq1, "diff_lk1": diff_lk1, "diff_lq2": diff_lq2, "diff_lk2": diff_lk2,
        "diff_subln": diff_subln, "w_router": w_router, "b_router": b_router,
        "w_gate": w_gate.astype(BF16), "w_up": w_up.astype(BF16), "w_down": w_down.astype(BF16),
    }
    past = {"diff_k": cache_diff_k, "diff_v": cache_diff_v, "sb_k": cache_sb_k, "sb_v": cache_sb_v,
            "fox_k": cache_fox_k, "fox_v": cache_fox_v, "fox_logf": cache_fox_logf}

    rows = Bp + Bs
    rows_pad = -(-rows // 16) * 16
    c_all = jnp.pad(jnp.concatenate([c_prompt, c_sample], axis=0), ((0, rows_pad - rows), (0, 0)))
    mod = _ada_mod(c_all, w_ada, b_ada)

    cfg_prompt = (1, 256, 512, 1024, 256, 256)
    cfg_sample = (Bs, Ts, Ts, Ts, 64, 256)
    y_p, even_p, odd_p = _trunk(x_prompt, mod[:, :Bp], None, wts, cfg_prompt)
    y_s, even_s, odd_s = _trunk(x_sample, mod[:, Bp:rows], past, wts, cfg_sample)

    def layers(rows):
        return rows[0][None] if len(rows) == 1 else jnp.stack(rows, axis=0)

    def pack(even, odd):
        return tuple(layers(r) for r in zip(*even)) + tuple(layers(r) for r in zip(*odd))

    return (y_p, y_s) + pack(even_p, odd_p) + pack(even_s, odd_s)
```

```python
import functools
import math

import jax
import jax.numpy as jnp
from jax import lax
from jax.experimental import pallas as pl
from jax.experimental.pallas import tpu as pltpu

D_MODEL = 1024
DEPTH = 2
CHUNK = 64
HEAD_DIM = 64
N_DIFF_HEADS = D_MODEL // 256
N_SB_HEADS = D_MODEL // 128
N_FOX_HEADS = D_MODEL // HEAD_DIM
DIFF_W = N_DIFF_HEADS * 2 * HEAD_DIM
SB_W = N_SB_HEADS * HEAD_DIM
FOX_W = N_FOX_HEADS * HEAD_DIM
ROPE_DIM = HEAD_DIM // 4
ROPE_THETA = 500000.0
N_EXPERTS = 16
N_GROUPS = 4
EXPERTS_PER_GROUP = N_EXPERTS // N_GROUPS
D_EXPERT = D_MODEL // 2
EPS = 1e-6

LANES = 128
BF16_ROWS = 16
SEG = 512
QK_SCALE = HEAD_DIM ** -0.5
MASKED = -1e30
N_CHAIN = 4
SB_DEAD = -110.0
MOE_CHUNK = 128
EXPERTS_PER_STEP = 2
PREP_BLOCKS = 2
VMEM_LIMIT = 56 * 1024 * 1024

F32 = jnp.float32
BF16 = jnp.bfloat16


def _params(*sem):
    return pltpu.CompilerParams(dimension_semantics=sem, vmem_limit_bytes=VMEM_LIMIT)


def _dot(a, b):
    return jnp.dot(a, b, preferred_element_type=F32)


def _dot_nt(a, b):
    return lax.dot_general(a, b, (((1,), (1,)), ((), ())), preferred_element_type=F32)


def _norm_mod(x, g, sc, sh):
    bb, tt, d = x.shape
    ms = jnp.mean(x * x, axis=-1, keepdims=True)
    h = x * lax.rsqrt(ms + EPS) * g
    h = h * (1.0 + sc) + sh
    return h.reshape(bb * tt, d)


def _log_sigmoid(x):
    return jnp.minimum(x, 0.0) - jnp.log(1.0 + jnp.exp(-jnp.abs(x)))


def _split3(x):
    hi = x.astype(BF16)
    r1 = x - hi.astype(F32)
    mid = r1.astype(BF16)
    lo = (r1 - mid.astype(F32)).astype(BF16)
    return hi, mid, lo


def _ada_kernel(c_ref, w_ref, b_ref, o_ref):
    c = c_ref[...]
    a = (c * jax.nn.sigmoid(c)).astype(BF16)
    o_ref[0] = _dot(a, w_ref[0].astype(BF16)) + b_ref[0]


def _ada_mod(c_all, w_ada, b_ada):
    rows = c_all.shape[0]
    tn = 1536
    width = 6 * D_MODEL
    return pl.pallas_call(
        _ada_kernel,
        grid=(DEPTH, width // tn),
        in_specs=[
            pl.BlockSpec((rows, D_MODEL), lambda l, j: (0, 0)),
            pl.BlockSpec((1, D_MODEL, tn), lambda l, j: (l, 0, j)),
            pl.BlockSpec((1, 1, tn), lambda l, j: (l, 0, j)),
        ],
        out_specs=pl.BlockSpec((1, rows, tn), lambda l, j: (l, 0, j)),
        out_shape=jax.ShapeDtypeStruct((DEPTH, rows, width), F32),
        compiler_params=_params("arbitrary", "arbitrary"),
        name="ada_mod",
    )(c_all, w_ada, b_ada.reshape(DEPTH, 1, width))


def _rope(p, cos, s1, s2):
    outs = []
    for c in range(SEG // LANES):
        pc = p[:, c * LANES:(c + 1) * LANES]
        up = pltpu.roll(pc, LANES - ROPE_DIM // 2, 1)
        down = pltpu.roll(pc, ROPE_DIM // 2, 1)
        outs.append(pc * cos + up * s1 + down * s2)
    return jnp.concatenate(outs, axis=1)


def _proj_even_kernel(x_ref, g_ref, sc_ref, sh_ref, w_ref, cos_ref, s1_ref, s2_ref,
                      qkv_ref, ak_ref, av_ref, bk_ref, bv_ref):
    bb, tt, _ = x_ref.shape
    h = _norm_mod(x_ref[...], g_ref[...], sc_ref[...], sh_ref[...]).astype(BF16)
    cos, s1, s2 = cos_ref[...], s1_ref[...], s2_ref[...]

    def seg(i):
        return _dot(h, w_ref[:, i * SEG:(i + 1) * SEG])

    def put(i, val, leaf_ref):
        val3 = val.reshape(bb, tt, SEG)
        qkv_ref[:, :, i * SEG:(i + 1) * SEG] = val3.astype(BF16)
        if leaf_ref is not None:
            leaf_ref[...] = val3

    put(0, _rope(seg(0), cos, s1, s2) * QK_SCALE, None)
    put(1, _rope(seg(1), cos, s1, s2), ak_ref)
    put(2, seg(2), av_ref)
    put(3, seg(3) * QK_SCALE, None)
    put(4, seg(4), bk_ref)
    put(5, seg(5), bv_ref)


def _rows_by_head(ref, val, n_heads):
    tt = val.shape[0]
    for hd in range(n_heads):
        ref[0, pl.ds(hd, tt, stride=n_heads), :] = val[:, hd * LANES:(hd + 1) * LANES]


def _key_blocks(ref, first, val_t):
    n = val_t.shape[0] // LANES
    ref[0, first:first + n, 0] = val_t.reshape(n, LANES, val_t.shape[1]).astype(BF16)


def _proj_even_wide_kernel(x_ref, g_ref, sc_ref, sh_ref, w_ref, cos_ref, s1_ref, s2_ref,
                           qk_ref, ak_ref, av_ref, bkt_ref, bvt_ref, dvt_ref, svt_ref):
    h = _norm_mod(x_ref[...], g_ref[...], sc_ref[...], sh_ref[...]).astype(BF16)
    cos, s1, s2 = cos_ref[...], s1_ref[...], s2_ref[...]

    def seg(i):
        return _dot(h, w_ref[:, i * SEG:(i + 1) * SEG])

    qk_ref[0, :, 0:SEG] = (_rope(seg(0), cos, s1, s2) * QK_SCALE).astype(BF16)
    ak = _rope(seg(1), cos, s1, s2)
    qk_ref[0, :, SEG:2 * SEG] = ak.astype(BF16)
    _rows_by_head(ak_ref, ak, N_DIFF_HEADS)
    av = seg(2)
    _rows_by_head(av_ref, av, N_DIFF_HEADS)
    _key_blocks(dvt_ref, 0, av.T)
    qk_ref[0, :, 2 * SEG:3 * SEG] = (seg(3) * QK_SCALE).astype(BF16)
    bk = seg(4)
    qk_ref[0, :, 3 * SEG:4 * SEG] = bk.astype(BF16)
    bkt_ref[0] = bk.T
    bvt = seg(5).T
    bvt_ref[0] = bvt
    _key_blocks(svt_ref, 0, bvt)


def _proj_odd_wide_kernel(x_ref, g_ref, sc_ref, sh_ref, w_ref, wf_ref, bf_ref,
                          qk_ref, kt_ref, vt_ref, fvt_ref, lf_ref):
    h = _norm_mod(x_ref[...], g_ref[...], sc_ref[...], sh_ref[...]).astype(BF16)

    def seg(i):
        return _dot(h, w_ref[:, i * SEG:(i + 1) * SEG])

    for i in range(2):
        qk_ref[0, :, i * SEG:(i + 1) * SEG] = (seg(i) * QK_SCALE).astype(BF16)
    for i in range(2):
        k = seg(2 + i)
        qk_ref[0, :, (2 + i) * SEG:(3 + i) * SEG] = k.astype(BF16)
        kt_ref[0, i * SEG:(i + 1) * SEG, :] = k.T
        v_t = seg(4 + i).T
        vt_ref[0, i * SEG:(i + 1) * SEG, :] = v_t
        _key_blocks(fvt_ref, i * (SEG // LANES), v_t)
    fl = _dot(h, wf_ref[...]) + bf_ref[...]
    lf_ref[0] = _log_sigmoid(fl)


def _proj_odd_kernel(x_ref, g_ref, sc_ref, sh_ref, w_ref, wf_ref, bf_ref,
                     qkv_ref, k_ref, v_ref, lf_ref):
    bb, tt, _ = x_ref.shape
    h = _norm_mod(x_ref[...], g_ref[...], sc_ref[...], sh_ref[...]).astype(BF16)

    def seg(i):
        return _dot(h, w_ref[:, i * SEG:(i + 1) * SEG]).reshape(bb, tt, SEG)

    for i in range(2):
        qkv_ref[:, :, i * SEG:(i + 1) * SEG] = (seg(i) * QK_SCALE).astype(BF16)
    for i in range(2, 6):
        val = seg(i)
        qkv_ref[:, :, i * SEG:(i + 1) * SEG] = val.astype(BF16)
        leaf_ref = k_ref if i < 4 else v_ref
        leaf_ref[:, :, (i % 2) * SEG:(i % 2 + 1) * SEG] = val
    fl = _dot(h, wf_ref[...]) + bf_ref[...]
    lf_ref[...] = _log_sigmoid(fl).reshape(bb, tt, N_FOX_HEADS)


def _row_specs(bb, tt):
    x_spec = pl.BlockSpec((bb, tt, D_MODEL), lambda i, j: (i, j, 0))
    g_spec = pl.BlockSpec((1, D_MODEL), lambda i, j: (0, 0))
    m_spec = pl.BlockSpec((bb, 1, D_MODEL), lambda i, j: (i, 0, 0))
    return x_spec, g_spec, m_spec


def _out_spec(bb, tt, width):
    return pl.BlockSpec((bb, tt, width), lambda i, j: (i, j, 0))


def _proj_even(x, g, sc, sh, w_bf, tables, bb, tt):
    B, T, _ = x.shape
    x_spec, g_spec, m_spec = _row_specs(bb, tt)
    tm = bb * tt
    t_spec = pl.BlockSpec((tm, LANES), (lambda i, j: (j, 0)) if bb == 1 else (lambda i, j: (0, 0)))
    leaf = jax.ShapeDtypeStruct((B, T, SEG), F32)
    return pl.pallas_call(
        _proj_even_kernel,
        grid=(B // bb, T // tt),
        in_specs=[x_spec, g_spec, m_spec, m_spec,
                  pl.BlockSpec(w_bf.shape, lambda i, j: (0, 0)), t_spec, t_spec, t_spec],
        out_specs=[_out_spec(bb, tt, 6 * SEG)] + [_out_spec(bb, tt, SEG)] * 4,
        out_shape=[jax.ShapeDtypeStruct((B, T, 6 * SEG), BF16), leaf, leaf, leaf, leaf],
        compiler_params=_params("arbitrary", "arbitrary"),
        name="proj_even",
    )(x, g, sc, sh, w_bf, *tables)


def _wide_specs(B, T, tt):
    def rows(width, dtype):
        return pl.BlockSpec((1, tt, width), lambda i, j: (i, j, 0)), jax.ShapeDtypeStruct((B, T, width), dtype)

    def by_head(n_heads):
        return (pl.BlockSpec((1, tt * n_heads, LANES), lambda i, j: (i, j, 0)),
                jax.ShapeDtypeStruct((B, T * n_heads, LANES), F32))

    def transposed(width):
        return pl.BlockSpec((1, width, tt), lambda i, j: (i, 0, j)), jax.ShapeDtypeStruct((B, width, T), F32)

    def key_blocks(n_col):
        return (pl.BlockSpec((1, n_col, 1, LANES, tt), lambda i, j: (i, 0, j, 0, 0)),
                jax.ShapeDtypeStruct((B, n_col, T // tt, LANES, tt), BF16))

    return rows, by_head, transposed, key_blocks


def _proj_even_wide(x, g, sc, sh, w_bf, tables, tt):
    B, T, _ = x.shape
    x_spec, g_spec, m_spec = _row_specs(1, tt)
    t_spec = pl.BlockSpec((tt, LANES), lambda i, j: (j, 0))
    rows, by_head, transposed, key_blocks = _wide_specs(B, T, tt)
    outs = [rows(4 * SEG, BF16), by_head(N_DIFF_HEADS), by_head(N_DIFF_HEADS), transposed(SEG), transposed(SEG),
            key_blocks(SEG // LANES), key_blocks(SEG // LANES)]
    return pl.pallas_call(
        _proj_even_wide_kernel,
        grid=(B, T // tt),
        in_specs=[x_spec, g_spec, m_spec, m_spec,
                  pl.BlockSpec(w_bf.shape, lambda i, j: (0, 0)), t_spec, t_spec, t_spec],
        out_specs=[o[0] for o in outs],
        out_shape=[o[1] for o in outs],
        compiler_params=_params("arbitrary", "arbitrary"),
        name="proj_even_wide",
    )(x, g, sc, sh, w_bf, *tables)


def _proj_odd_wide(x, g, sc, sh, w_bf, wf_bf, b_f, tt):
    B, T, _ = x.shape
    x_spec, g_spec, m_spec = _row_specs(1, tt)
    rows, by_head, transposed, key_blocks = _wide_specs(B, T, tt)
    outs = [rows(4 * SEG, BF16), transposed(FOX_W), transposed(FOX_W), key_blocks(FOX_W // LANES),
            rows(N_FOX_HEADS, F32)]
    return pl.pallas_call(
        _proj_odd_wide_kernel,
        grid=(B, T // tt),
        in_specs=[x_spec, g_spec, m_spec, m_spec,
                  pl.BlockSpec(w_bf.shape, lambda i, j: (0, 0)),
                  pl.BlockSpec(wf_bf.shape, lambda i, j: (0, 0)),
                  pl.BlockSpec((1, N_FOX_HEADS), lambda i, j: (0, 0))],
        out_specs=[o[0] for o in outs],
        out_shape=[o[1] for o in outs],
        compiler_params=_params("arbitrary", "arbitrary"),
        name="proj_odd_wide",
    )(x, g, sc, sh, w_bf, wf_bf, b_f.reshape(1, N_FOX_HEADS))


def _proj_odd(x, g, sc, sh, w_bf, wf_bf, b_f, bb, tt):
    B, T, _ = x.shape
    x_spec, g_spec, m_spec = _row_specs(bb, tt)
    leaf = jax.ShapeDtypeStruct((B, T, FOX_W), F32)
    return pl.pallas_call(
        _proj_odd_kernel,
        grid=(B // bb, T // tt),
        in_specs=[x_spec, g_spec, m_spec, m_spec,
                  pl.BlockSpec(w_bf.shape, lambda i, j: (0, 0)),
                  pl.BlockSpec(wf_bf.shape, lambda i, j: (0, 0)),
                  pl.BlockSpec((1, N_FOX_HEADS), lambda i, j: (0, 0))],
        out_specs=[_out_spec(bb, tt, 6 * SEG), _out_spec(bb, tt, FOX_W), _out_spec(bb, tt, FOX_W),
                   _out_spec(bb, tt, N_FOX_HEADS)],
        out_shape=[jax.ShapeDtypeStruct((B, T, 6 * SEG), BF16), leaf, leaf,
                   jax.ShapeDtypeStruct((B, T, N_FOX_HEADS), F32)],
        compiler_params=_params("arbitrary", "arbitrary"),
        name="proj_odd",
    )(x, g, sc, sh, w_bf, wf_bf, b_f.reshape(1, N_FOX_HEADS))


def _attn_kernel(*refs, kind, n_chain, tq, tk, q_off, n_valid, lam_init):
    if kind == "diff":
        q_ref, k_ref, vt_ref, lp_ref, sg_ref, o_ref, acc_ref, m_ref, s_ref = refs
    elif kind == "fox":
        q_ref, k_ref, vt_ref, nd_ref, o_ref, acc_ref, m_ref, s_ref = refs
    else:
        q_ref, k_ref, vt_ref, o_ref, acc_ref, m_ref, s_ref = refs
    tq2 = 2 * tq
    q0 = q_off + pl.program_id(2) * tq

    low = lax.broadcasted_iota(jnp.int32, (tq, LANES), 1) < HEAD_DIM
    if kind == "fox":
        row = lax.broadcasted_iota(jnp.int32, (tq2, LANES), 0)
        col = lax.broadcasted_iota(jnp.int32, (tq2, LANES), 1)
        first = jnp.right_shift(row, tq.bit_length() - 1) * 3
    qqs = []
    for c in range(n_chain):
        q = q_ref[0, :, c * LANES:(c + 1) * LANES]
        zero = jnp.zeros_like(q)
        qq = jnp.concatenate([jnp.where(low, q, zero), jnp.where(low, zero, q)], axis=0)
        if kind == "fox":
            pick = jnp.logical_and(first + 6 * c <= col, col < first + 6 * c + 3)
            qq = jnp.concatenate([qq, jnp.where(pick, 1.0, 0.0).astype(BF16)], axis=1)
        qqs.append(qq)

    acc_ref[...] = jnp.zeros_like(acc_ref)
    if kind == "sb":
        m_ref[...] = jnp.zeros_like(m_ref)
        tri_r = lax.broadcasted_iota(jnp.int32, (tk, tk), 0)
        tri_c = lax.broadcasted_iota(jnp.int32, (tk, tk), 1)
        after = jnp.where(tri_c > tri_r, 1.0, 0.0).astype(BF16)
        after2 = jnp.concatenate([after, after], axis=1)
    else:
        m_ref[...] = jnp.full_like(m_ref, MASKED)
        ones_rows = jnp.ones((BF16_ROWS, tk), BF16)

    qpos = q0 + jnp.bitwise_and(lax.broadcasted_iota(jnp.int32, (tk, tq2), 1), tq - 1)
    kidx = lax.broadcasted_iota(jnp.int32, (tk, tq2), 0)

    chains = range(n_chain)

    def qk(kb, slot):
        start = pl.multiple_of(kb * tk, tk)
        for c in chains:
            k = k_ref[0, pl.ds(start, tk), c * LANES:(c + 1) * LANES]
            if kind == "fox":
                k = jnp.concatenate([k, nd_ref[0, 0, pl.ds(start, tk), :]], axis=1)
            s_ref[slot, c] = _dot_nt(k, qqs[c])

    def consume(kb, slot, masked):
        scores = [s_ref[slot, c] for c in chains]
        if masked:
            kpos = kb * tk + kidx
            if kind == "diff":
                vis = jnp.right_shift(kpos, 6) <= jnp.right_shift(qpos, 6)
            elif kind == "fox":
                vis = kpos <= qpos
            else:
                vis = kpos < qpos
            vis = jnp.logical_and(vis, kpos < n_valid)
        vts = [vt_ref[0, c, kb] for c in chains]
        if kind == "sb":
            log_betas, laters = [], []
            for c in chains:
                s = scores[c]
                soft = jnp.log(1.0 + jnp.exp(-jnp.abs(s)))
                log_beta = jnp.minimum(s, 0.0) - soft
                log_keep = log_beta - s
                if masked:
                    log_keep = jnp.where(vis, log_keep, 0.0)
                hi = log_keep.astype(BF16)
                lo = (log_keep - hi.astype(F32)).astype(BF16)
                later = _dot(after2, jnp.concatenate([hi, lo], axis=0))
                laters.append(later + m_ref[c])
                m_ref[c] += later[0:1, :] + log_keep[0:1, :]
                log_betas.append(log_beta)
            weights = []
            for c in chains:
                w = jnp.exp(log_betas[c] + laters[c])
                if masked:
                    w = jnp.where(vis, w, 0.0)
                weights.append(w.astype(BF16))
            for c in chains:
                acc_ref[c] += _dot(vts[c], weights[c])
        else:
            probs, alphas = [], []
            for c in chains:
                s = scores[c]
                if masked:
                    s = jnp.where(vis, s, MASKED)
                m_prev = m_ref[c]
                m_new = jnp.maximum(m_prev, jnp.max(s, axis=0, keepdims=True))
                alphas.append(jnp.exp(m_prev - m_new))
                probs.append(jnp.exp(s - m_new).astype(BF16))
                m_ref[c] = m_new
            for c in chains:
                vt_aug = jnp.concatenate([vts[c], ones_rows], axis=0)
                acc_ref[c] = alphas[c] * acc_ref[c] + _dot(vt_aug, probs[c])

    n_full = q0 // tk
    qk(n_full, 0)
    qk(jnp.maximum(n_full - 1, 0), 1)
    consume(n_full, 0, True)

    def pair(j, carry):
        kb = n_full - 1 - 2 * j
        qk(jnp.maximum(kb - 1, 0), 0)
        consume(kb, 1, False)
        qk(jnp.maximum(kb - 2, 0), 1)
        consume(kb - 1, 0, False)
        return carry

    if kind == "sb":
        def top_sum():
            top = m_ref[0]
            for c in range(1, n_chain):
                top = jnp.maximum(top, m_ref[c])
            return jnp.max(top)

        def alive(carry):
            return jnp.logical_and(carry[0] < n_full // 2, carry[1] > SB_DEAD)

        def pair_and_check(carry):
            pair(carry[0], 0)
            return carry[0] + 1, top_sum()

        _, top = lax.while_loop(alive, pair_and_check, (jnp.int32(0), top_sum()))
        odd_block_left = jnp.logical_and(n_full % 2 == 1, top > SB_DEAD)
    else:
        lax.fori_loop(0, n_full // 2, pair, 0)
        odd_block_left = n_full % 2 == 1

    @pl.when(odd_block_left)
    def _():
        consume(0, 1, False)

    if kind == "diff":
        lp = lp_ref[...]
        lam = (jnp.exp(jnp.sum(lp[0:1] * lp[1:2], axis=1, keepdims=True))
               - jnp.exp(jnp.sum(lp[2:3] * lp[3:4], axis=1, keepdims=True)) + lam_init)
    for c in range(n_chain):
        acc = acc_ref[c]
        if kind == "sb":
            out_t = jnp.concatenate([acc[0:HEAD_DIM, 0:tq], acc[HEAD_DIM:LANES, tq:tq2]], axis=0)
        else:
            r = 1.0 / acc[LANES:LANES + 1, :]
            if kind == "fox":
                out_t = jnp.concatenate([acc[0:HEAD_DIM, 0:tq] * r[:, 0:tq],
                                         acc[HEAD_DIM:LANES, tq:tq2] * r[:, tq:tq2]], axis=0)
            else:
                a = acc[0:LANES, 0:tq] * r[:, 0:tq] - lam * (acc[0:LANES, tq:tq2] * r[:, tq:tq2])
                ms = jnp.mean(a * a, axis=0, keepdims=True)
                out_t = a * lax.rsqrt(ms + EPS) * sg_ref[...] * (1.0 - lam_init)
        if tq < LANES:
            out_t = jnp.concatenate([out_t, jnp.zeros((LANES, LANES - tq), F32)], axis=1)
        o_ref[0, :, c * LANES:(c + 1) * LANES] = out_t.T[0:tq].astype(o_ref.dtype)


def _attention(kind, q_arr, q_col, k_arr, k_col, vt_arr, n_col, *, n_chain, tq, tk, q_off, n_valid,
               extras=(), lam_init=0.0):
    B, Tq, _ = q_arr.shape
    Tk = k_arr.shape[1]
    n_kb = Tk // tk
    assert Tq % tq == 0 and Tk % tk == 0 and tk % tq == 0 and q_off % tk == 0 and n_valid <= Tk
    assert tq & (tq - 1) == 0 and n_col % n_chain == 0 and q_col % n_chain == 0 and k_col % n_chain == 0
    wide = n_chain * LANES
    in_specs = [
        pl.BlockSpec((1, tq, wide), lambda b, c, i: (b, i, q_col // n_chain + c)),
        pl.BlockSpec((1, Tk, wide), lambda b, c, i: (b, 0, k_col // n_chain + c)),
        pl.BlockSpec((1, n_chain, n_kb, LANES, tk), lambda b, c, i: (b, c, 0, 0, 0)),
    ]
    acc_rows = LANES if kind == "sb" else LANES + BF16_ROWS
    scratch = [pltpu.VMEM((n_chain, acc_rows, 2 * tq), F32), pltpu.VMEM((n_chain, 1, 2 * tq), F32),
               pltpu.VMEM((2, n_chain, tk, 2 * tq), F32)]
    if kind == "diff":
        lp, sg = extras
        in_specs += [pl.BlockSpec(lp.shape, lambda b, c, i: (0, 0)),
                     pl.BlockSpec(sg.shape, lambda b, c, i: (0, 0))]
    elif kind == "fox":
        (nd,) = extras
        in_specs.append(pl.BlockSpec((1, 1, Tk, LANES), lambda b, c, i: (b, c, 0, 0)))
    kernel = functools.partial(_attn_kernel, kind=kind, n_chain=n_chain, tq=tq, tk=tk, q_off=q_off,
                               n_valid=n_valid, lam_init=lam_init)
    return pl.pallas_call(
        kernel,
        grid=(B, n_col // n_chain, Tq // tq),
        in_specs=in_specs,
        out_specs=pl.BlockSpec((1, tq, wide), lambda b, c, i: (b, i, c)),
        out_shape=jax.ShapeDtypeStruct((B, Tq, n_col * LANES), BF16),
        scratch_shapes=scratch,
        compiler_params=_params("arbitrary", "arbitrary", "arbitrary"),
        name="attn_" + kind,
    )(q_arr, k_arr, vt_arr, *extras)


def _key_blocks_t(v, tk):
    B, Tk, W = v.shape
    return v.reshape(B, Tk // tk, tk, W // LANES, LANES).transpose(0, 3, 1, 4, 2)


def _cache_prep_kernel(k_ref, v_ref, nk_ref, nvt_ref, ko_ref, vto_ref, *, n_heads, tk, time_minor):
    step = pl.program_id(1)
    n_past = pl.num_programs(1) - 1

    @pl.when(step < n_past)
    def _():
        for u in range(vto_ref.shape[2]):
            for c in range(ko_ref.shape[2] // LANES):
                cols = slice(c * LANES, (c + 1) * LANES)
                if time_minor:
                    ko_ref[0, u * tk:(u + 1) * tk, cols] = k_ref[0, cols, u * tk:(u + 1) * tk].T.astype(BF16)
                    vto_ref[0, c, u] = v_ref[0, cols, u * tk:(u + 1) * tk].astype(BF16)
                else:
                    head_rows = pl.ds(u * tk * n_heads + c, tk, stride=n_heads)
                    ko_ref[0, u * tk:(u + 1) * tk, cols] = k_ref[0, head_rows, :].astype(BF16)
                    vto_ref[0, c, u] = v_ref[0, head_rows, :].T.astype(BF16)

    @pl.when(step == n_past)
    def _():
        ko_ref[...] = nk_ref[...]
        vto_ref[...] = nvt_ref[...]


def _cache_prep(cache_k, cache_v, new_k, new_vt, tk):
    B, P, H, dh = cache_k.shape
    W = H * dh
    g = PREP_BLOCKS
    n_col, n_past = W // LANES, P // (g * tk)
    assert P % (g * tk) == 0 and dh in (HEAD_DIM, LANES)
    time_minor = dh != LANES
    if time_minor:
        cache_k, cache_v = [c.transpose(0, 2, 3, 1).reshape(B, W, P) for c in (cache_k, cache_v)]
        rows = pl.BlockSpec((1, W, g * tk), lambda b, j: (b, 0, jnp.minimum(j, n_past - 1)))
    else:
        cache_k, cache_v = [c.reshape(B, P * H, dh) for c in (cache_k, cache_v)]
        rows = pl.BlockSpec((1, g * tk * H, dh), lambda b, j: (b, jnp.minimum(j, n_past - 1), 0))
    return pl.pallas_call(
        functools.partial(_cache_prep_kernel, n_heads=H, tk=tk, time_minor=time_minor),
        grid=(B, n_past + 1),
        in_specs=[rows, rows,
                  pl.BlockSpec((1, g * tk, W), lambda b, j: (b, 0, 0)),
                  pl.BlockSpec((1, n_col, g, LANES, tk), lambda b, j: (b, 0, 0, 0, 0))],
        out_specs=[pl.BlockSpec((1, g * tk, W), lambda b, j: (b, j, 0)),
                   pl.BlockSpec((1, n_col, g, LANES, tk), lambda b, j: (b, 0, j, 0, 0))],
        out_shape=[jax.ShapeDtypeStruct((B, P + g * tk, W), BF16),
                   jax.ShapeDtypeStruct((B, n_col, P // tk + g, LANES, tk), BF16)],
        compiler_params=_params("arbitrary", "arbitrary"),
        name="cache_prep",
    )(cache_k, cache_v, new_k, new_vt)


def _cumsum_kernel(x_ref, o_ref, *, blk):
    T, H = x_ref.shape[1:]
    n_blocks = o_ref.shape[1]
    r = lax.broadcasted_iota(jnp.int32, (blk, blk), 0)
    c = lax.broadcasted_iota(jnp.int32, (blk, blk), 1)
    upto = jnp.where(c <= r, 1.0, 0.0).astype(BF16)
    per_block = 2 * N_CHAIN
    src = lax.broadcasted_iota(jnp.int32, (3 * H, n_blocks * LANES), 0)
    col = lax.broadcasted_iota(jnp.int32, (3 * H, n_blocks * LANES), 1)
    head, piece = jnp.bitwise_and(src, H - 1), jnp.right_shift(src, H.bit_length() - 1)
    here = jnp.right_shift(col, 7) == jnp.right_shift(head, per_block.bit_length() - 1)
    lane = jnp.bitwise_and(col, LANES - 1) - 3 * jnp.bitwise_and(head, per_block - 1)
    place = jnp.where(jnp.logical_and(here, lane == piece), 1.0, 0.0).astype(BF16)

    def chunk(j, carry):
        rows = pl.ds(pl.multiple_of(j * blk, blk), blk)
        hi, mid, lo = _split3(x_ref[0, rows, :])
        total = _dot(upto, hi) + _dot(upto, mid) + _dot(upto, lo) + carry
        pieces = jnp.concatenate([p.astype(F32) for p in _split3(-total)], axis=1).astype(BF16)
        lanes = _dot(pieces, place)
        for blkno in range(n_blocks):
            o_ref[0, blkno, rows, :] = lanes[:, blkno * LANES:(blkno + 1) * LANES].astype(BF16)
        return total[blk - 1:blk, :]

    lax.fori_loop(0, T // blk, chunk, jnp.zeros((1, H), F32))


def _neg_cumsum(lf, blk):
    B, T, H = lf.shape
    n_blocks = H // (2 * N_CHAIN)
    assert H % (2 * N_CHAIN) == 0 and H & (H - 1) == 0
    return pl.pallas_call(
        functools.partial(_cumsum_kernel, blk=blk),
        grid=(B,),
        in_specs=[pl.BlockSpec((1, T, H), lambda b: (b, 0, 0))],
        out_specs=pl.BlockSpec((1, n_blocks, T, LANES), lambda b: (b, 0, 0, 0)),
        out_shape=jax.ShapeDtypeStruct((B, n_blocks, T, LANES), BF16),
        compiler_params=_params("arbitrary"),
        name="neg_cumsum",
    )(lf)


def _route_t(h, wr_t, br):
    tm = h.shape[0]
    h_hi = h.astype(BF16)
    h_lo = (h - h_hi.astype(F32)).astype(BF16)
    w_pieces = jnp.concatenate(_split3(wr_t), axis=0)
    parts = _dot_nt(w_pieces, h_hi) + _dot_nt(w_pieces, h_lo)
    logits = parts[0:N_EXPERTS] + parts[N_EXPERTS:2 * N_EXPERTS] + parts[2 * N_EXPERTS:]
    s = jax.nn.sigmoid(logits)
    sel = s + br
    row = lax.broadcasted_iota(jnp.int32, (N_EXPERTS, tm), 0).astype(F32)
    neg = -jnp.inf

    def first_argmax(vals):
        top = jnp.max(vals, axis=0, keepdims=True)
        idx = jnp.min(jnp.where(vals == top, row, float(N_EXPERTS)), axis=0, keepdims=True)
        return top, idx

    best = None
    for g in range(N_GROUPS):
        in_g = (row >= g * EXPERTS_PER_GROUP) & (row < (g + 1) * EXPERTS_PER_GROUP)
        top1, i1 = first_argmax(jnp.where(in_g, sel, neg))
        top2, i2 = first_argmax(jnp.where(in_g & (row != i1), sel, neg))
        cand = (top1 + top2, i1, i2, jnp.full((1, tm), float(g), F32))
        if best is None:
            best = cand
        else:
            better = cand[0] > best[0]
            best = tuple(jnp.where(better, new, old) for new, old in zip(cand, best))
    _, i1, i2, group = best
    picked = jnp.where((row == i1) | (row == i2), s, 0.0)
    return picked / jnp.sum(picked, axis=0, keepdims=True), group


def _moe_kernel(*refs, final, n_mixed):
    mixed_refs, refs = refs[:n_mixed], refs[n_mixed:]
    if final:
        (wo_ref, g1_ref, x_ref, g_ref, sc_ref, sh_ref, g2_ref, wrt_ref, br_ref, wg_ref, wu_ref, wd_ref, gf_ref,
         o_ref, h_ref, xg_ref, yg_ref, gg_ref, gate_ref, pos_ref, before_ref, n_ref) = refs
    else:
        (wo_ref, g1_ref, x_ref, g_ref, sc_ref, sh_ref, g2_ref, wrt_ref, br_ref, wg_ref, wu_ref, wd_ref,
         o_ref, h_ref, xg_ref, yg_ref, gg_ref, gate_ref, pos_ref, before_ref, n_ref) = refs
    bb, tt, d = x_ref.shape
    tm = bb * tt
    step = pl.program_id(2)
    first_expert = step * EXPERTS_PER_STEP
    group = first_expert // EXPERTS_PER_GROUP

    @pl.when((pl.program_id(0) == 0) & (pl.program_id(1) == 0) & (step == 0))
    def _():
        def fill(j, carry):
            rows = pl.ds(pl.multiple_of(j * MOE_CHUNK, MOE_CHUNK), MOE_CHUNK)
            r = lax.broadcasted_iota(jnp.int32, (MOE_CHUNK, tm), 0) + j * MOE_CHUNK
            c = lax.broadcasted_iota(jnp.int32, (MOE_CHUNK, tm), 1)
            before_ref[rows, :] = jnp.where(r < c, 1.0, 0.0).astype(BF16)
            return carry
        lax.fori_loop(0, tm // MOE_CHUNK, fill, 0)

    @pl.when(step == 0)
    def _():
        proj, off = None, 0
        for a_ref in mixed_refs:
            width = a_ref.shape[2]
            part = _dot(a_ref[...].reshape(tm, width), wo_ref[off:off + width, :])
            proj = part if proj is None else proj + part
            off += width
        x1 = x_ref[...] + g1_ref[...] * proj.reshape(bb, tt, d)
        o_ref[...] = x1
        h = _norm_mod(x1, g_ref[...], sc_ref[...], sh_ref[...])
        h_ref[...] = h.astype(BF16)
        gates, grp = _route_t(h, wrt_ref[...], br_ref[...])
        gates = jnp.concatenate([gates, jnp.zeros((LANES - N_EXPERTS, tm), F32)], axis=0)
        hi = gates.astype(BF16)
        gate_ref[0] = hi
        gate_ref[1] = (gates - hi.astype(F32)).astype(BF16)
        grow = lax.broadcasted_iota(jnp.int32, pos_ref.shape, 0).astype(F32)
        member = jnp.where(grow == grp, 1.0, 0.0)
        rank = _dot(member.astype(BF16), before_ref[...])
        pos_ref[...] = jnp.where(member > 0.0, rank, -1.0)
        for g in range(N_GROUPS):
            count = jnp.sum(member[g:g + 1, :]).astype(jnp.int32)
            n_ref[g] = (count + MOE_CHUNK - 1) // MOE_CHUNK

    n_chunks = n_ref[group]

    def chunk_rows(j):
        return pl.ds(pl.multiple_of(j * MOE_CHUNK, MOE_CHUNK), MOE_CHUNK)

    def one_hot(j, n_rows=MOE_CHUNK):
        want = (lax.broadcasted_iota(jnp.int32, (n_rows, tm), 0) + j * n_rows).astype(F32)
        return jnp.where(pos_ref[pl.ds(group, 1), :] == want, 1.0, 0.0).astype(BF16)

    @pl.when(first_expert % EXPERTS_PER_GROUP == 0)
    def _():
        def gather(j, carry):
            rows = pl.ds(pl.multiple_of(j * 2 * MOE_CHUNK, 2 * MOE_CHUNK), 2 * MOE_CHUNK)
            pick = one_hot(j, 2 * MOE_CHUNK)
            xg_ref[rows, :] = _dot(pick, h_ref[...]).astype(BF16)
            gg_ref[rows, :] = _dot_nt(pick, gate_ref[0]) + _dot_nt(pick, gate_ref[1])
            yg_ref[rows, :] = jnp.zeros((2 * MOE_CHUNK, d), F32)
            return carry
        lax.fori_loop(0, (n_chunks + 1) // 2, gather, 0)

    local = range(EXPERTS_PER_STEP)
    wd_all = wd_ref[0].reshape(EXPERTS_PER_STEP * D_EXPERT, d)
    lane = lax.broadcasted_iota(jnp.int32, (MOE_CHUNK, LANES), 1)

    def expert(chunks):
        rows = [chunk_rows(j) for j in chunks]
        xs = [xg_ref[r, :] for r in rows]
        gts = [[_dot(x, wg_ref[0, u]) for u in local] for x in xs]
        ups = [[_dot(x, wu_ref[0, u]) for u in local] for x in xs]
        acts = []
        for r, gt, up in zip(rows, gts, ups):
            gg = gg_ref[r, :]
            parts = []
            for u in local:
                ge = jnp.sum(jnp.where(lane == first_expert + u, gg, 0.0), axis=1, keepdims=True)
                parts.append(((gt[u] * jax.nn.sigmoid(gt[u])) * up[u] * ge).astype(BF16))
            acts.append(jnp.concatenate(parts, axis=1))
        for r, a in zip(rows, acts):
            yg_ref[r, :] += _dot(a, wd_all)

    def expert_pair(j, carry):
        expert((2 * j, 2 * j + 1))
        return carry

    lax.fori_loop(0, n_chunks // 2, expert_pair, 0)

    @pl.when(n_chunks % 2 == 1)
    def _():
        expert((n_chunks - 1,))

    @pl.when((first_expert + EXPERTS_PER_STEP) % EXPERTS_PER_GROUP == 0)
    def _():
        def scatter(j, carry):
            rows = pl.ds(pl.multiple_of(j * 2 * MOE_CHUNK, 2 * MOE_CHUNK), 2 * MOE_CHUNK)
            back = lax.dot_general(one_hot(j, 2 * MOE_CHUNK), yg_ref[rows, :].astype(BF16),
                                   (((0,), (0,)), ((), ())), preferred_element_type=F32)
            o_ref[...] += g2_ref[...] * back.reshape(bb, tt, d)
            return carry
        lax.fori_loop(0, (n_chunks + 1) // 2, scatter, 0)

    if final:
        @pl.when(step == N_EXPERTS // EXPERTS_PER_STEP - 1)
        def _():
            y = o_ref[...]
            ms = jnp.mean(y * y, axis=-1, keepdims=True)
            o_ref[...] = y * lax.rsqrt(ms + EPS) * gf_ref[...]


def _moe(mixed, w_out, g1, x, g, sc, sh, g2, w_router, b_router, w_gate, w_up, w_down, layer, g_final, bb, tt):
    B, T, _ = x.shape
    final = g_final is not None
    x_spec = pl.BlockSpec((bb, tt, D_MODEL), lambda i, j, e: (i, j, 0))
    g_spec = pl.BlockSpec((1, D_MODEL), lambda i, j, e: (0, 0))
    m_spec = pl.BlockSpec((bb, 1, D_MODEL), lambda i, j, e: (i, 0, 0))
    in_specs = [pl.BlockSpec((bb, tt, a.shape[2]), lambda i, j, e: (i, j, 0)) for a in mixed]
    in_specs += [pl.BlockSpec(w_out.shape, lambda i, j, e: (0, 0)), m_spec]
    in_specs += [x_spec, g_spec, m_spec, m_spec, m_spec,
                pl.BlockSpec((N_EXPERTS, D_MODEL), lambda i, j, e: (0, 0)),
                pl.BlockSpec((N_EXPERTS, 1), lambda i, j, e: (0, 0)),
                pl.BlockSpec((1, EXPERTS_PER_STEP, D_MODEL, D_EXPERT), lambda i, j, e: (layer, e, 0, 0)),
                pl.BlockSpec((1, EXPERTS_PER_STEP, D_MODEL, D_EXPERT), lambda i, j, e: (layer, e, 0, 0)),
                pl.BlockSpec((1, EXPERTS_PER_STEP, D_EXPERT, D_MODEL), lambda i, j, e: (layer, e, 0, 0))]
    args = [*mixed, w_out, g1, x, g, sc, sh, g2, w_router.T, b_router.reshape(N_EXPERTS, 1), w_gate, w_up, w_down]
    if final:
        in_specs.append(g_spec)
        args.append(g_final)
    tm = bb * tt
    assert tm % (2 * MOE_CHUNK) == 0
    packed = tm
    scratch = [pltpu.VMEM((tm, D_MODEL), BF16),
               pltpu.VMEM((packed, D_MODEL), BF16),
               pltpu.VMEM((packed, D_MODEL), F32),
               pltpu.VMEM((packed, LANES), F32),
               pltpu.VMEM((2, LANES, tm), BF16),
               pltpu.VMEM((2 * N_GROUPS, tm), F32),
               pltpu.VMEM((tm, tm), BF16),
               pltpu.SMEM((N_GROUPS,), jnp.int32)]
    return pl.pallas_call(
        functools.partial(_moe_kernel, final=final, n_mixed=len(mixed)),
        grid=(B // bb, T // tt, N_EXPERTS // EXPERTS_PER_STEP),
        in_specs=in_specs,
        out_specs=x_spec,
        out_shape=jax.ShapeDtypeStruct(x.shape, F32),
        scratch_shapes=scratch,
        compiler_params=_params("arbitrary", "arbitrary", "arbitrary"),
        name="moe",
    )(*args)


def _rope_tables(pos):
    half = ROPE_DIM // 2
    inv = ROPE_THETA ** (-jnp.arange(half, dtype=F32) / half)
    ang = pos.astype(F32)[:, None] * inv[None, :]
    cos, sin = jnp.cos(ang), jnp.sin(ang)
    ones = jnp.ones((pos.shape[0], HEAD_DIM - ROPE_DIM), F32)
    zeros = jnp.zeros_like(ones)
    zh = jnp.zeros_like(sin)
    c64 = jnp.concatenate([cos, cos, ones], axis=1)
    s1_64 = jnp.concatenate([-sin, zh, zeros], axis=1)
    s2_64 = jnp.concatenate([zh, sin, zeros], axis=1)
    return tuple(jnp.concatenate([t, t], axis=1) for t in (c64, s1_64, s2_64))


def _pad_time(a, total):
    if total == a.shape[1]:
        return a
    return jnp.pad(a, ((0, 0), (0, total - a.shape[1])) + ((0, 0),) * (a.ndim - 2))


def _trunk(x, mod, past, wts, cfg):
    B, T, _ = x.shape
    P = 0 if past is None else past["diff_k"].shape[2]
    bb, tt_proj, tt_moe, tq, tk = cfg
    wide = past is None and bb == 1 and tt_proj == tk
    n_valid = P + T
    if past is None:
        Tk = -(-n_valid // tk) * tk
    else:
        assert T <= tk
        Tk = P + PREP_BLOCKS * tk
    Tq = -(-T // tq) * tq
    tables = _rope_tables(P + jnp.arange(T))
    if bb > 1:
        tables = tuple(jnp.tile(t, (bb, 1)) for t in tables)
    attn = functools.partial(_attention, n_chain=N_CHAIN, tq=tq, tk=tk, q_off=P, n_valid=n_valid)
    leaves_even, leaves_odd = [], []

    def with_cache(qkv, k_cols, v_cols, cache_k, cache_v):
        new_k = _pad_time(qkv[:, :, k_cols[0]:k_cols[1]], Tk - P)
        new_vt = _key_blocks_t(_pad_time(qkv[:, :, v_cols[0]:v_cols[1]], Tk - P), tk)
        return _cache_prep(cache_k, cache_v, new_k, new_vt, tk)

    def heads_last(leaf_t, n_heads):
        return leaf_t.reshape(B, n_heads, -1, T).transpose(0, 3, 1, 2)

    for l in range(DEPTH):
        sh1, sc1, g1, sh2, sc2, g2 = [mod[l, :, k * D_MODEL:(k + 1) * D_MODEL][:, None, :] for k in range(6)]
        i = l // 2
        g_mix = wts["norm_mix"][l][None, :]
        if l % 2 == 0:
            lam_init = 0.8 - 0.6 * math.exp(-0.3 * l)
            lp = jnp.stack([wts["diff_lq1"][i], wts["diff_lk1"][i], wts["diff_lq2"][i], wts["diff_lk2"][i]])
            diff_extras = (lp, wts["diff_subln"][i][:, None])
            if wide:
                qk, ak, av, bkt, bvt, dvt, svt = _proj_even_wide(x, g_mix, sc1, sh1, wts["w_in_even"][i], tables, tt_proj)
                leaves_even.append((ak.reshape(B, T, N_DIFF_HEADS, LANES), av.reshape(B, T, N_DIFF_HEADS, LANES),
                                    heads_last(bkt, N_SB_HEADS), heads_last(bvt, N_SB_HEADS)))
                q_arr, q_cols = qk, (0, 2 * SEG // LANES)
                dk, dk_col, sk, sk_col = qk, SEG // LANES, qk, 3 * SEG // LANES
            else:
                qkv, ak, av, bk, bv = _proj_even(x, g_mix, sc1, sh1, wts["w_in_even"][i], tables, bb, tt_proj)
                leaves_even.append((ak.reshape(B, T, N_DIFF_HEADS, LANES), av.reshape(B, T, N_DIFF_HEADS, LANES),
                                    bk.reshape(B, T, N_SB_HEADS, HEAD_DIM), bv.reshape(B, T, N_SB_HEADS, HEAD_DIM)))
                q_arr, q_cols = _pad_time(qkv, Tq), (0, 3 * SEG // LANES)
                dk, dvt = with_cache(qkv, (SEG, 2 * SEG), (2 * SEG, 3 * SEG), past["diff_k"][i], past["diff_v"][i])
                sk, svt = with_cache(qkv, (4 * SEG, 5 * SEG), (5 * SEG, 6 * SEG), past["sb_k"][i], past["sb_v"][i])
                dk_col = sk_col = 0
            a_out = attn("diff", q_arr, q_cols[0], dk, dk_col, dvt, N_DIFF_HEADS, extras=diff_extras, lam_init=lam_init)
            b_out = attn("sb", q_arr, q_cols[1], sk, sk_col, svt, N_SB_HEADS // 2)
            mixed, w_out = [a_out[:, :T], b_out[:, :T]], wts["w_out_even"][i]
        else:
            if wide:
                qk, kt, vt, fvt, lf = _proj_odd_wide(x, g_mix, sc1, sh1, wts["w_in_odd"][i], wts["w_f_odd"][i],
                                                     wts["b_forget"][i], tt_proj)
                leaves_odd.append((heads_last(kt, N_FOX_HEADS), heads_last(vt, N_FOX_HEADS), lf))
                q_arr, fk, fk_col, lf_all = qk, qk, 2 * SEG // LANES, lf
            else:
                qkv, k, v, lf = _proj_odd(x, g_mix, sc1, sh1, wts["w_in_odd"][i], wts["w_f_odd"][i],
                                          wts["b_forget"][i], bb, tt_proj)
                leaves_odd.append((k.reshape(B, T, N_FOX_HEADS, HEAD_DIM), v.reshape(B, T, N_FOX_HEADS, HEAD_DIM), lf))
                q_arr = _pad_time(qkv, Tq)
                fk, fvt = with_cache(qkv, (2 * SEG, 4 * SEG), (4 * SEG, 6 * SEG), past["fox_k"][i], past["fox_v"][i])
                fk_col = 0
                lf_all = jnp.concatenate([past["fox_logf"][i], lf], axis=1)
            nd = _neg_cumsum(_pad_time(lf_all, Tk), tk)
            o = attn("fox", q_arr, 0, fk, fk_col, fvt, N_FOX_HEADS // 2, extras=(nd,))
            mixed, w_out = [o[:, :T]], wts["w_out_odd"][i]
        g_final = wts["norm_final"][None, :] if l == DEPTH - 1 else None
        x = _moe(mixed, w_out, g1, x, wts["norm_ffn"][l][None, :], sc2, sh2, g2, wts["w_router"], wts["b_router"],
                 wts["w_gate"], wts["w_up"], wts["w_down"], l, g_final, bb, tt_moe)
    return x, leaves_even, leaves_odd


def kernel(x_prompt, x_sample, c_prompt, c_sample, cache_diff_k, cache_diff_v, cache_sb_k, cache_sb_v, cache_fox_k, cache_fox_v, cache_fox_logf, w_ada, b_ada, norm_mix, norm_ffn, w_in_even, w_out_even, diff_lq1, diff_lk1, diff_lq2, diff_lk2, diff_subln, w_in_odd, b_forget, w_out_odd, w_router, b_router, w_gate, w_up, w_down, norm_final):
    Bp, Tp, _ = x_prompt.shape
    Bs, Ts, _ = x_sample.shape
    wts = {
        "norm_mix": norm_mix, "norm_ffn": norm_ffn, "norm_final": norm_final,
        "w_in_even": w_in_even.astype(BF16), "w_out_even": w_out_even.astype(BF16),
        "w_in_odd": w_in_odd[:, :, :3 * FOX_W].astype(BF16), "w_f_odd": w_in_odd[:, :, 3 * FOX_W:].astype(BF16),
        "w_out_odd": w_out_odd.astype(BF16), "b_forget": b_forget,
        "diff_lq1": diff_lq1, "diff_lk1": diff_lk1, "diff_lq2": diff_lq2, "diff_lk2": diff_lk2,
        "diff_subln": diff_subln, "w_router": w_router, "b_router": b_router,
        "w_gate": w_gate.astype(BF16), "w_up": w_up.astype(BF16), "w_down": w_down.astype(BF16),
    }
    past = {"diff_k": cache_diff_k, "diff_v": cache_diff_v, "sb_k": cache_sb_k, "sb_v": cache_sb_v,
            "fox_k": cache_fox_k, "fox_v": cache_fox_v, "fox_logf": cache_fox_logf}

    rows = Bp + Bs
    rows_pad = -(-rows // 16) * 16
    c_all = jnp.pad(jnp.concatenate([c_prompt, c_sample], axis=0), ((0, rows_pad - rows), (0, 0)))
    mod = _ada_mod(c_all, w_ada, b_ada)

    cfg_prompt = (1, 256, 1024, 256, 256)
    cfg_sample = (Bs, Ts, Ts, 64, 256)
    y_p, even_p, odd_p = _trunk(x_prompt, mod[:, :Bp], None, wts, cfg_prompt)
    y_s, even_s, odd_s = _trunk(x_sample, mod[:, Bp:rows], past, wts, cfg_sample)

    def layers(rows):
        return rows[0][None] if len(rows) == 1 else jnp.stack(rows, axis=0)

    def pack(even, odd):
        return tuple(layers(r) for r in zip(*even)) + tuple(layers(r) for r in zip(*odd))

    return (y_p, y_s) + pack(even_p, odd_p) + pack(even_s, odd_s)
```

```python
import functools
import math

import jax
import jax.numpy as jnp
from jax import lax
from jax.experimental import pallas as pl
from jax.experimental.pallas import tpu as pltpu

D_MODEL = 1024
DEPTH = 2
CHUNK = 64
HEAD_DIM = 64
N_DIFF_HEADS = D_MODEL // 256
N_SB_HEADS = D_MODEL // 128
N_FOX_HEADS = D_MODEL // HEAD_DIM
DIFF_W = N_DIFF_HEADS * 2 * HEAD_DIM
SB_W = N_SB_HEADS * HEAD_DIM
FOX_W = N_FOX_HEADS * HEAD_DIM
ROPE_DIM = HEAD_DIM // 4
ROPE_THETA = 500000.0
N_EXPERTS = 16
N_GROUPS = 4
EXPERTS_PER_GROUP = N_EXPERTS // N_GROUPS
D_EXPERT = D_MODEL // 2
EPS = 1e-6

LANES = 128
BF16_ROWS = 16
SEG = 512
QK_SCALE = HEAD_DIM ** -0.5
MASKED = -1e30
N_CHAIN = 4
SB_DEAD = -110.0
MOE_CHUNK = 128
EXPERTS_PER_STEP = 2
PREP_BLOCKS = 2
VMEM_LIMIT = 56 * 1024 * 1024

F32 = jnp.float32
BF16 = jnp.bfloat16


def _params(*sem):
    return pltpu.CompilerParams(dimension_semantics=sem, vmem_limit_bytes=VMEM_LIMIT)


def _dot(a, b):
    return jnp.dot(a, b, preferred_element_type=F32)


def _dot_nt(a, b):
    return lax.dot_general(a, b, (((1,), (1,)), ((), ())), preferred_element_type=F32)


def _norm_mod(x, g, sc, sh):
    bb, tt, d = x.shape
    ms = jnp.mean(x * x, axis=-1, keepdims=True)
    h = x * lax.rsqrt(ms + EPS) * g
    h = h * (1.0 + sc) + sh
    return h.reshape(bb * tt, d)


def _log_sigmoid(x):
    return jnp.minimum(x, 0.0) - jnp.log(1.0 + jnp.exp(-jnp.abs(x)))


def _split3(x):
    hi = x.astype(BF16)
    r1 = x - hi.astype(F32)
    mid = r1.astype(BF16)
    lo = (r1 - mid.astype(F32)).astype(BF16)
    return hi, mid, lo


def _ada_kernel(c_ref, w_ref, b_ref, o_ref):
    c = c_ref[...]
    a = (c * jax.nn.sigmoid(c)).astype(BF16)
    o_ref[0] = _dot(a, w_ref[0].astype(BF16)) + b_ref[0]


def _ada_mod(c_all, w_ada, b_ada):
    rows = c_all.shape[0]
    tn = 1536
    width = 6 * D_MODEL
    return pl.pallas_call(
        _ada_kernel,
        grid=(DEPTH, width // tn),
        in_specs=[
            pl.BlockSpec((rows, D_MODEL), lambda l, j: (0, 0)),
            pl.BlockSpec((1, D_MODEL, tn), lambda l, j: (l, 0, j)),
            pl.BlockSpec((1, 1, tn), lambda l, j: (l, 0, j)),
        ],
        out_specs=pl.BlockSpec((1, rows, tn), lambda l, j: (l, 0, j)),
        out_shape=jax.ShapeDtypeStruct((DEPTH, rows, width), F32),
        compiler_params=_params("arbitrary", "arbitrary"),
        name="ada_mod",
    )(c_all, w_ada, b_ada.reshape(DEPTH, 1, width))


def _rope(p, cos, s1, s2):
    outs = []
    for c in range(SEG // LANES):
        pc = p[:, c * LANES:(c + 1) * LANES]
        up = pltpu.roll(pc, LANES - ROPE_DIM // 2, 1)
        down = pltpu.roll(pc, ROPE_DIM // 2, 1)
        outs.append(pc * cos + up * s1 + down * s2)
    return jnp.concatenate(outs, axis=1)


def _proj_even_kernel(x_ref, g_ref, sc_ref, sh_ref, w_ref, cos_ref, s1_ref, s2_ref,
                      qkv_ref, ak_ref, av_ref, bk_ref, bv_ref):
    bb, tt, _ = x_ref.shape
    h = _norm_mod(x_ref[...], g_ref[...], sc_ref[...], sh_ref[...]).astype(BF16)
    cos, s1, s2 = cos_ref[...], s1_ref[...], s2_ref[...]

    def seg(i):
        return _dot(h, w_ref[:, i * SEG:(i + 1) * SEG])

    def put(i, val, leaf_ref):
        val3 = val.reshape(bb, tt, SEG)
        qkv_ref[:, :, i * SEG:(i + 1) * SEG] = val3.astype(BF16)
        if leaf_ref is not None:
            leaf_ref[...] = val3

    put(0, _rope(seg(0), cos, s1, s2) * QK_SCALE, None)
    put(1, _rope(seg(1), cos, s1, s2), ak_ref)
    put(2, seg(2), av_ref)
    put(3, seg(3) * QK_SCALE, None)
    put(4, seg(4), bk_ref)
    put(5, seg(5), bv_ref)


def _rows_by_head(ref, val, n_heads):
    tt = val.shape[0]
    for hd in range(n_heads):
        ref[0, pl.ds(hd, tt, stride=n_heads), :] = val[:, hd * LANES:(hd + 1) * LANES]


def _key_blocks(ref, first, val_t):
    n, tk = val_t.shape[0] // LANES, ref.shape[4]
    for u in range(ref.shape[2]):
        ref[0, first:first + n, u] = val_t[:, u * tk:(u + 1) * tk].reshape(n, LANES, tk).astype(BF16)


def _proj_even_wide_kernel(x_ref, g_ref, sc_ref, sh_ref, w_ref, cos_ref, s1_ref, s2_ref,
                           qk_ref, ak_ref, av_ref, bkt_ref, bvt_ref, dvt_ref, svt_ref):
    h = _norm_mod(x_ref[...], g_ref[...], sc_ref[...], sh_ref[...]).astype(BF16)
    cos, s1, s2 = cos_ref[...], s1_ref[...], s2_ref[...]

    def seg(i):
        return _dot(h, w_ref[:, i * SEG:(i + 1) * SEG])

    qk_ref[0, :, 0:SEG] = (_rope(seg(0), cos, s1, s2) * QK_SCALE).astype(BF16)
    ak = _rope(seg(1), cos, s1, s2)
    qk_ref[0, :, SEG:2 * SEG] = ak.astype(BF16)
    _rows_by_head(ak_ref, ak, N_DIFF_HEADS)
    av = seg(2)
    _rows_by_head(av_ref, av, N_DIFF_HEADS)
    _key_blocks(dvt_ref, 0, av.T)
    qk_ref[0, :, 2 * SEG:3 * SEG] = (seg(3) * QK_SCALE).astype(BF16)
    bk = seg(4)
    qk_ref[0, :, 3 * SEG:4 * SEG] = bk.astype(BF16)
    bkt_ref[0] = bk.T
    bvt = seg(5).T
    bvt_ref[0] = bvt
    _key_blocks(svt_ref, 0, bvt)


def _proj_odd_wide_kernel(x_ref, g_ref, sc_ref, sh_ref, w_ref, wf_ref, bf_ref,
                          qk_ref, kt_ref, vt_ref, fvt_ref, lf_ref):
    h = _norm_mod(x_ref[...], g_ref[...], sc_ref[...], sh_ref[...]).astype(BF16)

    def seg(i):
        return _dot(h, w_ref[:, i * SEG:(i + 1) * SEG])

    for i in range(2):
        qk_ref[0, :, i * SEG:(i + 1) * SEG] = (seg(i) * QK_SCALE).astype(BF16)
    for i in range(2):
        k = seg(2 + i)
        qk_ref[0, :, (2 + i) * SEG:(3 + i) * SEG] = k.astype(BF16)
        kt_ref[0, i * SEG:(i + 1) * SEG, :] = k.T
        v_t = seg(4 + i).T
        vt_ref[0, i * SEG:(i + 1) * SEG, :] = v_t
        _key_blocks(fvt_ref, i * (SEG // LANES), v_t)
    fl = _dot(h, wf_ref[...]) + bf_ref[...]
    lf_ref[0] = _log_sigmoid(fl)


def _proj_odd_kernel(x_ref, g_ref, sc_ref, sh_ref, w_ref, wf_ref, bf_ref,
                     qkv_ref, k_ref, v_ref, lf_ref):
    bb, tt, _ = x_ref.shape
    h = _norm_mod(x_ref[...], g_ref[...], sc_ref[...], sh_ref[...]).astype(BF16)

    def seg(i):
        return _dot(h, w_ref[:, i * SEG:(i + 1) * SEG]).reshape(bb, tt, SEG)

    for i in range(2):
        qkv_ref[:, :, i * SEG:(i + 1) * SEG] = (seg(i) * QK_SCALE).astype(BF16)
    for i in range(2, 6):
        val = seg(i)
        qkv_ref[:, :, i * SEG:(i + 1) * SEG] = val.astype(BF16)
        leaf_ref = k_ref if i < 4 else v_ref
        leaf_ref[:, :, (i % 2) * SEG:(i % 2 + 1) * SEG] = val
    fl = _dot(h, wf_ref[...]) + bf_ref[...]
    lf_ref[...] = _log_sigmoid(fl).reshape(bb, tt, N_FOX_HEADS)


def _row_specs(bb, tt):
    x_spec = pl.BlockSpec((bb, tt, D_MODEL), lambda i, j: (i, j, 0))
    g_spec = pl.BlockSpec((1, D_MODEL), lambda i, j: (0, 0))
    m_spec = pl.BlockSpec((bb, 1, D_MODEL), lambda i, j: (i, 0, 0))
    return x_spec, g_spec, m_spec


def _out_spec(bb, tt, width):
    return pl.BlockSpec((bb, tt, width), lambda i, j: (i, j, 0))


def _proj_even(x, g, sc, sh, w_bf, tables, bb, tt):
    B, T, _ = x.shape
    x_spec, g_spec, m_spec = _row_specs(bb, tt)
    tm = bb * tt
    t_spec = pl.BlockSpec((tm, LANES), (lambda i, j: (j, 0)) if bb == 1 else (lambda i, j: (0, 0)))
    leaf = jax.ShapeDtypeStruct((B, T, SEG), F32)
    return pl.pallas_call(
        _proj_even_kernel,
        grid=(B // bb, T // tt),
        in_specs=[x_spec, g_spec, m_spec, m_spec,
                  pl.BlockSpec(w_bf.shape, lambda i, j: (0, 0)), t_spec, t_spec, t_spec],
        out_specs=[_out_spec(bb, tt, 6 * SEG)] + [_out_spec(bb, tt, SEG)] * 4,
        out_shape=[jax.ShapeDtypeStruct((B, T, 6 * SEG), BF16), leaf, leaf, leaf, leaf],
        compiler_params=_params("arbitrary", "arbitrary"),
        name="proj_even",
    )(x, g, sc, sh, w_bf, *tables)


def _wide_specs(B, T, tt, tk):
    assert tt % tk == 0
    def rows(width, dtype):
        return pl.BlockSpec((1, tt, width), lambda i, j: (i, j, 0)), jax.ShapeDtypeStruct((B, T, width), dtype)

    def by_head(n_heads):
        return (pl.BlockSpec((1, tt * n_heads, LANES), lambda i, j: (i, j, 0)),
                jax.ShapeDtypeStruct((B, T * n_heads, LANES), F32))

    def transposed(width):
        return pl.BlockSpec((1, width, tt), lambda i, j: (i, 0, j)), jax.ShapeDtypeStruct((B, width, T), F32)

    def key_blocks(n_col):
        return (pl.BlockSpec((1, n_col, tt // tk, LANES, tk), lambda i, j: (i, 0, j, 0, 0)),
                jax.ShapeDtypeStruct((B, n_col, T // tk, LANES, tk), BF16))

    return rows, by_head, transposed, key_blocks


def _proj_even_wide(x, g, sc, sh, w_bf, tables, tt, tk):
    B, T, _ = x.shape
    x_spec, g_spec, m_spec = _row_specs(1, tt)
    t_spec = pl.BlockSpec((tt, LANES), lambda i, j: (j, 0))
    rows, by_head, transposed, key_blocks = _wide_specs(B, T, tt, tk)
    outs = [rows(4 * SEG, BF16), by_head(N_DIFF_HEADS), by_head(N_DIFF_HEADS), transposed(SEG), transposed(SEG),
            key_blocks(SEG // LANES), key_blocks(SEG // LANES)]
    return pl.pallas_call(
        _proj_even_wide_kernel,
        grid=(B, T // tt),
        in_specs=[x_spec, g_spec, m_spec, m_spec,
                  pl.BlockSpec(w_bf.shape, lambda i, j: (0, 0)), t_spec, t_spec, t_spec],
        out_specs=[o[0] for o in outs],
        out_shape=[o[1] for o in outs],
        compiler_params=_params("arbitrary", "arbitrary"),
        name="proj_even_wide",
    )(x, g, sc, sh, w_bf, *tables)


def _proj_odd_wide(x, g, sc, sh, w_bf, wf_bf, b_f, tt, tk):
    B, T, _ = x.shape
    x_spec, g_spec, m_spec = _row_specs(1, tt)
    rows, by_head, transposed, key_blocks = _wide_specs(B, T, tt, tk)
    outs = [rows(4 * SEG, BF16), transposed(FOX_W), transposed(FOX_W), key_blocks(FOX_W // LANES),
            rows(N_FOX_HEADS, F32)]
    return pl.pallas_call(
        _proj_odd_wide_kernel,
        grid=(B, T // tt),
        in_specs=[x_spec, g_spec, m_spec, m_spec,
                  pl.BlockSpec(w_bf.shape, lambda i, j: (0, 0)),
                  pl.BlockSpec(wf_bf.shape, lambda i, j: (0, 0)),
                  pl.BlockSpec((1, N_FOX_HEADS), lambda i, j: (0, 0))],
        out_specs=[o[0] for o in outs],
        out_shape=[o[1] for o in outs],
        compiler_params=_params("arbitrary", "arbitrary"),
        name="proj_odd_wide",
    )(x, g, sc, sh, w_bf, wf_bf, b_f.reshape(1, N_FOX_HEADS))


def _proj_odd(x, g, sc, sh, w_bf, wf_bf, b_f, bb, tt):
    B, T, _ = x.shape
    x_spec, g_spec, m_spec = _row_specs(bb, tt)
    leaf = jax.ShapeDtypeStruct((B, T, FOX_W), F32)
    return pl.pallas_call(
        _proj_odd_kernel,
        grid=(B // bb, T // tt),
        in_specs=[x_spec, g_spec, m_spec, m_spec,
                  pl.BlockSpec(w_bf.shape, lambda i, j: (0, 0)),
                  pl.BlockSpec(wf_bf.shape, lambda i, j: (0, 0)),
                  pl.BlockSpec((1, N_FOX_HEADS), lambda i, j: (0, 0))],
        out_specs=[_out_spec(bb, tt, 6 * SEG), _out_spec(bb, tt, FOX_W), _out_spec(bb, tt, FOX_W),
                   _out_spec(bb, tt, N_FOX_HEADS)],
        out_shape=[jax.ShapeDtypeStruct((B, T, 6 * SEG), BF16), leaf, leaf,
                   jax.ShapeDtypeStruct((B, T, N_FOX_HEADS), F32)],
        compiler_params=_params("arbitrary", "arbitrary"),
        name="proj_odd",
    )(x, g, sc, sh, w_bf, wf_bf, b_f.reshape(1, N_FOX_HEADS))


def _attn_kernel(*refs, kind, n_chain, tq, tk, q_off, n_valid, lam_init):
    if kind == "diff":
        q_ref, k_ref, vt_ref, lp_ref, sg_ref, o_ref, acc_ref, m_ref, s_ref = refs
    elif kind == "fox":
        q_ref, k_ref, vt_ref, nd_ref, o_ref, acc_ref, m_ref, s_ref = refs
    else:
        q_ref, k_ref, vt_ref, o_ref, acc_ref, m_ref, s_ref = refs
    tq2 = 2 * tq
    q0 = q_off + pl.program_id(2) * tq

    low = lax.broadcasted_iota(jnp.int32, (tq, LANES), 1) < HEAD_DIM
    if kind == "fox":
        row = lax.broadcasted_iota(jnp.int32, (tq2, LANES), 0)
        col = lax.broadcasted_iota(jnp.int32, (tq2, LANES), 1)
        first = jnp.right_shift(row, tq.bit_length() - 1) * 3
    qqs = []
    for c in range(n_chain):
        q = q_ref[0, :, c * LANES:(c + 1) * LANES]
        zero = jnp.zeros_like(q)
        qq = jnp.concatenate([jnp.where(low, q, zero), jnp.where(low, zero, q)], axis=0)
        if kind == "fox":
            pick = jnp.logical_and(first + 6 * c <= col, col < first + 6 * c + 3)
            qq = jnp.concatenate([qq, jnp.where(pick, 1.0, 0.0).astype(BF16)], axis=1)
        qqs.append(qq)

    acc_ref[...] = jnp.zeros_like(acc_ref)
    if kind == "sb":
        m_ref[...] = jnp.zeros_like(m_ref)
        tri_r = lax.broadcasted_iota(jnp.int32, (tk, tk), 0)
        tri_c = lax.broadcasted_iota(jnp.int32, (tk, tk), 1)
        after = jnp.where(tri_c > tri_r, 1.0, 0.0).astype(BF16)
        after2 = jnp.concatenate([after, after], axis=1)
    else:
        m_ref[...] = jnp.full_like(m_ref, MASKED)
        ones_rows = jnp.ones((BF16_ROWS, tk), BF16)

    qpos = q0 + jnp.bitwise_and(lax.broadcasted_iota(jnp.int32, (tk, tq2), 1), tq - 1)
    kidx = lax.broadcasted_iota(jnp.int32, (tk, tq2), 0)

    chains = range(n_chain)

    def qk(kb, slot):
        start = pl.multiple_of(kb * tk, tk)
        for c in chains:
            k = k_ref[0, pl.ds(start, tk), c * LANES:(c + 1) * LANES]
            if kind == "fox":
                k = jnp.concatenate([k, nd_ref[0, 0, pl.ds(start, tk), :]], axis=1)
            s_ref[slot, c] = _dot_nt(k, qqs[c])

    def consume(kb, slot, masked):
        scores = [s_ref[slot, c] for c in chains]
        if masked:
            kpos = kb * tk + kidx
            if kind == "diff":
                vis = jnp.right_shift(kpos, 6) <= jnp.right_shift(qpos, 6)
            elif kind == "fox":
                vis = kpos <= qpos
            else:
                vis = kpos < qpos
            vis = jnp.logical_and(vis, kpos < n_valid)
        vts = [vt_ref[0, c, kb] for c in chains]
        if kind == "sb":
            log_betas, laters = [], []
            for c in chains:
                s = scores[c]
                soft = jnp.log(1.0 + jnp.exp(-jnp.abs(s)))
                log_beta = jnp.minimum(s, 0.0) - soft
                log_keep = log_beta - s
                if masked:
                    log_keep = jnp.where(vis, log_keep, 0.0)
                hi = log_keep.astype(BF16)
                lo = (log_keep - hi.astype(F32)).astype(BF16)
                later = _dot(after2, jnp.concatenate([hi, lo], axis=0))
                laters.append(later + m_ref[c])
                m_ref[c] += later[0:1, :] + log_keep[0:1, :]
                log_betas.append(log_beta)
            weights = []
            for c in chains:
                w = jnp.exp(log_betas[c] + laters[c])
                if masked:
                    w = jnp.where(vis, w, 0.0)
                weights.append(w.astype(BF16))
            for c in chains:
                acc_ref[c] += _dot(vts[c], weights[c])
        else:
            probs, alphas = [], []
            for c in chains:
                s = scores[c]
                if masked:
                    s = jnp.where(vis, s, MASKED)
                m_prev = m_ref[c]
                m_new = jnp.maximum(m_prev, jnp.max(s, axis=0, keepdims=True))
                alphas.append(jnp.exp(m_prev - m_new))
                probs.append(jnp.exp(s - m_new).astype(BF16))
                m_ref[c] = m_new
            for c in chains:
                vt_aug = jnp.concatenate([vts[c], ones_rows], axis=0)
                acc_ref[c] = alphas[c] * acc_ref[c] + _dot(vt_aug, probs[c])

    n_full = q0 // tk
    qk(n_full, 0)
    qk(jnp.maximum(n_full - 1, 0), 1)
    consume(n_full, 0, True)

    def pair(j, carry):
        kb = n_full - 1 - 2 * j
        qk(jnp.maximum(kb - 1, 0), 0)
        consume(kb, 1, False)
        qk(jnp.maximum(kb - 2, 0), 1)
        consume(kb - 1, 0, False)
        return carry

    if kind == "sb":
        def top_sum():
            top = m_ref[0]
            for c in range(1, n_chain):
                top = jnp.maximum(top, m_ref[c])
            return jnp.max(top)

        def alive(carry):
            return jnp.logical_and(carry[0] < n_full // 2, carry[1] > SB_DEAD)

        def pair_and_check(carry):
            pair(carry[0], 0)
            return carry[0] + 1, top_sum()

        _, top = lax.while_loop(alive, pair_and_check, (jnp.int32(0), top_sum()))
        odd_block_left = jnp.logical_and(n_full % 2 == 1, top > SB_DEAD)
    else:
        lax.fori_loop(0, n_full // 2, pair, 0)
        odd_block_left = n_full % 2 == 1

    @pl.when(odd_block_left)
    def _():
        consume(0, 1, False)

    if kind == "diff":
        lp = lp_ref[...]
        lam = (jnp.exp(jnp.sum(lp[0:1] * lp[1:2], axis=1, keepdims=True))
               - jnp.exp(jnp.sum(lp[2:3] * lp[3:4], axis=1, keepdims=True)) + lam_init)
    for c in range(n_chain):
        acc = acc_ref[c]
        if kind == "sb":
            out_t = jnp.concatenate([acc[0:HEAD_DIM, 0:tq], acc[HEAD_DIM:LANES, tq:tq2]], axis=0)
        else:
            r = 1.0 / acc[LANES:LANES + 1, :]
            if kind == "fox":
                out_t = jnp.concatenate([acc[0:HEAD_DIM, 0:tq] * r[:, 0:tq],
                                         acc[HEAD_DIM:LANES, tq:tq2] * r[:, tq:tq2]], axis=0)
            else:
                a = acc[0:LANES, 0:tq] * r[:, 0:tq] - lam * (acc[0:LANES, tq:tq2] * r[:, tq:tq2])
                ms = jnp.mean(a * a, axis=0, keepdims=True)
                out_t = a * lax.rsqrt(ms + EPS) * sg_ref[...] * (1.0 - lam_init)
        if tq < LANES:
            out_t = jnp.concatenate([out_t, jnp.zeros((LANES, LANES - tq), F32)], axis=1)
        o_ref[0, :, c * LANES:(c + 1) * LANES] = out_t.T[0:tq].astype(o_ref.dtype)


def _attention(kind, q_arr, q_col, k_arr, k_col, vt_arr, n_col, *, n_chain, tq, tk, q_off, n_valid,
               extras=(), lam_init=0.0):
    B, Tq, _ = q_arr.shape
    Tk = k_arr.shape[1]
    n_kb = Tk // tk
    assert Tq % tq == 0 and Tk % tk == 0 and tk % tq == 0 and q_off % tk == 0 and n_valid <= Tk
    assert tq & (tq - 1) == 0 and n_col % n_chain == 0 and q_col % n_chain == 0 and k_col % n_chain == 0
    wide = n_chain * LANES
    in_specs = [
        pl.BlockSpec((1, tq, wide), lambda b, c, i: (b, i, q_col // n_chain + c)),
        pl.BlockSpec((1, Tk, wide), lambda b, c, i: (b, 0, k_col // n_chain + c)),
        pl.BlockSpec((1, n_chain, n_kb, LANES, tk), lambda b, c, i: (b, c, 0, 0, 0)),
    ]
    acc_rows = LANES if kind == "sb" else LANES + BF16_ROWS
    scratch = [pltpu.VMEM((n_chain, acc_rows, 2 * tq), F32), pltpu.VMEM((n_chain, 1, 2 * tq), F32),
               pltpu.VMEM((2, n_chain, tk, 2 * tq), F32)]
    if kind == "diff":
        lp, sg = extras
        in_specs += [pl.BlockSpec(lp.shape, lambda b, c, i: (0, 0)),
                     pl.BlockSpec(sg.shape, lambda b, c, i: (0, 0))]
    elif kind == "fox":
        (nd,) = extras
        in_specs.append(pl.BlockSpec((1, 1, Tk, LANES), lambda b, c, i: (b, c, 0, 0)))
    kernel = functools.partial(_attn_kernel, kind=kind, n_chain=n_chain, tq=tq, tk=tk, q_off=q_off,
                               n_valid=n_valid, lam_init=lam_init)
    return pl.pallas_call(
        kernel,
        grid=(B, n_col // n_chain, Tq // tq),
        in_specs=in_specs,
        out_specs=pl.BlockSpec((1, tq, wide), lambda b, c, i: (b, i, c)),
        out_shape=jax.ShapeDtypeStruct((B, Tq, n_col * LANES), BF16),
        scratch_shapes=scratch,
        compiler_params=_params("arbitrary", "arbitrary", "arbitrary"),
        name="attn_" + kind,
    )(q_arr, k_arr, vt_arr, *extras)


def _key_blocks_t(v, tk):
    B, Tk, W = v.shape
    return v.reshape(B, Tk // tk, tk, W // LANES, LANES).transpose(0, 3, 1, 4, 2)


def _cache_prep_kernel(k_ref, v_ref, nk_ref, nvt_ref, ko_ref, vto_ref, *, n_heads, tk, time_minor):
    step = pl.program_id(1)
    n_past = pl.num_programs(1) - 1

    @pl.when(step < n_past)
    def _():
        for u in range(vto_ref.shape[2]):
            for c in range(ko_ref.shape[2] // LANES):
                cols = slice(c * LANES, (c + 1) * LANES)
                if time_minor:
                    ko_ref[0, u * tk:(u + 1) * tk, cols] = k_ref[0, cols, u * tk:(u + 1) * tk].T.astype(BF16)
                    vto_ref[0, c, u] = v_ref[0, cols, u * tk:(u + 1) * tk].astype(BF16)
                else:
                    head_rows = pl.ds(u * tk * n_heads + c, tk, stride=n_heads)
                    ko_ref[0, u * tk:(u + 1) * tk, cols] = k_ref[0, head_rows, :].astype(BF16)
                    vto_ref[0, c, u] = v_ref[0, head_rows, :].T.astype(BF16)

    @pl.when(step == n_past)
    def _():
        ko_ref[...] = nk_ref[...]
        vto_ref[...] = nvt_ref[...]


def _cache_prep(cache_k, cache_v, new_k, new_vt, tk):
    B, P, H, dh = cache_k.shape
    W = H * dh
    g = PREP_BLOCKS
    n_col, n_past = W // LANES, P // (g * tk)
    assert P % (g * tk) == 0 and dh in (HEAD_DIM, LANES)
    time_minor = dh != LANES
    if time_minor:
        cache_k, cache_v = [c.transpose(0, 2, 3, 1).reshape(B, W, P) for c in (cache_k, cache_v)]
        rows = pl.BlockSpec((1, W, g * tk), lambda b, j: (b, 0, jnp.minimum(j, n_past - 1)))
    else:
        cache_k, cache_v = [c.reshape(B, P * H, dh) for c in (cache_k, cache_v)]
        rows = pl.BlockSpec((1, g * tk * H, dh), lambda b, j: (b, jnp.minimum(j, n_past - 1), 0))
    return pl.pallas_call(
        functools.partial(_cache_prep_kernel, n_heads=H, tk=tk, time_minor=time_minor),
        grid=(B, n_past + 1),
        in_specs=[rows, rows,
                  pl.BlockSpec((1, g * tk, W), lambda b, j: (b, 0, 0)),
                  pl.BlockSpec((1, n_col, g, LANES, tk), lambda b, j: (b, 0, 0, 0, 0))],
        out_specs=[pl.BlockSpec((1, g * tk, W), lambda b, j: (b, j, 0)),
                   pl.BlockSpec((1, n_col, g, LANES, tk), lambda b, j: (b, 0, j, 0, 0))],
        out_shape=[jax.ShapeDtypeStruct((B, P + g * tk, W), BF16),
                   jax.ShapeDtypeStruct((B, n_col, P // tk + g, LANES, tk), BF16)],
        compiler_params=_params("arbitrary", "arbitrary"),
        name="cache_prep",
    )(cache_k, cache_v, new_k, new_vt)


def _cumsum_kernel(x_ref, o_ref, *, blk):
    T, H = x_ref.shape[1:]
    n_blocks = o_ref.shape[1]
    r = lax.broadcasted_iota(jnp.int32, (blk, blk), 0)
    c = lax.broadcasted_iota(jnp.int32, (blk, blk), 1)
    upto = jnp.where(c <= r, 1.0, 0.0).astype(BF16)
    per_block = 2 * N_CHAIN
    src = lax.broadcasted_iota(jnp.int32, (3 * H, n_blocks * LANES), 0)
    col = lax.broadcasted_iota(jnp.int32, (3 * H, n_blocks * LANES), 1)
    head, piece = jnp.bitwise_and(src, H - 1), jnp.right_shift(src, H.bit_length() - 1)
    here = jnp.right_shift(col, 7) == jnp.right_shift(head, per_block.bit_length() - 1)
    lane = jnp.bitwise_and(col, LANES - 1) - 3 * jnp.bitwise_and(head, per_block - 1)
    place = jnp.where(jnp.logical_and(here, lane == piece), 1.0, 0.0).astype(BF16)

    def chunk(j, carry):
        rows = pl.ds(pl.multiple_of(j * blk, blk), blk)
        hi, mid, lo = _split3(x_ref[0, rows, :])
        total = _dot(upto, hi) + _dot(upto, mid) + _dot(upto, lo) + carry
        pieces = jnp.concatenate([p.astype(F32) for p in _split3(-total)], axis=1).astype(BF16)
        lanes = _dot(pieces, place)
        for blkno in range(n_blocks):
            o_ref[0, blkno, rows, :] = lanes[:, blkno * LANES:(blkno + 1) * LANES].astype(BF16)
        return total[blk - 1:blk, :]

    lax.fori_loop(0, T // blk, chunk, jnp.zeros((1, H), F32))


def _neg_cumsum(lf, blk):
    B, T, H = lf.shape
    n_blocks = H // (2 * N_CHAIN)
    assert H % (2 * N_CHAIN) == 0 and H & (H - 1) == 0
    return pl.pallas_call(
        functools.partial(_cumsum_kernel, blk=blk),
        grid=(B,),
        in_specs=[pl.BlockSpec((1, T, H), lambda b: (b, 0, 0))],
        out_specs=pl.BlockSpec((1, n_blocks, T, LANES), lambda b: (b, 0, 0, 0)),
        out_shape=jax.ShapeDtypeStruct((B, n_blocks, T, LANES), BF16),
        compiler_params=_params("arbitrary"),
        name="neg_cumsum",
    )(lf)


def _route_t(h, wr_t, br):
    tm = h.shape[0]
    h_hi = h.astype(BF16)
    h_lo = (h - h_hi.astype(F32)).astype(BF16)
    w_pieces = jnp.concatenate(_split3(wr_t), axis=0)
    parts = _dot_nt(w_pieces, h_hi) + _dot_nt(w_pieces, h_lo)
    logits = parts[0:N_EXPERTS] + parts[N_EXPERTS:2 * N_EXPERTS] + parts[2 * N_EXPERTS:]
    s = jax.nn.sigmoid(logits)
    sel = s + br
    row = lax.broadcasted_iota(jnp.int32, (N_EXPERTS, tm), 0).astype(F32)
    neg = -jnp.inf

    def first_argmax(vals):
        top = jnp.max(vals, axis=0, keepdims=True)
        idx = jnp.min(jnp.where(vals == top, row, float(N_EXPERTS)), axis=0, keepdims=True)
        return top, idx

    best = None
    for g in range(N_GROUPS):
        in_g = (row >= g * EXPERTS_PER_GROUP) & (row < (g + 1) * EXPERTS_PER_GROUP)
        top1, i1 = first_argmax(jnp.where(in_g, sel, neg))
        top2, i2 = first_argmax(jnp.where(in_g & (row != i1), sel, neg))
        cand = (top1 + top2, i1, i2, jnp.full((1, tm), float(g), F32))
        if best is None:
            best = cand
        else:
            better = cand[0] > best[0]
            best = tuple(jnp.where(better, new, old) for new, old in zip(cand, best))
    _, i1, i2, group = best
    picked = jnp.where((row == i1) | (row == i2), s, 0.0)
    return picked / jnp.sum(picked, axis=0, keepdims=True), group


def _moe_kernel(*refs, final, n_mixed):
    mixed_refs, refs = refs[:n_mixed], refs[n_mixed:]
    if final:
        (wo_ref, g1_ref, x_ref, g_ref, sc_ref, sh_ref, g2_ref, wrt_ref, br_ref, wg_ref, wu_ref, wd_ref, gf_ref,
         o_ref, h_ref, xg_ref, yg_ref, gg_ref, gate_ref, pos_ref, before_ref, n_ref) = refs
    else:
        (wo_ref, g1_ref, x_ref, g_ref, sc_ref, sh_ref, g2_ref, wrt_ref, br_ref, wg_ref, wu_ref, wd_ref,
         o_ref, h_ref, xg_ref, yg_ref, gg_ref, gate_ref, pos_ref, before_ref, n_ref) = refs
    bb, tt, d = x_ref.shape
    tm = bb * tt
    step = pl.program_id(2)
    first_expert = step * EXPERTS_PER_STEP
    group = first_expert // EXPERTS_PER_GROUP

    @pl.when((pl.program_id(0) == 0) & (pl.program_id(1) == 0) & (step == 0))
    def _():
        def fill(j, carry):
            rows = pl.ds(pl.multiple_of(j * MOE_CHUNK, MOE_CHUNK), MOE_CHUNK)
            r = lax.broadcasted_iota(jnp.int32, (MOE_CHUNK, tm), 0) + j * MOE_CHUNK
            c = lax.broadcasted_iota(jnp.int32, (MOE_CHUNK, tm), 1)
            before_ref[rows, :] = jnp.where(r < c, 1.0, 0.0).astype(BF16)
            return carry
        lax.fori_loop(0, tm // MOE_CHUNK, fill, 0)

    @pl.when(step == 0)
    def _():
        proj, off = None, 0
        for a_ref in mixed_refs:
            width = a_ref.shape[2]
            part = _dot(a_ref[...].reshape(tm, width), wo_ref[off:off + width, :])
            proj = part if proj is None else proj + part
            off += width
        x1 = x_ref[...] + g1_ref[...] * proj.reshape(bb, tt, d)
        o_ref[...] = x1
        h = _norm_mod(x1, g_ref[...], sc_ref[...], sh_ref[...])
        h_ref[...] = h.astype(BF16)
        gates, grp = _route_t(h, wrt_ref[...], br_ref[...])
        gates = jnp.concatenate([gates, jnp.zeros((LANES - N_EXPERTS, tm), F32)], axis=0)
        hi = gates.astype(BF16)
        gate_ref[0] = hi
        gate_ref[1] = (gates - hi.astype(F32)).astype(BF16)
        grow = lax.broadcasted_iota(jnp.int32, pos_ref.shape, 0).astype(F32)
        member = jnp.where(grow == grp, 1.0, 0.0)
        rank = _dot(member.astype(BF16), before_ref[...])
        pos_ref[...] = jnp.where(member > 0.0, rank, -1.0)
        for g in range(N_GROUPS):
            count = jnp.sum(member[g:g + 1, :]).astype(jnp.int32)
            n_ref[g] = (count + MOE_CHUNK - 1) // MOE_CHUNK

    n_chunks = n_ref[group]

    def chunk_rows(j):
        return pl.ds(pl.multiple_of(j * MOE_CHUNK, MOE_CHUNK), MOE_CHUNK)

    def one_hot(j, n_rows=MOE_CHUNK):
        want = (lax.broadcasted_iota(jnp.int32, (n_rows, tm), 0) + j * n_rows).astype(F32)
        return jnp.where(pos_ref[pl.ds(group, 1), :] == want, 1.0, 0.0).astype(BF16)

    @pl.when(first_expert % EXPERTS_PER_GROUP == 0)
    def _():
        def gather(j, carry):
            rows = pl.ds(pl.multiple_of(j * 2 * MOE_CHUNK, 2 * MOE_CHUNK), 2 * MOE_CHUNK)
            pick = one_hot(j, 2 * MOE_CHUNK)
            xg_ref[rows, :] = _dot(pick, h_ref[...]).astype(BF16)
            gg_ref[rows, :] = _dot_nt(pick, gate_ref[0]) + _dot_nt(pick, gate_ref[1])
            yg_ref[rows, :] = jnp.zeros((2 * MOE_CHUNK, d), F32)
            return carry
        lax.fori_loop(0, (n_chunks + 1) // 2, gather, 0)

    local = range(EXPERTS_PER_STEP)
    wd_all = wd_ref[0].reshape(EXPERTS_PER_STEP * D_EXPERT, d)
    lane = lax.broadcasted_iota(jnp.int32, (MOE_CHUNK, LANES), 1)

    def expert(chunks):
        rows = [chunk_rows(j) for j in chunks]
        xs = [xg_ref[r, :] for r in rows]
        gts = [[_dot(x, wg_ref[0, u]) for u in local] for x in xs]
        ups = [[_dot(x, wu_ref[0, u]) for u in local] for x in xs]
        acts = []
        for r, gt, up in zip(rows, gts, ups):
            gg = gg_ref[r, :]
            parts = []
            for u in local:
                ge = jnp.sum(jnp.where(lane == first_expert + u, gg, 0.0), axis=1, keepdims=True)
                parts.append(((gt[u] * jax.nn.sigmoid(gt[u])) * up[u] * ge).astype(BF16))
            acts.append(jnp.concatenate(parts, axis=1))
        for r, a in zip(rows, acts):
            yg_ref[r, :] += _dot(a, wd_all)

    def expert_pair(j, carry):
        expert((2 * j, 2 * j + 1))
        return carry

    lax.fori_loop(0, n_chunks // 2, expert_pair, 0)

    @pl.when(n_chunks % 2 == 1)
    def _():
        expert((n_chunks - 1,))

    @pl.when((first_expert + EXPERTS_PER_STEP) % EXPERTS_PER_GROUP == 0)
    def _():
        def scatter(j, carry):
            rows = pl.ds(pl.multiple_of(j * 2 * MOE_CHUNK, 2 * MOE_CHUNK), 2 * MOE_CHUNK)
            back = lax.dot_general(one_hot(j, 2 * MOE_CHUNK), yg_ref[rows, :].astype(BF16),
                                   (((0,), (0,)), ((), ())), preferred_element_type=F32)
            o_ref[...] += g2_ref[...] * back.reshape(bb, tt, d)
            return carry
        lax.fori_loop(0, (n_chunks + 1) // 2, scatter, 0)

    if final:
        @pl.when(step == N_EXPERTS // EXPERTS_PER_STEP - 1)
        def _():
            y = o_ref[...]
            ms = jnp.mean(y * y, axis=-1, keepdims=True)
            o_ref[...] = y * lax.rsqrt(ms + EPS) * gf_ref[...]


def _moe(mixed, w_out, g1, x, g, sc, sh, g2, w_router, b_router, w_gate, w_up, w_down, layer, g_final, bb, tt):
    B, T, _ = x.shape
    final = g_final is not None
    x_spec = pl.BlockSpec((bb, tt, D_MODEL), lambda i, j, e: (i, j, 0))
    g_spec = pl.BlockSpec((1, D_MODEL), lambda i, j, e: (0, 0))
    m_spec = pl.BlockSpec((bb, 1, D_MODEL), lambda i, j, e: (i, 0, 0))
    in_specs = [pl.BlockSpec((bb, tt, a.shape[2]), lambda i, j, e: (i, j, 0)) for a in mixed]
    in_specs += [pl.BlockSpec(w_out.shape, lambda i, j, e: (0, 0)), m_spec]
    in_specs += [x_spec, g_spec, m_spec, m_spec, m_spec,
                pl.BlockSpec((N_EXPERTS, D_MODEL), lambda i, j, e: (0, 0)),
                pl.BlockSpec((N_EXPERTS, 1), lambda i, j, e: (0, 0)),
                pl.BlockSpec((1, EXPERTS_PER_STEP, D_MODEL, D_EXPERT), lambda i, j, e: (layer, e, 0, 0)),
                pl.BlockSpec((1, EXPERTS_PER_STEP, D_MODEL, D_EXPERT), lambda i, j, e: (layer, e, 0, 0)),
                pl.BlockSpec((1, EXPERTS_PER_STEP, D_EXPERT, D_MODEL), lambda i, j, e: (layer, e, 0, 0))]
    args = [*mixed, w_out, g1, x, g, sc, sh, g2, w_router.T, b_router.reshape(N_EXPERTS, 1), w_gate, w_up, w_down]
    if final:
        in_specs.append(g_spec)
        args.append(g_final)
    tm = bb * tt
    assert tm % (2 * MOE_CHUNK) == 0
    packed = tm
    scratch = [pltpu.VMEM((tm, D_MODEL), BF16),
               pltpu.VMEM((packed, D_MODEL), BF16),
               pltpu.VMEM((packed, D_MODEL), F32),
               pltpu.VMEM((packed, LANES), F32),
               pltpu.VMEM((2, LANES, tm), BF16),
               pltpu.VMEM((2 * N_GROUPS, tm), F32),
               pltpu.VMEM((tm, tm), BF16),
               pltpu.SMEM((N_GROUPS,), jnp.int32)]
    return pl.pallas_call(
        functools.partial(_moe_kernel, final=final, n_mixed=len(mixed)),
        grid=(B // bb, T // tt, N_EXPERTS // EXPERTS_PER_STEP),
        in_specs=in_specs,
        out_specs=x_spec,
        out_shape=jax.ShapeDtypeStruct(x.shape, F32),
        scratch_shapes=scratch,
        compiler_params=_params("arbitrary", "arbitrary", "arbitrary"),
        name="moe",
    )(*args)


def _rope_tables(pos):
    half = ROPE_DIM // 2
    inv = ROPE_THETA ** (-jnp.arange(half, dtype=F32) / half)
    ang = pos.astype(F32)[:, None] * inv[None, :]
    cos, sin = jnp.cos(ang), jnp.sin(ang)
    ones = jnp.ones((pos.shape[0], HEAD_DIM - ROPE_DIM), F32)
    zeros = jnp.zeros_like(ones)
    zh = jnp.zeros_like(sin)
    c64 = jnp.concatenate([cos, cos, ones], axis=1)
    s1_64 = jnp.concatenate([-sin, zh, zeros], axis=1)
    s2_64 = jnp.concatenate([zh, sin, zeros], axis=1)
    return tuple(jnp.concatenate([t, t], axis=1) for t in (c64, s1_64, s2_64))


def _pad_time(a, total):
    if total == a.shape[1]:
        return a
    return jnp.pad(a, ((0, 0), (0, total - a.shape[1])) + ((0, 0),) * (a.ndim - 2))


def _trunk(x, mod, past, wts, cfg):
    B, T, _ = x.shape
    P = 0 if past is None else past["diff_k"].shape[2]
    bb, tt_proj, tt_moe, tq, tk = cfg
    wide = past is None and bb == 1 and tt_proj % tk == 0
    n_valid = P + T
    if past is None:
        Tk = -(-n_valid // tk) * tk
    else:
        assert T <= tk
        Tk = P + PREP_BLOCKS * tk
    Tq = -(-T // tq) * tq
    tables = _rope_tables(P + jnp.arange(T))
    if bb > 1:
        tables = tuple(jnp.tile(t, (bb, 1)) for t in tables)
    attn = functools.partial(_attention, n_chain=N_CHAIN, tq=tq, tk=tk, q_off=P, n_valid=n_valid)
    leaves_even, leaves_odd = [], []

    def with_cache(qkv, k_cols, v_cols, cache_k, cache_v):
        new_k = _pad_time(qkv[:, :, k_cols[0]:k_cols[1]], Tk - P)
        new_vt = _key_blocks_t(_pad_time(qkv[:, :, v_cols[0]:v_cols[1]], Tk - P), tk)
        return _cache_prep(cache_k, cache_v, new_k, new_vt, tk)

    def heads_last(leaf_t, n_heads):
        return leaf_t.reshape(B, n_heads, -1, T).transpose(0, 3, 1, 2)

    for l in range(DEPTH):
        sh1, sc1, g1, sh2, sc2, g2 = [mod[l, :, k * D_MODEL:(k + 1) * D_MODEL][:, None, :] for k in range(6)]
        i = l // 2
        g_mix = wts["norm_mix"][l][None, :]
        if l % 2 == 0:
            lam_init = 0.8 - 0.6 * math.exp(-0.3 * l)
            lp = jnp.stack([wts["diff_lq1"][i], wts["diff_lk1"][i], wts["diff_lq2"][i], wts["diff_lk2"][i]])
            diff_extras = (lp, wts["diff_subln"][i][:, None])
            if wide:
                qk, ak, av, bkt, bvt, dvt, svt = _proj_even_wide(x, g_mix, sc1, sh1, wts["w_in_even"][i], tables,
                                                                 tt_proj, tk)
                leaves_even.append((ak.reshape(B, T, N_DIFF_HEADS, LANES), av.reshape(B, T, N_DIFF_HEADS, LANES),
                                    heads_last(bkt, N_SB_HEADS), heads_last(bvt, N_SB_HEADS)))
                q_arr, q_cols = qk, (0, 2 * SEG // LANES)
                dk, dk_col, sk, sk_col = qk, SEG // LANES, qk, 3 * SEG // LANES
            else:
                qkv, ak, av, bk, bv = _proj_even(x, g_mix, sc1, sh1, wts["w_in_even"][i], tables, bb, tt_proj)
                leaves_even.append((ak.reshape(B, T, N_DIFF_HEADS, LANES), av.reshape(B, T, N_DIFF_HEADS, LANES),
                                    bk.reshape(B, T, N_SB_HEADS, HEAD_DIM), bv.reshape(B, T, N_SB_HEADS, HEAD_DIM)))
                q_arr, q_cols = _pad_time(qkv, Tq), (0, 3 * SEG // LANES)
                dk, dvt = with_cache(qkv, (SEG, 2 * SEG), (2 * SEG, 3 * SEG), past["diff_k"][i], past["diff_v"][i])
                sk, svt = with_cache(qkv, (4 * SEG, 5 * SEG), (5 * SEG, 6 * SEG), past["sb_k"][i], past["sb_v"][i])
                dk_col = sk_col = 0
            a_out = attn("diff", q_arr, q_cols[0], dk, dk_col, dvt, N_DIFF_HEADS, extras=diff_extras, lam_init=lam_init)
            b_out = attn("sb", q_arr, q_cols[1], sk, sk_col, svt, N_SB_HEADS // 2)
            mixed, w_out = [a_out[:, :T], b_out[:, :T]], wts["w_out_even"][i]
        else:
            if wide:
                qk, kt, vt, fvt, lf = _proj_odd_wide(x, g_mix, sc1, sh1, wts["w_in_odd"][i], wts["w_f_odd"][i],
                                                     wts["b_forget"][i], tt_proj, tk)
                leaves_odd.append((heads_last(kt, N_FOX_HEADS), heads_last(vt, N_FOX_HEADS), lf))
                q_arr, fk, fk_col, lf_all = qk, qk, 2 * SEG // LANES, lf
            else:
                qkv, k, v, lf = _proj_odd(x, g_mix, sc1, sh1, wts["w_in_odd"][i], wts["w_f_odd"][i],
                                          wts["b_forget"][i], bb, tt_proj)
                leaves_odd.append((k.reshape(B, T, N_FOX_HEADS, HEAD_DIM), v.reshape(B, T, N_FOX_HEADS, HEAD_DIM), lf))
                q_arr = _pad_time(qkv, Tq)
                fk, fvt = with_cache(qkv, (2 * SEG, 4 * SEG), (4 * SEG, 6 * SEG), past["fox_k"][i], past["fox_v"][i])
                fk_col = 0
                lf_all = jnp.concatenate([past["fox_logf"][i], lf], axis=1)
            nd = _neg_cumsum(_pad_time(lf_all, Tk), tk)
            o = attn("fox", q_arr, 0, fk, fk_col, fvt, N_FOX_HEADS // 2, extras=(nd,))
            mixed, w_out = [o[:, :T]], wts["w_out_odd"][i]
        g_final = wts["norm_final"][None, :] if l == DEPTH - 1 else None
        x = _moe(mixed, w_out, g1, x, wts["norm_ffn"][l][None, :], sc2, sh2, g2, wts["w_router"], wts["b_router"],
                 wts["w_gate"], wts["w_up"], wts["w_down"], l, g_final, bb, tt_moe)
    return x, leaves_even, leaves_odd


def kernel(x_prompt, x_sample, c_prompt, c_sample, cache_diff_k, cache_diff_v, cache_sb_k, cache_sb_v, cache_fox_k, cache_fox_v, cache_fox_logf, w_ada, b_ada, norm_mix, norm_ffn, w_in_even, w_out_even, diff_lq1, diff_lk1, diff_lq2, diff_lk2, diff_subln, w_in_odd, b_forget, w_out_odd, w_router, b_router, w_gate, w_up, w_down, norm_final):
    Bp, Tp, _ = x_prompt.shape
    Bs, Ts, _ = x_sample.shape
    wts = {
        "norm_mix": norm_mix, "norm_ffn": norm_ffn, "norm_final": norm_final,
        "w_in_even": w_in_even.astype(BF16), "w_out_even": w_out_even.astype(BF16),
        "w_in_odd": w_in_odd[:, :, :3 * FOX_W].astype(BF16), "w_f_odd": w_in_odd[:, :, 3 * FOX_W:].astype(BF16),
        "w_out_odd": w_out_odd.astype(BF16), "b_forget": b_forget,
        "diff_lq1": diff_lq1, "diff_lk1": diff_lk1, "diff_lq2": diff_lq2, "diff_lk2": diff_lk2,
        "diff_subln": diff_subln, "w_router": w_router, "b_router": b_router,
        "w_gate": w_gate.astype(BF16), "w_up": w_up.astype(BF16), "w_down": w_down.astype(BF16),
    }
    past = {"diff_k": cache_diff_k, "diff_v": cache_diff_v, "sb_k": cache_sb_k, "sb_v": cache_sb_v,
            "fox_k": cache_fox_k, "fox_v": cache_fox_v, "fox_logf": cache_fox_logf}

    rows = Bp + Bs
    rows_pad = -(-rows // 16) * 16
    c_all = jnp.pad(jnp.concatenate([c_prompt, c_sample], axis=0), ((0, rows_pad - rows), (0, 0)))
    mod = _ada_mod(c_all, w_ada, b_ada)

    cfg_prompt = (1, 512, 1024, 256, 256)
    cfg_sample = (Bs, Ts, Ts, 64, 256)
    y_p, even_p, odd_p = _trunk(x_prompt, mod[:, :Bp], None, wts, cfg_prompt)
    y_s, even_s, odd_s = _trunk(x_sample, mod[:, Bp:rows], past, wts, cfg_sample)

    def layers(rows):
        return rows[0][None] if len(rows) == 1 else jnp.stack(rows, axis=0)

    def pack(even, odd):
        return tuple(layers(r) for r in zip(*even)) + tuple(layers(r) for r in zip(*odd))

    return (y_p, y_s) + pack(even_p, odd_p) + pack(even_s, odd_s)
```

```python
import functools
import math

import jax
import jax.numpy as jnp
from jax import lax
from jax.experimental import pallas as pl
from jax.experimental.pallas import tpu as pltpu

D_MODEL = 1024
DEPTH = 2
CHUNK = 64
HEAD_DIM = 64
N_DIFF_HEADS = D_MODEL // 256
N_SB_HEADS = D_MODEL // 128
N_FOX_HEADS = D_MODEL // HEAD_DIM
DIFF_W = N_DIFF_HEADS * 2 * HEAD_DIM
SB_W = N_SB_HEADS * HEAD_DIM
FOX_W = N_FOX_HEADS * HEAD_DIM
ROPE_DIM = HEAD_DIM // 4
ROPE_THETA = 500000.0
N_EXPERTS = 16
N_GROUPS = 4
EXPERTS_PER_GROUP = N_EXPERTS // N_GROUPS
D_EXPERT = D_MODEL // 2
EPS = 1e-6

LANES = 128
BF16_ROWS = 16
SEG = 512
QK_SCALE = HEAD_DIM ** -0.5
MASKED = -1e30
N_CHAIN = 4
SB_DEAD = -110.0
MOE_CHUNK = 128
EXPERTS_PER_STEP = 2
PREP_BLOCKS = 2
VMEM_LIMIT = 56 * 1024 * 1024

F32 = jnp.float32
BF16 = jnp.bfloat16


def _params(*sem):
    return pltpu.CompilerParams(dimension_semantics=sem, vmem_limit_bytes=VMEM_LIMIT)


def _dot(a, b):
    return jnp.dot(a, b, preferred_element_type=F32)


def _dot_nt(a, b):
    return lax.dot_general(a, b, (((1,), (1,)), ((), ())), preferred_element_type=F32)


def _norm_mod(x, g, sc, sh):
    bb, tt, d = x.shape
    ms = jnp.mean(x * x, axis=-1, keepdims=True)
    h = x * lax.rsqrt(ms + EPS) * g
    h = h * (1.0 + sc) + sh
    return h.reshape(bb * tt, d)


def _log_sigmoid(x):
    return jnp.minimum(x, 0.0) - jnp.log(1.0 + jnp.exp(-jnp.abs(x)))


def _split3(x):
    hi = x.astype(BF16)
    r1 = x - hi.astype(F32)
    mid = r1.astype(BF16)
    lo = (r1 - mid.astype(F32)).astype(BF16)
    return hi, mid, lo


def _ada_kernel(c_ref, w_ref, b_ref, o_ref):
    c = c_ref[...]
    a = (c * jax.nn.sigmoid(c)).astype(BF16)
    o_ref[0] = _dot(a, w_ref[0].astype(BF16)) + b_ref[0]


def _ada_mod(c_all, w_ada, b_ada):
    rows = c_all.shape[0]
    tn = 1536
    width = 6 * D_MODEL
    return pl.pallas_call(
        _ada_kernel,
        grid=(DEPTH, width // tn),
        in_specs=[
            pl.BlockSpec((rows, D_MODEL), lambda l, j: (0, 0)),
            pl.BlockSpec((1, D_MODEL, tn), lambda l, j: (l, 0, j)),
            pl.BlockSpec((1, 1, tn), lambda l, j: (l, 0, j)),
        ],
        out_specs=pl.BlockSpec((1, rows, tn), lambda l, j: (l, 0, j)),
        out_shape=jax.ShapeDtypeStruct((DEPTH, rows, width), F32),
        compiler_params=_params("arbitrary", "arbitrary"),
        name="ada_mod",
    )(c_all, w_ada, b_ada.reshape(DEPTH, 1, width))


def _rope(p, cos, s1, s2):
    outs = []
    for c in range(SEG // LANES):
        pc = p[:, c * LANES:(c + 1) * LANES]
        up = pltpu.roll(pc, LANES - ROPE_DIM // 2, 1)
        down = pltpu.roll(pc, ROPE_DIM // 2, 1)
        outs.append(pc * cos + up * s1 + down * s2)
    return jnp.concatenate(outs, axis=1)


def _proj_even_kernel(x_ref, g_ref, sc_ref, sh_ref, w_ref, cos_ref, s1_ref, s2_ref,
                      qkv_ref, ak_ref, av_ref, bk_ref, bv_ref):
    bb, tt, _ = x_ref.shape
    h = _norm_mod(x_ref[...], g_ref[...], sc_ref[...], sh_ref[...]).astype(BF16)
    cos, s1, s2 = cos_ref[...], s1_ref[...], s2_ref[...]

    def seg(i):
        return _dot(h, w_ref[:, i * SEG:(i + 1) * SEG])

    def put(i, val, leaf_ref):
        val3 = val.reshape(bb, tt, SEG)
        qkv_ref[:, :, i * SEG:(i + 1) * SEG] = val3.astype(BF16)
        if leaf_ref is not None:
            leaf_ref[...] = val3

    put(0, _rope(seg(0), cos, s1, s2) * QK_SCALE, None)
    put(1, _rope(seg(1), cos, s1, s2), ak_ref)
    put(2, seg(2), av_ref)
    put(3, seg(3) * QK_SCALE, None)
    put(4, seg(4), bk_ref)
    put(5, seg(5), bv_ref)


def _rows_by_head(ref, val, n_heads):
    tt = val.shape[0]
    for hd in range(n_heads):
        ref[0, pl.ds(hd, tt, stride=n_heads), :] = val[:, hd * LANES:(hd + 1) * LANES]


def _key_blocks(ref, first, val_t):
    n, tk = val_t.shape[0] // LANES, ref.shape[4]
    for u in range(ref.shape[2]):
        ref[0, first:first + n, u] = val_t[:, u * tk:(u + 1) * tk].reshape(n, LANES, tk).astype(BF16)


def _proj_even_wide_kernel(x_ref, g_ref, sc_ref, sh_ref, w_ref, cos_ref, s1_ref, s2_ref,
                           qk_ref, ak_ref, av_ref, bkt_ref, bvt_ref, dvt_ref, svt_ref):
    h = _norm_mod(x_ref[...], g_ref[...], sc_ref[...], sh_ref[...]).astype(BF16)
    cos, s1, s2 = cos_ref[...], s1_ref[...], s2_ref[...]

    def seg(i):
        return _dot(h, w_ref[:, i * SEG:(i + 1) * SEG])

    qk_ref[0, :, 0:SEG] = (_rope(seg(0), cos, s1, s2) * QK_SCALE).astype(BF16)
    ak = _rope(seg(1), cos, s1, s2)
    qk_ref[0, :, SEG:2 * SEG] = ak.astype(BF16)
    _rows_by_head(ak_ref, ak, N_DIFF_HEADS)
    av = seg(2)
    _rows_by_head(av_ref, av, N_DIFF_HEADS)
    _key_blocks(dvt_ref, 0, av.T)
    qk_ref[0, :, 2 * SEG:3 * SEG] = (seg(3) * QK_SCALE).astype(BF16)
    bk = seg(4)
    qk_ref[0, :, 3 * SEG:4 * SEG] = bk.astype(BF16)
    bkt_ref[0] = bk.T
    bvt = seg(5).T
    bvt_ref[0] = bvt
    _key_blocks(svt_ref, 0, bvt)


def _proj_odd_wide_kernel(x_ref, g_ref, sc_ref, sh_ref, w_ref, wf_ref, bf_ref,
                          qk_ref, kt_ref, vt_ref, fvt_ref, lf_ref):
    h = _norm_mod(x_ref[...], g_ref[...], sc_ref[...], sh_ref[...]).astype(BF16)

    def seg(i):
        return _dot(h, w_ref[:, i * SEG:(i + 1) * SEG])

    for i in range(2):
        qk_ref[0, :, i * SEG:(i + 1) * SEG] = (seg(i) * QK_SCALE).astype(BF16)
    for i in range(2):
        k = seg(2 + i)
        qk_ref[0, :, (2 + i) * SEG:(3 + i) * SEG] = k.astype(BF16)
        kt_ref[0, i * SEG:(i + 1) * SEG, :] = k.T
        v_t = seg(4 + i).T
        vt_ref[0, i * SEG:(i + 1) * SEG, :] = v_t
        _key_blocks(fvt_ref, i * (SEG // LANES), v_t)
    fl = _dot(h, wf_ref[...]) + bf_ref[...]
    lf_ref[0] = _log_sigmoid(fl)


def _proj_odd_kernel(x_ref, g_ref, sc_ref, sh_ref, w_ref, wf_ref, bf_ref,
                     qkv_ref, k_ref, v_ref, lf_ref):
    bb, tt, _ = x_ref.shape
    h = _norm_mod(x_ref[...], g_ref[...], sc_ref[...], sh_ref[...]).astype(BF16)

    def seg(i):
        return _dot(h, w_ref[:, i * SEG:(i + 1) * SEG]).reshape(bb, tt, SEG)

    for i in range(2):
        qkv_ref[:, :, i * SEG:(i + 1) * SEG] = (seg(i) * QK_SCALE).astype(BF16)
    for i in range(2, 6):
        val = seg(i)
        qkv_ref[:, :, i * SEG:(i + 1) * SEG] = val.astype(BF16)
        leaf_ref = k_ref if i < 4 else v_ref
        leaf_ref[:, :, (i % 2) * SEG:(i % 2 + 1) * SEG] = val
    fl = _dot(h, wf_ref[...]) + bf_ref[...]
    lf_ref[...] = _log_sigmoid(fl).reshape(bb, tt, N_FOX_HEADS)


def _row_specs(bb, tt):
    x_spec = pl.BlockSpec((bb, tt, D_MODEL), lambda i, j: (i, j, 0))
    g_spec = pl.BlockSpec((1, D_MODEL), lambda i, j: (0, 0))
    m_spec = pl.BlockSpec((bb, 1, D_MODEL), lambda i, j: (i, 0, 0))
    return x_spec, g_spec, m_spec


def _out_spec(bb, tt, width):
    return pl.BlockSpec((bb, tt, width), lambda i, j: (i, j, 0))


def _proj_even(x, g, sc, sh, w_bf, tables, bb, tt):
    B, T, _ = x.shape
    x_spec, g_spec, m_spec = _row_specs(bb, tt)
    tm = bb * tt
    t_spec = pl.BlockSpec((tm, LANES), (lambda i, j: (j, 0)) if bb == 1 else (lambda i, j: (0, 0)))
    leaf = jax.ShapeDtypeStruct((B, T, SEG), F32)
    return pl.pallas_call(
        _proj_even_kernel,
        grid=(B // bb, T // tt),
        in_specs=[x_spec, g_spec, m_spec, m_spec,
                  pl.BlockSpec(w_bf.shape, lambda i, j: (0, 0)), t_spec, t_spec, t_spec],
        out_specs=[_out_spec(bb, tt, 6 * SEG)] + [_out_spec(bb, tt, SEG)] * 4,
        out_shape=[jax.ShapeDtypeStruct((B, T, 6 * SEG), BF16), leaf, leaf, leaf, leaf],
        compiler_params=_params("arbitrary", "arbitrary"),
        name="proj_even",
    )(x, g, sc, sh, w_bf, *tables)


def _wide_specs(B, T, tt, tk):
    assert tt % tk == 0
    def rows(width, dtype):
        return pl.BlockSpec((1, tt, width), lambda i, j: (i, j, 0)), jax.ShapeDtypeStruct((B, T, width), dtype)

    def by_head(n_heads):
        return (pl.BlockSpec((1, tt * n_heads, LANES), lambda i, j: (i, j, 0)),
                jax.ShapeDtypeStruct((B, T * n_heads, LANES), F32))

    def transposed(width):
        return pl.BlockSpec((1, width, tt), lambda i, j: (i, 0, j)), jax.ShapeDtypeStruct((B, width, T), F32)

    def key_blocks(n_col):
        return (pl.BlockSpec((1, n_col, tt // tk, LANES, tk), lambda i, j: (i, 0, j, 0, 0)),
                jax.ShapeDtypeStruct((B, n_col, T // tk, LANES, tk), BF16))

    return rows, by_head, transposed, key_blocks


def _proj_even_wide(x, g, sc, sh, w_bf, tables, tt, tk):
    B, T, _ = x.shape
    x_spec, g_spec, m_spec = _row_specs(1, tt)
    t_spec = pl.BlockSpec((tt, LANES), lambda i, j: (j, 0))
    rows, by_head, transposed, key_blocks = _wide_specs(B, T, tt, tk)
    outs = [rows(4 * SEG, BF16), by_head(N_DIFF_HEADS), by_head(N_DIFF_HEADS), transposed(SEG), transposed(SEG),
            key_blocks(SEG // LANES), key_blocks(SEG // LANES)]
    return pl.pallas_call(
        _proj_even_wide_kernel,
        grid=(B, T // tt),
        in_specs=[x_spec, g_spec, m_spec, m_spec,
                  pl.BlockSpec(w_bf.shape, lambda i, j: (0, 0)), t_spec, t_spec, t_spec],
        out_specs=[o[0] for o in outs],
        out_shape=[o[1] for o in outs],
        compiler_params=_params("arbitrary", "arbitrary"),
        name="proj_even_wide",
    )(x, g, sc, sh, w_bf, *tables)


def _proj_odd_wide(x, g, sc, sh, w_bf, wf_bf, b_f, tt, tk):
    B, T, _ = x.shape
    x_spec, g_spec, m_spec = _row_specs(1, tt)
    rows, by_head, transposed, key_blocks = _wide_specs(B, T, tt, tk)
    outs = [rows(4 * SEG, BF16), transposed(FOX_W), transposed(FOX_W), key_blocks(FOX_W // LANES),
            rows(N_FOX_HEADS, F32)]
    return pl.pallas_call(
        _proj_odd_wide_kernel,
        grid=(B, T // tt),
        in_specs=[x_spec, g_spec, m_spec, m_spec,
                  pl.BlockSpec(w_bf.shape, lambda i, j: (0, 0)),
                  pl.BlockSpec(wf_bf.shape, lambda i, j: (0, 0)),
                  pl.BlockSpec((1, N_FOX_HEADS), lambda i, j: (0, 0))],
        out_specs=[o[0] for o in outs],
        out_shape=[o[1] for o in outs],
        compiler_params=_params("arbitrary", "arbitrary"),
        name="proj_odd_wide",
    )(x, g, sc, sh, w_bf, wf_bf, b_f.reshape(1, N_FOX_HEADS))


def _proj_odd(x, g, sc, sh, w_bf, wf_bf, b_f, bb, tt):
    B, T, _ = x.shape
    x_spec, g_spec, m_spec = _row_specs(bb, tt)
    leaf = jax.ShapeDtypeStruct((B, T, FOX_W), F32)
    return pl.pallas_call(
        _proj_odd_kernel,
        grid=(B // bb, T // tt),
        in_specs=[x_spec, g_spec, m_spec, m_spec,
                  pl.BlockSpec(w_bf.shape, lambda i, j: (0, 0)),
                  pl.BlockSpec(wf_bf.shape, lambda i, j: (0, 0)),
                  pl.BlockSpec((1, N_FOX_HEADS), lambda i, j: (0, 0))],
        out_specs=[_out_spec(bb, tt, 6 * SEG), _out_spec(bb, tt, FOX_W), _out_spec(bb, tt, FOX_W),
                   _out_spec(bb, tt, N_FOX_HEADS)],
        out_shape=[jax.ShapeDtypeStruct((B, T, 6 * SEG), BF16), leaf, leaf,
                   jax.ShapeDtypeStruct((B, T, N_FOX_HEADS), F32)],
        compiler_params=_params("arbitrary", "arbitrary"),
        name="proj_odd",
    )(x, g, sc, sh, w_bf, wf_bf, b_f.reshape(1, N_FOX_HEADS))


def _attn_kernel(*refs, kind, n_chain, tq, tk, q_off, n_valid, lam_init):
    if kind == "diff":
        q_ref, k_ref, vt_ref, lp_ref, sg_ref, o_ref, acc_ref, m_ref, s_ref = refs
    elif kind == "fox":
        q_ref, k_ref, vt_ref, nd_ref, o_ref, acc_ref, m_ref, s_ref = refs
    else:
        q_ref, k_ref, vt_ref, o_ref, acc_ref, m_ref, s_ref = refs
    tq2 = 2 * tq
    q0 = q_off + pl.program_id(2) * tq

    low = lax.broadcasted_iota(jnp.int32, (tq, LANES), 1) < HEAD_DIM
    if kind == "fox":
        row = lax.broadcasted_iota(jnp.int32, (tq2, LANES), 0)
        col = lax.broadcasted_iota(jnp.int32, (tq2, LANES), 1)
        first = jnp.right_shift(row, tq.bit_length() - 1) * 3
    qqs = []
    for c in range(n_chain):
        q = q_ref[0, :, c * LANES:(c + 1) * LANES]
        zero = jnp.zeros_like(q)
        qq = jnp.concatenate([jnp.where(low, q, zero), jnp.where(low, zero, q)], axis=0)
        if kind == "fox":
            pick = jnp.logical_and(first + 6 * c <= col, col < first + 6 * c + 3)
            qq = jnp.concatenate([qq, jnp.where(pick, 1.0, 0.0).astype(BF16)], axis=1)
        qqs.append(qq)

    acc_ref[...] = jnp.zeros_like(acc_ref)
    if kind == "sb":
        m_ref[...] = jnp.zeros_like(m_ref)
        tri_r = lax.broadcasted_iota(jnp.int32, (tk, tk), 0)
        tri_c = lax.broadcasted_iota(jnp.int32, (tk, tk), 1)
        after = jnp.where(tri_c > tri_r, 1.0, 0.0).astype(BF16)
        after2 = jnp.concatenate([after, after], axis=1)
    else:
        m_ref[...] = jnp.full_like(m_ref, MASKED)
        ones_rows = jnp.ones((BF16_ROWS, tk), BF16)

    qpos = q0 + jnp.bitwise_and(lax.broadcasted_iota(jnp.int32, (tk, tq2), 1), tq - 1)
    kidx = lax.broadcasted_iota(jnp.int32, (tk, tq2), 0)

    chains = range(n_chain)

    def qk(kb, slot):
        start = pl.multiple_of(kb * tk, tk)
        for c in chains:
            k = k_ref[0, pl.ds(start, tk), c * LANES:(c + 1) * LANES]
            if kind == "fox":
                k = jnp.concatenate([k, nd_ref[0, 0, pl.ds(start, tk), :]], axis=1)
            s_ref[slot, c] = _dot_nt(k, qqs[c])

    def consume(kb, slot, masked):
        scores = [s_ref[slot, c] for c in chains]
        if masked:
            kpos = kb * tk + kidx
            if kind == "diff":
                vis = jnp.right_shift(kpos, 6) <= jnp.right_shift(qpos, 6)
            elif kind == "fox":
                vis = kpos <= qpos
            else:
                vis = kpos < qpos
            vis = jnp.logical_and(vis, kpos < n_valid)
        vts = [vt_ref[0, c, kb] for c in chains]
        if kind == "sb":
            log_betas, laters = [], []
            for c in chains:
                s = scores[c]
                soft = jnp.log(1.0 + jnp.exp(-jnp.abs(s)))
                log_beta = jnp.minimum(s, 0.0) - soft
                log_keep = log_beta - s
                if masked:
                    log_keep = jnp.where(vis, log_keep, 0.0)
                hi = log_keep.astype(BF16)
                lo = (log_keep - hi.astype(F32)).astype(BF16)
                later = _dot(after2, jnp.concatenate([hi, lo], axis=0))
                laters.append(later + m_ref[c])
                m_ref[c] += later[0:1, :] + log_keep[0:1, :]
                log_betas.append(log_beta)
            weights = []
            for c in chains:
                w = jnp.exp(log_betas[c] + laters[c])
                if masked:
                    w = jnp.where(vis, w, 0.0)
                weights.append(w.astype(BF16))
            for c in chains:
                acc_ref[c] += _dot(vts[c], weights[c])
        else:
            probs, alphas = [], []
            for c in chains:
                s = scores[c]
                if masked:
                    s = jnp.where(vis, s, MASKED)
                m_prev = m_ref[c]
                m_new = jnp.maximum(m_prev, jnp.max(s, axis=0, keepdims=True))
                alphas.append(jnp.exp(m_prev - m_new))
                probs.append(jnp.exp(s - m_new).astype(BF16))
                m_ref[c] = m_new
            for c in chains:
                vt_aug = jnp.concatenate([vts[c], ones_rows], axis=0)
                acc_ref[c] = alphas[c] * acc_ref[c] + _dot(vt_aug, probs[c])

    n_full = q0 // tk
    qk(n_full, 0)
    qk(jnp.maximum(n_full - 1, 0), 1)
    consume(n_full, 0, True)

    def pair(j, carry):
        kb = n_full - 1 - 2 * j
        qk(jnp.maximum(kb - 1, 0), 0)
        consume(kb, 1, False)
        qk(jnp.maximum(kb - 2, 0), 1)
        consume(kb - 1, 0, False)
        return carry

    if kind == "sb":
        def top_sum():
            top = m_ref[0]
            for c in range(1, n_chain):
                top = jnp.maximum(top, m_ref[c])
            return jnp.max(top)

        def alive(carry):
            return jnp.logical_and(carry[0] < n_full // 2, carry[1] > SB_DEAD)

        def pair_and_check(carry):
            pair(carry[0], 0)
            return carry[0] + 1, top_sum()

        _, top = lax.while_loop(alive, pair_and_check, (jnp.int32(0), top_sum()))
        odd_block_left = jnp.logical_and(n_full % 2 == 1, top > SB_DEAD)
    else:
        lax.fori_loop(0, n_full // 2, pair, 0)
        odd_block_left = n_full % 2 == 1

    @pl.when(odd_block_left)
    def _():
        consume(0, 1, False)

    if kind == "diff":
        lp = lp_ref[...]
        lam = (jnp.exp(jnp.sum(lp[0:1] * lp[1:2], axis=1, keepdims=True))
               - jnp.exp(jnp.sum(lp[2:3] * lp[3:4], axis=1, keepdims=True)) + lam_init)
    for c in range(n_chain):
        acc = acc_ref[c]
        if kind == "sb":
            out_t = jnp.concatenate([acc[0:HEAD_DIM, 0:tq], acc[HEAD_DIM:LANES, tq:tq2]], axis=0)
        else:
            r = 1.0 / acc[LANES:LANES + 1, :]
            if kind == "fox":
                out_t = jnp.concatenate([acc[0:HEAD_DIM, 0:tq] * r[:, 0:tq],
                                         acc[HEAD_DIM:LANES, tq:tq2] * r[:, tq:tq2]], axis=0)
            else:
                a = acc[0:LANES, 0:tq] * r[:, 0:tq] - lam * (acc[0:LANES, tq:tq2] * r[:, tq:tq2])
                ms = jnp.mean(a * a, axis=0, keepdims=True)
                out_t = a * lax.rsqrt(ms + EPS) * sg_ref[...] * (1.0 - lam_init)
        if tq < LANES:
            out_t = jnp.concatenate([out_t, jnp.zeros((LANES, LANES - tq), F32)], axis=1)
        o_ref[0, :, c * LANES:(c + 1) * LANES] = out_t.T[0:tq].astype(o_ref.dtype)


def _attention(kind, q_arr, q_col, k_arr, k_col, vt_arr, n_col, *, n_chain, tq, tk, q_off, n_valid,
               extras=(), lam_init=0.0):
    B, Tq, _ = q_arr.shape
    Tk = k_arr.shape[1]
    n_kb = Tk // tk
    assert Tq % tq == 0 and Tk % tk == 0 and tk % tq == 0 and q_off % tk == 0 and n_valid <= Tk
    assert tq & (tq - 1) == 0 and n_col % n_chain == 0 and q_col % n_chain == 0 and k_col % n_chain == 0
    wide = n_chain * LANES
    in_specs = [
        pl.BlockSpec((1, tq, wide), lambda b, c, i: (b, i, q_col // n_chain + c)),
        pl.BlockSpec((1, Tk, wide), lambda b, c, i: (b, 0, k_col // n_chain + c)),
        pl.BlockSpec((1, n_chain, n_kb, LANES, tk), lambda b, c, i: (b, c, 0, 0, 0)),
    ]
    acc_rows = LANES if kind == "sb" else LANES + BF16_ROWS
    scratch = [pltpu.VMEM((n_chain, acc_rows, 2 * tq), F32), pltpu.VMEM((n_chain, 1, 2 * tq), F32),
               pltpu.VMEM((2, n_chain, tk, 2 * tq), F32)]
    if kind == "diff":
        lp, sg = extras
        in_specs += [pl.BlockSpec(lp.shape, lambda b, c, i: (0, 0)),
                     pl.BlockSpec(sg.shape, lambda b, c, i: (0, 0))]
    elif kind == "fox":
        (nd,) = extras
        in_specs.append(pl.BlockSpec((1, 1, Tk, LANES), lambda b, c, i: (b, c, 0, 0)))
    kernel = functools.partial(_attn_kernel, kind=kind, n_chain=n_chain, tq=tq, tk=tk, q_off=q_off,
                               n_valid=n_valid, lam_init=lam_init)
    return pl.pallas_call(
        kernel,
        grid=(B, n_col // n_chain, Tq // tq),
        in_specs=in_specs,
        out_specs=pl.BlockSpec((1, tq, wide), lambda b, c, i: (b, i, c)),
        out_shape=jax.ShapeDtypeStruct((B, Tq, n_col * LANES), BF16),
        scratch_shapes=scratch,
        compiler_params=_params("arbitrary", "arbitrary", "arbitrary"),
        name="attn_" + kind,
    )(q_arr, k_arr, vt_arr, *extras)


def _key_blocks_t(v, tk):
    B, Tk, W = v.shape
    return v.reshape(B, Tk // tk, tk, W // LANES, LANES).transpose(0, 3, 1, 4, 2)


def _cache_prep_kernel(k_ref, v_ref, nk_ref, nvt_ref, ko_ref, vto_ref, *, n_heads, tk, time_minor):
    step = pl.program_id(1)
    n_past = pl.num_programs(1) - 1

    @pl.when(step < n_past)
    def _():
        for u in range(vto_ref.shape[2]):
            for c in range(ko_ref.shape[2] // LANES):
                cols = slice(c * LANES, (c + 1) * LANES)
                if time_minor:
                    ko_ref[0, u * tk:(u + 1) * tk, cols] = k_ref[0, cols, u * tk:(u + 1) * tk].T.astype(BF16)
                    vto_ref[0, c, u] = v_ref[0, cols, u * tk:(u + 1) * tk].astype(BF16)
                else:
                    head_rows = pl.ds(u * tk * n_heads + c, tk, stride=n_heads)
                    ko_ref[0, u * tk:(u + 1) * tk, cols] = k_ref[0, head_rows, :].astype(BF16)
                    vto_ref[0, c, u] = v_ref[0, head_rows, :].T.astype(BF16)

    @pl.when(step == n_past)
    def _():
        ko_ref[...] = nk_ref[...]
        vto_ref[...] = nvt_ref[...]


def _cache_prep(cache_k, cache_v, new_k, new_vt, tk):
    B, P, H, dh = cache_k.shape
    W = H * dh
    g = PREP_BLOCKS
    n_col, n_past = W // LANES, P // (g * tk)
    assert P % (g * tk) == 0 and dh in (HEAD_DIM, LANES)
    time_minor = dh != LANES
    if time_minor:
        cache_k, cache_v = [c.transpose(0, 2, 3, 1).reshape(B, W, P) for c in (cache_k, cache_v)]
        rows = pl.BlockSpec((1, W, g * tk), lambda b, j: (b, 0, jnp.minimum(j, n_past - 1)))
    else:
        cache_k, cache_v = [c.reshape(B, P * H, dh) for c in (cache_k, cache_v)]
        rows = pl.BlockSpec((1, g * tk * H, dh), lambda b, j: (b, jnp.minimum(j, n_past - 1), 0))
    return pl.pallas_call(
        functools.partial(_cache_prep_kernel, n_heads=H, tk=tk, time_minor=time_minor),
        grid=(B, n_past + 1),
        in_specs=[rows, rows,
                  pl.BlockSpec((1, g * tk, W), lambda b, j: (b, 0, 0)),
                  pl.BlockSpec((1, n_col, g, LANES, tk), lambda b, j: (b, 0, 0, 0, 0))],
        out_specs=[pl.BlockSpec((1, g * tk, W), lambda b, j: (b, j, 0)),
                   pl.BlockSpec((1, n_col, g, LANES, tk), lambda b, j: (b, 0, j, 0, 0))],
        out_shape=[jax.ShapeDtypeStruct((B, P + g * tk, W), BF16),
                   jax.ShapeDtypeStruct((B, n_col, P // tk + g, LANES, tk), BF16)],
        compiler_params=_params("arbitrary", "arbitrary"),
        name="cache_prep",
    )(cache_k, cache_v, new_k, new_vt)


def _cumsum_kernel(x_ref, o_ref, *, blk):
    T, H = x_ref.shape[1:]
    n_blocks = o_ref.shape[1]
    r = lax.broadcasted_iota(jnp.int32, (blk, blk), 0)
    c = lax.broadcasted_iota(jnp.int32, (blk, blk), 1)
    upto = jnp.where(c <= r, 1.0, 0.0).astype(BF16)
    per_block = 2 * N_CHAIN
    src = lax.broadcasted_iota(jnp.int32, (3 * H, n_blocks * LANES), 0)
    col = lax.broadcasted_iota(jnp.int32, (3 * H, n_blocks * LANES), 1)
    head, piece = jnp.bitwise_and(src, H - 1), jnp.right_shift(src, H.bit_length() - 1)
    here = jnp.right_shift(col, 7) == jnp.right_shift(head, per_block.bit_length() - 1)
    lane = jnp.bitwise_and(col, LANES - 1) - 3 * jnp.bitwise_and(head, per_block - 1)
    place = jnp.where(jnp.logical_and(here, lane == piece), 1.0, 0.0).astype(BF16)

    local_sums = []
    for j in range(T // blk):
        hi, mid, lo = _split3(x_ref[0, j * blk:(j + 1) * blk, :])
        local_sums.append(_dot(upto, hi) + _dot(upto, mid) + _dot(upto, lo))
    carry = jnp.zeros((1, H), F32)
    for j, local in enumerate(local_sums):
        total = local + carry
        carry = total[blk - 1:blk, :]
        pieces = jnp.concatenate([p.astype(F32) for p in _split3(-total)], axis=1).astype(BF16)
        lanes = _dot(pieces, place)
        for blkno in range(n_blocks):
            o_ref[0, blkno, j * blk:(j + 1) * blk, :] = lanes[:, blkno * LANES:(blkno + 1) * LANES].astype(BF16)


def _neg_cumsum(lf, blk):
    B, T, H = lf.shape
    n_blocks = H // (2 * N_CHAIN)
    assert H % (2 * N_CHAIN) == 0 and H & (H - 1) == 0
    return pl.pallas_call(
        functools.partial(_cumsum_kernel, blk=blk),
        grid=(B,),
        in_specs=[pl.BlockSpec((1, T, H), lambda b: (b, 0, 0))],
        out_specs=pl.BlockSpec((1, n_blocks, T, LANES), lambda b: (b, 0, 0, 0)),
        out_shape=jax.ShapeDtypeStruct((B, n_blocks, T, LANES), BF16),
        compiler_params=_params("arbitrary"),
        name="neg_cumsum",
    )(lf)


def _route_t(h, wr_t, br):
    tm = h.shape[0]
    h_hi = h.astype(BF16)
    h_lo = (h - h_hi.astype(F32)).astype(BF16)
    w_pieces = jnp.concatenate(_split3(wr_t), axis=0)
    parts = _dot_nt(w_pieces, h_hi) + _dot_nt(w_pieces, h_lo)
    logits = parts[0:N_EXPERTS] + parts[N_EXPERTS:2 * N_EXPERTS] + parts[2 * N_EXPERTS:]
    s = jax.nn.sigmoid(logits)
    sel = s + br
    row = lax.broadcasted_iota(jnp.int32, (N_EXPERTS, tm), 0).astype(F32)
    neg = -jnp.inf

    def first_argmax(vals):
        top = jnp.max(vals, axis=0, keepdims=True)
        idx = jnp.min(jnp.where(vals == top, row, float(N_EXPERTS)), axis=0, keepdims=True)
        return top, idx

    best = None
    for g in range(N_GROUPS):
        in_g = (row >= g * EXPERTS_PER_GROUP) & (row < (g + 1) * EXPERTS_PER_GROUP)
        top1, i1 = first_argmax(jnp.where(in_g, sel, neg))
        top2, i2 = first_argmax(jnp.where(in_g & (row != i1), sel, neg))
        cand = (top1 + top2, i1, i2, jnp.full((1, tm), float(g), F32))
        if best is None:
            best = cand
        else:
            better = cand[0] > best[0]
            best = tuple(jnp.where(better, new, old) for new, old in zip(cand, best))
    _, i1, i2, group = best
    picked = jnp.where((row == i1) | (row == i2), s, 0.0)
    return picked / jnp.sum(picked, axis=0, keepdims=True), group


def _moe_kernel(*refs, final, n_mixed):
    mixed_refs, refs = refs[:n_mixed], refs[n_mixed:]
    if final:
        (wo_ref, g1_ref, x_ref, g_ref, sc_ref, sh_ref, g2_ref, wrt_ref, br_ref, wg_ref, wu_ref, wd_ref, gf_ref,
         o_ref, h_ref, xg_ref, yg_ref, gg_ref, gate_ref, pos_ref, before_ref, n_ref) = refs
    else:
        (wo_ref, g1_ref, x_ref, g_ref, sc_ref, sh_ref, g2_ref, wrt_ref, br_ref, wg_ref, wu_ref, wd_ref,
         o_ref, h_ref, xg_ref, yg_ref, gg_ref, gate_ref, pos_ref, before_ref, n_ref) = refs
    bb, tt, d = x_ref.shape
    tm = bb * tt
    step = pl.program_id(2)
    first_expert = step * EXPERTS_PER_STEP
    group = first_expert // EXPERTS_PER_GROUP

    @pl.when((pl.program_id(0) == 0) & (pl.program_id(1) == 0) & (step == 0))
    def _():
        def fill(j, carry):
            rows = pl.ds(pl.multiple_of(j * MOE_CHUNK, MOE_CHUNK), MOE_CHUNK)
            r = lax.broadcasted_iota(jnp.int32, (MOE_CHUNK, tm), 0) + j * MOE_CHUNK
            c = lax.broadcasted_iota(jnp.int32, (MOE_CHUNK, tm), 1)
            before_ref[rows, :] = jnp.where(r < c, 1.0, 0.0).astype(BF16)
            return carry
        lax.fori_loop(0, tm // MOE_CHUNK, fill, 0)

    @pl.when(step == 0)
    def _():
        proj, off = None, 0
        for a_ref in mixed_refs:
            width = a_ref.shape[2]
            part = _dot(a_ref[...].reshape(tm, width), wo_ref[off:off + width, :])
            proj = part if proj is None else proj + part
            off += width
        x1 = x_ref[...] + g1_ref[...] * proj.reshape(bb, tt, d)
        o_ref[...] = x1
        h = _norm_mod(x1, g_ref[...], sc_ref[...], sh_ref[...])
        h_ref[...] = h.astype(BF16)
        gates, grp = _route_t(h, wrt_ref[...], br_ref[...])
        gates = jnp.concatenate([gates, jnp.zeros((LANES - N_EXPERTS, tm), F32)], axis=0)
        hi = gates.astype(BF16)
        gate_ref[0] = hi
        gate_ref[1] = (gates - hi.astype(F32)).astype(BF16)
        grow = lax.broadcasted_iota(jnp.int32, pos_ref.shape, 0).astype(F32)
        member = jnp.where(grow == grp, 1.0, 0.0)
        rank = _dot(member.astype(BF16), before_ref[...])
        pos_ref[...] = jnp.where(member > 0.0, rank, -1.0)
        for g in range(N_GROUPS):
            count = jnp.sum(member[g:g + 1, :]).astype(jnp.int32)
            n_ref[g] = (count + MOE_CHUNK - 1) // MOE_CHUNK

    n_chunks = n_ref[group]

    def chunk_rows(j):
        return pl.ds(pl.multiple_of(j * MOE_CHUNK, MOE_CHUNK), MOE_CHUNK)

    def one_hot(j, n_rows=MOE_CHUNK):
        want = (lax.broadcasted_iota(jnp.int32, (n_rows, tm), 0) + j * n_rows).astype(F32)
        return jnp.where(pos_ref[pl.ds(group, 1), :] == want, 1.0, 0.0).astype(BF16)

    @pl.when(first_expert % EXPERTS_PER_GROUP == 0)
    def _():
        def gather(j, carry):
            rows = pl.ds(pl.multiple_of(j * 2 * MOE_CHUNK, 2 * MOE_CHUNK), 2 * MOE_CHUNK)
            pick = one_hot(j, 2 * MOE_CHUNK)
            xg_ref[rows, :] = _dot(pick, h_ref[...]).astype(BF16)
            gg_ref[rows, :] = _dot_nt(pick, gate_ref[0]) + _dot_nt(pick, gate_ref[1])
            yg_ref[rows, :] = jnp.zeros((2 * MOE_CHUNK, d), F32)
            return carry
        lax.fori_loop(0, (n_chunks + 1) // 2, gather, 0)

    local = range(EXPERTS_PER_STEP)
    wd_all = wd_ref[0].reshape(EXPERTS_PER_STEP * D_EXPERT, d)
    lane = lax.broadcasted_iota(jnp.int32, (MOE_CHUNK, LANES), 1)

    def expert(chunks):
        rows = [chunk_rows(j) for j in chunks]
        xs = [xg_ref[r, :] for r in rows]
        gts = [[_dot(x, wg_ref[0, u]) for u in local] for x in xs]
        ups = [[_dot(x, wu_ref[0, u]) for u in local] for x in xs]
        acts = []
        for r, gt, up in zip(rows, gts, ups):
            gg = gg_ref[r, :]
            parts = []
            for u in local:
                ge = jnp.sum(jnp.where(lane == first_expert + u, gg, 0.0), axis=1, keepdims=True)
                parts.append(((gt[u] * jax.nn.sigmoid(gt[u])) * up[u] * ge).astype(BF16))
            acts.append(jnp.concatenate(parts, axis=1))
        for r, a in zip(rows, acts):
            yg_ref[r, :] += _dot(a, wd_all)

    def expert_pair(j, carry):
        expert((2 * j, 2 * j + 1))
        return carry

    lax.fori_loop(0, n_chunks // 2, expert_pair, 0)

    @pl.when(n_chunks % 2 == 1)
    def _():
        expert((n_chunks - 1,))

    @pl.when((first_expert + EXPERTS_PER_STEP) % EXPERTS_PER_GROUP == 0)
    def _():
        def scatter(j, carry):
            rows = pl.ds(pl.multiple_of(j * 2 * MOE_CHUNK, 2 * MOE_CHUNK), 2 * MOE_CHUNK)
            back = lax.dot_general(one_hot(j, 2 * MOE_CHUNK), yg_ref[rows, :].astype(BF16),
                                   (((0,), (0,)), ((), ())), preferred_element_type=F32)
            o_ref[...] += g2_ref[...] * back.reshape(bb, tt, d)
            return carry
        lax.fori_loop(0, (n_chunks + 1) // 2, scatter, 0)

    if final:
        @pl.when(step == N_EXPERTS // EXPERTS_PER_STEP - 1)
        def _():
            y = o_ref[...]
            ms = jnp.mean(y * y, axis=-1, keepdims=True)
            o_ref[...] = y * lax.rsqrt(ms + EPS) * gf_ref[...]


def _moe(mixed, w_out, g1, x, g, sc, sh, g2, w_router, b_router, w_gate, w_up, w_down, layer, g_final, bb, tt):
    B, T, _ = x.shape
    final = g_final is not None
    x_spec = pl.BlockSpec((bb, tt, D_MODEL), lambda i, j, e: (i, j, 0))
    g_spec = pl.BlockSpec((1, D_MODEL), lambda i, j, e: (0, 0))
    m_spec = pl.BlockSpec((bb, 1, D_MODEL), lambda i, j, e: (i, 0, 0))
    in_specs = [pl.BlockSpec((bb, tt, a.shape[2]), lambda i, j, e: (i, j, 0)) for a in mixed]
    in_specs += [pl.BlockSpec(w_out.shape, lambda i, j, e: (0, 0)), m_spec]
    in_specs += [x_spec, g_spec, m_spec, m_spec, m_spec,
                pl.BlockSpec((N_EXPERTS, D_MODEL), lambda i, j, e: (0, 0)),
                pl.BlockSpec((N_EXPERTS, 1), lambda i, j, e: (0, 0)),
                pl.BlockSpec((1, EXPERTS_PER_STEP, D_MODEL, D_EXPERT), lambda i, j, e: (layer, e, 0, 0)),
                pl.BlockSpec((1, EXPERTS_PER_STEP, D_MODEL, D_EXPERT), lambda i, j, e: (layer, e, 0, 0)),
                pl.BlockSpec((1, EXPERTS_PER_STEP, D_EXPERT, D_MODEL), lambda i, j, e: (layer, e, 0, 0))]
    args = [*mixed, w_out, g1, x, g, sc, sh, g2, w_router.T, b_router.reshape(N_EXPERTS, 1), w_gate, w_up, w_down]
    if final:
        in_specs.append(g_spec)
        args.append(g_final)
    tm = bb * tt
    assert tm % (2 * MOE_CHUNK) == 0
    packed = tm
    scratch = [pltpu.VMEM((tm, D_MODEL), BF16),
               pltpu.VMEM((packed, D_MODEL), BF16),
               pltpu.VMEM((packed, D_MODEL), F32),
               pltpu.VMEM((packed, LANES), F32),
               pltpu.VMEM((2, LANES, tm), BF16),
               pltpu.VMEM((2 * N_GROUPS, tm), F32),
               pltpu.VMEM((tm, tm), BF16),
               pltpu.SMEM((N_GROUPS,), jnp.int32)]
    return pl.pallas_call(
        functools.partial(_moe_kernel, final=final, n_mixed=len(mixed)),
        grid=(B // bb, T // tt, N_EXPERTS // EXPERTS_PER_STEP),
        in_specs=in_specs,
        out_specs=x_spec,
        out_shape=jax.ShapeDtypeStruct(x.shape, F32),
        scratch_shapes=scratch,
        compiler_params=_params("arbitrary", "arbitrary", "arbitrary"),
        name="moe",
    )(*args)


def _rope_tables(pos):
    half = ROPE_DIM // 2
    inv = ROPE_THETA ** (-jnp.arange(half, dtype=F32) / half)
    ang = pos.astype(F32)[:, None] * inv[None, :]
    cos, sin = jnp.cos(ang), jnp.sin(ang)
    ones = jnp.ones((pos.shape[0], HEAD_DIM - ROPE_DIM), F32)
    zeros = jnp.zeros_like(ones)
    zh = jnp.zeros_like(sin)
    c64 = jnp.concatenate([cos, cos, ones], axis=1)
    s1_64 = jnp.concatenate([-sin, zh, zeros], axis=1)
    s2_64 = jnp.concatenate([zh, sin, zeros], axis=1)
    return tuple(jnp.concatenate([t, t], axis=1) for t in (c64, s1_64, s2_64))


def _pad_time(a, total):
    if total == a.shape[1]:
        return a
    return jnp.pad(a, ((0, 0), (0, total - a.shape[1])) + ((0, 0),) * (a.ndim - 2))


def _trunk(x, mod, past, wts, cfg):
    B, T, _ = x.shape
    P = 0 if past is None else past["diff_k"].shape[2]
    bb, tt_proj, tt_moe, tq, tk = cfg
    wide = past is None and bb == 1 and tt_proj % tk == 0
    n_valid = P + T
    if past is None:
        Tk = -(-n_valid // tk) * tk
    else:
        assert T <= tk
        Tk = P + PREP_BLOCKS * tk
    Tq = -(-T // tq) * tq
    tables = _rope_tables(P + jnp.arange(T))
    if bb > 1:
        tables = tuple(jnp.tile(t, (bb, 1)) for t in tables)
    attn = functools.partial(_attention, n_chain=N_CHAIN, tq=tq, tk=tk, q_off=P, n_valid=n_valid)
    leaves_even, leaves_odd = [], []

    def with_cache(qkv, k_cols, v_cols, cache_k, cache_v):
        new_k = _pad_time(qkv[:, :, k_cols[0]:k_cols[1]], Tk - P)
        new_vt = _key_blocks_t(_pad_time(qkv[:, :, v_cols[0]:v_cols[1]], Tk - P), tk)
        return _cache_prep(cache_k, cache_v, new_k, new_vt, tk)

    def heads_last(leaf_t, n_heads):
        return leaf_t.reshape(B, n_heads, -1, T).transpose(0, 3, 1, 2)

    for l in range(DEPTH):
        sh1, sc1, g1, sh2, sc2, g2 = [mod[l, :, k * D_MODEL:(k + 1) * D_MODEL][:, None, :] for k in range(6)]
        i = l // 2
        g_mix = wts["norm_mix"][l][None, :]
        if l % 2 == 0:
            lam_init = 0.8 - 0.6 * math.exp(-0.3 * l)
            lp = jnp.stack([wts["diff_lq1"][i], wts["diff_lk1"][i], wts["diff_lq2"][i], wts["diff_lk2"][i]])
            diff_extras = (lp, wts["diff_subln"][i][:, None])
            if wide:
                qk, ak, av, bkt, bvt, dvt, svt = _proj_even_wide(x, g_mix, sc1, sh1, wts["w_in_even"][i], tables,
                                                                 tt_proj, tk)
                leaves_even.append((ak.reshape(B, T, N_DIFF_HEADS, LANES), av.reshape(B, T, N_DIFF_HEADS, LANES),
                                    heads_last(bkt, N_SB_HEADS), heads_last(bvt, N_SB_HEADS)))
                q_arr, q_cols = qk, (0, 2 * SEG // LANES)
                dk, dk_col, sk, sk_col = qk, SEG // LANES, qk, 3 * SEG // LANES
            else:
                qkv, ak, av, bk, bv = _proj_even(x, g_mix, sc1, sh1, wts["w_in_even"][i], tables, bb, tt_proj)
                leaves_even.append((ak.reshape(B, T, N_DIFF_HEADS, LANES), av.reshape(B, T, N_DIFF_HEADS, LANES),
                                    bk.reshape(B, T, N_SB_HEADS, HEAD_DIM), bv.reshape(B, T, N_SB_HEADS, HEAD_DIM)))
                q_arr, q_cols = _pad_time(qkv, Tq), (0, 3 * SEG // LANES)
                dk, dvt = with_cache(qkv, (SEG, 2 * SEG), (2 * SEG, 3 * SEG), past["diff_k"][i], past["diff_v"][i])
                sk, svt = with_cache(qkv, (4 * SEG, 5 * SEG), (5 * SEG, 6 * SEG), past["sb_k"][i], past["sb_v"][i])
                dk_col = sk_col = 0
            a_out = attn("diff", q_arr, q_cols[0], dk, dk_col, dvt, N_DIFF_HEADS, extras=diff_extras, lam_init=lam_init)
            b_out = attn("sb", q_arr, q_cols[1], sk, sk_col, svt, N_SB_HEADS // 2)
            mixed, w_out = [a_out[:, :T], b_out[:, :T]], wts["w_out_even"][i]
        else:
            if wide:
                qk, kt, vt, fvt, lf = _proj_odd_wide(x, g_mix, sc1, sh1, wts["w_in_odd"][i], wts["w_f_odd"][i],
                                                     wts["b_forget"][i], tt_proj, tk)
                leaves_odd.append((heads_last(kt, N_FOX_HEADS), heads_last(vt, N_FOX_HEADS), lf))
                q_arr, fk, fk_col, lf_all = qk, qk, 2 * SEG // LANES, lf
            else:
                qkv, k, v, lf = _proj_odd(x, g_mix, sc1, sh1, wts["w_in_odd"][i], wts["w_f_odd"][i],
                                          wts["b_forget"][i], bb, tt_proj)
                leaves_odd.append((k.reshape(B, T, N_FOX_HEADS, HEAD_DIM), v.reshape(B, T, N_FOX_HEADS, HEAD_DIM), lf))
                q_arr = _pad_time(qkv, Tq)
                fk, fvt = with_cache(qkv, (2 * SEG, 4 * SEG), (4 * SEG, 6 * SEG), past["fox_k"][i], past["fox_v"][i])
                fk_col = 0
                lf_all = jnp.concatenate([past["fox_logf"][i], lf], axis=1)
            nd = _neg_cumsum(_pad_time(lf_all, Tk), tk)
            o = attn("fox", q_arr, 0, fk, fk_col, fvt, N_FOX_HEADS // 2, extras=(nd,))
            mixed, w_out = [o[:, :T]], wts["w_out_odd"][i]
        g_final = wts["norm_final"][None, :] if l == DEPTH - 1 else None
        x = _moe(mixed, w_out, g1, x, wts["norm_ffn"][l][None, :], sc2, sh2, g2, wts["w_router"], wts["b_router"],
                 wts["w_gate"], wts["w_up"], wts["w_down"], l, g_final, bb, tt_moe)
    return x, leaves_even, leaves_odd


def kernel(x_prompt, x_sample, c_prompt, c_sample, cache_diff_k, cache_diff_v, cache_sb_k, cache_sb_v, cache_fox_k, cache_fox_v, cache_fox_logf, w_ada, b_ada, norm_mix, norm_ffn, w_in_even, w_out_even, diff_lq1, diff_lk1, diff_lq2, diff_lk2, diff_subln, w_in_odd, b_forget, w_out_odd, w_router, b_router, w_gate, w_up, w_down, norm_final):
    Bp, Tp, _ = x_prompt.shape
    Bs, Ts, _ = x_sample.shape
    wts = {
        "norm_mix": norm_mix, "norm_ffn": norm_ffn, "norm_final": norm_final,
        "w_in_even": w_in_even.astype(BF16), "w_out_even": w_out_even.astype(BF16),
        "w_in_odd": w_in_odd[:, :, :3 * FOX_W].astype(BF16), "w_f_odd": w_in_odd[:, :, 3 * FOX_W:].astype(BF16),
        "w_out_odd": w_out_odd.astype(BF16), "b_forget": b_forget,
        "diff_lq1": diff_lq1, "diff_lk1": diff_lk1, "diff_lq2": diff_lq2, "diff_lk2": diff_lk2,
        "diff_subln": diff_subln, "w_router": w_router, "b_router": b_router,
        "w_gate": w_gate.astype(BF16), "w_up": w_up.astype(BF16), "w_down": w_down.astype(BF16),
    }
    past = {"diff_k": cache_diff_k, "diff_v": cache_diff_v, "sb_k": cache_sb_k, "sb_v": cache_sb_v,
            "fox_k": cache_fox_k, "fox_v": cache_fox_v, "fox_logf": cache_fox_logf}

    rows = Bp + Bs
    rows_pad = -(-rows // 16) * 16
    c_all = jnp.pad(jnp.concatenate([c_prompt, c_sample], axis=0), ((0, rows_pad - rows), (0, 0)))
    mod = _ada_mod(c_all, w_ada, b_ada)

    cfg_prompt = (1, 512, 1024, 256, 256)
    cfg_sample = (Bs, Ts, Ts, 64, 256)
    y_p, even_p, odd_p = _trunk(x_prompt, mod[:, :Bp], None, wts, cfg_prompt)
    y_s, even_s, odd_s = _trunk(x_sample, mod[:, Bp:rows], past, wts, cfg_sample)

    def layers(rows):
        return rows[0][None] if len(rows) == 1 else jnp.stack(rows, axis=0)

    def pack(even, odd):
        return tuple(layers(r) for r in zip(*even)) + tuple(layers(r) for r in zip(*odd))

    return (y_p, y_s) + pack(even_p, odd_p) + pack(even_s, odd_s)
```

```python
import functools
import math

import jax
import jax.numpy as jnp
from jax import lax
from jax.experimental import pallas as pl
from jax.experimental.pallas import tpu as pltpu

D_MODEL = 1024
DEPTH = 2
CHUNK = 64
HEAD_DIM = 64
N_DIFF_HEADS = D_MODEL // 256
N_SB_HEADS = D_MODEL // 128
N_FOX_HEADS = D_MODEL // HEAD_DIM
DIFF_W = N_DIFF_HEADS * 2 * HEAD_DIM
SB_W = N_SB_HEADS * HEAD_DIM
FOX_W = N_FOX_HEADS * HEAD_DIM
ROPE_DIM = HEAD_DIM // 4
ROPE_THETA = 500000.0
N_EXPERTS = 16
N_GROUPS = 4
EXPERTS_PER_GROUP = N_EXPERTS // N_GROUPS
D_EXPERT = D_MODEL // 2
EPS = 1e-6

LANES = 128
BF16_ROWS = 16
SEG = 512
QK_SCALE = HEAD_DIM ** -0.5
MASKED = -1e30
N_CHAIN = 4
SB_DEAD = -110.0
MOE_CHUNK = 128
EXPERTS_PER_STEP = 2
PREP_BLOCKS = 2
VMEM_LIMIT = 56 * 1024 * 1024

F32 = jnp.float32
BF16 = jnp.bfloat16


def _params(*sem):
    return pltpu.CompilerParams(dimension_semantics=sem, vmem_limit_bytes=VMEM_LIMIT)


def _dot(a, b):
    return jnp.dot(a, b, preferred_element_type=F32)


def _dot_nt(a, b):
    return lax.dot_general(a, b, (((1,), (1,)), ((), ())), preferred_element_type=F32)


def _norm_mod(x, g, sc, sh):
    bb, tt, d = x.shape
    ms = jnp.mean(x * x, axis=-1, keepdims=True)
    h = x * lax.rsqrt(ms + EPS) * g
    h = h * (1.0 + sc) + sh
    return h.reshape(bb * tt, d)


def _log_sigmoid(x):
    return jnp.minimum(x, 0.0) - jnp.log(1.0 + jnp.exp(-jnp.abs(x)))


def _split3(x):
    hi = x.astype(BF16)
    r1 = x - hi.astype(F32)
    mid = r1.astype(BF16)
    lo = (r1 - mid.astype(F32)).astype(BF16)
    return hi, mid, lo


def _ada_kernel(c_ref, w_ref, b_ref, o_ref):
    c = c_ref[...]
    a = (c * jax.nn.sigmoid(c)).astype(BF16)
    o_ref[0] = _dot(a, w_ref[0].astype(BF16)) + b_ref[0]


def _ada_mod(c_all, w_ada, b_ada):
    rows = c_all.shape[0]
    tn = 1536
    width = 6 * D_MODEL
    return pl.pallas_call(
        _ada_kernel,
        grid=(DEPTH, width // tn),
        in_specs=[
            pl.BlockSpec((rows, D_MODEL), lambda l, j: (0, 0)),
            pl.BlockSpec((1, D_MODEL, tn), lambda l, j: (l, 0, j)),
            pl.BlockSpec((1, 1, tn), lambda l, j: (l, 0, j)),
        ],
        out_specs=pl.BlockSpec((1, rows, tn), lambda l, j: (l, 0, j)),
        out_shape=jax.ShapeDtypeStruct((DEPTH, rows, width), F32),
        compiler_params=_params("arbitrary", "arbitrary"),
        name="ada_mod",
    )(c_all, w_ada, b_ada.reshape(DEPTH, 1, width))


def _rope(p, cos, s1, s2):
    outs = []
    for c in range(SEG // LANES):
        pc = p[:, c * LANES:(c + 1) * LANES]
        up = pltpu.roll(pc, LANES - ROPE_DIM // 2, 1)
        down = pltpu.roll(pc, ROPE_DIM // 2, 1)
        outs.append(pc * cos + up * s1 + down * s2)
    return jnp.concatenate(outs, axis=1)


def _proj_even_kernel(x_ref, g_ref, sc_ref, sh_ref, w_ref, cos_ref, s1_ref, s2_ref,
                      qkv_ref, ak_ref, av_ref, bk_ref, bv_ref):
    bb, tt, _ = x_ref.shape
    h = _norm_mod(x_ref[...], g_ref[...], sc_ref[...], sh_ref[...]).astype(BF16)
    cos, s1, s2 = cos_ref[...], s1_ref[...], s2_ref[...]

    def seg(i):
        return _dot(h, w_ref[:, i * SEG:(i + 1) * SEG])

    def put(i, val, leaf_ref):
        val3 = val.reshape(bb, tt, SEG)
        qkv_ref[:, :, i * SEG:(i + 1) * SEG] = val3.astype(BF16)
        if leaf_ref is not None:
            leaf_ref[...] = val3

    put(0, _rope(seg(0), cos, s1, s2) * QK_SCALE, None)
    put(1, _rope(seg(1), cos, s1, s2), ak_ref)
    put(2, seg(2), av_ref)
    put(3, seg(3) * QK_SCALE, None)
    put(4, seg(4), bk_ref)
    put(5, seg(5), bv_ref)


def _rows_by_head(ref, val, n_heads):
    tt = val.shape[0]
    for hd in range(n_heads):
        ref[0, pl.ds(hd, tt, stride=n_heads), :] = val[:, hd * LANES:(hd + 1) * LANES]


def _key_blocks(ref, first, val_t):
    n, tk = val_t.shape[0] // LANES, ref.shape[4]
    for u in range(ref.shape[2]):
        ref[0, first:first + n, u] = val_t[:, u * tk:(u + 1) * tk].reshape(n, LANES, tk).astype(BF16)


def _proj_even_wide_kernel(x_ref, g_ref, sc_ref, sh_ref, w_ref, cos_ref, s1_ref, s2_ref,
                           qk_ref, ak_ref, av_ref, bkt_ref, bvt_ref, dvt_ref, svt_ref):
    h = _norm_mod(x_ref[...], g_ref[...], sc_ref[...], sh_ref[...]).astype(BF16)
    cos, s1, s2 = cos_ref[...], s1_ref[...], s2_ref[...]

    def seg(i):
        return _dot(h, w_ref[:, i * SEG:(i + 1) * SEG])

    qk_ref[0, :, 0:SEG] = (_rope(seg(0), cos, s1, s2) * QK_SCALE).astype(BF16)
    ak = _rope(seg(1), cos, s1, s2)
    qk_ref[0, :, SEG:2 * SEG] = ak.astype(BF16)
    _rows_by_head(ak_ref, ak, N_DIFF_HEADS)
    av = seg(2)
    _rows_by_head(av_ref, av, N_DIFF_HEADS)
    _key_blocks(dvt_ref, 0, av.T)
    qk_ref[0, :, 2 * SEG:3 * SEG] = (seg(3) * QK_SCALE).astype(BF16)
    bk = seg(4)
    qk_ref[0, :, 3 * SEG:4 * SEG] = bk.astype(BF16)
    bkt_ref[0] = bk.T
    bvt = seg(5).T
    bvt_ref[0] = bvt
    _key_blocks(svt_ref, 0, bvt)


def _proj_odd_wide_kernel(x_ref, g_ref, sc_ref, sh_ref, w_ref, wf_ref, bf_ref,
                          qk_ref, kt_ref, vt_ref, fvt_ref, lf_ref):
    h = _norm_mod(x_ref[...], g_ref[...], sc_ref[...], sh_ref[...]).astype(BF16)

    def seg(i):
        return _dot(h, w_ref[:, i * SEG:(i + 1) * SEG])

    for i in range(2):
        qk_ref[0, :, i * SEG:(i + 1) * SEG] = (seg(i) * QK_SCALE).astype(BF16)
    for i in range(2):
        k = seg(2 + i)
        qk_ref[0, :, (2 + i) * SEG:(3 + i) * SEG] = k.astype(BF16)
        kt_ref[0, i * SEG:(i + 1) * SEG, :] = k.T
        v_t = seg(4 + i).T
        vt_ref[0, i * SEG:(i + 1) * SEG, :] = v_t
        _key_blocks(fvt_ref, i * (SEG // LANES), v_t)
    fl = _dot(h, wf_ref[...]) + bf_ref[...]
    lf_ref[0] = _log_sigmoid(fl)


def _proj_odd_kernel(x_ref, g_ref, sc_ref, sh_ref, w_ref, wf_ref, bf_ref,
                     qkv_ref, k_ref, v_ref, lf_ref):
    bb, tt, _ = x_ref.shape
    h = _norm_mod(x_ref[...], g_ref[...], sc_ref[...], sh_ref[...]).astype(BF16)

    def seg(i):
        return _dot(h, w_ref[:, i * SEG:(i + 1) * SEG]).reshape(bb, tt, SEG)

    for i in range(2):
        qkv_ref[:, :, i * SEG:(i + 1) * SEG] = (seg(i) * QK_SCALE).astype(BF16)
    for i in range(2, 6):
        val = seg(i)
        qkv_ref[:, :, i * SEG:(i + 1) * SEG] = val.astype(BF16)
        leaf_ref = k_ref if i < 4 else v_ref
        leaf_ref[:, :, (i % 2) * SEG:(i % 2 + 1) * SEG] = val
    fl = _dot(h, wf_ref[...]) + bf_ref[...]
    lf_ref[...] = _log_sigmoid(fl).reshape(bb, tt, N_FOX_HEADS)


def _row_specs(bb, tt):
    x_spec = pl.BlockSpec((bb, tt, D_MODEL), lambda i, j: (i, j, 0))
    g_spec = pl.BlockSpec((1, D_MODEL), lambda i, j: (0, 0))
    m_spec = pl.BlockSpec((bb, 1, D_MODEL), lambda i, j: (i, 0, 0))
    return x_spec, g_spec, m_spec


def _out_spec(bb, tt, width):
    return pl.BlockSpec((bb, tt, width), lambda i, j: (i, j, 0))


def _proj_even(x, g, sc, sh, w_bf, tables, bb, tt):
    B, T, _ = x.shape
    x_spec, g_spec, m_spec = _row_specs(bb, tt)
    tm = bb * tt
    t_spec = pl.BlockSpec((tm, LANES), (lambda i, j: (j, 0)) if bb == 1 else (lambda i, j: (0, 0)))
    leaf = jax.ShapeDtypeStruct((B, T, SEG), F32)
    return pl.pallas_call(
        _proj_even_kernel,
        grid=(B // bb, T // tt),
        in_specs=[x_spec, g_spec, m_spec, m_spec,
                  pl.BlockSpec(w_bf.shape, lambda i, j: (0, 0)), t_spec, t_spec, t_spec],
        out_specs=[_out_spec(bb, tt, 6 * SEG)] + [_out_spec(bb, tt, SEG)] * 4,
        out_shape=[jax.ShapeDtypeStruct((B, T, 6 * SEG), BF16), leaf, leaf, leaf, leaf],
        compiler_params=_params("arbitrary", "arbitrary"),
        name="proj_even",
    )(x, g, sc, sh, w_bf, *tables)


def _wide_specs(B, T, tt, tk):
    assert tt % tk == 0
    def rows(width, dtype):
        return pl.BlockSpec((1, tt, width), lambda i, j: (i, j, 0)), jax.ShapeDtypeStruct((B, T, width), dtype)

    def by_head(n_heads):
        return (pl.BlockSpec((1, tt * n_heads, LANES), lambda i, j: (i, j, 0)),
                jax.ShapeDtypeStruct((B, T * n_heads, LANES), F32))

    def transposed(width):
        return pl.BlockSpec((1, width, tt), lambda i, j: (i, 0, j)), jax.ShapeDtypeStruct((B, width, T), F32)

    def key_blocks(n_col):
        return (pl.BlockSpec((1, n_col, tt // tk, LANES, tk), lambda i, j: (i, 0, j, 0, 0)),
                jax.ShapeDtypeStruct((B, n_col, T // tk, LANES, tk), BF16))

    return rows, by_head, transposed, key_blocks


def _proj_even_wide(x, g, sc, sh, w_bf, tables, tt, tk):
    B, T, _ = x.shape
    x_spec, g_spec, m_spec = _row_specs(1, tt)
    t_spec = pl.BlockSpec((tt, LANES), lambda i, j: (j, 0))
    rows, by_head, transposed, key_blocks = _wide_specs(B, T, tt, tk)
    outs = [rows(4 * SEG, BF16), by_head(N_DIFF_HEADS), by_head(N_DIFF_HEADS), transposed(SEG), transposed(SEG),
            key_blocks(SEG // LANES), key_blocks(SEG // LANES)]
    return pl.pallas_call(
        _proj_even_wide_kernel,
        grid=(B, T // tt),
        in_specs=[x_spec, g_spec, m_spec, m_spec,
                  pl.BlockSpec(w_bf.shape, lambda i, j: (0, 0)), t_spec, t_spec, t_spec],
        out_specs=[o[0] for o in outs],
        out_shape=[o[1] for o in outs],
        compiler_params=_params("arbitrary", "arbitrary"),
        name="proj_even_wide",
    )(x, g, sc, sh, w_bf, *tables)


def _proj_odd_wide(x, g, sc, sh, w_bf, wf_bf, b_f, tt, tk):
    B, T, _ = x.shape
    x_spec, g_spec, m_spec = _row_specs(1, tt)
    rows, by_head, transposed, key_blocks = _wide_specs(B, T, tt, tk)
    outs = [rows(4 * SEG, BF16), transposed(FOX_W), transposed(FOX_W), key_blocks(FOX_W // LANES),
            rows(N_FOX_HEADS, F32)]
    return pl.pallas_call(
        _proj_odd_wide_kernel,
        grid=(B, T // tt),
        in_specs=[x_spec, g_spec, m_spec, m_spec,
                  pl.BlockSpec(w_bf.shape, lambda i, j: (0, 0)),
                  pl.BlockSpec(wf_bf.shape, lambda i, j: (0, 0)),
                  pl.BlockSpec((1, N_FOX_HEADS), lambda i, j: (0, 0))],
        out_specs=[o[0] for o in outs],
        out_shape=[o[1] for o in outs],
        compiler_params=_params("arbitrary", "arbitrary"),
        name="proj_odd_wide",
    )(x, g, sc, sh, w_bf, wf_bf, b_f.reshape(1, N_FOX_HEADS))


def _proj_odd(x, g, sc, sh, w_bf, wf_bf, b_f, bb, tt):
    B, T, _ = x.shape
    x_spec, g_spec, m_spec = _row_specs(bb, tt)
    leaf = jax.ShapeDtypeStruct((B, T, FOX_W), F32)
    return pl.pallas_call(
        _proj_odd_kernel,
        grid=(B // bb, T // tt),
        in_specs=[x_spec, g_spec, m_spec, m_spec,
                  pl.BlockSpec(w_bf.shape, lambda i, j: (0, 0)),
                  pl.BlockSpec(wf_bf.shape, lambda i, j: (0, 0)),
                  pl.BlockSpec((1, N_FOX_HEADS), lambda i, j: (0, 0))],
        out_specs=[_out_spec(bb, tt, 6 * SEG), _out_spec(bb, tt, FOX_W), _out_spec(bb, tt, FOX_W),
                   _out_spec(bb, tt, N_FOX_HEADS)],
        out_shape=[jax.ShapeDtypeStruct((B, T, 6 * SEG), BF16), leaf, leaf,
                   jax.ShapeDtypeStruct((B, T, N_FOX_HEADS), F32)],
        compiler_params=_params("arbitrary", "arbitrary"),
        name="proj_odd",
    )(x, g, sc, sh, w_bf, wf_bf, b_f.reshape(1, N_FOX_HEADS))


def _attn_kernel(*refs, kind, n_chain, tq, tk, q_off, n_valid, lam_init):
    if kind == "diff":
        q_ref, k_ref, vt_ref, lp_ref, sg_ref, o_ref, acc_ref, m_ref, s_ref = refs
    elif kind == "fox":
        q_ref, k_ref, vt_ref, nd_ref, o_ref, acc_ref, m_ref, s_ref = refs
    else:
        q_ref, k_ref, vt_ref, o_ref, acc_ref, m_ref, s_ref = refs
    tq2 = 2 * tq
    q0 = q_off + pl.program_id(2) * tq

    low = lax.broadcasted_iota(jnp.int32, (tq, LANES), 1) < HEAD_DIM
    if kind == "fox":
        row = lax.broadcasted_iota(jnp.int32, (tq2, LANES), 0)
        col = lax.broadcasted_iota(jnp.int32, (tq2, LANES), 1)
        first = jnp.right_shift(row, tq.bit_length() - 1) * 3
    qqs = []
    for c in range(n_chain):
        q = q_ref[0, :, c * LANES:(c + 1) * LANES]
        zero = jnp.zeros_like(q)
        qq = jnp.concatenate([jnp.where(low, q, zero), jnp.where(low, zero, q)], axis=0)
        if kind == "fox":
            pick = jnp.logical_and(first + 6 * c <= col, col < first + 6 * c + 3)
            qq = jnp.concatenate([qq, jnp.where(pick, 1.0, 0.0).astype(BF16)], axis=1)
        qqs.append(qq)

    acc_ref[...] = jnp.zeros_like(acc_ref)
    if kind == "sb":
        m_ref[...] = jnp.zeros_like(m_ref)
        tri_r = lax.broadcasted_iota(jnp.int32, (tk, tk), 0)
        tri_c = lax.broadcasted_iota(jnp.int32, (tk, tk), 1)
        after = jnp.where(tri_c > tri_r, 1.0, 0.0).astype(BF16)
        after2 = jnp.concatenate([after, after], axis=1)
    else:
        m_ref[...] = jnp.full_like(m_ref, MASKED)
        ones_rows = jnp.ones((BF16_ROWS, tk), BF16)

    qpos = q0 + jnp.bitwise_and(lax.broadcasted_iota(jnp.int32, (tk, tq2), 1), tq - 1)
    kidx = lax.broadcasted_iota(jnp.int32, (tk, tq2), 0)

    chains = range(n_chain)

    def qk(kb, slot):
        start = pl.multiple_of(kb * tk, tk)
        for c in chains:
            k = k_ref[0, pl.ds(start, tk), c * LANES:(c + 1) * LANES]
            if kind == "fox":
                k = jnp.concatenate([k, nd_ref[0, 0, pl.ds(start, tk), :]], axis=1)
            s_ref[slot, c] = _dot_nt(k, qqs[c])

    def consume(kb, slot, masked):
        scores = [s_ref[slot, c] for c in chains]
        if masked:
            kpos = kb * tk + kidx
            if kind == "diff":
                vis = jnp.right_shift(kpos, 6) <= jnp.right_shift(qpos, 6)
            elif kind == "fox":
                vis = kpos <= qpos
            else:
                vis = kpos < qpos
            vis = jnp.logical_and(vis, kpos < n_valid)
        vts = [vt_ref[0, c, kb] for c in chains]
        if kind == "sb":
            log_betas, laters = [], []
            for c in chains:
                s = scores[c]
                soft = jnp.log(1.0 + jnp.exp(-jnp.abs(s)))
                log_beta = jnp.minimum(s, 0.0) - soft
                log_keep = log_beta - s
                if masked:
                    log_keep = jnp.where(vis, log_keep, 0.0)
                hi = log_keep.astype(BF16)
                lo = (log_keep - hi.astype(F32)).astype(BF16)
                later = _dot(after2, jnp.concatenate([hi, lo], axis=0))
                laters.append(later + m_ref[c])
                m_ref[c] += later[0:1, :] + log_keep[0:1, :]
                log_betas.append(log_beta)
            weights = []
            for c in chains:
                w = jnp.exp(log_betas[c] + laters[c])
                if masked:
                    w = jnp.where(vis, w, 0.0)
                weights.append(w.astype(BF16))
            for c in chains:
                acc_ref[c] += _dot(vts[c], weights[c])
        else:
            probs, alphas = [], []
            for c in chains:
                s = scores[c]
                if masked:
                    s = jnp.where(vis, s, MASKED)
                m_prev = m_ref[c]
                m_new = jnp.maximum(m_prev, jnp.max(s, axis=0, keepdims=True))
                alphas.append(jnp.exp(m_prev - m_new))
                probs.append(jnp.exp(s - m_new).astype(BF16))
                m_ref[c] = m_new
            for c in chains:
                vt_aug = jnp.concatenate([vts[c], ones_rows], axis=0)
                acc_ref[c] = alphas[c] * acc_ref[c] + _dot(vt_aug, probs[c])

    n_full = q0 // tk
    qk(n_full, 0)
    qk(jnp.maximum(n_full - 1, 0), 1)
    consume(n_full, 0, True)

    def pair(kb, here, other):
        qk(jnp.maximum(kb - 1, 0), other)
        consume(kb, here, False)
        qk(jnp.maximum(kb - 2, 0), here)
        consume(kb - 1, other, False)

    if kind == "sb":
        def top_sum():
            top = m_ref[0]
            for c in range(1, n_chain):
                top = jnp.maximum(top, m_ref[c])
            return jnp.max(top)

        @pl.when(jnp.logical_and(n_full >= 1, top_sum() > SB_DEAD))
        def _():
            qk(jnp.maximum(n_full - 2, 0), 0)
            consume(n_full - 1, 1, False)

        n_rest = jnp.maximum(n_full - 1, 0)

        def alive(carry):
            return jnp.logical_and(carry[0] < n_rest // 2, carry[1] > SB_DEAD)

        def pair_and_check(carry):
            pair(n_full - 2 - 2 * carry[0], 0, 1)
            return carry[0] + 1, top_sum()

        _, top = lax.while_loop(alive, pair_and_check, (jnp.int32(0), top_sum()))
        last_slot = 0
        odd_block_left = jnp.logical_and(n_rest % 2 == 1, top > SB_DEAD)
    else:
        def pair_step(j, carry):
            pair(n_full - 1 - 2 * j, 1, 0)
            return carry

        lax.fori_loop(0, n_full // 2, pair_step, 0)
        last_slot = 1
        odd_block_left = n_full % 2 == 1

    @pl.when(odd_block_left)
    def _():
        consume(0, last_slot, False)

    if kind == "diff":
        lp = lp_ref[...]
        lam = (jnp.exp(jnp.sum(lp[0:1] * lp[1:2], axis=1, keepdims=True))
               - jnp.exp(jnp.sum(lp[2:3] * lp[3:4], axis=1, keepdims=True)) + lam_init)
    for c in range(n_chain):
        acc = acc_ref[c]
        if kind == "sb":
            out_t = jnp.concatenate([acc[0:HEAD_DIM, 0:tq], acc[HEAD_DIM:LANES, tq:tq2]], axis=0)
        else:
            r = 1.0 / acc[LANES:LANES + 1, :]
            if kind == "fox":
                out_t = jnp.concatenate([acc[0:HEAD_DIM, 0:tq] * r[:, 0:tq],
                                         acc[HEAD_DIM:LANES, tq:tq2] * r[:, tq:tq2]], axis=0)
            else:
                a = acc[0:LANES, 0:tq] * r[:, 0:tq] - lam * (acc[0:LANES, tq:tq2] * r[:, tq:tq2])
                ms = jnp.mean(a * a, axis=0, keepdims=True)
                out_t = a * lax.rsqrt(ms + EPS) * sg_ref[...] * (1.0 - lam_init)
        if tq < LANES:
            out_t = jnp.concatenate([out_t, jnp.zeros((LANES, LANES - tq), F32)], axis=1)
        o_ref[0, :, c * LANES:(c + 1) * LANES] = out_t.T[0:tq].astype(o_ref.dtype)


def _attention(kind, q_arr, q_col, k_arr, k_col, vt_arr, n_col, *, n_chain, tq, tk, q_off, n_valid,
               extras=(), lam_init=0.0):
    B, Tq, _ = q_arr.shape
    Tk = k_arr.shape[1]
    n_kb = Tk // tk
    assert Tq % tq == 0 and Tk % tk == 0 and tk % tq == 0 and q_off % tk == 0 and n_valid <= Tk
    assert tq & (tq - 1) == 0 and n_col % n_chain == 0 and q_col % n_chain == 0 and k_col % n_chain == 0
    wide = n_chain * LANES
    in_specs = [
        pl.BlockSpec((1, tq, wide), lambda b, c, i: (b, i, q_col // n_chain + c)),
        pl.BlockSpec((1, Tk, wide), lambda b, c, i: (b, 0, k_col // n_chain + c)),
        pl.BlockSpec((1, n_chain, n_kb, LANES, tk), lambda b, c, i: (b, c, 0, 0, 0)),
    ]
    acc_rows = LANES if kind == "sb" else LANES + BF16_ROWS
    scratch = [pltpu.VMEM((n_chain, acc_rows, 2 * tq), F32), pltpu.VMEM((n_chain, 1, 2 * tq), F32),
               pltpu.VMEM((2, n_chain, tk, 2 * tq), F32)]
    if kind == "diff":
        lp, sg = extras
        in_specs += [pl.BlockSpec(lp.shape, lambda b, c, i: (0, 0)),
                     pl.BlockSpec(sg.shape, lambda b, c, i: (0, 0))]
    elif kind == "fox":
        (nd,) = extras
        in_specs.append(pl.BlockSpec((1, 1, Tk, LANES), lambda b, c, i: (b, c, 0, 0)))
    kernel = functools.partial(_attn_kernel, kind=kind, n_chain=n_chain, tq=tq, tk=tk, q_off=q_off,
                               n_valid=n_valid, lam_init=lam_init)
    return pl.pallas_call(
        kernel,
        grid=(B, n_col // n_chain, Tq // tq),
        in_specs=in_specs,
        out_specs=pl.BlockSpec((1, tq, wide), lambda b, c, i: (b, i, c)),
        out_shape=jax.ShapeDtypeStruct((B, Tq, n_col * LANES), BF16),
        scratch_shapes=scratch,
        compiler_params=_params("arbitrary", "arbitrary", "arbitrary"),
        name="attn_" + kind,
    )(q_arr, k_arr, vt_arr, *extras)


def _key_blocks_t(v, tk):
    B, Tk, W = v.shape
    return v.reshape(B, Tk // tk, tk, W // LANES, LANES).transpose(0, 3, 1, 4, 2)


def _cache_prep_kernel(k_ref, v_ref, nk_ref, nvt_ref, ko_ref, vto_ref, *, n_heads, tk, time_minor):
    step = pl.program_id(1)
    n_past = pl.num_programs(1) - 1

    @pl.when(step < n_past)
    def _():
        for u in range(vto_ref.shape[2]):
            for c in range(ko_ref.shape[2] // LANES):
                cols = slice(c * LANES, (c + 1) * LANES)
                if time_minor:
                    ko_ref[0, u * tk:(u + 1) * tk, cols] = k_ref[0, cols, u * tk:(u + 1) * tk].T.astype(BF16)
                    vto_ref[0, c, u] = v_ref[0, cols, u * tk:(u + 1) * tk].astype(BF16)
                else:
                    head_rows = pl.ds(u * tk * n_heads + c, tk, stride=n_heads)
                    ko_ref[0, u * tk:(u + 1) * tk, cols] = k_ref[0, head_rows, :].astype(BF16)
                    vto_ref[0, c, u] = v_ref[0, head_rows, :].T.astype(BF16)

    @pl.when(step == n_past)
    def _():
        ko_ref[...] = nk_ref[...]
        vto_ref[...] = nvt_ref[...]


def _cache_prep(cache_k, cache_v, new_k, new_vt, tk):
    B, P, H, dh = cache_k.shape
    W = H * dh
    g = PREP_BLOCKS
    n_col, n_past = W // LANES, P // (g * tk)
    assert P % (g * tk) == 0 and dh in (HEAD_DIM, LANES)
    time_minor = dh != LANES
    if time_minor:
        cache_k, cache_v = [c.transpose(0, 2, 3, 1).reshape(B, W, P) for c in (cache_k, cache_v)]
        rows = pl.BlockSpec((1, W, g * tk), lambda b, j: (b, 0, jnp.minimum(j, n_past - 1)))
    else:
        cache_k, cache_v = [c.reshape(B, P * H, dh) for c in (cache_k, cache_v)]
        rows = pl.BlockSpec((1, g * tk * H, dh), lambda b, j: (b, jnp.minimum(j, n_past - 1), 0))
    return pl.pallas_call(
        functools.partial(_cache_prep_kernel, n_heads=H, tk=tk, time_minor=time_minor),
        grid=(B, n_past + 1),
        in_specs=[rows, rows,
                  pl.BlockSpec((1, g * tk, W), lambda b, j: (b, 0, 0)),
                  pl.BlockSpec((1, n_col, g, LANES, tk), lambda b, j: (b, 0, 0, 0, 0))],
        out_specs=[pl.BlockSpec((1, g * tk, W), lambda b, j: (b, j, 0)),
                   pl.BlockSpec((1, n_col, g, LANES, tk), lambda b, j: (b, 0, j, 0, 0))],
        out_shape=[jax.ShapeDtypeStruct((B, P + g * tk, W), BF16),
                   jax.ShapeDtypeStruct((B, n_col, P // tk + g, LANES, tk), BF16)],
        compiler_params=_params("arbitrary", "arbitrary"),
        name="cache_prep",
    )(cache_k, cache_v, new_k, new_vt)


def _cumsum_kernel(x_ref, o_ref, *, blk):
    T, H = x_ref.shape[1:]
    n_blocks = o_ref.shape[1]
    r = lax.broadcasted_iota(jnp.int32, (blk, blk), 0)
    c = lax.broadcasted_iota(jnp.int32, (blk, blk), 1)
    upto = jnp.where(c <= r, 1.0, 0.0).astype(BF16)
    per_block = 2 * N_CHAIN
    src = lax.broadcasted_iota(jnp.int32, (3 * H, n_blocks * LANES), 0)
    col = lax.broadcasted_iota(jnp.int32, (3 * H, n_blocks * LANES), 1)
    head, piece = jnp.bitwise_and(src, H - 1), jnp.right_shift(src, H.bit_length() - 1)
    here = jnp.right_shift(col, 7) == jnp.right_shift(head, per_block.bit_length() - 1)
    lane = jnp.bitwise_and(col, LANES - 1) - 3 * jnp.bitwise_and(head, per_block - 1)
    place = jnp.where(jnp.logical_and(here, lane == piece), 1.0, 0.0).astype(BF16)

    local_sums = []
    for j in range(T // blk):
        hi, mid, lo = _split3(x_ref[0, j * blk:(j + 1) * blk, :])
        local_sums.append(_dot(upto, hi) + _dot(upto, mid) + _dot(upto, lo))
    carry = jnp.zeros((1, H), F32)
    for j, local in enumerate(local_sums):
        total = local + carry
        carry = total[blk - 1:blk, :]
        pieces = jnp.concatenate([p.astype(F32) for p in _split3(-total)], axis=1).astype(BF16)
        lanes = _dot(pieces, place)
        for blkno in range(n_blocks):
            o_ref[0, blkno, j * blk:(j + 1) * blk, :] = lanes[:, blkno * LANES:(blkno + 1) * LANES].astype(BF16)


def _neg_cumsum(lf, blk):
    B, T, H = lf.shape
    n_blocks = H // (2 * N_CHAIN)
    assert H % (2 * N_CHAIN) == 0 and H & (H - 1) == 0
    return pl.pallas_call(
        functools.partial(_cumsum_kernel, blk=blk),
        grid=(B,),
        in_specs=[pl.BlockSpec((1, T, H), lambda b: (b, 0, 0))],
        out_specs=pl.BlockSpec((1, n_blocks, T, LANES), lambda b: (b, 0, 0, 0)),
        out_shape=jax.ShapeDtypeStruct((B, n_blocks, T, LANES), BF16),
        compiler_params=_params("arbitrary"),
        name="neg_cumsum",
    )(lf)


def _route_t(h, wr_t, br):
    tm = h.shape[0]
    h_hi = h.astype(BF16)
    h_lo = (h - h_hi.astype(F32)).astype(BF16)
    w_pieces = jnp.concatenate(_split3(wr_t), axis=0)
    parts = _dot_nt(w_pieces, h_hi) + _dot_nt(w_pieces, h_lo)
    logits = parts[0:N_EXPERTS] + parts[N_EXPERTS:2 * N_EXPERTS] + parts[2 * N_EXPERTS:]
    s = jax.nn.sigmoid(logits)
    sel = s + br
    row = lax.broadcasted_iota(jnp.int32, (N_EXPERTS, tm), 0).astype(F32)
    neg = -jnp.inf

    def first_argmax(vals):
        top = jnp.max(vals, axis=0, keepdims=True)
        idx = jnp.min(jnp.where(vals == top, row, float(N_EXPERTS)), axis=0, keepdims=True)
        return top, idx

    best = None
    for g in range(N_GROUPS):
        in_g = (row >= g * EXPERTS_PER_GROUP) & (row < (g + 1) * EXPERTS_PER_GROUP)
        top1, i1 = first_argmax(jnp.where(in_g, sel, neg))
        top2, i2 = first_argmax(jnp.where(in_g & (row != i1), sel, neg))
        cand = (top1 + top2, i1, i2, jnp.full((1, tm), float(g), F32))
        if best is None:
            best = cand
        else:
            better = cand[0] > best[0]
            best = tuple(jnp.where(better, new, old) for new, old in zip(cand, best))
    _, i1, i2, group = best
    picked = jnp.where((row == i1) | (row == i2), s, 0.0)
    return picked / jnp.sum(picked, axis=0, keepdims=True), group


def _moe_kernel(*refs, final, n_mixed):
    mixed_refs, refs = refs[:n_mixed], refs[n_mixed:]
    if final:
        (wo_ref, g1_ref, x_ref, g_ref, sc_ref, sh_ref, g2_ref, wrt_ref, br_ref, wg_ref, wu_ref, wd_ref, gf_ref,
         o_ref, h_ref, xg_ref, yg_ref, gg_ref, gate_ref, pos_ref, before_ref, n_ref) = refs
    else:
        (wo_ref, g1_ref, x_ref, g_ref, sc_ref, sh_ref, g2_ref, wrt_ref, br_ref, wg_ref, wu_ref, wd_ref,
         o_ref, h_ref, xg_ref, yg_ref, gg_ref, gate_ref, pos_ref, before_ref, n_ref) = refs
    bb, tt, d = x_ref.shape
    tm = bb * tt
    step = pl.program_id(2)
    first_expert = step * EXPERTS_PER_STEP
    group = first_expert // EXPERTS_PER_GROUP

    @pl.when((pl.program_id(0) == 0) & (pl.program_id(1) == 0) & (step == 0))
    def _():
        def fill(j, carry):
            rows = pl.ds(pl.multiple_of(j * MOE_CHUNK, MOE_CHUNK), MOE_CHUNK)
            r = lax.broadcasted_iota(jnp.int32, (MOE_CHUNK, tm), 0) + j * MOE_CHUNK
            c = lax.broadcasted_iota(jnp.int32, (MOE_CHUNK, tm), 1)
            before_ref[rows, :] = jnp.where(r < c, 1.0, 0.0).astype(BF16)
            return carry
        lax.fori_loop(0, tm // MOE_CHUNK, fill, 0)

    @pl.when(step == 0)
    def _():
        proj, off = None, 0
        for a_ref in mixed_refs:
            width = a_ref.shape[2]
            part = _dot(a_ref[...].reshape(tm, width), wo_ref[off:off + width, :])
            proj = part if proj is None else proj + part
            off += width
        x1 = x_ref[...] + g1_ref[...] * proj.reshape(bb, tt, d)
        o_ref[...] = x1
        h = _norm_mod(x1, g_ref[...], sc_ref[...], sh_ref[...])
        h_ref[...] = h.astype(BF16)
        gates, grp = _route_t(h, wrt_ref[...], br_ref[...])
        gates = jnp.concatenate([gates, jnp.zeros((LANES - N_EXPERTS, tm), F32)], axis=0)
        hi = gates.astype(BF16)
        gate_ref[0] = hi
        gate_ref[1] = (gates - hi.astype(F32)).astype(BF16)
        grow = lax.broadcasted_iota(jnp.int32, pos_ref.shape, 0).astype(F32)
        member = jnp.where(grow == grp, 1.0, 0.0)
        rank = _dot(member.astype(BF16), before_ref[...])
        pos_ref[...] = jnp.where(member > 0.0, rank, -1.0)
        for g in range(N_GROUPS):
            count = jnp.sum(member[g:g + 1, :]).astype(jnp.int32)
            n_ref[g] = (count + MOE_CHUNK - 1) // MOE_CHUNK

    n_chunks = n_ref[group]

    def chunk_rows(j):
        return pl.ds(pl.multiple_of(j * MOE_CHUNK, MOE_CHUNK), MOE_CHUNK)

    def one_hot(j, n_rows=MOE_CHUNK):
        want = (lax.broadcasted_iota(jnp.int32, (n_rows, tm), 0) + j * n_rows).astype(F32)
        return jnp.where(pos_ref[pl.ds(group, 1), :] == want, 1.0, 0.0).astype(BF16)

    @pl.when(first_expert % EXPERTS_PER_GROUP == 0)
    def _():
        def gather(j, carry):
            rows = pl.ds(pl.multiple_of(j * 2 * MOE_CHUNK, 2 * MOE_CHUNK), 2 * MOE_CHUNK)
            pick = one_hot(j, 2 * MOE_CHUNK)
            xg_ref[rows, :] = _dot(pick, h_ref[...]).astype(BF16)
            gg_ref[rows, :] = _dot_nt(pick, gate_ref[0]) + _dot_nt(pick, gate_ref[1])
            yg_ref[rows, :] = jnp.zeros((2 * MOE_CHUNK, d), F32)
            return carry
        lax.fori_loop(0, (n_chunks + 1) // 2, gather, 0)

    local = range(EXPERTS_PER_STEP)
    wd_all = wd_ref[0].reshape(EXPERTS_PER_STEP * D_EXPERT, d)
    lane = lax.broadcasted_iota(jnp.int32, (MOE_CHUNK, LANES), 1)

    def expert(chunks):
        rows = [chunk_rows(j) for j in chunks]
        xs = [xg_ref[r, :] for r in rows]
        gts = [[_dot(x, wg_ref[0, u]) for u in local] for x in xs]
        ups = [[_dot(x, wu_ref[0, u]) for u in local] for x in xs]
        acts = []
        for r, gt, up in zip(rows, gts, ups):
            gg = gg_ref[r, :]
            parts = []
            for u in local:
                ge = jnp.sum(jnp.where(lane == first_expert + u, gg, 0.0), axis=1, keepdims=True)
                parts.append(((gt[u] * jax.nn.sigmoid(gt[u])) * up[u] * ge).astype(BF16))
            acts.append(jnp.concatenate(parts, axis=1))
        for r, a in zip(rows, acts):
            yg_ref[r, :] += _dot(a, wd_all)

    def expert_pair(j, carry):
        expert((2 * j, 2 * j + 1))
        return carry

    lax.fori_loop(0, n_chunks // 2, expert_pair, 0)

    @pl.when(n_chunks % 2 == 1)
    def _():
        expert((n_chunks - 1,))

    @pl.when((first_expert + EXPERTS_PER_STEP) % EXPERTS_PER_GROUP == 0)
    def _():
        def scatter(j, carry):
            rows = pl.ds(pl.multiple_of(j * 2 * MOE_CHUNK, 2 * MOE_CHUNK), 2 * MOE_CHUNK)
            back = lax.dot_general(one_hot(j, 2 * MOE_CHUNK), yg_ref[rows, :].astype(BF16),
                                   (((0,), (0,)), ((), ())), preferred_element_type=F32)
            o_ref[...] += g2_ref[...] * back.reshape(bb, tt, d)
            return carry
        lax.fori_loop(0, (n_chunks + 1) // 2, scatter, 0)

    if final:
        @pl.when(step == N_EXPERTS // EXPERTS_PER_STEP - 1)
        def _():
            y = o_ref[...]
            ms = jnp.mean(y * y, axis=-1, keepdims=True)
            o_ref[...] = y * lax.rsqrt(ms + EPS) * gf_ref[...]


def _moe(mixed, w_out, g1, x, g, sc, sh, g2, w_router, b_router, w_gate, w_up, w_down, layer, g_final, bb, tt):
    B, T, _ = x.shape
    final = g_final is not None
    x_spec = pl.BlockSpec((bb, tt, D_MODEL), lambda i, j, e: (i, j, 0))
    g_spec = pl.BlockSpec((1, D_MODEL), lambda i, j, e: (0, 0))
    m_spec = pl.BlockSpec((bb, 1, D_MODEL), lambda i, j, e: (i, 0, 0))
    in_specs = [pl.BlockSpec((bb, tt, a.shape[2]), lambda i, j, e: (i, j, 0)) for a in mixed]
    in_specs += [pl.BlockSpec(w_out.shape, lambda i, j, e: (0, 0)), m_spec]
    in_specs += [x_spec, g_spec, m_spec, m_spec, m_spec,
                pl.BlockSpec((N_EXPERTS, D_MODEL), lambda i, j, e: (0, 0)),
                pl.BlockSpec((N_EXPERTS, 1), lambda i, j, e: (0, 0)),
                pl.BlockSpec((1, EXPERTS_PER_STEP, D_MODEL, D_EXPERT), lambda i, j, e: (layer, e, 0, 0)),
                pl.BlockSpec((1, EXPERTS_PER_STEP, D_MODEL, D_EXPERT), lambda i, j, e: (layer, e, 0, 0)),
                pl.BlockSpec((1, EXPERTS_PER_STEP, D_EXPERT, D_MODEL), lambda i, j, e: (layer, e, 0, 0))]
    args = [*mixed, w_out, g1, x, g, sc, sh, g2, w_router.T, b_router.reshape(N_EXPERTS, 1), w_gate, w_up, w_down]
    if final:
        in_specs.append(g_spec)
        args.append(g_final)
    tm = bb * tt
    assert tm % (2 * MOE_CHUNK) == 0
    packed = tm
    scratch = [pltpu.VMEM((tm, D_MODEL), BF16),
               pltpu.VMEM((packed, D_MODEL), BF16),
               pltpu.VMEM((packed, D_MODEL), F32),
               pltpu.VMEM((packed, LANES), F32),
               pltpu.VMEM((2, LANES, tm), BF16),
               pltpu.VMEM((2 * N_GROUPS, tm), F32),
               pltpu.VMEM((tm, tm), BF16),
               pltpu.SMEM((N_GROUPS,), jnp.int32)]
    return pl.pallas_call(
        functools.partial(_moe_kernel, final=final, n_mixed=len(mixed)),
        grid=(B // bb, T // tt, N_EXPERTS // EXPERTS_PER_STEP),
        in_specs=in_specs,
        out_specs=x_spec,
        out_shape=jax.ShapeDtypeStruct(x.shape, F32),
        scratch_shapes=scratch,
        compiler_params=_params("arbitrary", "arbitrary", "arbitrary"),
        name="moe",
    )(*args)


def _rope_tables(pos):
    half = ROPE_DIM // 2
    inv = ROPE_THETA ** (-jnp.arange(half, dtype=F32) / half)
    ang = pos.astype(F32)[:, None] * inv[None, :]
    cos, sin = jnp.cos(ang), jnp.sin(ang)
    ones = jnp.ones((pos.shape[0], HEAD_DIM - ROPE_DIM), F32)
    zeros = jnp.zeros_like(ones)
    zh = jnp.zeros_like(sin)
    c64 = jnp.concatenate([cos, cos, ones], axis=1)
    s1_64 = jnp.concatenate([-sin, zh, zeros], axis=1)
    s2_64 = jnp.concatenate([zh, sin, zeros], axis=1)
    return tuple(jnp.concatenate([t, t], axis=1) for t in (c64, s1_64, s2_64))


def _pad_time(a, total):
    if total == a.shape[1]:
        return a
    return jnp.pad(a, ((0, 0), (0, total - a.shape[1])) + ((0, 0),) * (a.ndim - 2))


def _trunk(x, mod, past, wts, cfg):
    B, T, _ = x.shape
    P = 0 if past is None else past["diff_k"].shape[2]
    bb, tt_proj, tt_moe, tq, tk = cfg
    wide = past is None and bb == 1 and tt_proj % tk == 0
    n_valid = P + T
    if past is None:
        Tk = -(-n_valid // tk) * tk
    else:
        assert T <= tk
        Tk = P + PREP_BLOCKS * tk
    Tq = -(-T // tq) * tq
    tables = _rope_tables(P + jnp.arange(T))
    if bb > 1:
        tables = tuple(jnp.tile(t, (bb, 1)) for t in tables)
    attn = functools.partial(_attention, n_chain=N_CHAIN, tq=tq, tk=tk, q_off=P, n_valid=n_valid)
    leaves_even, leaves_odd = [], []

    def with_cache(qkv, k_cols, v_cols, cache_k, cache_v):
        new_k = _pad_time(qkv[:, :, k_cols[0]:k_cols[1]], Tk - P)
        new_vt = _key_blocks_t(_pad_time(qkv[:, :, v_cols[0]:v_cols[1]], Tk - P), tk)
        return _cache_prep(cache_k, cache_v, new_k, new_vt, tk)

    def heads_last(leaf_t, n_heads):
        return leaf_t.reshape(B, n_heads, -1, T).transpose(0, 3, 1, 2)

    for l in range(DEPTH):
        sh1, sc1, g1, sh2, sc2, g2 = [mod[l, :, k * D_MODEL:(k + 1) * D_MODEL][:, None, :] for k in range(6)]
        i = l // 2
        g_mix = wts["norm_mix"][l][None, :]
        if l % 2 == 0:
            lam_init = 0.8 - 0.6 * math.exp(-0.3 * l)
            lp = jnp.stack([wts["diff_lq1"][i], wts["diff_lk1"][i], wts["diff_lq2"][i], wts["diff_lk2"][i]])
            diff_extras = (lp, wts["diff_subln"][i][:, None])
            if wide:
                qk, ak, av, bkt, bvt, dvt, svt = _proj_even_wide(x, g_mix, sc1, sh1, wts["w_in_even"][i], tables,
                                                                 tt_proj, tk)
                leaves_even.append((ak.reshape(B, T, N_DIFF_HEADS, LANES), av.reshape(B, T, N_DIFF_HEADS, LANES),
                                    heads_last(bkt, N_SB_HEADS), heads_last(bvt, N_SB_HEADS)))
                q_arr, q_cols = qk, (0, 2 * SEG // LANES)
                dk, dk_col, sk, sk_col = qk, SEG // LANES, qk, 3 * SEG // LANES
            else:
                qkv, ak, av, bk, bv = _proj_even(x, g_mix, sc1, sh1, wts["w_in_even"][i], tables, bb, tt_proj)
                leaves_even.append((ak.reshape(B, T, N_DIFF_HEADS, LANES), av.reshape(B, T, N_DIFF_HEADS, LANES),
                                    bk.reshape(B, T, N_SB_HEADS, HEAD_DIM), bv.reshape(B, T, N_SB_HEADS, HEAD_DIM)))
                q_arr, q_cols = _pad_time(qkv, Tq), (0, 3 * SEG // LANES)
                dk, dvt = with_cache(qkv, (SEG, 2 * SEG), (2 * SEG, 3 * SEG), past["diff_k"][i], past["diff_v"][i])
                sk, svt = with_cache(qkv, (4 * SEG, 5 * SEG), (5 * SEG, 6 * SEG), past["sb_k"][i], past["sb_v"][i])
                dk_col = sk_col = 0
            a_out = attn("diff", q_arr, q_cols[0], dk, dk_col, dvt, N_DIFF_HEADS, extras=diff_extras, lam_init=lam_init)
            b_out = attn("sb", q_arr, q_cols[1], sk, sk_col, svt, N_SB_HEADS // 2)
            mixed, w_out = [a_out[:, :T], b_out[:, :T]], wts["w_out_even"][i]
        else:
            if wide:
                qk, kt, vt, fvt, lf = _proj_odd_wide(x, g_mix, sc1, sh1, wts["w_in_odd"][i], wts["w_f_odd"][i],
                                                     wts["b_forget"][i], tt_proj, tk)
                leaves_odd.append((heads_last(kt, N_FOX_HEADS), heads_last(vt, N_FOX_HEADS), lf))
                q_arr, fk, fk_col, lf_all = qk, qk, 2 * SEG // LANES, lf
            else:
                qkv, k, v, lf = _proj_odd(x, g_mix, sc1, sh1, wts["w_in_odd"][i], wts["w_f_odd"][i],
                                          wts["b_forget"][i], bb, tt_proj)
                leaves_odd.append((k.reshape(B, T, N_FOX_HEADS, HEAD_DIM), v.reshape(B, T, N_FOX_HEADS, HEAD_DIM), lf))
                q_arr = _pad_time(qkv, Tq)
                fk, fvt = with_cache(qkv, (2 * SEG, 4 * SEG), (4 * SEG, 6 * SEG), past["fox_k"][i], past["fox_v"][i])
                fk_col = 0
                lf_all = jnp.concatenate([past["fox_logf"][i], lf], axis=1)
            nd = _neg_cumsum(_pad_time(lf_all, Tk), tk)
            o = attn("fox", q_arr, 0, fk, fk_col, fvt, N_FOX_HEADS // 2, extras=(nd,))
            mixed, w_out = [o[:, :T]], wts["w_out_odd"][i]
        g_final = wts["norm_final"][None, :] if l == DEPTH - 1 else None
        x = _moe(mixed, w_out, g1, x, wts["norm_ffn"][l][None, :], sc2, sh2, g2, wts["w_router"], wts["b_router"],
                 wts["w_gate"], wts["w_up"], wts["w_down"], l, g_final, bb, tt_moe)
    return x, leaves_even, leaves_odd


def kernel(x_prompt, x_sample, c_prompt, c_sample, cache_diff_k, cache_diff_v, cache_sb_k, cache_sb_v, cache_fox_k, cache_fox_v, cache_fox_logf, w_ada, b_ada, norm_mix, norm_ffn, w_in_even, w_out_even, diff_lq1, diff_lk1, diff_lq2, diff_lk2, diff_subln, w_in_odd, b_forget, w_out_odd, w_router, b_router, w_gate, w_up, w_down, norm_final):
    Bp, Tp, _ = x_prompt.shape
    Bs, Ts, _ = x_sample.shape
    wts = {
        "norm_mix": norm_mix, "norm_ffn": norm_ffn, "norm_final": norm_final,
        "w_in_even": w_in_even.astype(BF16), "w_out_even": w_out_even.astype(BF16),
        "w_in_odd": w_in_odd[:, :, :3 * FOX_W].astype(BF16), "w_f_odd": w_in_odd[:, :, 3 * FOX_W:].astype(BF16),
        "w_out_odd": w_out_odd.astype(BF16), "b_forget": b_forget,
        "diff_lq1": diff_lq1, "diff_lk1": diff_lk1, "diff_lq2": diff_lq2, "diff_lk2": diff_lk2,
        "diff_subln": diff_subln, "w_router": w_router, "b_router": b_router,
        "w_gate": w_gate.astype(BF16), "w_up": w_up.astype(BF16), "w_down": w_down.astype(BF16),
    }
    past = {"diff_k": cache_diff_k, "diff_v": cache_diff_v, "sb_k": cache_sb_k, "sb_v": cache_sb_v,
            "fox_k": cache_fox_k, "fox_v": cache_fox_v, "fox_logf": cache_fox_logf}

    rows = Bp + Bs
    rows_pad = -(-rows // 16) * 16
    c_all = jnp.pad(jnp.concatenate([c_prompt, c_sample], axis=0), ((0, rows_pad - rows), (0, 0)))
    mod = _ada_mod(c_all, w_ada, b_ada)

    cfg_prompt = (1, 512, 1024, 256, 256)
    cfg_sample = (Bs, Ts, Ts, 64, 256)
    y_p, even_p, odd_p = _trunk(x_prompt, mod[:, :Bp], None, wts, cfg_prompt)
    y_s, even_s, odd_s = _trunk(x_sample, mod[:, Bp:rows], past, wts, cfg_sample)

    def layers(rows):
        return rows[0][None] if len(rows) == 1 else jnp.stack(rows, axis=0)

    def pack(even, odd):
        return tuple(layers(r) for r in zip(*even)) + tuple(layers(r) for r in zip(*odd))

    return (y_p, y_s) + pack(even_p, odd_p) + pack(even_s, odd_s)
```
